```python
import math
import jax, jax.numpy as jnp
from jax import lax
import numpy as np

D_MODEL = 2048
BATCH = 4
SEQ = 2048
DEPTH = 1

HEAD_DIM = 128
SB_WIDTH = D_MODEL // 2
N_SB_HEADS = SB_WIDTH // HEAD_DIM
POOL_WIDTH = D_MODEL - SB_WIDTH
POOL_WINDOWS = (2, 4, 8, 16)
N_POOL_GROUPS = len(POOL_WINDOWS)
POOL_GROUP_DIM = POOL_WIDTH // N_POOL_GROUPS
MIX_WIDTH = SB_WIDTH + POOL_WIDTH
IN_PROJ_WIDTH = 3 * SB_WIDTH + POOL_WIDTH
N_EXPERTS = 32
TOP_K = 4
D_FF = D_MODEL
SWIGLU_ALPHA = 1.702
SWIGLU_LIMIT = 7.0
Q_BLOCK = 128
EXPERT_BLOCK = 256
EPS = 1e-6

kernel_name = "hybrid_stickbreak_pool_moe_adaln"


def rms_norm(x, w):
    xf = x.astype(jnp.float32)
    y = xf * lax.rsqrt(jnp.mean(xf * xf, axis=-1, keepdims=True) + EPS)
    return (y * w.astype(jnp.float32)).astype(x.dtype)


def stick_breaking_attention(q, k, v):
    B, S, H, d = q.shape
    nb = S // Q_BLOCK
    qh = q.astype(jnp.float32).transpose(0, 2, 1, 3)
    kh = k.astype(jnp.float32).transpose(0, 2, 1, 3)
    vh = v.astype(jnp.float32).transpose(0, 2, 1, 3)
    q_blocks = qh.reshape(B, H, nb, Q_BLOCK, d).transpose(2, 0, 1, 3, 4)
    starts = jnp.arange(nb, dtype=jnp.int32) * Q_BLOCK
    s_idx = jnp.arange(S, dtype=jnp.int32)
    inv_sqrt_d = 1.0 / math.sqrt(d)

    def one_block(args):
        qblk, start = args
        z = jnp.einsum('bhqd,bhkd->bhqk', qblk, kh) * inv_sqrt_d
        t_idx = start + jnp.arange(Q_BLOCK, dtype=jnp.int32)
        causal = s_idx[None, :] < t_idx[:, None]
        log_1m_beta = jnp.where(causal, jax.nn.log_sigmoid(-z), 0.0)
        later = lax.cumsum(log_1m_beta, axis=3, reverse=True) - log_1m_beta
        log_a = jax.nn.log_sigmoid(z) + later
        a = jnp.where(causal, jnp.exp(log_a), 0.0)
        return jnp.einsum('bhqk,bhkd->bhqd', a, vh)

    o = lax.map(one_block, (q_blocks, starts))
    return o.transpose(1, 0, 3, 2, 4).reshape(B, S, H * d)


def pool_mixer(u, w_pool, pool_scale):
    B, S, _ = u.shape
    uf = u.astype(jnp.float32)
    cs = jnp.cumsum(uf, axis=1)
    t = jnp.arange(S, dtype=jnp.int32)
    outs = []
    for g, w in enumerate(POOL_WINDOWS):
        lo, hi = g * POOL_GROUP_DIM, (g + 1) * POOL_GROUP_DIM
        cs_g = cs[..., lo:hi]
        prev = jnp.pad(cs_g, ((0, 0), (w, 0), (0, 0)))[:, :S]
        cnt = jnp.minimum(t + 1, w).astype(jnp.float32)[None, :, None]
        outs.append((cs_g - prev) / cnt - uf[..., lo:hi])
    p = jnp.stack(outs, axis=2)
    y = jnp.einsum('bsgc,gcd->bsgd', p, w_pool.astype(jnp.float32)).reshape(B, S, POOL_WIDTH)
    return (y * pool_scale.astype(jnp.float32)).astype(u.dtype)


def clamped_swiglu_expert(xb, w_in_e, b_in_e, w_out_e, b_out_e):
    gu = xb @ w_in_e + b_in_e
    g, lin = gu[:, :D_FF], gu[:, D_FF:]
    g = jnp.minimum(g, SWIGLU_LIMIT)
    lin = jnp.clip(lin, -SWIGLU_LIMIT, SWIGLU_LIMIT)
    act = g * jax.nn.sigmoid(SWIGLU_ALPHA * g) * (lin + 1.0)
    return act @ w_out_e + b_out_e


def moe_ffn(h, w_router, b_router, w_exp_in, b_exp_in, w_exp_out, b_exp_out):
    B, S, D = h.shape
    T = B * S
    N = T * TOP_K
    xf = h.reshape(T, D)
    logits = (xf @ w_router + b_router).astype(jnp.float32)
    top_vals, top_idx = lax.top_k(logits, TOP_K)
    gates = jax.nn.softmax(top_vals, axis=-1)

    eid = top_idx.reshape(N).astype(jnp.int32)
    tok = jnp.repeat(jnp.arange(T, dtype=jnp.int32), TOP_K)
    order = jnp.argsort(eid)
    eid_s, tok_s = eid[order], tok[order]
    counts = jnp.bincount(eid, length=N_EXPERTS).astype(jnp.int32)
    starts = jnp.cumsum(counts) - counts
    padded = ((counts + EXPERT_BLOCK - 1) // EXPERT_BLOCK) * EXPERT_BLOCK
    pends = jnp.cumsum(padded)
    pstarts = pends - padded
    dest_s = pstarts[eid_s] + (jnp.arange(N, dtype=jnp.int32) - starts[eid_s])

    n_blocks = (N + N_EXPERTS * (EXPERT_BLOCK - 1)) // EXPERT_BLOCK
    P = n_blocks * EXPERT_BLOCK
    tok_buf = jnp.full((P,), T, dtype=jnp.int32).at[dest_s].set(tok_s)
    x_ext = jnp.concatenate([xf, jnp.zeros((1, D), xf.dtype)], axis=0)
    x_pad = x_ext[tok_buf].reshape(n_blocks, EXPERT_BLOCK, D)
    block_start = jnp.arange(n_blocks, dtype=jnp.int32) * EXPERT_BLOCK
    block_e = jnp.minimum(jnp.searchsorted(pends, block_start, side='right'), N_EXPERTS - 1)

    def run_block(args):
        xb, e = args
        return clamped_swiglu_expert(xb, w_exp_in[e], b_exp_in[e], w_exp_out[e], b_exp_out[e])

    y_pad = lax.map(run_block, (x_pad, block_e)).reshape(P, D)
    dest = jnp.zeros((N,), jnp.int32).at[order].set(dest_s)
    y = jnp.sum(y_pad[dest].reshape(T, TOP_K, D).astype(jnp.float32) * gates[..., None], axis=1)
    return y.reshape(B, S, D).astype(h.dtype)


def setup_inputs(seed: int = 0) -> dict:
    key = jax.random.key(seed)
    ks = jax.random.split(key, 18)
    L = DEPTH

    def nrm(k, shape, scale):
        return jax.random.normal(k, shape, jnp.float32) * scale

    return {
        "x": nrm(ks[0], (BATCH, SEQ, D_MODEL), 1.0),
        "c": nrm(ks[1], (BATCH, D_MODEL), 1.0),
        "norm1_w": 1.0 + nrm(ks[2], (L, D_MODEL), 0.05),
        "norm2_w": 1.0 + nrm(ks[3], (L, D_MODEL), 0.05),
        "w_ada": nrm(ks[4], (L, D_MODEL, 6 * D_MODEL), 0.3 * D_MODEL ** -0.5),
        "b_ada": nrm(ks[5], (L, 6 * D_MODEL), 0.02),
        "w_in": nrm(ks[6], (L, D_MODEL, IN_PROJ_WIDTH), D_MODEL ** -0.5),
        "q_norm_w": 1.0 + nrm(ks[7], (L, HEAD_DIM), 0.05),
        "k_norm_w": 1.0 + nrm(ks[8], (L, HEAD_DIM), 0.05),
        "w_pool": nrm(ks[9], (L, N_POOL_GROUPS, POOL_GROUP_DIM, POOL_GROUP_DIM), POOL_GROUP_DIM ** -0.5),
        "pool_scale": 1.0 + nrm(ks[10], (L, POOL_WIDTH), 0.1),
        "w_o": nrm(ks[11], (L, MIX_WIDTH, D_MODEL), MIX_WIDTH ** -0.5),
        "w_router": nrm(ks[12], (L, D_MODEL, N_EXPERTS), D_MODEL ** -0.5),
        "b_router": nrm(ks[13], (L, N_EXPERTS), 0.01),
        "w_exp_in": nrm(ks[14], (L, N_EXPERTS, D_MODEL, 2 * D_FF), D_MODEL ** -0.5),
        "b_exp_in": nrm(ks[15], (L, N_EXPERTS, 2 * D_FF), 0.02),
        "w_exp_out": nrm(ks[16], (L, N_EXPERTS, D_FF, D_MODEL), D_FF ** -0.5),
        "b_exp_out": nrm(ks[17], (L, N_EXPERTS, D_MODEL), 0.02),
    }


def reference(x, c, norm1_w, norm2_w, w_ada, b_ada, w_in, q_norm_w, k_norm_w, w_pool,
              pool_scale, w_o, w_router, b_router, w_exp_in, b_exp_in, w_exp_out, b_exp_out):
    B, S, D = x.shape
    c_act = jax.nn.silu(c)
    for l in range(DEPTH):
        mod = c_act @ w_ada[l] + b_ada[l]
        shift1, scale1, gate1, shift2, scale2, gate2 = [m[:, None, :] for m in jnp.split(mod, 6, axis=-1)]

        h = rms_norm(x, norm1_w[l]) * (1.0 + scale1) + shift1
        proj = h @ w_in[l]
        q = proj[..., :SB_WIDTH].reshape(B, S, N_SB_HEADS, HEAD_DIM)
        k = proj[..., SB_WIDTH:2 * SB_WIDTH].reshape(B, S, N_SB_HEADS, HEAD_DIM)
        v = proj[..., 2 * SB_WIDTH:3 * SB_WIDTH].reshape(B, S, N_SB_HEADS, HEAD_DIM)
        u = proj[..., 3 * SB_WIDTH:]
        q = rms_norm(q, q_norm_w[l])
        k = rms_norm(k, k_norm_w[l])
        o_sb = stick_breaking_attention(q, k, v).astype(x.dtype)
        o_pool = pool_mixer(u, w_pool[l], pool_scale[l])
        mixed = jnp.concatenate([o_sb, o_pool], axis=-1) @ w_o[l]
        x = x + gate1 * mixed

        h2 = rms_norm(x, norm2_w[l]) * (1.0 + scale2) + shift2
        y = moe_ffn(h2, w_router[l], b_router[l], w_exp_in[l], b_exp_in[l], w_exp_out[l], b_exp_out[l])
        x = x + gate2 * y
    return x
```

```python
import functools
import math

import jax
import jax.numpy as jnp
from jax import lax
from jax.experimental import pallas as pl
from jax.experimental.pallas import tpu as pltpu

F32 = jnp.float32
BF16 = jnp.bfloat16
I32 = jnp.int32
U32 = jnp.uint32

EPS = 1e-6
HEAD_DIM = 128
POOL_WINDOWS = (2, 4, 8, 16)
TOP_K = 4
SWIGLU_ALPHA = 1.702
SWIGLU_LIMIT = 7.0

LANES = 128
EXPERT_ROWS = 256
LOG_UNDERFLOW = 104.0
VMEM_LIMIT = 56 * 1024 * 1024


def _params(sem=None, vmem=None):
    return pltpu.CompilerParams(dimension_semantics=sem, vmem_limit_bytes=vmem)


def _adaln_kernel(c_ref, w_ref, b_ref, o_ref):
    c = c_ref[...]
    ca = c / (1.0 + jnp.exp(-c))
    o_ref[...] = jnp.dot(ca.astype(BF16), w_ref[...].astype(BF16),
                         preferred_element_type=F32) + b_ref[...]


def _adaln(c, w_ada, b_ada):
    B, D = c.shape
    N = w_ada.shape[1]
    rows = 8
    tn = 1024
    cp = jnp.zeros((rows, D), F32).at[:B].set(c)
    out = pl.pallas_call(
        _adaln_kernel,
        out_shape=jax.ShapeDtypeStruct((rows, N), F32),
        grid=(N // tn,),
        in_specs=[pl.BlockSpec((rows, D), lambda j: (0, 0)),
                  pl.BlockSpec((D, tn), lambda j: (0, j)),
                  pl.BlockSpec((1, tn), lambda j: (0, j))],
        out_specs=pl.BlockSpec((rows, tn), lambda j: (0, j)),
        compiler_params=_params(("arbitrary",), VMEM_LIMIT),
        name="adaln",
    )(cp, w_ada, b_ada.reshape(1, N))
    return out[:B]


def _inproj_kernel(x_ref, nw_ref, sh_ref, sc_ref, w_ref, o_ref, h_ref, *, tm, ch):
    @pl.when(pl.program_id(1) == 0)
    def _():
        mul = nw_ref[...] * (1.0 + sc_ref[0])
        add = sh_ref[0]

        def body(c, _):
            r0 = pl.multiple_of(c * ch, ch)
            x = x_ref[pl.ds(r0, ch), :]
            inv = lax.rsqrt(jnp.mean(x * x, axis=-1, keepdims=True) + EPS)
            h_ref[pl.ds(r0, ch), :] = (x * inv * mul + add).astype(BF16)
            return 0

        lax.fori_loop(0, tm // ch, body, 0)

    o_ref[...] = jnp.dot(h_ref[...], w_ref[...],
                         preferred_element_type=F32).astype(o_ref.dtype)


def _inproj(x2, norm_w, shift, scale, w_bf, S):
    T, D = x2.shape
    N = w_bf.shape[1]
    tm, tn, ch = 512, 1024, 128
    per_b = S // tm
    return pl.pallas_call(
        functools.partial(_inproj_kernel, tm=tm, ch=ch),
        out_shape=jax.ShapeDtypeStruct((T, N), BF16),
        grid=(T // tm, N // tn),
        in_specs=[pl.BlockSpec((tm, D), lambda i, j: (i, 0)),
                  pl.BlockSpec((1, D), lambda i, j: (0, 0)),
                  pl.BlockSpec((1, 1, D), lambda i, j: (i // per_b, 0, 0)),
                  pl.BlockSpec((1, 1, D), lambda i, j: (i // per_b, 0, 0)),
                  pl.BlockSpec((D, tn), lambda i, j: (0, j))],
        out_specs=pl.BlockSpec((tm, tn), lambda i, j: (i, j)),
        scratch_shapes=[pltpu.VMEM((tm, D), BF16)],
        compiler_params=_params(("arbitrary", "arbitrary"), VMEM_LIMIT),
        name="inproj",
    )(x2, norm_w.reshape(1, D), shift[:, None, :], scale[:, None, :], w_bf)


def _attn_kernel(q_ref, k_ref, v_ref, qw_ref, kw_ref, o_ref, kn_ref, carry_ref, acc_ref,
                 *, S, tq, scale):
    i = pl.program_id(2)

    @pl.when(i == 0)
    def _():
        def body(c, _):
            r0 = pl.multiple_of(c * tq, tq)
            kk = k_ref[0, pl.ds(r0, tq), :].astype(F32)
            inv = lax.rsqrt(jnp.mean(kk * kk, axis=-1, keepdims=True) + EPS)
            kn_ref[pl.ds(r0, tq), :] = (kk * inv * kw_ref[...]).astype(BF16)
            return 0

        lax.fori_loop(0, S // tq, body, 0)

    q = q_ref[0].astype(F32)
    qn = q * lax.rsqrt(jnp.mean(q * q, axis=-1, keepdims=True) + EPS) * qw_ref[...]
    qb = (qn * scale).astype(BF16)

    row = lax.broadcasted_iota(I32, (tq, tq), 0)
    col = lax.broadcasted_iota(I32, (tq, tq), 1)
    causal = col < row
    tri = (row > col).astype(BF16)

    def block(kb, diag):
        r0 = pl.multiple_of(kb * tq, tq)
        kblk = kn_ref[pl.ds(r0, tq), :]
        vblk = v_ref[0, pl.ds(r0, tq), :]
        z = lax.dot_general(qb, kblk, (((1,), (1,)), ((), ())), preferred_element_type=F32)
        t = jnp.log(1.0 + jnp.exp(-jnp.abs(z)))
        lsn = jnp.minimum(-z, 0.0) - t
        lsp = jnp.minimum(z, 0.0) - t
        if diag:
            lsn = jnp.where(causal, lsn, 0.0)
        hi = lsn.astype(BF16)
        lo = (lsn - hi.astype(F32)).astype(BF16)
        later = (jnp.dot(hi, tri, preferred_element_type=F32)
                 + jnp.dot(lo, tri, preferred_element_type=F32))
        if diag:
            a = jnp.where(causal, jnp.exp(lsp + later), 0.0)
            acc_ref[...] = jnp.dot(a.astype(BF16), vblk, preferred_element_type=F32)
            carry = later[:, :1] + lsn[:, :1]
        else:
            a = jnp.exp(lsp + later + carry_ref[...])
            acc_ref[...] += jnp.dot(a.astype(BF16), vblk, preferred_element_type=F32)
            carry = carry_ref[...] + later[:, :1] + lsn[:, :1]
        carry_ref[...] = carry
        return jnp.max(carry)

    m0 = block(i, True)

    def cond(st):
        kb, m = st
        return jnp.logical_and(kb >= 0, m > -LOG_UNDERFLOW)

    def body(st):
        kb, _ = st
        return kb - 1, block(kb, False)

    lax.while_loop(cond, body, (i - 1, m0))
    o_ref[0] = acc_ref[...].astype(o_ref.dtype)


def _attention(proj3, q_norm_w, k_norm_w, n_heads):
    B, S, _ = proj3.shape
    d = HEAD_DIM
    tq = 256
    H = n_heads
    return pl.pallas_call(
        functools.partial(_attn_kernel, S=S, tq=tq, scale=1.0 / math.sqrt(d)),
        out_shape=jax.ShapeDtypeStruct((B, S, H * d), BF16),
        grid=(B, H, S // tq),
        in_specs=[pl.BlockSpec((1, tq, d), lambda b, h, i: (b, i, h)),
                  pl.BlockSpec((1, S, d), lambda b, h, i: (b, 0, H + h)),
                  pl.BlockSpec((1, S, d), lambda b, h, i: (b, 0, 2 * H + h)),
                  pl.BlockSpec((1, d), lambda b, h, i: (0, 0)),
                  pl.BlockSpec((1, d), lambda b, h, i: (0, 0))],
        out_specs=pl.BlockSpec((1, tq, d), lambda b, h, i: (b, i, h)),
        scratch_shapes=[pltpu.VMEM((S, d), BF16),
                        pltpu.VMEM((tq, 1), F32),
                        pltpu.VMEM((tq, d), F32)],
        compiler_params=_params(("arbitrary", "arbitrary", "arbitrary"), VMEM_LIMIT),
        name="stickbreak_attn",
    )(proj3, proj3, proj3, q_norm_w.reshape(1, d), k_norm_w.reshape(1, d))


def _pool_kernel(u_ref, w_ref, ps_ref, o_ref, *, S, ch, gd):
    halo = 16
    for g, win in enumerate(POOL_WINDOWS):
        lo, hi = g * gd, (g + 1) * gd
        wg = w_ref[g].astype(BF16)
        sc = ps_ref[:, lo:hi]

        def body(c, _, win=win, lo=lo, hi=hi, wg=wg, sc=sc):
            r0 = pl.multiple_of(c * ch, ch)
            cur = u_ref[0, pl.ds(r0, ch), lo:hi].astype(F32)
            p0 = pl.multiple_of(jnp.maximum(r0 - halo, 0), halo)
            prev = u_ref[0, pl.ds(p0, halo), lo:hi].astype(F32)
            prev = jnp.where(c > 0, prev, 0.0)
            s = jnp.concatenate([prev, cur], axis=0)
            n = 1
            while n < win:
                s = s + pltpu.roll(s, n, 0)
                n *= 2
            s = s[halo:]
            t = r0 + lax.broadcasted_iota(I32, (ch, 1), 0)
            cnt = jnp.minimum(t + 1, win).astype(F32)
            p = s / cnt - cur
            y = jnp.dot(p.astype(BF16), wg, preferred_element_type=F32) * sc
            o_ref[0, pl.ds(r0, ch), lo:hi] = y.astype(o_ref.dtype)
            return 0

        lax.fori_loop(0, S // ch, body, 0)


def _pool(proj3, w_pool, pool_scale, pool_width):
    B, S, NP = proj3.shape
    G, gd, _ = w_pool.shape
    return pl.pallas_call(
        functools.partial(_pool_kernel, S=S, ch=256, gd=gd),
        out_shape=jax.ShapeDtypeStruct((B, S, pool_width), BF16),
        grid=(B,),
        in_specs=[pl.BlockSpec((1, S, pool_width), lambda b: (b, 0, NP // pool_width - 1)),
                  pl.BlockSpec((G, gd, gd), lambda b: (0, 0, 0)),
                  pl.BlockSpec((1, pool_width), lambda b: (0, 0))],
        out_specs=pl.BlockSpec((1, S, pool_width), lambda b: (b, 0, 0)),
        compiler_params=_params(("arbitrary",), VMEM_LIMIT),
        name="pool_mixer",
    )(proj3, w_pool, pool_scale.reshape(1, pool_width))


def _outproj_kernel(osb_ref, opool_ref, wo_ref, x_ref, g1_ref, sh_ref, sc_ref, nw_ref,
                    wr_ref, br_ref, x1_ref, h2p_ref, gates_ref, idx_ref, *, sbw, n_exp):
    tm, D = x_ref.shape
    mixed = (jnp.dot(osb_ref[...], wo_ref[:sbw, :], preferred_element_type=F32)
             + jnp.dot(opool_ref[...], wo_ref[sbw:, :], preferred_element_type=F32))
    x1 = x_ref[...] + g1_ref[0] * mixed
    x1_ref[...] = x1
    inv = lax.rsqrt(jnp.mean(x1 * x1, axis=-1, keepdims=True) + EPS)
    h2 = x1 * inv * (nw_ref[...] * (1.0 + sc_ref[0])) + sh_ref[0]
    hb = h2.astype(BF16)

    bits = lax.bitcast_convert_type(hb.astype(F32), U32)
    half = D // 2
    h2p_ref[...] = (bits[:, :half] & jnp.uint32(0xFFFF0000)) | (bits[:, half:] >> 16)

    lane = lax.broadcasted_iota(I32, (tm, LANES), 1)
    lanef = lane.astype(F32)
    logits = jnp.dot(hb, wr_ref[...].astype(BF16), preferred_element_type=F32) + br_ref[...]
    vals = jnp.where(lane < n_exp, logits, -jnp.inf)
    tops, ids = [], []
    for _ in range(TOP_K):
        m = jnp.max(vals, axis=-1, keepdims=True)
        first = jnp.min(jnp.where(vals == m, lanef, float(LANES)), axis=-1, keepdims=True)
        tops.append(m)
        ids.append(first)
        vals = jnp.where(lanef == first, -jnp.inf, vals)
    es = [jnp.exp(m - tops[0]) for m in tops]
    den = es[0]
    for e in es[1:]:
        den = den + e
    gates = jnp.zeros((tm, LANES), F32)
    idx = jnp.zeros((tm, LANES), F32)
    for k in range(TOP_K):
        gates = jnp.where(lane == k, es[k] / den, gates)
        idx = jnp.where(lane == k, ids[k], idx)
    gates_ref[...] = gates
    idx_ref[...] = idx.astype(I32)


def _outproj(o_sb, o_pool, wo_bf, x2, gate1, shift2, scale2, norm2_w, w_router, b_router, S):
    T, D = x2.shape
    sbw = o_sb.shape[1]
    pw = o_pool.shape[1]
    n_exp = w_router.shape[1]
    tm = 256
    per_b = S // tm
    wr = jnp.zeros((D, LANES), F32).at[:, :n_exp].set(w_router)
    br = jnp.zeros((1, LANES), F32).at[0, :n_exp].set(b_router)
    mod_spec = pl.BlockSpec((1, 1, D), lambda i: (i // per_b, 0, 0))
    return pl.pallas_call(
        functools.partial(_outproj_kernel, sbw=sbw, n_exp=n_exp),
        out_shape=(jax.ShapeDtypeStruct((T, D), F32),
                   jax.ShapeDtypeStruct((T, D // 2), U32),
                   jax.ShapeDtypeStruct((T, LANES), F32),
                   jax.ShapeDtypeStruct((T, LANES), I32)),
        grid=(T // tm,),
        in_specs=[pl.BlockSpec((tm, sbw), lambda i: (i, 0)),
                  pl.BlockSpec((tm, pw), lambda i: (i, 0)),
                  pl.BlockSpec((sbw + pw, D), lambda i: (0, 0)),
                  pl.BlockSpec((tm, D), lambda i: (i, 0)),
                  mod_spec, mod_spec, mod_spec,
                  pl.BlockSpec((1, D), lambda i: (0, 0)),
                  pl.BlockSpec((D, LANES), lambda i: (0, 0)),
                  pl.BlockSpec((1, LANES), lambda i: (0, 0))],
        out_specs=(pl.BlockSpec((tm, D), lambda i: (i, 0)),
                   pl.BlockSpec((tm, D // 2), lambda i: (i, 0)),
                   pl.BlockSpec((tm, LANES), lambda i: (i, 0)),
                   pl.BlockSpec((tm, LANES), lambda i: (i, 0))),
        compiler_params=_params(("arbitrary",), VMEM_LIMIT),
        name="outproj_router",
    )(o_sb, o_pool, wo_bf, x2, gate1[:, None, :], shift2[:, None, :], scale2[:, None, :],
      norm2_w.reshape(1, D), wr, br)


def _route_kernel(idx_ref, dest_ref, meta_ref, rank_ref, *, T, ch):
    lane = lax.broadcasted_iota(I32, (ch, LANES), 1)
    row = lax.broadcasted_iota(I32, (ch, ch), 0)
    col = lax.broadcasted_iota(I32, (ch, ch), 1)
    before = (col < row).astype(BF16)

    def load(c):
        return idx_ref[pl.ds(pl.multiple_of(c * ch, ch), ch), :]

    def count(c, cnt):
        ii = load(c)
        member = lane == ii[:, 0:1]
        for k in range(1, TOP_K):
            member = jnp.logical_or(member, lane == ii[:, k:k + 1])
        mf = jnp.where(member, 1.0, 0.0)
        rank = jnp.dot(before, mf.astype(BF16), preferred_element_type=F32) + cnt
        rank_ref[pl.ds(pl.multiple_of(c * ch, ch), ch), :] = rank
        return cnt + jnp.sum(mf, axis=0, keepdims=True)

    cnt = lax.fori_loop(0, T // ch, count, jnp.zeros((1, LANES), F32))
    padded = jnp.ceil(cnt / EXPERT_ROWS) * EXPERT_ROWS
    rows = 8
    lane8 = lax.broadcasted_iota(I32, (rows, LANES), 1)
    ends = jnp.broadcast_to(padded, (rows, LANES))
    sh = 1
    while sh < LANES:
        ends = ends + jnp.where(lane8 >= sh, pltpu.roll(ends, sh, 1), 0.0)
        sh *= 2
    starts = ends - padded
    sub8 = lax.broadcasted_iota(I32, (rows, LANES), 0)
    meta = jnp.where(sub8 == 0, cnt, jnp.where(sub8 == 1, starts, padded))
    meta_ref[...] = meta.astype(I32)
    start_row = starts[0:1, :]

    def place(c, _):
        ii = load(c)
        val = rank_ref[pl.ds(pl.multiple_of(c * ch, ch), ch), :] + start_row
        out = jnp.zeros((ch, LANES), F32)
        for k in range(TOP_K):
            d = jnp.sum(jnp.where(lane == ii[:, k:k + 1], val, 0.0), axis=-1, keepdims=True)
            out = jnp.where(lane == k, d, out)
        dest_ref[pl.ds(pl.multiple_of(c * ch, ch), ch), :] = out.astype(I32)
        return 0

    lax.fori_loop(0, T // ch, place, 0)


def _route(idx_wide):
    T = idx_wide.shape[0]
    return pl.pallas_call(
        functools.partial(_route_kernel, T=T, ch=256),
        out_shape=(jax.ShapeDtypeStruct((T, LANES), I32),
                   jax.ShapeDtypeStruct((8, LANES), I32)),
        grid=(1,),
        in_specs=[pl.BlockSpec((T, LANES), lambda i: (0, 0))],
        out_specs=(pl.BlockSpec((T, LANES), lambda i: (0, 0)),
                   pl.BlockSpec((8, LANES), lambda i: (0, 0))),
        scratch_shapes=[pltpu.VMEM((T, LANES), F32)],
        compiler_params=_params(("arbitrary",), VMEM_LIMIT),
        name="route_ranks",
    )(idx_wide)


def _dispatch_kernel(dest_ref, cnt_ref, start_ref, h_ref, z_ref, x_ref, sem, zsem,
                     *, T, n_exp, n_blocks):
    def row_copy(t, d):
        return pltpu.make_async_copy(h_ref.at[pl.ds(t, 1), :], x_ref.at[pl.ds(d, 1), :], sem)

    def issue(t, _):
        for k in range(TOP_K):
            row_copy(t, dest_ref[TOP_K * t + k]).start()
        return 0

    lax.fori_loop(0, T, issue, 0)

    def zero_fill(e, wait):
        cnt = cnt_ref[e]
        npad = (-cnt) & (EXPERT_ROWS - 1)
        off = start_ref[e] + cnt

        def one(i, _):
            cp = pltpu.make_async_copy(z_ref.at[pl.ds(0, 1), :],
                                       x_ref.at[pl.ds(off + i, 1), :], zsem)
            if wait:
                cp.wait()
            else:
                cp.start()
            return 0

        lax.fori_loop(0, npad, one, 0)
        return 0

    used = (start_ref[n_exp - 1] + cnt_ref[n_exp - 1] + EXPERT_ROWS - 1) // EXPERT_ROWS

    def tail_fill(blk, wait):
        r0 = pl.multiple_of(blk * EXPERT_ROWS, EXPERT_ROWS)
        cp = pltpu.make_async_copy(z_ref, x_ref.at[pl.ds(r0, EXPERT_ROWS), :], zsem)
        if wait:
            cp.wait()
        else:
            cp.start()
        return 0

    lax.fori_loop(0, n_exp, lambda e, _: zero_fill(e, False), 0)
    lax.fori_loop(used, n_blocks, lambda b, _: tail_fill(b, False), 0)
    for _ in range(TOP_K):
        pltpu.make_async_copy(h_ref, x_ref.at[pl.ds(0, T), :], sem).wait()
    lax.fori_loop(0, n_exp, lambda e, _: zero_fill(e, True), 0)
    lax.fori_loop(used, n_blocks, lambda b, _: tail_fill(b, True), 0)


def _dispatch(dest_flat, cnt, starts, h2p, n_rows):
    T, W = h2p.shape
    n_exp = cnt.shape[0]
    zsrc = jnp.zeros((EXPERT_ROWS, W), U32)
    return pl.pallas_call(
        functools.partial(_dispatch_kernel, T=T, n_exp=n_exp, n_blocks=n_rows // EXPERT_ROWS),
        out_shape=jax.ShapeDtypeStruct((n_rows, W), U32),
        grid_spec=pltpu.PrefetchScalarGridSpec(
            num_scalar_prefetch=3,
            grid=(1,),
            in_specs=[pl.BlockSpec(memory_space=pl.ANY),
                      pl.BlockSpec(memory_space=pl.ANY)],
            out_specs=pl.BlockSpec(memory_space=pl.ANY),
            scratch_shapes=[pltpu.SemaphoreType.DMA, pltpu.SemaphoreType.DMA]),
        compiler_params=_params(("arbitrary",), VMEM_LIMIT),
        name="dispatch_rows",
    )(dest_flat, cnt, starts, h2p, zsrc)


def _unpack_rows(p):
    hi = lax.bitcast_convert_type(p & jnp.uint32(0xFFFF0000), F32).astype(BF16)
    lo = lax.bitcast_convert_type(p << 16, F32).astype(BF16)
    return jnp.concatenate([hi, lo], axis=1)


def _ffn1_kernel(ri_ref, ro_ref, e_ref, j_ref, first_ref, valid_ref, slot_ref, ne_ref, nj_ref, more_ref,
                 x_ref, w_ref, b_ref, o_ref, stage, wbf, sem, *, F, tn):
    q = pl.program_id(0)

    def wcopy(e, j, slot, part):
        c0 = pl.multiple_of(part * F + j * tn, tn)
        return pltpu.make_async_copy(w_ref.at[e, :, pl.ds(c0, tn)], stage.at[slot, part],
                                     sem.at[slot])

    @pl.when(q == 0)
    def _():
        for part in range(2):
            wcopy(e_ref[0], j_ref[0], 0, part).start()

    @pl.when(first_ref[q] == 1)
    def _():
        slot = slot_ref[q]
        for part in range(2):
            wcopy(e_ref[q], j_ref[q], slot, part).wait()

        @pl.when(more_ref[q] == 1)
        def _():
            for part in range(2):
                wcopy(ne_ref[q], nj_ref[q], 1 - slot, part).start()

        for part in range(2):
            wbf[part] = stage[slot, part].astype(BF16)

    @pl.when(valid_ref[q] == 1)
    def _():
        xb = _unpack_rows(x_ref[...])
        g = jnp.dot(xb, wbf[0], preferred_element_type=F32) + b_ref[0, 0, 0]
        lin = jnp.dot(xb, wbf[1], preferred_element_type=F32) + b_ref[0, 1, 0]
        g = jnp.minimum(g, SWIGLU_LIMIT)
        lin = jnp.clip(lin, -SWIGLU_LIMIT, SWIGLU_LIMIT)
        act = g / (1.0 + jnp.exp(-SWIGLU_ALPHA * g)) * (lin + 1.0)
        o_ref[...] = act.astype(o_ref.dtype)

    @pl.when(valid_ref[q] == 0)
    def _():
        o_ref[...] = jnp.zeros_like(o_ref)


def _ffn2_kernel(ri_ref, ro_ref, e_ref, j_ref, first_ref, valid_ref, slot_ref, ne_ref, nj_ref, more_ref,
                 a_ref, w_ref, b_ref, o_ref, stage, wbf, sem):
    q = pl.program_id(0)

    def wcopy(e, slot):
        return pltpu.make_async_copy(w_ref.at[e], stage.at[slot], sem.at[slot])

    @pl.when(q == 0)
    def _():
        wcopy(e_ref[0], 0).start()

    @pl.when(first_ref[q] == 1)
    def _():
        slot = slot_ref[q]
        wcopy(e_ref[q], slot).wait()

        @pl.when(more_ref[q] == 1)
        def _():
            wcopy(ne_ref[q], 1 - slot).start()

        wbf[...] = stage[slot].astype(BF16)

    @pl.when(valid_ref[q] == 1)
    def _():
        o_ref[...] = jnp.dot(a_ref[...], wbf[...], preferred_element_type=F32) + b_ref[0]

    @pl.when(valid_ref[q] == 0)
    def _():
        o_ref[...] = jnp.zeros_like(o_ref)


def _work_items(cnt, n_col_tiles, n_blocks):
    n_exp = cnt.shape[0]
    nblk = (cnt + EXPERT_ROWS - 1) // EXPERT_ROWS
    bstart = jnp.cumsum(nblk) - nblk
    gsize = jnp.repeat(nblk, n_col_tiles)
    gend = jnp.cumsum(gsize)
    gstart = gend - gsize
    n_groups = n_exp * n_col_tiles
    total = gend[-1]
    q = jnp.arange(n_blocks * n_col_tiles, dtype=I32)
    qc = jnp.minimum(q, total - 1)
    g = jnp.searchsorted(gend, qc, side="right").astype(I32)
    e = g // n_col_tiles
    j = g % n_col_tiles
    r = qc - gstart[g]
    valid = q < total
    first = jnp.logical_and(valid, r == 0)
    nonempty = gsize > 0
    ordinal = jnp.cumsum(nonempty) - 1
    cand = jnp.where(nonempty, jnp.arange(n_groups), n_groups)
    nxt_incl = lax.cummin(cand[::-1])[::-1]
    nxt = jnp.concatenate([nxt_incl[1:], jnp.full((1,), n_groups, nxt_incl.dtype)])[g]
    more = nxt < n_groups
    nxt = jnp.minimum(nxt, n_groups - 1)
    over = q - total
    row_in = bstart[e] + r
    row_out = jnp.where(valid, row_in, jnp.sum(nblk) + over // n_col_tiles)
    col_out = jnp.where(valid, j, over % n_col_tiles)
    as_i32 = lambda a: a.astype(I32)
    return tuple(map(as_i32, (row_in, row_out, e, col_out, first, valid, ordinal[g] % 2,
                              nxt // n_col_tiles, nxt % n_col_tiles, more)))


def _ffn1(items, x_pad, w_exp_in, b_exp_in):
    P, W = x_pad.shape
    n_exp, D, F2 = w_exp_in.shape
    F = F2 // 2
    tn = 1024
    nj = F // tn
    n_items = items[0].shape[0]
    bias = b_exp_in.reshape(n_exp, 2, nj, 1, tn)
    return pl.pallas_call(
        functools.partial(_ffn1_kernel, F=F, tn=tn),
        out_shape=jax.ShapeDtypeStruct((P, F), BF16),
        grid_spec=pltpu.PrefetchScalarGridSpec(
            num_scalar_prefetch=10,
            grid=(n_items,),
            in_specs=[pl.BlockSpec((EXPERT_ROWS, W), lambda q, ri, *_: (ri[q], 0)),
                      pl.BlockSpec(memory_space=pl.ANY),
                      pl.BlockSpec((1, 2, 1, 1, tn),
                                   lambda q, ri, ro, e, j, *_: (e[q], 0, j[q], 0, 0))],
            out_specs=pl.BlockSpec((EXPERT_ROWS, tn),
                                   lambda q, ri, ro, e, j, *_: (ro[q], j[q])),
            scratch_shapes=[pltpu.VMEM((2, 2, D, tn), F32),
                            pltpu.VMEM((2, D, tn), BF16),
                            pltpu.SemaphoreType.DMA((2,))]),
        compiler_params=_params(("arbitrary",), VMEM_LIMIT),
        name="expert_in_swiglu",
    )(*items, x_pad, w_exp_in, bias)


def _ffn2(items, act, w_exp_out, b_exp_out):
    P, F = act.shape
    n_exp, _, D = w_exp_out.shape
    n_items = items[0].shape[0]
    return pl.pallas_call(
        _ffn2_kernel,
        out_shape=jax.ShapeDtypeStruct((P, D), F32),
        grid_spec=pltpu.PrefetchScalarGridSpec(
            num_scalar_prefetch=10,
            grid=(n_items,),
            in_specs=[pl.BlockSpec((EXPERT_ROWS, F), lambda q, ri, *_: (ri[q], 0)),
                      pl.BlockSpec(memory_space=pl.ANY),
                      pl.BlockSpec((1, 1, D), lambda q, ri, ro, e, *_: (e[q], 0, 0))],
            out_specs=pl.BlockSpec((EXPERT_ROWS, D), lambda q, ri, ro, *_: (ro[q], 0)),
            scratch_shapes=[pltpu.VMEM((2, F, D), F32),
                            pltpu.VMEM((F, D), BF16),
                            pltpu.SemaphoreType.DMA((2,))]),
        compiler_params=_params(("arbitrary",), VMEM_LIMIT),
        name="expert_out",
    )(*items, act, w_exp_out, b_exp_out[:, None, :])


def _combine_kernel(dest_ref, y_ref, gates_ref, x1_ref, g2_ref, o_ref, buf, sem, *, tm):
    s = pl.program_id(0)
    ns = pl.num_programs(0)

    def gather(step, slot):
        def issue(t, _):
            for k in range(TOP_K):
                d = dest_ref[TOP_K * (step * tm + t) + k]
                pltpu.make_async_copy(y_ref.at[pl.ds(d, 1), :],
                                      buf.at[slot, k, pl.ds(t, 1), :], sem.at[slot]).start()
            return 0

        lax.fori_loop(0, tm, issue, 0)

    @pl.when(s == 0)
    def _():
        gather(0, 0)

    @pl.when(s + 1 < ns)
    def _():
        gather(s + 1, (s + 1) % 2)

    slot = s % 2
    for k in range(TOP_K):
        pltpu.make_async_copy(y_ref.at[pl.ds(0, tm), :], buf.at[slot, k], sem.at[slot]).wait()
    gates = gates_ref[...]
    y = gates[:, 0:1] * buf[slot, 0]
    for k in range(1, TOP_K):
        y = y + gates[:, k:k + 1] * buf[slot, k]
    o_ref[...] = x1_ref[...] + g2_ref[0] * y


def _combine(dest_flat, y_pad, gates_wide, x1, gate2, S):
    T, D = x1.shape
    tm = 128
    per_b = S // tm
    return pl.pallas_call(
        functools.partial(_combine_kernel, tm=tm),
        out_shape=jax.ShapeDtypeStruct((T, D), F32),
        grid_spec=pltpu.PrefetchScalarGridSpec(
            num_scalar_prefetch=1,
            grid=(T // tm,),
            in_specs=[pl.BlockSpec(memory_space=pl.ANY),
                      pl.BlockSpec((tm, LANES), lambda i, d: (i, 0)),
                      pl.BlockSpec((tm, D), lambda i, d: (i, 0)),
                      pl.BlockSpec((1, 1, D), lambda i, d: (i // per_b, 0, 0))],
            out_specs=pl.BlockSpec((tm, D), lambda i, d: (i, 0)),
            scratch_shapes=[pltpu.VMEM((2, TOP_K, tm, D), F32),
                            pltpu.SemaphoreType.DMA((2,))]),
        compiler_params=_params(("arbitrary",), VMEM_LIMIT),
        name="combine_rows",
    )(dest_flat, y_pad, gates_wide, x1, gate2[:, None, :])


def kernel(x, c, norm1_w, norm2_w, w_ada, b_ada, w_in, q_norm_w, k_norm_w, w_pool, pool_scale,
           w_o, w_router, b_router, w_exp_in, b_exp_in, w_exp_out, b_exp_out):
    B, S, D = x.shape
    T = B * S
    depth = w_ada.shape[0]
    n_exp = w_router.shape[-1]
    pool_width = pool_scale.shape[-1]
    sb_width = w_o.shape[1] - pool_width
    n_heads = sb_width // HEAD_DIM
    n_blocks = (T * TOP_K + n_exp * (EXPERT_ROWS - 1)) // EXPERT_ROWS
    n_rows = n_blocks * EXPERT_ROWS

    x2 = x.reshape(T, D)
    for l in range(depth):
        mod = _adaln(c, w_ada[l], b_ada[l])
        shift1, scale1, gate1, shift2, scale2, gate2 = jnp.split(mod, 6, axis=-1)

        proj = _inproj(x2, norm1_w[l], shift1, scale1, w_in[l].astype(BF16), S)
        proj3 = proj.reshape(B, S, -1)
        o_sb = _attention(proj3, q_norm_w[l], k_norm_w[l], n_heads)
        o_pool = _pool(proj3, w_pool[l], pool_scale[l], pool_width)
        x1, h2p, gates_wide, idx_wide = _outproj(
            o_sb.reshape(T, sb_width), o_pool.reshape(T, pool_width), w_o[l].astype(BF16),
            x2, gate1, shift2, scale2, norm2_w[l], w_router[l], b_router[l], S)

        dest_wide, meta = _route(idx_wide)
        cnt = meta[0, :n_exp]
        starts = meta[1, :n_exp]
        dest_flat = dest_wide[:, :TOP_K].reshape(T * TOP_K)
        x_pad = _dispatch(dest_flat, cnt, starts, h2p, n_rows)

        F = w_exp_out.shape[2]
        act = _ffn1(_work_items(cnt, F // 1024, n_blocks), x_pad, w_exp_in[l], b_exp_in[l])
        y_pad = _ffn2(_work_items(cnt, 1, n_blocks), act, w_exp_out[l], b_exp_out[l])
        x2 = _combine(dest_flat, y_pad, gates_wide, x1, gate2, S)
    return x2.reshape(B, S, D)
```

```python
import functools
import math

import jax
import jax.numpy as jnp
from jax import lax
from jax.experimental import pallas as pl
from jax.experimental.pallas import tpu as pltpu

F32 = jnp.float32
BF16 = jnp.bfloat16
I32 = jnp.int32
U32 = jnp.uint32

EPS = 1e-6
HEAD_DIM = 128
POOL_WINDOWS = (2, 4, 8, 16)
TOP_K = 4
SWIGLU_ALPHA = 1.702
SWIGLU_LIMIT = 7.0

LANES = 128
EXPERT_ROWS = 256
LOG_UNDERFLOW = 104.0
VMEM_LIMIT = 56 * 1024 * 1024


def _params(sem=None, vmem=None):
    return pltpu.CompilerParams(dimension_semantics=sem, vmem_limit_bytes=vmem)


_HIGH_HALF = 0xFFFF0000


def _pack_rows(v):
    bits = lax.bitcast_convert_type(v.astype(BF16).astype(F32), U32)
    half = v.shape[1] // 2
    return (bits[:, :half] & jnp.uint32(_HIGH_HALF)) | (bits[:, half:] >> 16)


def _unpack_halves(p):
    hi = lax.bitcast_convert_type(p & jnp.uint32(_HIGH_HALF), F32)
    lo = lax.bitcast_convert_type(p << 16, F32)
    return hi, lo


def _unpack_rows(p):
    hi, lo = _unpack_halves(p)
    return jnp.concatenate([hi.astype(BF16), lo.astype(BF16)], axis=1)


def _adaln_kernel(c_ref, w_ref, b_ref, o_ref):
    c = c_ref[...]
    ca = c / (1.0 + jnp.exp(-c))
    o_ref[...] = jnp.dot(ca.astype(BF16), w_ref[...].astype(BF16),
                         preferred_element_type=F32) + b_ref[...]


def _adaln(c, w_ada, b_ada):
    B, D = c.shape
    N = w_ada.shape[1]
    rows = 8
    tn = 1024
    cp = jnp.zeros((rows, D), F32).at[:B].set(c)
    out = pl.pallas_call(
        _adaln_kernel,
        out_shape=jax.ShapeDtypeStruct((rows, N), F32),
        grid=(N // tn,),
        in_specs=[pl.BlockSpec((rows, D), lambda j: (0, 0)),
                  pl.BlockSpec((D, tn), lambda j: (0, j)),
                  pl.BlockSpec((1, tn), lambda j: (0, j))],
        out_specs=pl.BlockSpec((rows, tn), lambda j: (0, j)),
        compiler_params=_params(("arbitrary",), VMEM_LIMIT),
        name="adaln",
    )(cp, w_ada, b_ada.reshape(1, N))
    return out[:B]


def _inproj_kernel(x_ref, nw_ref, sh_ref, sc_ref, w_ref, o_ref, h_ref, *, tm, ch):
    @pl.when(pl.program_id(1) == 0)
    def _():
        mul = nw_ref[...] * (1.0 + sc_ref[0])
        add = sh_ref[0]

        def body(c, _):
            r0 = pl.multiple_of(c * ch, ch)
            x = x_ref[pl.ds(r0, ch), :]
            inv = lax.rsqrt(jnp.mean(x * x, axis=-1, keepdims=True) + EPS)
            h_ref[pl.ds(r0, ch), :] = (x * inv * mul + add).astype(BF16)
            return 0

        lax.fori_loop(0, tm // ch, body, 0)

    o_ref[...] = jnp.dot(h_ref[...], w_ref[...],
                         preferred_element_type=F32).astype(o_ref.dtype)


def _inproj(x2, norm_w, shift, scale, w_bf, S):
    T, D = x2.shape
    N = w_bf.shape[1]
    tm, tn, ch = 1024, 1024, 128
    per_b = S // tm
    return pl.pallas_call(
        functools.partial(_inproj_kernel, tm=tm, ch=ch),
        out_shape=jax.ShapeDtypeStruct((T, N), BF16),
        grid=(T // tm, N // tn),
        in_specs=[pl.BlockSpec((tm, D), lambda i, j: (i, 0)),
                  pl.BlockSpec((1, D), lambda i, j: (0, 0)),
                  pl.BlockSpec((1, 1, D), lambda i, j: (i // per_b, 0, 0)),
                  pl.BlockSpec((1, 1, D), lambda i, j: (i // per_b, 0, 0)),
                  pl.BlockSpec((D, tn), lambda i, j: (0, j))],
        out_specs=pl.BlockSpec((tm, tn), lambda i, j: (i, j)),
        scratch_shapes=[pltpu.VMEM((tm, D), BF16)],
        compiler_params=_params(("arbitrary", "arbitrary"), VMEM_LIMIT),
        name="inproj",
    )(x2, norm_w.reshape(1, D), shift[:, None, :], scale[:, None, :], w_bf)


def _attn_kernel(q_ref, k_ref, v_ref, qw_ref, kw_ref, o_ref, kn_ref, carry_ref, acc_ref,
                 *, S, tq, hg, scale):
    i = pl.program_id(2)
    d = HEAD_DIM

    def head_norm(x, w):
        parts = []
        for h in range(hg):
            xh = x[:, h * d:(h + 1) * d]
            inv = lax.rsqrt(jnp.mean(xh * xh, axis=-1, keepdims=True) + EPS)
            parts.append(xh * inv * w)
        return parts

    @pl.when(i == 0)
    def _():
        def body(c, _):
            r0 = pl.multiple_of(c * tq, tq)
            parts = head_norm(k_ref[0, pl.ds(r0, tq), :].astype(F32), kw_ref[...])
            for h in range(hg):
                kn_ref[pl.ds(r0, tq), h * d:(h + 1) * d] = parts[h].astype(BF16)
            return 0

        lax.fori_loop(0, S // tq, body, 0)

    qb = [(p * scale).astype(BF16) for p in head_norm(q_ref[0].astype(F32), qw_ref[...])]

    row = lax.broadcasted_iota(I32, (tq, tq), 0)
    col = lax.broadcasted_iota(I32, (tq, tq), 1)
    causal = col < row
    tri = (row > col).astype(BF16)

    def block(kb, diag):
        r0 = pl.multiple_of(kb * tq, tq)
        worst = None
        for h in range(hg):
            cols = slice(h * d, (h + 1) * d)
            kblk = kn_ref[pl.ds(r0, tq), cols]
            vblk = v_ref[0, pl.ds(r0, tq), cols]
            z = lax.dot_general(qb[h], kblk, (((1,), (1,)), ((), ())),
                                preferred_element_type=F32)
            t = jnp.log(1.0 + jnp.exp(-jnp.abs(z)))
            lsn = jnp.minimum(-z, 0.0) - t
            lsp = jnp.minimum(z, 0.0) - t
            if diag:
                lsn = jnp.where(causal, lsn, 0.0)
            hi = lsn.astype(BF16)
            lo = (lsn - hi.astype(F32)).astype(BF16)
            later = (jnp.dot(hi, tri, preferred_element_type=F32)
                     + jnp.dot(lo, tri, preferred_element_type=F32))
            if diag:
                a = jnp.where(causal, jnp.exp(lsp + later), 0.0)
                acc_ref[:, cols] = jnp.dot(a.astype(BF16), vblk, preferred_element_type=F32)
                carry = later[:, :1] + lsn[:, :1]
            else:
                a = jnp.exp(lsp + later + carry_ref[h])
                acc_ref[:, cols] += jnp.dot(a.astype(BF16), vblk, preferred_element_type=F32)
                carry = carry_ref[h] + later[:, :1] + lsn[:, :1]
            carry_ref[h] = carry
            m = jnp.max(carry)
            worst = m if worst is None else jnp.maximum(worst, m)
        return worst

    m0 = block(i, True)

    def cond(st):
        kb, m = st
        return jnp.logical_and(kb >= 0, m > -LOG_UNDERFLOW)

    def body(st):
        kb, _ = st
        return kb - 1, block(kb, False)

    lax.while_loop(cond, body, (i - 1, m0))
    o_ref[0] = acc_ref[...].astype(o_ref.dtype)


def _attention(proj3, q_norm_w, k_norm_w, n_heads):
    B, S, _ = proj3.shape
    d = HEAD_DIM
    tq = 256
    hg = 2
    G = n_heads // hg
    w = hg * d
    return pl.pallas_call(
        functools.partial(_attn_kernel, S=S, tq=tq, hg=hg, scale=1.0 / math.sqrt(d)),
        out_shape=jax.ShapeDtypeStruct((B, S, n_heads * d), BF16),
        grid=(B, G, S // tq),
        in_specs=[pl.BlockSpec((1, tq, w), lambda b, g, i: (b, i, g)),
                  pl.BlockSpec((1, S, w), lambda b, g, i: (b, 0, G + g)),
                  pl.BlockSpec((1, S, w), lambda b, g, i: (b, 0, 2 * G + g)),
                  pl.BlockSpec((1, d), lambda b, g, i: (0, 0)),
                  pl.BlockSpec((1, d), lambda b, g, i: (0, 0))],
        out_specs=pl.BlockSpec((1, tq, w), lambda b, g, i: (b, i, g)),
        scratch_shapes=[pltpu.VMEM((S, w), BF16),
                        pltpu.VMEM((hg, tq, 1), F32),
                        pltpu.VMEM((tq, w), F32)],
        compiler_params=_params(("arbitrary", "arbitrary", "arbitrary"), VMEM_LIMIT),
        name="stickbreak_attn",
    )(proj3, proj3, proj3, q_norm_w.reshape(1, d), k_norm_w.reshape(1, d))


def _pool_kernel(u_ref, w_ref, ps_ref, o_ref, *, S, ch, gd):
    halo = 16
    for g, win in enumerate(POOL_WINDOWS):
        lo, hi = g * gd, (g + 1) * gd
        wg = w_ref[g].astype(BF16)
        sc = ps_ref[:, lo:hi]

        def body(c, _, win=win, lo=lo, hi=hi, wg=wg, sc=sc):
            r0 = pl.multiple_of(c * ch, ch)
            cur = u_ref[0, pl.ds(r0, ch), lo:hi].astype(F32)
            p0 = pl.multiple_of(jnp.maximum(r0 - halo, 0), halo)
            prev = u_ref[0, pl.ds(p0, halo), lo:hi].astype(F32)
            prev = jnp.where(c > 0, prev, 0.0)
            s = jnp.concatenate([prev, cur], axis=0)
            n = 1
            while n < win:
                s = s + pltpu.roll(s, n, 0)
                n *= 2
            s = s[halo:]
            t = r0 + lax.broadcasted_iota(I32, (ch, 1), 0)
            cnt = jnp.minimum(t + 1, win).astype(F32)
            p = s / cnt - cur
            y = jnp.dot(p.astype(BF16), wg, preferred_element_type=F32) * sc
            o_ref[0, pl.ds(r0, ch), lo:hi] = y.astype(o_ref.dtype)
            return 0

        lax.fori_loop(0, S // ch, body, 0)


def _pool(proj3, w_pool, pool_scale, pool_width):
    B, S, NP = proj3.shape
    G, gd, _ = w_pool.shape
    return pl.pallas_call(
        functools.partial(_pool_kernel, S=S, ch=256, gd=gd),
        out_shape=jax.ShapeDtypeStruct((B, S, pool_width), BF16),
        grid=(B,),
        in_specs=[pl.BlockSpec((1, S, pool_width), lambda b: (b, 0, NP // pool_width - 1)),
                  pl.BlockSpec((G, gd, gd), lambda b: (0, 0, 0)),
                  pl.BlockSpec((1, pool_width), lambda b: (0, 0))],
        out_specs=pl.BlockSpec((1, S, pool_width), lambda b: (b, 0, 0)),
        compiler_params=_params(("arbitrary",), VMEM_LIMIT),
        name="pool_mixer",
    )(proj3, w_pool, pool_scale.reshape(1, pool_width))


def _outproj_kernel(osb_ref, opool_ref, wo_ref, x_ref, g1_ref, sh_ref, sc_ref, nw_ref,
                    wr_ref, br_ref, x1_ref, h2p_ref, gates_ref, idx_ref, *, sbw, n_exp):
    tm, D = x_ref.shape
    mixed = (jnp.dot(osb_ref[...], wo_ref[:sbw, :], preferred_element_type=F32)
             + jnp.dot(opool_ref[...], wo_ref[sbw:, :], preferred_element_type=F32))
    x1 = x_ref[...] + g1_ref[0] * mixed
    x1_ref[...] = x1
    inv = lax.rsqrt(jnp.mean(x1 * x1, axis=-1, keepdims=True) + EPS)
    h2 = x1 * inv * (nw_ref[...] * (1.0 + sc_ref[0])) + sh_ref[0]
    hb = h2.astype(BF16)
    h2p_ref[...] = _pack_rows(h2)

    lane = lax.broadcasted_iota(I32, (tm, LANES), 1)
    lanef = lane.astype(F32)
    logits = jnp.dot(hb, wr_ref[...].astype(BF16), preferred_element_type=F32) + br_ref[...]
    vals = jnp.where(lane < n_exp, logits, -jnp.inf)
    tops, ids = [], []
    for _ in range(TOP_K):
        m = jnp.max(vals, axis=-1, keepdims=True)
        first = jnp.min(jnp.where(vals == m, lanef, float(LANES)), axis=-1, keepdims=True)
        tops.append(m)
        ids.append(first)
        vals = jnp.where(lanef == first, -jnp.inf, vals)
    es = [jnp.exp(m - tops[0]) for m in tops]
    den = es[0]
    for e in es[1:]:
        den = den + e
    gates = jnp.zeros((tm, LANES), F32)
    idx = jnp.zeros((tm, LANES), F32)
    for k in range(TOP_K):
        gates = jnp.where(lane == k, es[k] / den, gates)
        idx = jnp.where(lane == k, ids[k], idx)
    gates_ref[...] = gates
    idx_ref[...] = idx.astype(I32)


def _outproj(o_sb, o_pool, wo_bf, x2, gate1, shift2, scale2, norm2_w, w_router, b_router, S):
    T, D = x2.shape
    sbw = o_sb.shape[1]
    pw = o_pool.shape[1]
    n_exp = w_router.shape[1]
    tm = 256
    per_b = S // tm
    wr = jnp.zeros((D, LANES), F32).at[:, :n_exp].set(w_router)
    br = jnp.zeros((1, LANES), F32).at[0, :n_exp].set(b_router)
    mod_spec = pl.BlockSpec((1, 1, D), lambda i: (i // per_b, 0, 0))
    return pl.pallas_call(
        functools.partial(_outproj_kernel, sbw=sbw, n_exp=n_exp),
        out_shape=(jax.ShapeDtypeStruct((T, D), F32),
                   jax.ShapeDtypeStruct((T, D // 2), U32),
                   jax.ShapeDtypeStruct((T, LANES), F32),
                   jax.ShapeDtypeStruct((T, LANES), I32)),
        grid=(T // tm,),
        in_specs=[pl.BlockSpec((tm, sbw), lambda i: (i, 0)),
                  pl.BlockSpec((tm, pw), lambda i: (i, 0)),
                  pl.BlockSpec((sbw + pw, D), lambda i: (0, 0)),
                  pl.BlockSpec((tm, D), lambda i: (i, 0)),
                  mod_spec, mod_spec, mod_spec,
                  pl.BlockSpec((1, D), lambda i: (0, 0)),
                  pl.BlockSpec((D, LANES), lambda i: (0, 0)),
                  pl.BlockSpec((1, LANES), lambda i: (0, 0))],
        out_specs=(pl.BlockSpec((tm, D), lambda i: (i, 0)),
                   pl.BlockSpec((tm, D // 2), lambda i: (i, 0)),
                   pl.BlockSpec((tm, LANES), lambda i: (i, 0)),
                   pl.BlockSpec((tm, LANES), lambda i: (i, 0))),
        compiler_params=_params(("arbitrary",), VMEM_LIMIT),
        name="outproj_router",
    )(o_sb, o_pool, wo_bf, x2, gate1[:, None, :], shift2[:, None, :], scale2[:, None, :],
      norm2_w.reshape(1, D), wr, br)


def _route_kernel(idx_ref, dest_ref, meta_ref, rank_ref, *, T, ch):
    lane = lax.broadcasted_iota(I32, (ch, LANES), 1)
    row = lax.broadcasted_iota(I32, (ch, ch), 0)
    col = lax.broadcasted_iota(I32, (ch, ch), 1)
    before = (col < row).astype(BF16)

    def load(c):
        return idx_ref[pl.ds(pl.multiple_of(c * ch, ch), ch), :]

    def count(c, cnt):
        ii = load(c)
        member = lane == ii[:, 0:1]
        for k in range(1, TOP_K):
            member = jnp.logical_or(member, lane == ii[:, k:k + 1])
        mf = jnp.where(member, 1.0, 0.0)
        rank = jnp.dot(before, mf.astype(BF16), preferred_element_type=F32) + cnt
        rank_ref[pl.ds(pl.multiple_of(c * ch, ch), ch), :] = rank
        return cnt + jnp.sum(mf, axis=0, keepdims=True)

    cnt = lax.fori_loop(0, T // ch, count, jnp.zeros((1, LANES), F32))
    padded = jnp.ceil(cnt / EXPERT_ROWS) * EXPERT_ROWS
    rows = 8
    lane8 = lax.broadcasted_iota(I32, (rows, LANES), 1)
    ends = jnp.broadcast_to(padded, (rows, LANES))
    sh = 1
    while sh < LANES:
        ends = ends + jnp.where(lane8 >= sh, pltpu.roll(ends, sh, 1), 0.0)
        sh *= 2
    starts = ends - padded
    sub8 = lax.broadcasted_iota(I32, (rows, LANES), 0)
    meta = jnp.where(sub8 == 0, cnt, jnp.where(sub8 == 1, starts, padded))
    meta_ref[...] = meta.astype(I32)
    start_row = starts[0:1, :]

    def place(c, _):
        ii = load(c)
        val = rank_ref[pl.ds(pl.multiple_of(c * ch, ch), ch), :] + start_row
        out = jnp.zeros((ch, LANES), F32)
        for k in range(TOP_K):
            d = jnp.sum(jnp.where(lane == ii[:, k:k + 1], val, 0.0), axis=-1, keepdims=True)
            out = jnp.where(lane == k, d, out)
        dest_ref[pl.ds(pl.multiple_of(c * ch, ch), ch), :] = out.astype(I32)
        return 0

    lax.fori_loop(0, T // ch, place, 0)


def _route(idx_wide):
    T = idx_wide.shape[0]
    return pl.pallas_call(
        functools.partial(_route_kernel, T=T, ch=256),
        out_shape=(jax.ShapeDtypeStruct((T, LANES), I32),
                   jax.ShapeDtypeStruct((8, LANES), I32)),
        grid=(1,),
        in_specs=[pl.BlockSpec((T, LANES), lambda i: (0, 0))],
        out_specs=(pl.BlockSpec((T, LANES), lambda i: (0, 0)),
                   pl.BlockSpec((8, LANES), lambda i: (0, 0))),
        scratch_shapes=[pltpu.VMEM((T, LANES), F32)],
        compiler_params=_params(("arbitrary",), VMEM_LIMIT),
        name="route_ranks",
    )(idx_wide)


def _dispatch_kernel(dest_ref, cnt_ref, start_ref, h_ref, x_ref, z_ref, sem, zsem,
                     *, tb, n_exp, n_blocks):
    s = pl.program_id(0)

    def issue(t, _):
        for k in range(TOP_K):
            d = dest_ref[TOP_K * (s * tb + t) + k]
            pltpu.make_async_copy(h_ref.at[pl.ds(t, 1), :], x_ref.at[pl.ds(d, 1), :], sem).start()
        return 0

    lax.fori_loop(0, tb, issue, 0)

    @pl.when(s == 0)
    def _():
        z_ref[...] = jnp.zeros_like(z_ref)
        _dispatch_zero_fill(cnt_ref, start_ref, x_ref, z_ref, zsem, n_exp, n_blocks)

    for _ in range(TOP_K):
        pltpu.make_async_copy(h_ref, x_ref.at[pl.ds(0, tb), :], sem).wait()


def _dispatch_zero_fill(cnt_ref, start_ref, x_ref, z_ref, zsem, n_exp, n_blocks):
    def zero_fill(e, wait):
        cnt = cnt_ref[e]
        npad = (-cnt) & (EXPERT_ROWS - 1)
        off = start_ref[e] + cnt

        def one(i, _):
            cp = pltpu.make_async_copy(z_ref.at[pl.ds(0, 1), :],
                                       x_ref.at[pl.ds(off + i, 1), :], zsem)
            if wait:
                cp.wait()
            else:
                cp.start()
            return 0

        lax.fori_loop(0, npad, one, 0)
        return 0

    used = (start_ref[n_exp - 1] + cnt_ref[n_exp - 1] + EXPERT_ROWS - 1) // EXPERT_ROWS

    def tail_fill(blk, wait):
        r0 = pl.multiple_of(blk * EXPERT_ROWS, EXPERT_ROWS)
        cp = pltpu.make_async_copy(z_ref, x_ref.at[pl.ds(r0, EXPERT_ROWS), :], zsem)
        if wait:
            cp.wait()
        else:
            cp.start()
        return 0

    lax.fori_loop(0, n_exp, lambda e, _: zero_fill(e, False), 0)
    lax.fori_loop(used, n_blocks, lambda b, _: tail_fill(b, False), 0)
    lax.fori_loop(0, n_exp, lambda e, _: zero_fill(e, True), 0)
    lax.fori_loop(used, n_blocks, lambda b, _: tail_fill(b, True), 0)


def _dispatch(dest_flat, cnt, starts, h2p, n_rows):
    T, W = h2p.shape
    n_exp = cnt.shape[0]
    tb = 512
    return pl.pallas_call(
        functools.partial(_dispatch_kernel, tb=tb, n_exp=n_exp, n_blocks=n_rows // EXPERT_ROWS),
        out_shape=jax.ShapeDtypeStruct((n_rows, W), U32),
        grid_spec=pltpu.PrefetchScalarGridSpec(
            num_scalar_prefetch=3,
            grid=(T // tb,),
            in_specs=[pl.BlockSpec((tb, W), lambda s, *_: (s, 0))],
            out_specs=pl.BlockSpec(memory_space=pl.ANY),
            scratch_shapes=[pltpu.VMEM((EXPERT_ROWS, W), U32),
                            pltpu.SemaphoreType.DMA, pltpu.SemaphoreType.DMA]),
        compiler_params=_params(("arbitrary",), VMEM_LIMIT),
        name="dispatch_rows",
    )(dest_flat, cnt, starts, h2p)


def _ffn1_kernel(ri_ref, ro_ref, e_ref, j_ref, first_ref, valid_ref, slot_ref, ne_ref, nj_ref, more_ref,
                 x_ref, w_ref, b_ref, o_ref, stage, wbf, sem, *, F, tn):
    q = pl.program_id(0)

    def wcopy(e, j, slot, part):
        c0 = pl.multiple_of(part * F + j * tn, tn)
        return pltpu.make_async_copy(w_ref.at[e, :, pl.ds(c0, tn)], stage.at[slot, part],
                                     sem.at[slot])

    @pl.when(q == 0)
    def _():
        for part in range(2):
            wcopy(e_ref[0], j_ref[0], 0, part).start()

    @pl.when(first_ref[q] == 1)
    def _():
        slot = slot_ref[q]
        for part in range(2):
            wcopy(e_ref[q], j_ref[q], slot, part).wait()

        @pl.when(more_ref[q] == 1)
        def _():
            for part in range(2):
                wcopy(ne_ref[q], nj_ref[q], 1 - slot, part).start()

        for part in range(2):
            wbf[part] = stage[slot, part].astype(BF16)

    @pl.when(valid_ref[q] == 1)
    def _():
        xb = _unpack_rows(x_ref[...])
        g = jnp.dot(xb, wbf[0], preferred_element_type=F32) + b_ref[0, 0, 0]
        lin = jnp.dot(xb, wbf[1], preferred_element_type=F32) + b_ref[0, 1, 0]
        g = jnp.minimum(g, SWIGLU_LIMIT)
        lin = jnp.clip(lin, -SWIGLU_LIMIT, SWIGLU_LIMIT)
        act = g / (1.0 + jnp.exp(-SWIGLU_ALPHA * g)) * (lin + 1.0)
        o_ref[...] = act.astype(o_ref.dtype)

    @pl.when(valid_ref[q] == 0)
    def _():
        o_ref[...] = jnp.zeros_like(o_ref)


def _ffn2_kernel(ri_ref, ro_ref, e_ref, j_ref, first_ref, valid_ref, slot_ref, ne_ref, nj_ref, more_ref,
                 a_ref, w_ref, b_ref, o_ref, stage, wbf, sem):
    q = pl.program_id(0)

    def wcopy(e, slot):
        return pltpu.make_async_copy(w_ref.at[e], stage.at[slot], sem.at[slot])

    @pl.when(q == 0)
    def _():
        wcopy(e_ref[0], 0).start()

    @pl.when(first_ref[q] == 1)
    def _():
        slot = slot_ref[q]
        wcopy(e_ref[q], slot).wait()

        @pl.when(more_ref[q] == 1)
        def _():
            wcopy(ne_ref[q], 1 - slot).start()

        wbf[...] = stage[slot].astype(BF16)

    @pl.when(valid_ref[q] == 1)
    def _():
        y = jnp.dot(a_ref[...], wbf[...], preferred_element_type=F32) + b_ref[0]
        o_ref[...] = _pack_rows(y)

    @pl.when(valid_ref[q] == 0)
    def _():
        o_ref[...] = jnp.zeros_like(o_ref)


def _work_items(cnt, n_col_tiles, n_blocks):
    n_exp = cnt.shape[0]
    nblk = (cnt + EXPERT_ROWS - 1) // EXPERT_ROWS
    bstart = jnp.cumsum(nblk) - nblk
    gsize = jnp.repeat(nblk, n_col_tiles)
    gend = jnp.cumsum(gsize)
    gstart = gend - gsize
    n_groups = n_exp * n_col_tiles
    total = gend[-1]
    q = jnp.arange(n_blocks * n_col_tiles, dtype=I32)
    qc = jnp.minimum(q, total - 1)
    g = jnp.searchsorted(gend, qc, side="right").astype(I32)
    e = g // n_col_tiles
    j = g % n_col_tiles
    r = qc - gstart[g]
    valid = q < total
    first = jnp.logical_and(valid, r == 0)
    nonempty = gsize > 0
    ordinal = jnp.cumsum(nonempty) - 1
    cand = jnp.where(nonempty, jnp.arange(n_groups), n_groups)
    nxt_incl = lax.cummin(cand[::-1])[::-1]
    nxt = jnp.concatenate([nxt_incl[1:], jnp.full((1,), n_groups, nxt_incl.dtype)])[g]
    more = nxt < n_groups
    nxt = jnp.minimum(nxt, n_groups - 1)
    over = q - total
    row_in = bstart[e] + r
    row_out = jnp.where(valid, row_in, jnp.sum(nblk) + over // n_col_tiles)
    col_out = jnp.where(valid, j, over % n_col_tiles)
    as_i32 = lambda a: a.astype(I32)
    return tuple(map(as_i32, (row_in, row_out, e, col_out, first, valid, ordinal[g] % 2,
                              nxt // n_col_tiles, nxt % n_col_tiles, more)))


def _ffn1(items, x_pad, w_exp_in, b_exp_in):
    P, W = x_pad.shape
    n_exp, D, F2 = w_exp_in.shape
    F = F2 // 2
    tn = 1024
    nj = F // tn
    n_items = items[0].shape[0]
    bias = b_exp_in.reshape(n_exp, 2, nj, 1, tn)
    return pl.pallas_call(
        functools.partial(_ffn1_kernel, F=F, tn=tn),
        out_shape=jax.ShapeDtypeStruct((P, F), BF16),
        grid_spec=pltpu.PrefetchScalarGridSpec(
            num_scalar_prefetch=10,
            grid=(n_items,),
            in_specs=[pl.BlockSpec((EXPERT_ROWS, W), lambda q, ri, *_: (ri[q], 0)),
                      pl.BlockSpec(memory_space=pl.ANY),
                      pl.BlockSpec((1, 2, 1, 1, tn),
                                   lambda q, ri, ro, e, j, *_: (e[q], 0, j[q], 0, 0))],
            out_specs=pl.BlockSpec((EXPERT_ROWS, tn),
                                   lambda q, ri, ro, e, j, *_: (ro[q], j[q])),
            scratch_shapes=[pltpu.VMEM((2, 2, D, tn), F32),
                            pltpu.VMEM((2, D, tn), BF16),
                            pltpu.SemaphoreType.DMA((2,))]),
        compiler_params=_params(("arbitrary",), VMEM_LIMIT),
        name="expert_in_swiglu",
    )(*items, x_pad, w_exp_in, bias)


def _ffn2(items, act, w_exp_out, b_exp_out):
    P, F = act.shape
    n_exp, _, D = w_exp_out.shape
    n_items = items[0].shape[0]
    return pl.pallas_call(
        _ffn2_kernel,
        out_shape=jax.ShapeDtypeStruct((P, D // 2), U32),
        grid_spec=pltpu.PrefetchScalarGridSpec(
            num_scalar_prefetch=10,
            grid=(n_items,),
            in_specs=[pl.BlockSpec((EXPERT_ROWS, F), lambda q, ri, *_: (ri[q], 0)),
                      pl.BlockSpec(memory_space=pl.ANY),
                      pl.BlockSpec((1, 1, D), lambda q, ri, ro, e, *_: (e[q], 0, 0))],
            out_specs=pl.BlockSpec((EXPERT_ROWS, D // 2), lambda q, ri, ro, *_: (ro[q], 0)),
            scratch_shapes=[pltpu.VMEM((2, F, D), F32),
                            pltpu.VMEM((F, D), BF16),
                            pltpu.SemaphoreType.DMA((2,))]),
        compiler_params=_params(("arbitrary",), VMEM_LIMIT),
        name="expert_out",
    )(*items, act, w_exp_out, b_exp_out[:, None, :])


def _combine_kernel(dest_ref, y_ref, gates_ref, x1_ref, g2_ref, o_ref, buf, sem, *, tm):
    s = pl.program_id(0)
    ns = pl.num_programs(0)

    def gather(step, slot):
        def issue(t, _):
            for k in range(TOP_K):
                d = dest_ref[TOP_K * (step * tm + t) + k]
                pltpu.make_async_copy(y_ref.at[pl.ds(d, 1), :],
                                      buf.at[slot, k, pl.ds(t, 1), :], sem.at[slot]).start()
            return 0

        lax.fori_loop(0, tm, issue, 0)

    @pl.when(s == 0)
    def _():
        gather(0, 0)

    @pl.when(s + 1 < ns)
    def _():
        gather(s + 1, (s + 1) % 2)

    slot = s % 2
    for k in range(TOP_K):
        pltpu.make_async_copy(y_ref.at[pl.ds(0, tm), :], buf.at[slot, k], sem.at[slot]).wait()
    gates = gates_ref[...]
    y_hi = y_lo = None
    for k in range(TOP_K):
        hi, lo = _unpack_halves(buf[slot, k])
        g = gates[:, k:k + 1]
        y_hi = g * hi if y_hi is None else y_hi + g * hi
        y_lo = g * lo if y_lo is None else y_lo + g * lo
    y = jnp.concatenate([y_hi, y_lo], axis=1)
    o_ref[...] = x1_ref[...] + g2_ref[0] * y


def _combine(dest_flat, y_pad, gates_wide, x1, gate2, S):
    T, D = x1.shape
    tm = 128
    per_b = S // tm
    return pl.pallas_call(
        functools.partial(_combine_kernel, tm=tm),
        out_shape=jax.ShapeDtypeStruct((T, D), F32),
        grid_spec=pltpu.PrefetchScalarGridSpec(
            num_scalar_prefetch=1,
            grid=(T // tm,),
            in_specs=[pl.BlockSpec(memory_space=pl.ANY),
                      pl.BlockSpec((tm, LANES), lambda i, d: (i, 0)),
                      pl.BlockSpec((tm, D), lambda i, d: (i, 0)),
                      pl.BlockSpec((1, 1, D), lambda i, d: (i // per_b, 0, 0))],
            out_specs=pl.BlockSpec((tm, D), lambda i, d: (i, 0)),
            scratch_shapes=[pltpu.VMEM((2, TOP_K, tm, D // 2), U32),
                            pltpu.SemaphoreType.DMA((2,))]),
        compiler_params=_params(("arbitrary",), VMEM_LIMIT),
        name="combine_rows",
    )(dest_flat, y_pad, gates_wide, x1, gate2[:, None, :])


def kernel(x, c, norm1_w, norm2_w, w_ada, b_ada, w_in, q_norm_w, k_norm_w, w_pool, pool_scale,
           w_o, w_router, b_router, w_exp_in, b_exp_in, w_exp_out, b_exp_out):
    B, S, D = x.shape
    T = B * S
    depth = w_ada.shape[0]
    n_exp = w_router.shape[-1]
    pool_width = pool_scale.shape[-1]
    sb_width = w_o.shape[1] - pool_width
    n_heads = sb_width // HEAD_DIM
    n_blocks = (T * TOP_K + n_exp * (EXPERT_ROWS - 1)) // EXPERT_ROWS
    n_rows = n_blocks * EXPERT_ROWS

    x2 = x.reshape(T, D)
    for l in range(depth):
        mod = _adaln(c, w_ada[l], b_ada[l])
        shift1, scale1, gate1, shift2, scale2, gate2 = jnp.split(mod, 6, axis=-1)

        proj = _inproj(x2, norm1_w[l], shift1, scale1, w_in[l].astype(BF16), S)
        proj3 = proj.reshape(B, S, -1)
        o_sb = _attention(proj3, q_norm_w[l], k_norm_w[l], n_heads)
        o_pool = _pool(proj3, w_pool[l], pool_scale[l], pool_width)
        x1, h2p, gates_wide, idx_wide = _outproj(
            o_sb.reshape(T, sb_width), o_pool.reshape(T, pool_width), w_o[l].astype(BF16),
            x2, gate1, shift2, scale2, norm2_w[l], w_router[l], b_router[l], S)

        dest_wide, meta = _route(idx_wide)
        cnt = meta[0, :n_exp]
        starts = meta[1, :n_exp]
        dest_flat = dest_wide[:, :TOP_K].reshape(T * TOP_K)
        x_pad = _dispatch(dest_flat, cnt, starts, h2p, n_rows)

        F = w_exp_out.shape[2]
        act = _ffn1(_work_items(cnt, F // 1024, n_blocks), x_pad, w_exp_in[l], b_exp_in[l])
        y_pad = _ffn2(_work_items(cnt, 1, n_blocks), act, w_exp_out[l], b_exp_out[l])
        x2 = _combine(dest_flat, y_pad, gates_wide, x1, gate2, S)
    return x2.reshape(B, S, D)
```

```python
import functools
import math

import jax
import jax.numpy as jnp
from jax import lax
from jax.experimental import pallas as pl
from jax.experimental.pallas import tpu as pltpu

F32 = jnp.float32
BF16 = jnp.bfloat16
I32 = jnp.int32
U32 = jnp.uint32

EPS = 1e-6
HEAD_DIM = 128
POOL_WINDOWS = (2, 4, 8, 16)
TOP_K = 4
SWIGLU_ALPHA = 1.702
SWIGLU_LIMIT = 7.0

LANES = 128
EXPERT_ROWS = 256
LOG_UNDERFLOW = 104.0
VMEM_LIMIT = 56 * 1024 * 1024
WEIGHT_DMA_PRIORITY = 1


def _params(sem=None, vmem=None):
    return pltpu.CompilerParams(dimension_semantics=sem, vmem_limit_bytes=vmem)


_HIGH_HALF = 0xFFFF0000


def _pack_rows(v):
    bits = lax.bitcast_convert_type(v.astype(BF16).astype(F32), U32)
    half = v.shape[1] // 2
    return (bits[:, :half] & jnp.uint32(_HIGH_HALF)) | (bits[:, half:] >> 16)


def _unpack_halves(p):
    hi = lax.bitcast_convert_type(p & jnp.uint32(_HIGH_HALF), F32)
    lo = lax.bitcast_convert_type(p << 16, F32)
    return hi, lo


def _unpack_rows(p):
    hi, lo = _unpack_halves(p)
    return jnp.concatenate([hi.astype(BF16), lo.astype(BF16)], axis=1)


def _adaln_kernel(c_ref, w_ref, b_ref, o_ref):
    c = c_ref[...]
    ca = c / (1.0 + jnp.exp(-c))
    o_ref[...] = jnp.dot(ca.astype(BF16), w_ref[...].astype(BF16),
                         preferred_element_type=F32) + b_ref[...]


def _adaln(c, w_ada, b_ada):
    B, D = c.shape
    N = w_ada.shape[1]
    rows = 8
    tn = 1024
    cp = jnp.zeros((rows, D), F32).at[:B].set(c)
    out = pl.pallas_call(
        _adaln_kernel,
        out_shape=jax.ShapeDtypeStruct((rows, N), F32),
        grid=(N // tn,),
        in_specs=[pl.BlockSpec((rows, D), lambda j: (0, 0)),
                  pl.BlockSpec((D, tn), lambda j: (0, j)),
                  pl.BlockSpec((1, tn), lambda j: (0, j))],
        out_specs=pl.BlockSpec((rows, tn), lambda j: (0, j)),
        compiler_params=_params(("arbitrary",), VMEM_LIMIT),
        name="adaln",
    )(cp, w_ada, b_ada.reshape(1, N))
    return out[:B]


def _inproj_kernel(x_ref, nw_ref, sh_ref, sc_ref, w_ref, o_ref, h_ref, *, tm, ch):
    @pl.when(pl.program_id(1) == 0)
    def _():
        mul = nw_ref[...] * (1.0 + sc_ref[0])
        add = sh_ref[0]

        def body(c, _):
            r0 = pl.multiple_of(c * ch, ch)
            x = x_ref[pl.ds(r0, ch), :]
            inv = lax.rsqrt(jnp.mean(x * x, axis=-1, keepdims=True) + EPS)
            h_ref[pl.ds(r0, ch), :] = (x * inv * mul + add).astype(BF16)
            return 0

        lax.fori_loop(0, tm // ch, body, 0)

    o_ref[...] = jnp.dot(h_ref[...], w_ref[...],
                         preferred_element_type=F32).astype(o_ref.dtype)


def _inproj(x2, norm_w, shift, scale, w_bf, S):
    T, D = x2.shape
    N = w_bf.shape[1]
    tm, tn, ch = 1024, 1024, 128
    per_b = S // tm
    return pl.pallas_call(
        functools.partial(_inproj_kernel, tm=tm, ch=ch),
        out_shape=jax.ShapeDtypeStruct((T, N), BF16),
        grid=(T // tm, N // tn),
        in_specs=[pl.BlockSpec((tm, D), lambda i, j: (i, 0)),
                  pl.BlockSpec((1, D), lambda i, j: (0, 0)),
                  pl.BlockSpec((1, 1, D), lambda i, j: (i // per_b, 0, 0)),
                  pl.BlockSpec((1, 1, D), lambda i, j: (i // per_b, 0, 0)),
                  pl.BlockSpec((D, tn), lambda i, j: (0, j))],
        out_specs=pl.BlockSpec((tm, tn), lambda i, j: (i, j)),
        scratch_shapes=[pltpu.VMEM((tm, D), BF16)],
        compiler_params=_params(("arbitrary", "arbitrary"), VMEM_LIMIT),
        name="inproj",
    )(x2, norm_w.reshape(1, D), shift[:, None, :], scale[:, None, :], w_bf)


def _attn_kernel(q_ref, k_ref, v_ref, qw_ref, kw_ref, o_ref, kn_ref, carry_ref, acc_ref,
                 *, S, tq, hg, scale):
    i = pl.program_id(2)
    d = HEAD_DIM

    def head_norm(x, w):
        parts = []
        for h in range(hg):
            xh = x[:, h * d:(h + 1) * d]
            inv = lax.rsqrt(jnp.mean(xh * xh, axis=-1, keepdims=True) + EPS)
            parts.append(xh * inv * w)
        return parts

    @pl.when(i == 0)
    def _():
        def body(c, _):
            r0 = pl.multiple_of(c * tq, tq)
            parts = head_norm(k_ref[0, pl.ds(r0, tq), :].astype(F32), kw_ref[...])
            for h in range(hg):
                kn_ref[pl.ds(r0, tq), h * d:(h + 1) * d] = parts[h].astype(BF16)
            return 0

        lax.fori_loop(0, S // tq, body, 0)

    qb = [(p * scale).astype(BF16) for p in head_norm(q_ref[0].astype(F32), qw_ref[...])]

    row = lax.broadcasted_iota(I32, (tq, tq), 0)
    col = lax.broadcasted_iota(I32, (tq, tq), 1)
    causal = col < row
    tri = (row > col).astype(BF16)

    def block(kb, diag):
        r0 = pl.multiple_of(kb * tq, tq)
        worst = None
        for h in range(hg):
            cols = slice(h * d, (h + 1) * d)
            kblk = kn_ref[pl.ds(r0, tq), cols]
            vblk = v_ref[0, pl.ds(r0, tq), cols]
            z = lax.dot_general(qb[h], kblk, (((1,), (1,)), ((), ())),
                                preferred_element_type=F32)
            t = jnp.log(1.0 + jnp.exp(-jnp.abs(z)))
            lsn = jnp.minimum(-z, 0.0) - t
            lsp = jnp.minimum(z, 0.0) - t
            if diag:
                lsn = jnp.where(causal, lsn, 0.0)
            hi = lsn.astype(BF16)
            lo = (lsn - hi.astype(F32)).astype(BF16)
            later = (jnp.dot(hi, tri, preferred_element_type=F32)
                     + jnp.dot(lo, tri, preferred_element_type=F32))
            if diag:
                a = jnp.where(causal, jnp.exp(lsp + later), 0.0)
                acc_ref[:, cols] = jnp.dot(a.astype(BF16), vblk, preferred_element_type=F32)
                carry = later[:, :1] + lsn[:, :1]
            else:
                a = jnp.exp(lsp + later + carry_ref[h])
                acc_ref[:, cols] += jnp.dot(a.astype(BF16), vblk, preferred_element_type=F32)
                carry = carry_ref[h] + later[:, :1] + lsn[:, :1]
            carry_ref[h] = carry
            m = jnp.max(carry)
            worst = m if worst is None else jnp.maximum(worst, m)
        return worst

    m0 = block(i, True)

    def cond(st):
        kb, m = st
        return jnp.logical_and(kb >= 0, m > -LOG_UNDERFLOW)

    def body(st):
        kb, _ = st
        return kb - 1, block(kb, False)

    lax.while_loop(cond, body, (i - 1, m0))
    o_ref[0] = acc_ref[...].astype(o_ref.dtype)


def _attention(proj3, q_norm_w, k_norm_w, n_heads):
    B, S, _ = proj3.shape
    d = HEAD_DIM
    tq = 256
    hg = 2
    G = n_heads // hg
    w = hg * d
    return pl.pallas_call(
        functools.partial(_attn_kernel, S=S, tq=tq, hg=hg, scale=1.0 / math.sqrt(d)),
        out_shape=jax.ShapeDtypeStruct((B, S, n_heads * d), BF16),
        grid=(B, G, S // tq),
        in_specs=[pl.BlockSpec((1, tq, w), lambda b, g, i: (b, i, g)),
                  pl.BlockSpec((1, S, w), lambda b, g, i: (b, 0, G + g)),
                  pl.BlockSpec((1, S, w), lambda b, g, i: (b, 0, 2 * G + g)),
                  pl.BlockSpec((1, d), lambda b, g, i: (0, 0)),
                  pl.BlockSpec((1, d), lambda b, g, i: (0, 0))],
        out_specs=pl.BlockSpec((1, tq, w), lambda b, g, i: (b, i, g)),
        scratch_shapes=[pltpu.VMEM((S, w), BF16),
                        pltpu.VMEM((hg, tq, 1), F32),
                        pltpu.VMEM((tq, w), F32)],
        compiler_params=_params(("arbitrary", "arbitrary", "arbitrary"), VMEM_LIMIT),
        name="stickbreak_attn",
    )(proj3, proj3, proj3, q_norm_w.reshape(1, d), k_norm_w.reshape(1, d))


def _pool_kernel(u_ref, w_ref, ps_ref, o_ref, *, S, ch, gd):
    halo = 16
    for g, win in enumerate(POOL_WINDOWS):
        lo, hi = g * gd, (g + 1) * gd
        wg = w_ref[g].astype(BF16)
        sc = ps_ref[:, lo:hi]

        def body(c, _, win=win, lo=lo, hi=hi, wg=wg, sc=sc):
            r0 = pl.multiple_of(c * ch, ch)
            cur = u_ref[0, pl.ds(r0, ch), lo:hi].astype(F32)
            p0 = pl.multiple_of(jnp.maximum(r0 - halo, 0), halo)
            prev = u_ref[0, pl.ds(p0, halo), lo:hi].astype(F32)
            prev = jnp.where(c > 0, prev, 0.0)
            s = jnp.concatenate([prev, cur], axis=0)
            n = 1
            while n < win:
                s = s + pltpu.roll(s, n, 0)
                n *= 2
            s = s[halo:]
            t = r0 + lax.broadcasted_iota(I32, (ch, 1), 0)
            cnt = jnp.minimum(t + 1, win).astype(F32)
            p = s / cnt - cur
            y = jnp.dot(p.astype(BF16), wg, preferred_element_type=F32) * sc
            o_ref[0, pl.ds(r0, ch), lo:hi] = y.astype(o_ref.dtype)
            return 0

        lax.fori_loop(0, S // ch, body, 0)


def _pool(proj3, w_pool, pool_scale, pool_width):
    B, S, NP = proj3.shape
    G, gd, _ = w_pool.shape
    return pl.pallas_call(
        functools.partial(_pool_kernel, S=S, ch=256, gd=gd),
        out_shape=jax.ShapeDtypeStruct((B, S, pool_width), BF16),
        grid=(B,),
        in_specs=[pl.BlockSpec((1, S, pool_width), lambda b: (b, 0, NP // pool_width - 1)),
                  pl.BlockSpec((G, gd, gd), lambda b: (0, 0, 0)),
                  pl.BlockSpec((1, pool_width), lambda b: (0, 0))],
        out_specs=pl.BlockSpec((1, S, pool_width), lambda b: (b, 0, 0)),
        compiler_params=_params(("arbitrary",), VMEM_LIMIT),
        name="pool_mixer",
    )(proj3, w_pool, pool_scale.reshape(1, pool_width))


def _outproj_kernel(osb_ref, opool_ref, wo_ref, x_ref, g1_ref, sh_ref, sc_ref, nw_ref,
                    wr_ref, br_ref, x1_ref, h2p_ref, gates_ref, idx_ref, *, sbw, n_exp):
    tm, D = x_ref.shape
    mixed = (jnp.dot(osb_ref[...], wo_ref[:sbw, :], preferred_element_type=F32)
             + jnp.dot(opool_ref[...], wo_ref[sbw:, :], preferred_element_type=F32))
    x1 = x_ref[...] + g1_ref[0] * mixed
    x1_ref[...] = x1
    inv = lax.rsqrt(jnp.mean(x1 * x1, axis=-1, keepdims=True) + EPS)
    h2 = x1 * inv * (nw_ref[...] * (1.0 + sc_ref[0])) + sh_ref[0]
    hb = h2.astype(BF16)
    h2p_ref[...] = _pack_rows(h2)

    lane = lax.broadcasted_iota(I32, (tm, LANES), 1)
    lanef = lane.astype(F32)
    logits = jnp.dot(hb, wr_ref[...].astype(BF16), preferred_element_type=F32) + br_ref[...]
    vals = jnp.where(lane < n_exp, logits, -jnp.inf)
    tops, ids = [], []
    for _ in range(TOP_K):
        m = jnp.max(vals, axis=-1, keepdims=True)
        first = jnp.min(jnp.where(vals == m, lanef, float(LANES)), axis=-1, keepdims=True)
        tops.append(m)
        ids.append(first)
        vals = jnp.where(lanef == first, -jnp.inf, vals)
    es = [jnp.exp(m - tops[0]) for m in tops]
    den = es[0]
    for e in es[1:]:
        den = den + e
    gates = jnp.zeros((tm, LANES), F32)
    idx = jnp.zeros((tm, LANES), F32)
    for k in range(TOP_K):
        gates = jnp.where(lane == k, es[k] / den, gates)
        idx = jnp.where(lane == k, ids[k], idx)
    gates_ref[...] = gates
    idx_ref[...] = idx.astype(I32)


def _outproj(o_sb, o_pool, wo_bf, x2, gate1, shift2, scale2, norm2_w, w_router, b_router, S):
    T, D = x2.shape
    sbw = o_sb.shape[1]
    pw = o_pool.shape[1]
    n_exp = w_router.shape[1]
    tm = 256
    per_b = S // tm
    wr = jnp.zeros((D, LANES), F32).at[:, :n_exp].set(w_router)
    br = jnp.zeros((1, LANES), F32).at[0, :n_exp].set(b_router)
    mod_spec = pl.BlockSpec((1, 1, D), lambda i: (i // per_b, 0, 0))
    return pl.pallas_call(
        functools.partial(_outproj_kernel, sbw=sbw, n_exp=n_exp),
        out_shape=(jax.ShapeDtypeStruct((T, D), F32),
                   jax.ShapeDtypeStruct((T, D // 2), U32),
                   jax.ShapeDtypeStruct((T, LANES), F32),
                   jax.ShapeDtypeStruct((T, LANES), I32)),
        grid=(T // tm,),
        in_specs=[pl.BlockSpec((tm, sbw), lambda i: (i, 0)),
                  pl.BlockSpec((tm, pw), lambda i: (i, 0)),
                  pl.BlockSpec((sbw + pw, D), lambda i: (0, 0)),
                  pl.BlockSpec((tm, D), lambda i: (i, 0)),
                  mod_spec, mod_spec, mod_spec,
                  pl.BlockSpec((1, D), lambda i: (0, 0)),
                  pl.BlockSpec((D, LANES), lambda i: (0, 0)),
                  pl.BlockSpec((1, LANES), lambda i: (0, 0))],
        out_specs=(pl.BlockSpec((tm, D), lambda i: (i, 0)),
                   pl.BlockSpec((tm, D // 2), lambda i: (i, 0)),
                   pl.BlockSpec((tm, LANES), lambda i: (i, 0)),
                   pl.BlockSpec((tm, LANES), lambda i: (i, 0))),
        compiler_params=_params(("arbitrary",), VMEM_LIMIT),
        name="outproj_router",
    )(o_sb, o_pool, wo_bf, x2, gate1[:, None, :], shift2[:, None, :], scale2[:, None, :],
      norm2_w.reshape(1, D), wr, br)


def _route_kernel(idx_ref, dest_ref, meta_ref, rank_ref, *, T, ch):
    lane = lax.broadcasted_iota(I32, (ch, LANES), 1)
    row = lax.broadcasted_iota(I32, (ch, ch), 0)
    col = lax.broadcasted_iota(I32, (ch, ch), 1)
    before = (col < row).astype(BF16)

    def load(c):
        return idx_ref[pl.ds(pl.multiple_of(c * ch, ch), ch), :]

    def count(c, cnt):
        ii = load(c)
        member = lane == ii[:, 0:1]
        for k in range(1, TOP_K):
            member = jnp.logical_or(member, lane == ii[:, k:k + 1])
        mf = jnp.where(member, 1.0, 0.0)
        rank = jnp.dot(before, mf.astype(BF16), preferred_element_type=F32) + cnt
        rank_ref[pl.ds(pl.multiple_of(c * ch, ch), ch), :] = rank
        return cnt + jnp.sum(mf, axis=0, keepdims=True)

    cnt = lax.fori_loop(0, T // ch, count, jnp.zeros((1, LANES), F32))
    padded = jnp.ceil(cnt / EXPERT_ROWS) * EXPERT_ROWS
    rows = 8
    lane8 = lax.broadcasted_iota(I32, (rows, LANES), 1)
    ends = jnp.broadcast_to(padded, (rows, LANES))
    sh = 1
    while sh < LANES:
        ends = ends + jnp.where(lane8 >= sh, pltpu.roll(ends, sh, 1), 0.0)
        sh *= 2
    starts = ends - padded
    sub8 = lax.broadcasted_iota(I32, (rows, LANES), 0)
    meta = jnp.where(sub8 == 0, cnt, jnp.where(sub8 == 1, starts, padded))
    meta_ref[...] = meta.astype(I32)
    start_row = starts[0:1, :]

    def place(c, _):
        ii = load(c)
        val = rank_ref[pl.ds(pl.multiple_of(c * ch, ch), ch), :] + start_row
        out = jnp.zeros((ch, LANES), F32)
        for k in range(TOP_K):
            d = jnp.sum(jnp.where(lane == ii[:, k:k + 1], val, 0.0), axis=-1, keepdims=True)
            out = jnp.where(lane == k, d, out)
        dest_ref[pl.ds(pl.multiple_of(c * ch, ch), ch), :] = out.astype(I32)
        return 0

    lax.fori_loop(0, T // ch, place, 0)


def _route(idx_wide):
    T = idx_wide.shape[0]
    return pl.pallas_call(
        functools.partial(_route_kernel, T=T, ch=256),
        out_shape=(jax.ShapeDtypeStruct((T, LANES), I32),
                   jax.ShapeDtypeStruct((8, LANES), I32)),
        grid=(1,),
        in_specs=[pl.BlockSpec((T, LANES), lambda i: (0, 0))],
        out_specs=(pl.BlockSpec((T, LANES), lambda i: (0, 0)),
                   pl.BlockSpec((8, LANES), lambda i: (0, 0))),
        scratch_shapes=[pltpu.VMEM((T, LANES), F32)],
        compiler_params=_params(("arbitrary",), VMEM_LIMIT),
        name="route_ranks",
    )(idx_wide)


def _dispatch_kernel(dest_ref, cnt_ref, start_ref, h_ref, x_ref, z_ref, sem, zsem,
                     *, tb, n_exp, n_blocks):
    s = pl.program_id(0)

    def issue(t, _):
        for k in range(TOP_K):
            d = dest_ref[TOP_K * (s * tb + t) + k]
            pltpu.make_async_copy(h_ref.at[pl.ds(t, 1), :], x_ref.at[pl.ds(d, 1), :], sem).start()
        return 0

    lax.fori_loop(0, tb, issue, 0, unroll=4)

    @pl.when(s == 0)
    def _():
        z_ref[...] = jnp.zeros_like(z_ref)
        _dispatch_zero_fill(cnt_ref, start_ref, x_ref, z_ref, zsem, n_exp, n_blocks)

    for _ in range(TOP_K):
        pltpu.make_async_copy(h_ref, x_ref.at[pl.ds(0, tb), :], sem).wait()


def _dispatch_zero_fill(cnt_ref, start_ref, x_ref, z_ref, zsem, n_exp, n_blocks):
    def zero_fill(e, wait):
        cnt = cnt_ref[e]
        npad = (-cnt) & (EXPERT_ROWS - 1)
        off = start_ref[e] + cnt

        def one(i, _):
            cp = pltpu.make_async_copy(z_ref.at[pl.ds(0, 1), :],
                                       x_ref.at[pl.ds(off + i, 1), :], zsem)
            if wait:
                cp.wait()
            else:
                cp.start()
            return 0

        lax.fori_loop(0, npad, one, 0)
        return 0

    used = (start_ref[n_exp - 1] + cnt_ref[n_exp - 1] + EXPERT_ROWS - 1) // EXPERT_ROWS

    def tail_fill(blk, wait):
        r0 = pl.multiple_of(blk * EXPERT_ROWS, EXPERT_ROWS)
        cp = pltpu.make_async_copy(z_ref, x_ref.at[pl.ds(r0, EXPERT_ROWS), :], zsem)
        if wait:
            cp.wait()
        else:
            cp.start()
        return 0

    lax.fori_loop(0, n_exp, lambda e, _: zero_fill(e, False), 0)
    lax.fori_loop(used, n_blocks, lambda b, _: tail_fill(b, False), 0)
    lax.fori_loop(0, n_exp, lambda e, _: zero_fill(e, True), 0)
    lax.fori_loop(used, n_blocks, lambda b, _: tail_fill(b, True), 0)


def _dispatch(dest_flat, cnt, starts, h2p, n_rows):
    T, W = h2p.shape
    n_exp = cnt.shape[0]
    tb = 512
    return pl.pallas_call(
        functools.partial(_dispatch_kernel, tb=tb, n_exp=n_exp, n_blocks=n_rows // EXPERT_ROWS),
        out_shape=jax.ShapeDtypeStruct((n_rows, W), U32),
        grid_spec=pltpu.PrefetchScalarGridSpec(
            num_scalar_prefetch=3,
            grid=(T // tb,),
            in_specs=[pl.BlockSpec((tb, W), lambda s, *_: (s, 0))],
            out_specs=pl.BlockSpec(memory_space=pl.ANY),
            scratch_shapes=[pltpu.VMEM((EXPERT_ROWS, W), U32),
                            pltpu.SemaphoreType.DMA, pltpu.SemaphoreType.DMA]),
        compiler_params=_params(("arbitrary",), VMEM_LIMIT),
        name="dispatch_rows",
    )(dest_flat, cnt, starts, h2p)


def _ffn1_kernel(ri_ref, ro_ref, e_ref, j_ref, first_ref, valid_ref, slot_ref, ne_ref, nj_ref, more_ref,
                 x_ref, w_ref, b_ref, o_ref, stage, wbf, sem, *, F, tn):
    q = pl.program_id(0)

    def wcopy(e, j, slot, part):
        c0 = pl.multiple_of(part * F + j * tn, tn)
        return pltpu.make_async_copy(w_ref.at[e, :, pl.ds(c0, tn)], stage.at[slot, part],
                                     sem.at[slot])

    @pl.when(q == 0)
    def _():
        for part in range(2):
            wcopy(e_ref[0], j_ref[0], 0, part).start(priority=WEIGHT_DMA_PRIORITY)

    @pl.when(first_ref[q] == 1)
    def _():
        slot = slot_ref[q]
        for part in range(2):
            wcopy(e_ref[q], j_ref[q], slot, part).wait()

        @pl.when(more_ref[q] == 1)
        def _():
            for part in range(2):
                wcopy(ne_ref[q], nj_ref[q], 1 - slot, part).start(priority=WEIGHT_DMA_PRIORITY)

        for part in range(2):
            wbf[part] = stage[slot, part].astype(BF16)

    @pl.when(valid_ref[q] == 1)
    def _():
        xb = _unpack_rows(x_ref[...])
        g = jnp.dot(xb, wbf[0], preferred_element_type=F32) + b_ref[0, 0, 0]
        lin = jnp.dot(xb, wbf[1], preferred_element_type=F32) + b_ref[0, 1, 0]
        g = jnp.minimum(g, SWIGLU_LIMIT)
        lin = jnp.clip(lin, -SWIGLU_LIMIT, SWIGLU_LIMIT)
        act = g / (1.0 + jnp.exp(-SWIGLU_ALPHA * g)) * (lin + 1.0)
        o_ref[...] = act.astype(o_ref.dtype)

    @pl.when(valid_ref[q] == 0)
    def _():
        o_ref[...] = jnp.zeros_like(o_ref)


def _ffn2_kernel(ri_ref, ro_ref, e_ref, j_ref, first_ref, valid_ref, slot_ref, ne_ref, nj_ref, more_ref,
                 a_ref, w_ref, b_ref, o_ref, stage, wbf, sem):
    q = pl.program_id(0)

    def wcopy(e, slot):
        return pltpu.make_async_copy(w_ref.at[e], stage.at[slot], sem.at[slot])

    @pl.when(q == 0)
    def _():
        wcopy(e_ref[0], 0).start(priority=WEIGHT_DMA_PRIORITY)

    @pl.when(first_ref[q] == 1)
    def _():
        slot = slot_ref[q]
        wcopy(e_ref[q], slot).wait()

        @pl.when(more_ref[q] == 1)
        def _():
            wcopy(ne_ref[q], 1 - slot).start(priority=WEIGHT_DMA_PRIORITY)

        wbf[...] = stage[slot].astype(BF16)

    @pl.when(valid_ref[q] == 1)
    def _():
        y = jnp.dot(a_ref[...], wbf[...], preferred_element_type=F32) + b_ref[0]
        o_ref[...] = _pack_rows(y)

    @pl.when(valid_ref[q] == 0)
    def _():
        o_ref[...] = jnp.zeros_like(o_ref)


def _work_items(cnt, n_col_tiles, n_blocks):
    n_exp = cnt.shape[0]
    nblk = (cnt + EXPERT_ROWS - 1) // EXPERT_ROWS
    bstart = jnp.cumsum(nblk) - nblk
    gsize = jnp.repeat(nblk, n_col_tiles)
    gend = jnp.cumsum(gsize)
    n_groups = n_exp * n_col_tiles
    gid = jnp.arange(n_groups, dtype=I32)
    total = gend[-1]
    q = jnp.arange(n_blocks * n_col_tiles, dtype=I32)
    qc = jnp.minimum(q, total - 1)
    g = jnp.sum((gend[None, :] <= qc[:, None]).astype(I32), axis=1)
    nonempty = gsize > 0
    ordinal = jnp.cumsum(nonempty.astype(I32)) - 1
    nxt_incl = lax.cummin(jnp.where(nonempty, gid, n_groups), reverse=True)
    nxt = jnp.concatenate([nxt_incl[1:], jnp.full((1,), n_groups, I32)])
    more = nxt < n_groups
    nxt = jnp.minimum(nxt, n_groups - 1)
    per_group = jnp.stack([gend - gsize, gid // n_col_tiles, gid % n_col_tiles,
                           jnp.repeat(bstart, n_col_tiles), ordinal % 2,
                           nxt // n_col_tiles, nxt % n_col_tiles, more.astype(I32)])
    pick = (g[None, :, None] == gid[None, None, :]).astype(I32)
    gstart, e, j, brow, slot, ne, nj, more = jnp.sum(pick * per_group[:, None, :], axis=2)
    r = qc - gstart
    valid = q < total
    first = jnp.logical_and(valid, r == 0)
    over = q - total
    row_in = brow + r
    row_out = jnp.where(valid, row_in, jnp.sum(nblk) + over // n_col_tiles)
    col_out = jnp.where(valid, j, over % n_col_tiles)
    as_i32 = lambda a: a.astype(I32)
    return tuple(map(as_i32, (row_in, row_out, e, col_out, first, valid, slot, ne, nj, more)))


def _ffn1(items, x_pad, w_exp_in, b_exp_in):
    P, W = x_pad.shape
    n_exp, D, F2 = w_exp_in.shape
    F = F2 // 2
    tn = 1024
    nj = F // tn
    n_items = items[0].shape[0]
    bias = b_exp_in.reshape(n_exp, 2, nj, 1, tn)
    return pl.pallas_call(
        functools.partial(_ffn1_kernel, F=F, tn=tn),
        out_shape=jax.ShapeDtypeStruct((P, F), BF16),
        grid_spec=pltpu.PrefetchScalarGridSpec(
            num_scalar_prefetch=10,
            grid=(n_items,),
            in_specs=[pl.BlockSpec((EXPERT_ROWS, W), lambda q, ri, *_: (ri[q], 0)),
                      pl.BlockSpec(memory_space=pl.ANY),
                      pl.BlockSpec((1, 2, 1, 1, tn),
                                   lambda q, ri, ro, e, j, *_: (e[q], 0, j[q], 0, 0))],
            out_specs=pl.BlockSpec((EXPERT_ROWS, tn),
                                   lambda q, ri, ro, e, j, *_: (ro[q], j[q])),
            scratch_shapes=[pltpu.VMEM((2, 2, D, tn), F32),
                            pltpu.VMEM((2, D, tn), BF16),
                            pltpu.SemaphoreType.DMA((2,))]),
        compiler_params=_params(("arbitrary",), VMEM_LIMIT),
        name="expert_in_swiglu",
    )(*items, x_pad, w_exp_in, bias)


def _ffn2(items, act, w_exp_out, b_exp_out):
    P, F = act.shape
    n_exp, _, D = w_exp_out.shape
    n_items = items[0].shape[0]
    return pl.pallas_call(
        _ffn2_kernel,
        out_shape=jax.ShapeDtypeStruct((P, D // 2), U32),
        grid_spec=pltpu.PrefetchScalarGridSpec(
            num_scalar_prefetch=10,
            grid=(n_items,),
            in_specs=[pl.BlockSpec((EXPERT_ROWS, F), lambda q, ri, *_: (ri[q], 0)),
                      pl.BlockSpec(memory_space=pl.ANY),
                      pl.BlockSpec((1, 1, D), lambda q, ri, ro, e, *_: (e[q], 0, 0))],
            out_specs=pl.BlockSpec((EXPERT_ROWS, D // 2), lambda q, ri, ro, *_: (ro[q], 0)),
            scratch_shapes=[pltpu.VMEM((2, F, D), F32),
                            pltpu.VMEM((F, D), BF16),
                            pltpu.SemaphoreType.DMA((2,))]),
        compiler_params=_params(("arbitrary",), VMEM_LIMIT),
        name="expert_out",
    )(*items, act, w_exp_out, b_exp_out[:, None, :])


def _combine_kernel(dest_ref, y_ref, gates_ref, x1_ref, g2_ref, o_ref, buf, sem, *, tm):
    s = pl.program_id(0)
    ns = pl.num_programs(0)

    def gather(step, slot):
        def issue(t, _):
            for k in range(TOP_K):
                d = dest_ref[TOP_K * (step * tm + t) + k]
                pltpu.make_async_copy(y_ref.at[pl.ds(d, 1), :],
                                      buf.at[slot, k, pl.ds(t, 1), :], sem.at[slot]).start()
            return 0

        lax.fori_loop(0, tm, issue, 0, unroll=4)

    @pl.when(s == 0)
    def _():
        gather(0, 0)

    @pl.when(s + 1 < ns)
    def _():
        gather(s + 1, (s + 1) % 2)

    slot = s % 2
    for k in range(TOP_K):
        pltpu.make_async_copy(y_ref.at[pl.ds(0, tm), :], buf.at[slot, k], sem.at[slot]).wait()
    gates = gates_ref[...]
    y_hi = y_lo = None
    for k in range(TOP_K):
        hi, lo = _unpack_halves(buf[slot, k])
        g = gates[:, k:k + 1]
        y_hi = g * hi if y_hi is None else y_hi + g * hi
        y_lo = g * lo if y_lo is None else y_lo + g * lo
    y = jnp.concatenate([y_hi, y_lo], axis=1)
    o_ref[...] = x1_ref[...] + g2_ref[0] * y


def _combine(dest_flat, y_pad, gates_wide, x1, gate2, S):
    T, D = x1.shape
    tm = 128
    per_b = S // tm
    return pl.pallas_call(
        functools.partial(_combine_kernel, tm=tm),
        out_shape=jax.ShapeDtypeStruct((T, D), F32),
        grid_spec=pltpu.PrefetchScalarGridSpec(
            num_scalar_prefetch=1,
            grid=(T // tm,),
            in_specs=[pl.BlockSpec(memory_space=pl.ANY),
                      pl.BlockSpec((tm, LANES), lambda i, d: (i, 0)),
                      pl.BlockSpec((tm, D), lambda i, d: (i, 0)),
                      pl.BlockSpec((1, 1, D), lambda i, d: (i // per_b, 0, 0))],
            out_specs=pl.BlockSpec((tm, D), lambda i, d: (i, 0)),
            scratch_shapes=[pltpu.VMEM((2, TOP_K, tm, D // 2), U32),
                            pltpu.SemaphoreType.DMA((2,))]),
        compiler_params=_params(("arbitrary",), VMEM_LIMIT),
        name="combine_rows",
    )(dest_flat, y_pad, gates_wide, x1, gate2[:, None, :])


def kernel(x, c, norm1_w, norm2_w, w_ada, b_ada, w_in, q_norm_w, k_norm_w, w_pool, pool_scale,
           w_o, w_router, b_router, w_exp_in, b_exp_in, w_exp_out, b_exp_out):
    B, S, D = x.shape
    T = B * S
    depth = w_ada.shape[0]
    n_exp = w_router.shape[-1]
    pool_width = pool_scale.shape[-1]
    sb_width = w_o.shape[1] - pool_width
    n_heads = sb_width // HEAD_DIM
    n_blocks = (T * TOP_K + n_exp * (EXPERT_ROWS - 1)) // EXPERT_ROWS
    n_rows = n_blocks * EXPERT_ROWS

    x2 = x.reshape(T, D)
    for l in range(depth):
        mod = _adaln(c, w_ada[l], b_ada[l])
        shift1, scale1, gate1, shift2, scale2, gate2 = jnp.split(mod, 6, axis=-1)

        proj = _inproj(x2, norm1_w[l], shift1, scale1, w_in[l].astype(BF16), S)
        proj3 = proj.reshape(B, S, -1)
        o_sb = _attention(proj3, q_norm_w[l], k_norm_w[l], n_heads)
        o_pool = _pool(proj3, w_pool[l], pool_scale[l], pool_width)
        x1, h2p, gates_wide, idx_wide = _outproj(
            o_sb.reshape(T, sb_width), o_pool.reshape(T, pool_width), w_o[l].astype(BF16),
            x2, gate1, shift2, scale2, norm2_w[l], w_router[l], b_router[l], S)

        dest_wide, meta = _route(idx_wide)
        cnt = meta[0, :n_exp]
        starts = meta[1, :n_exp]
        dest_flat = dest_wide[:, :TOP_K].reshape(T * TOP_K)
        x_pad = _dispatch(dest_flat, cnt, starts, h2p, n_rows)

        F = w_exp_out.shape[2]
        act = _ffn1(_work_items(cnt, F // 1024, n_blocks), x_pad, w_exp_in[l], b_exp_in[l])
        y_pad = _ffn2(_work_items(cnt, 1, n_blocks), act, w_exp_out[l], b_exp_out[l])
        x2 = _combine(dest_flat, y_pad, gates_wide, x1, gate2, S)
    return x2.reshape(B, S, D)
```

```python
import functools
import math

import jax
import jax.numpy as jnp
from jax import lax
from jax.experimental import pallas as pl
from jax.experimental.pallas import tpu as pltpu

F32 = jnp.float32
BF16 = jnp.bfloat16
I32 = jnp.int32
U32 = jnp.uint32

EPS = 1e-6
HEAD_DIM = 128
POOL_WINDOWS = (2, 4, 8, 16)
TOP_K = 4
SWIGLU_ALPHA = 1.702
SWIGLU_LIMIT = 7.0

LANES = 128
EXPERT_ROWS = 256
LOG_UNDERFLOW = 104.0
VMEM_LIMIT = 56 * 1024 * 1024
WEIGHT_DMA_PRIORITY = 1


def _params(sem=None, vmem=None):
    return pltpu.CompilerParams(dimension_semantics=sem, vmem_limit_bytes=vmem)


_HIGH_HALF = 0xFFFF0000


def _pack_rows(v):
    bits = lax.bitcast_convert_type(v.astype(BF16).astype(F32), U32)
    half = v.shape[1] // 2
    return (bits[:, :half] & jnp.uint32(_HIGH_HALF)) | (bits[:, half:] >> 16)


def _unpack_halves(p):
    hi = lax.bitcast_convert_type(p & jnp.uint32(_HIGH_HALF), F32)
    lo = lax.bitcast_convert_type(p << 16, F32)
    return hi, lo


def _unpack_rows(p):
    hi, lo = _unpack_halves(p)
    return jnp.concatenate([hi.astype(BF16), lo.astype(BF16)], axis=1)


def _adaln_kernel(c_ref, w_ref, b_ref, o_ref):
    c = c_ref[...]
    ca = c / (1.0 + jnp.exp(-c))
    o_ref[...] = jnp.dot(ca.astype(BF16), w_ref[...].astype(BF16),
                         preferred_element_type=F32) + b_ref[...]


def _adaln(c, w_ada, b_ada):
    B, D = c.shape
    N = w_ada.shape[1]
    rows = 8
    tn = 1024
    cp = jnp.zeros((rows, D), F32).at[:B].set(c)
    out = pl.pallas_call(
        _adaln_kernel,
        out_shape=jax.ShapeDtypeStruct((rows, N), F32),
        grid=(N // tn,),
        in_specs=[pl.BlockSpec((rows, D), lambda j: (0, 0)),
                  pl.BlockSpec((D, tn), lambda j: (0, j)),
                  pl.BlockSpec((1, tn), lambda j: (0, j))],
        out_specs=pl.BlockSpec((rows, tn), lambda j: (0, j)),
        compiler_params=_params(("arbitrary",), VMEM_LIMIT),
        name="adaln",
    )(cp, w_ada, b_ada.reshape(1, N))
    return out[:B]


def _inproj_kernel(x_ref, nw_ref, sh_ref, sc_ref, w_ref, o_ref, h_ref, *, tm, ch):
    @pl.when(pl.program_id(1) == 0)
    def _():
        mul = nw_ref[...] * (1.0 + sc_ref[0])
        add = sh_ref[0]

        def body(c, _):
            r0 = pl.multiple_of(c * ch, ch)
            x = x_ref[pl.ds(r0, ch), :]
            inv = lax.rsqrt(jnp.mean(x * x, axis=-1, keepdims=True) + EPS)
            h_ref[pl.ds(r0, ch), :] = (x * inv * mul + add).astype(BF16)
            return 0

        lax.fori_loop(0, tm // ch, body, 0)

    o_ref[...] = jnp.dot(h_ref[...], w_ref[...],
                         preferred_element_type=F32).astype(o_ref.dtype)


def _inproj(x2, norm_w, shift, scale, w_bf, S):
    T, D = x2.shape
    N = w_bf.shape[1]
    tm, tn, ch = 1024, 1024, 128
    per_b = S // tm
    return pl.pallas_call(
        functools.partial(_inproj_kernel, tm=tm, ch=ch),
        out_shape=jax.ShapeDtypeStruct((T, N), BF16),
        grid=(T // tm, N // tn),
        in_specs=[pl.BlockSpec((tm, D), lambda i, j: (i, 0)),
                  pl.BlockSpec((1, D), lambda i, j: (0, 0)),
                  pl.BlockSpec((1, 1, D), lambda i, j: (i // per_b, 0, 0)),
                  pl.BlockSpec((1, 1, D), lambda i, j: (i // per_b, 0, 0)),
                  pl.BlockSpec((D, tn), lambda i, j: (0, j))],
        out_specs=pl.BlockSpec((tm, tn), lambda i, j: (i, j)),
        scratch_shapes=[pltpu.VMEM((tm, D), BF16)],
        compiler_params=_params(("arbitrary", "arbitrary"), VMEM_LIMIT),
        name="inproj",
    )(x2, norm_w.reshape(1, D), shift[:, None, :], scale[:, None, :], w_bf)


def _attn_kernel(q_ref, k_ref, v_ref, qw_ref, kw_ref, o_ref, kn_ref, carry_ref, acc_ref,
                 *, S, tq, hg, scale):
    i = pl.program_id(2)
    d = HEAD_DIM

    def head_norm(x, w):
        parts = []
        for h in range(hg):
            xh = x[:, h * d:(h + 1) * d]
            inv = lax.rsqrt(jnp.mean(xh * xh, axis=-1, keepdims=True) + EPS)
            parts.append(xh * inv * w)
        return parts

    @pl.when(i == 0)
    def _():
        def body(c, _):
            r0 = pl.multiple_of(c * tq, tq)
            parts = head_norm(k_ref[0, pl.ds(r0, tq), :].astype(F32), kw_ref[...])
            for h in range(hg):
                kn_ref[pl.ds(r0, tq), h * d:(h + 1) * d] = parts[h].astype(BF16)
            return 0

        lax.fori_loop(0, S // tq, body, 0)

    qb = [(p * scale).astype(BF16) for p in head_norm(q_ref[0].astype(F32), qw_ref[...])]

    row = lax.broadcasted_iota(I32, (tq, tq), 0)
    col = lax.broadcasted_iota(I32, (tq, tq), 1)
    causal = col < row
    tri = (row > col).astype(BF16)

    def scores(h, r0, mask):
        kblk = kn_ref[pl.ds(r0, tq), h * d:(h + 1) * d]
        z = lax.dot_general(qb[h], kblk, (((1,), (1,)), ((), ())), preferred_element_type=F32)
        t = jnp.log(1.0 + jnp.exp(-jnp.abs(z)))
        lsn = jnp.minimum(-z, 0.0) - t
        lsp = jnp.minimum(z, 0.0) - t
        if mask:
            lsn = jnp.where(causal, lsn, 0.0)
        hi = lsn.astype(BF16)
        lo = (lsn - hi.astype(F32)).astype(BF16)
        later = (jnp.dot(hi, tri, preferred_element_type=F32)
                 + jnp.dot(lo, tri, preferred_element_type=F32))
        return lsp + later, later[:, :1] + lsn[:, :1]

    def weighted(a, h, r0):
        vblk = v_ref[0, pl.ds(r0, tq), h * d:(h + 1) * d]
        return jnp.dot(a.astype(BF16), vblk, preferred_element_type=F32)

    has_prev = i > 0
    rd = pl.multiple_of(i * tq, tq)
    rp = pl.multiple_of(jnp.maximum(i - 1, 0) * tq, tq)
    worst = None
    for h in range(hg):
        cols = slice(h * d, (h + 1) * d)
        log_d, sum_d = scores(h, rd, True)
        log_p, sum_p = scores(h, rp, False)
        a_d = jnp.where(causal, jnp.exp(log_d), 0.0)
        a_p = jnp.where(has_prev, jnp.exp(log_p + sum_d), 0.0)
        acc_ref[:, cols] = weighted(a_d, h, rd) + weighted(a_p, h, rp)
        carry = jnp.where(has_prev, sum_d + sum_p, sum_d)
        carry_ref[h] = carry
        m = jnp.max(carry)
        worst = m if worst is None else jnp.maximum(worst, m)

    def earlier(kb):
        r0 = pl.multiple_of(kb * tq, tq)
        worst = None
        for h in range(hg):
            cols = slice(h * d, (h + 1) * d)
            log_a, row_sum = scores(h, r0, False)
            acc_ref[:, cols] += weighted(jnp.exp(log_a + carry_ref[h]), h, r0)
            carry = carry_ref[h] + row_sum
            carry_ref[h] = carry
            m = jnp.max(carry)
            worst = m if worst is None else jnp.maximum(worst, m)
        return worst

    def cond(st):
        kb, m = st
        return jnp.logical_and(kb >= 0, m > -LOG_UNDERFLOW)

    def body(st):
        kb, _ = st
        return kb - 1, earlier(kb)

    lax.while_loop(cond, body, (i - 2, worst))
    o_ref[0] = acc_ref[...].astype(o_ref.dtype)


def _attention(proj3, q_norm_w, k_norm_w, n_heads):
    B, S, _ = proj3.shape
    d = HEAD_DIM
    tq = 256
    hg = 4
    G = n_heads // hg
    w = hg * d
    return pl.pallas_call(
        functools.partial(_attn_kernel, S=S, tq=tq, hg=hg, scale=1.0 / math.sqrt(d)),
        out_shape=jax.ShapeDtypeStruct((B, S, n_heads * d), BF16),
        grid=(B, G, S // tq),
        in_specs=[pl.BlockSpec((1, tq, w), lambda b, g, i: (b, i, g)),
                  pl.BlockSpec((1, S, w), lambda b, g, i: (b, 0, G + g)),
                  pl.BlockSpec((1, S, w), lambda b, g, i: (b, 0, 2 * G + g)),
                  pl.BlockSpec((1, d), lambda b, g, i: (0, 0)),
                  pl.BlockSpec((1, d), lambda b, g, i: (0, 0))],
        out_specs=pl.BlockSpec((1, tq, w), lambda b, g, i: (b, i, g)),
        scratch_shapes=[pltpu.VMEM((S, w), BF16),
                        pltpu.VMEM((hg, tq, 1), F32),
                        pltpu.VMEM((tq, w), F32)],
        compiler_params=_params(("arbitrary", "arbitrary", "arbitrary"), VMEM_LIMIT),
        name="stickbreak_attn",
    )(proj3, proj3, proj3, q_norm_w.reshape(1, d), k_norm_w.reshape(1, d))


def _pool_kernel(u_ref, w_ref, ps_ref, o_ref, *, S, ch, gd):
    halo = 16
    for g, win in enumerate(POOL_WINDOWS):
        lo, hi = g * gd, (g + 1) * gd
        wg = w_ref[g].astype(BF16)
        sc = ps_ref[:, lo:hi]

        def body(c, _, win=win, lo=lo, hi=hi, wg=wg, sc=sc):
            r0 = pl.multiple_of(c * ch, ch)
            cur = u_ref[0, pl.ds(r0, ch), lo:hi].astype(F32)
            p0 = pl.multiple_of(jnp.maximum(r0 - halo, 0), halo)
            prev = u_ref[0, pl.ds(p0, halo), lo:hi].astype(F32)
            prev = jnp.where(c > 0, prev, 0.0)
            s = jnp.concatenate([prev, cur], axis=0)
            n = 1
            while n < win:
                s = s + pltpu.roll(s, n, 0)
                n *= 2
            s = s[halo:]
            t = r0 + lax.broadcasted_iota(I32, (ch, 1), 0)
            cnt = jnp.minimum(t + 1, win).astype(F32)
            p = s / cnt - cur
            y = jnp.dot(p.astype(BF16), wg, preferred_element_type=F32) * sc
            o_ref[0, pl.ds(r0, ch), lo:hi] = y.astype(o_ref.dtype)
            return 0

        lax.fori_loop(0, S // ch, body, 0)


def _pool(proj3, w_pool, pool_scale, pool_width):
    B, S, NP = proj3.shape
    G, gd, _ = w_pool.shape
    return pl.pallas_call(
        functools.partial(_pool_kernel, S=S, ch=256, gd=gd),
        out_shape=jax.ShapeDtypeStruct((B, S, pool_width), BF16),
        grid=(B,),
        in_specs=[pl.BlockSpec((1, S, pool_width), lambda b: (b, 0, NP // pool_width - 1)),
                  pl.BlockSpec((G, gd, gd), lambda b: (0, 0, 0)),
                  pl.BlockSpec((1, pool_width), lambda b: (0, 0))],
        out_specs=pl.BlockSpec((1, S, pool_width), lambda b: (b, 0, 0)),
        compiler_params=_params(("arbitrary",), VMEM_LIMIT),
        name="pool_mixer",
    )(proj3, w_pool, pool_scale.reshape(1, pool_width))


def _outproj_kernel(osb_ref, opool_ref, wo_ref, x_ref, g1_ref, sh_ref, sc_ref, nw_ref,
                    wr_ref, br_ref, x1_ref, h2p_ref, gates_ref, idx_ref, *, sbw, n_exp):
    tm, D = x_ref.shape
    mixed = (jnp.dot(osb_ref[...], wo_ref[:sbw, :], preferred_element_type=F32)
             + jnp.dot(opool_ref[...], wo_ref[sbw:, :], preferred_element_type=F32))
    x1 = x_ref[...] + g1_ref[0] * mixed
    x1_ref[...] = x1
    inv = lax.rsqrt(jnp.mean(x1 * x1, axis=-1, keepdims=True) + EPS)
    h2 = x1 * inv * (nw_ref[...] * (1.0 + sc_ref[0])) + sh_ref[0]
    hb = h2.astype(BF16)
    h2p_ref[...] = _pack_rows(h2)

    lane = lax.broadcasted_iota(I32, (tm, LANES), 1)
    lanef = lane.astype(F32)
    logits = jnp.dot(hb, wr_ref[...].astype(BF16), preferred_element_type=F32) + br_ref[...]
    vals = jnp.where(lane < n_exp, logits, -jnp.inf)
    tops, ids = [], []
    for _ in range(TOP_K):
        m = jnp.max(vals, axis=-1, keepdims=True)
        first = jnp.min(jnp.where(vals == m, lanef, float(LANES)), axis=-1, keepdims=True)
        tops.append(m)
        ids.append(first)
        vals = jnp.where(lanef == first, -jnp.inf, vals)
    es = [jnp.exp(m - tops[0]) for m in tops]
    den = es[0]
    for e in es[1:]:
        den = den + e
    gates = jnp.zeros((tm, LANES), F32)
    idx = jnp.zeros((tm, LANES), F32)
    for k in range(TOP_K):
        gates = jnp.where(lane == k, es[k] / den, gates)
        idx = jnp.where(lane == k, ids[k], idx)
    gates_ref[...] = gates
    idx_ref[...] = idx.astype(I32)


def _outproj(o_sb, o_pool, wo_bf, x2, gate1, shift2, scale2, norm2_w, w_router, b_router, S):
    T, D = x2.shape
    sbw = o_sb.shape[1]
    pw = o_pool.shape[1]
    n_exp = w_router.shape[1]
    tm = 256
    per_b = S // tm
    wr = jnp.zeros((D, LANES), F32).at[:, :n_exp].set(w_router)
    br = jnp.zeros((1, LANES), F32).at[0, :n_exp].set(b_router)
    mod_spec = pl.BlockSpec((1, 1, D), lambda i: (i // per_b, 0, 0))
    return pl.pallas_call(
        functools.partial(_outproj_kernel, sbw=sbw, n_exp=n_exp),
        out_shape=(jax.ShapeDtypeStruct((T, D), F32),
                   jax.ShapeDtypeStruct((T, D // 2), U32),
                   jax.ShapeDtypeStruct((T, LANES), F32),
                   jax.ShapeDtypeStruct((T, LANES), I32)),
        grid=(T // tm,),
        in_specs=[pl.BlockSpec((tm, sbw), lambda i: (i, 0)),
                  pl.BlockSpec((tm, pw), lambda i: (i, 0)),
                  pl.BlockSpec((sbw + pw, D), lambda i: (0, 0)),
                  pl.BlockSpec((tm, D), lambda i: (i, 0)),
                  mod_spec, mod_spec, mod_spec,
                  pl.BlockSpec((1, D), lambda i: (0, 0)),
                  pl.BlockSpec((D, LANES), lambda i: (0, 0)),
                  pl.BlockSpec((1, LANES), lambda i: (0, 0))],
        out_specs=(pl.BlockSpec((tm, D), lambda i: (i, 0)),
                   pl.BlockSpec((tm, D // 2), lambda i: (i, 0)),
                   pl.BlockSpec((tm, LANES), lambda i: (i, 0)),
                   pl.BlockSpec((tm, LANES), lambda i: (i, 0))),
        compiler_params=_params(("arbitrary",), VMEM_LIMIT),
        name="outproj_router",
    )(o_sb, o_pool, wo_bf, x2, gate1[:, None, :], shift2[:, None, :], scale2[:, None, :],
      norm2_w.reshape(1, D), wr, br)


def _route_kernel(idx_ref, dest_ref, meta_ref, rank_ref, *, T, ch):
    lane = lax.broadcasted_iota(I32, (ch, LANES), 1)
    row = lax.broadcasted_iota(I32, (ch, ch), 0)
    col = lax.broadcasted_iota(I32, (ch, ch), 1)
    before = (col < row).astype(BF16)

    def load(c):
        return idx_ref[pl.ds(pl.multiple_of(c * ch, ch), ch), :]

    def count(c, cnt):
        ii = load(c)
        member = lane == ii[:, 0:1]
        for k in range(1, TOP_K):
            member = jnp.logical_or(member, lane == ii[:, k:k + 1])
        mf = jnp.where(member, 1.0, 0.0)
        rank = jnp.dot(before, mf.astype(BF16), preferred_element_type=F32) + cnt
        rank_ref[pl.ds(pl.multiple_of(c * ch, ch), ch), :] = rank
        return cnt + jnp.sum(mf, axis=0, keepdims=True)

    cnt = lax.fori_loop(0, T // ch, count, jnp.zeros((1, LANES), F32))
    padded = jnp.ceil(cnt / EXPERT_ROWS) * EXPERT_ROWS
    rows = 8
    lane8 = lax.broadcasted_iota(I32, (rows, LANES), 1)
    ends = jnp.broadcast_to(padded, (rows, LANES))
    sh = 1
    while sh < LANES:
        ends = ends + jnp.where(lane8 >= sh, pltpu.roll(ends, sh, 1), 0.0)
        sh *= 2
    starts = ends - padded
    sub8 = lax.broadcasted_iota(I32, (rows, LANES), 0)
    meta = jnp.where(sub8 == 0, cnt, jnp.where(sub8 == 1, starts, padded))
    meta_ref[...] = meta.astype(I32)
    start_row = starts[0:1, :]

    def place(c, _):
        ii = load(c)
        val = rank_ref[pl.ds(pl.multiple_of(c * ch, ch), ch), :] + start_row
        out = jnp.zeros((ch, LANES), F32)
        for k in range(TOP_K):
            d = jnp.sum(jnp.where(lane == ii[:, k:k + 1], val, 0.0), axis=-1, keepdims=True)
            out = jnp.where(lane == k, d, out)
        dest_ref[pl.ds(pl.multiple_of(c * ch, ch), ch), :] = out.astype(I32)
        return 0

    lax.fori_loop(0, T // ch, place, 0)


def _route(idx_wide):
    T = idx_wide.shape[0]
    return pl.pallas_call(
        functools.partial(_route_kernel, T=T, ch=256),
        out_shape=(jax.ShapeDtypeStruct((T, LANES), I32),
                   jax.ShapeDtypeStruct((8, LANES), I32)),
        grid=(1,),
        in_specs=[pl.BlockSpec((T, LANES), lambda i: (0, 0))],
        out_specs=(pl.BlockSpec((T, LANES), lambda i: (0, 0)),
                   pl.BlockSpec((8, LANES), lambda i: (0, 0))),
        scratch_shapes=[pltpu.VMEM((T, LANES), F32)],
        compiler_params=_params(("arbitrary",), VMEM_LIMIT),
        name="route_ranks",
    )(idx_wide)


def _row_token_kernel(dest_ref, cnt_ref, start_ref, tok_ref, *, n_pairs, n_exp, n_rows):
    def pad_group(e, _):
        cnt = cnt_ref[e]
        off = start_ref[e] + cnt

        def one(i, _):
            tok_ref[off + i] = 0
            return 0

        lax.fori_loop(0, (-cnt) & (EXPERT_ROWS - 1), one, 0)
        return 0

    lax.fori_loop(0, n_exp, pad_group, 0)
    used = start_ref[n_exp - 1] + cnt_ref[n_exp - 1]
    used = (used + EXPERT_ROWS - 1) // EXPERT_ROWS * EXPERT_ROWS

    def tail(r, _):
        tok_ref[r] = 0
        return 0

    lax.fori_loop(used, n_rows, tail, 0)

    def place(t, _):
        for k in range(TOP_K):
            tok_ref[dest_ref[TOP_K * t + k]] = t
        return 0

    lax.fori_loop(0, n_pairs // TOP_K, place, 0, unroll=4)


def _row_tokens(dest_flat, cnt, starts, n_rows):
    n_exp = cnt.shape[0]
    return pl.pallas_call(
        functools.partial(_row_token_kernel, n_pairs=dest_flat.shape[0], n_exp=n_exp,
                          n_rows=n_rows),
        out_shape=jax.ShapeDtypeStruct((n_rows,), I32),
        grid_spec=pltpu.PrefetchScalarGridSpec(
            num_scalar_prefetch=3,
            grid=(1,),
            in_specs=[],
            out_specs=pl.BlockSpec(memory_space=pltpu.SMEM)),
        compiler_params=_params(("arbitrary",), VMEM_LIMIT),
        name="row_tokens",
    )(dest_flat, cnt, starts)


def _ffn1_kernel(tok_ref, ri_ref, ro_ref, e_ref, j_ref, first_ref, valid_ref, slot_ref, ne_ref,
                 nj_ref, more_ref, h_ref, w_ref, b_ref, o_ref, x_even, x_odd, stage, wbf, sem, xsem,
                 *, F, tn):
    q = pl.program_id(0)
    last = pl.num_programs(0) - 1
    xbufs = (x_even, x_odd)

    def gather_rows(item, par, unrolled):
        base = ri_ref[item] * EXPERT_ROWS

        def one(t):
            pltpu.make_async_copy(h_ref.at[pl.ds(tok_ref[base + t], 1), :],
                                  xbufs[par].at[pl.ds(t, 1), :], xsem.at[par]).start()

        if unrolled:
            for t in range(EXPERT_ROWS):
                one(t)
        else:
            lax.fori_loop(0, EXPERT_ROWS, lambda t, _: (one(t), 0)[1], 0)

    def gather_wait(par):
        pltpu.make_async_copy(h_ref.at[pl.ds(0, EXPERT_ROWS), :], xbufs[par],
                              xsem.at[par]).wait()

    def wcopy(e, j, slot, part):
        c0 = pl.multiple_of(part * F + j * tn, tn)
        return pltpu.make_async_copy(w_ref.at[e, :, pl.ds(c0, tn)], stage.at[slot, part],
                                     sem.at[slot])

    @pl.when(q == 0)
    def _():
        for part in range(2):
            wcopy(e_ref[0], j_ref[0], 0, part).start(priority=WEIGHT_DMA_PRIORITY)
        gather_rows(0, 0, False)

    @pl.when(first_ref[q] == 1)
    def _():
        slot = slot_ref[q]
        for part in range(2):
            wcopy(e_ref[q], j_ref[q], slot, part).wait()

        @pl.when(more_ref[q] == 1)
        def _():
            for part in range(2):
                wcopy(ne_ref[q], nj_ref[q], 1 - slot, part).start(priority=WEIGHT_DMA_PRIORITY)

        for part in range(2):
            wbf[part] = stage[slot, part].astype(BF16)

    nxt = jnp.minimum(q + 1, last)
    for par in range(2):
        mine = (q % 2) == par

        @pl.when(jnp.logical_and(mine, valid_ref[q] == 1))
        def _(par=par):
            gather_wait(par)
            gather_rows(nxt, 1 - par, True)
            xb = _unpack_rows(xbufs[par][...])
            g = jnp.dot(xb, wbf[0], preferred_element_type=F32) + b_ref[0, 0, 0]
            lin = jnp.dot(xb, wbf[1], preferred_element_type=F32) + b_ref[0, 1, 0]
            g = jnp.minimum(g, SWIGLU_LIMIT)
            lin = jnp.clip(lin, -SWIGLU_LIMIT, SWIGLU_LIMIT)
            act = g / (1.0 + jnp.exp(-SWIGLU_ALPHA * g)) * (lin + 1.0)
            o_ref[...] = act.astype(o_ref.dtype)

        @pl.when(jnp.logical_and(mine, valid_ref[q] == 0))
        def _(par=par):
            gather_wait(par)
            gather_rows(nxt, 1 - par, False)
            o_ref[...] = jnp.zeros_like(o_ref)

        @pl.when(jnp.logical_and(mine, q == last))
        def _(par=par):
            gather_wait(1 - par)


def _ffn2_kernel(ri_ref, ro_ref, e_ref, j_ref, first_ref, valid_ref, slot_ref, ne_ref, nj_ref, more_ref,
                 a_ref, w_ref, b_ref, o_ref, stage, wbf, sem):
    q = pl.program_id(0)

    def wcopy(e, slot):
        return pltpu.make_async_copy(w_ref.at[e], stage.at[slot], sem.at[slot])

    @pl.when(q == 0)
    def _():
        wcopy(e_ref[0], 0).start(priority=WEIGHT_DMA_PRIORITY)

    @pl.when(first_ref[q] == 1)
    def _():
        slot = slot_ref[q]
        wcopy(e_ref[q], slot).wait()

        @pl.when(more_ref[q] == 1)
        def _():
            wcopy(ne_ref[q], 1 - slot).start(priority=WEIGHT_DMA_PRIORITY)

        wbf[...] = stage[slot].astype(BF16)

    @pl.when(valid_ref[q] == 1)
    def _():
        y = jnp.dot(a_ref[...], wbf[...], preferred_element_type=F32) + b_ref[0]
        o_ref[...] = _pack_rows(y)

    @pl.when(valid_ref[q] == 0)
    def _():
        o_ref[...] = jnp.zeros_like(o_ref)


def _work_items(cnt, n_col_tiles, n_blocks):
    n_exp = cnt.shape[0]
    nblk = (cnt + EXPERT_ROWS - 1) // EXPERT_ROWS
    bstart = jnp.cumsum(nblk) - nblk
    gsize = jnp.repeat(nblk, n_col_tiles)
    gend = jnp.cumsum(gsize)
    n_groups = n_exp * n_col_tiles
    gid = jnp.arange(n_groups, dtype=I32)
    total = gend[-1]
    q = jnp.arange(n_blocks * n_col_tiles, dtype=I32)
    qc = jnp.minimum(q, total - 1)
    g = jnp.sum((gend[None, :] <= qc[:, None]).astype(I32), axis=1)
    nonempty = gsize > 0
    ordinal = jnp.cumsum(nonempty.astype(I32)) - 1
    nxt_incl = lax.cummin(jnp.where(nonempty, gid, n_groups), reverse=True)
    nxt = jnp.concatenate([nxt_incl[1:], jnp.full((1,), n_groups, I32)])
    more = nxt < n_groups
    nxt = jnp.minimum(nxt, n_groups - 1)
    per_group = jnp.stack([gend - gsize, gid // n_col_tiles, gid % n_col_tiles,
                           jnp.repeat(bstart, n_col_tiles), ordinal % 2,
                           nxt // n_col_tiles, nxt % n_col_tiles, more.astype(I32)])
    pick = (g[None, :, None] == gid[None, None, :]).astype(I32)
    gstart, e, j, brow, slot, ne, nj, more = jnp.sum(pick * per_group[:, None, :], axis=2)
    r = qc - gstart
    valid = q < total
    first = jnp.logical_and(valid, r == 0)
    over = q - total
    row_in = brow + r
    row_out = jnp.where(valid, row_in, jnp.sum(nblk) + over // n_col_tiles)
    col_out = jnp.where(valid, j, over % n_col_tiles)
    as_i32 = lambda a: a.astype(I32)
    return tuple(map(as_i32, (row_in, row_out, e, col_out, first, valid, slot, ne, nj, more)))


def _ffn1(items, row_token, h2p, w_exp_in, b_exp_in):
    P = row_token.shape[0]
    W = h2p.shape[1]
    n_exp, D, F2 = w_exp_in.shape
    F = F2 // 2
    tn = 1024
    nj = F // tn
    n_items = items[0].shape[0]
    bias = b_exp_in.reshape(n_exp, 2, nj, 1, tn)
    return pl.pallas_call(
        functools.partial(_ffn1_kernel, F=F, tn=tn),
        out_shape=jax.ShapeDtypeStruct((P, F), BF16),
        grid_spec=pltpu.PrefetchScalarGridSpec(
            num_scalar_prefetch=11,
            grid=(n_items,),
            in_specs=[pl.BlockSpec(memory_space=pl.ANY),
                      pl.BlockSpec(memory_space=pl.ANY),
                      pl.BlockSpec((1, 2, 1, 1, tn),
                                   lambda q, tok, ri, ro, e, j, *_: (e[q], 0, j[q], 0, 0))],
            out_specs=pl.BlockSpec((EXPERT_ROWS, tn),
                                   lambda q, tok, ri, ro, e, j, *_: (ro[q], j[q])),
            scratch_shapes=[pltpu.VMEM((EXPERT_ROWS, W), U32),
                            pltpu.VMEM((EXPERT_ROWS, W), U32),
                            pltpu.VMEM((2, 2, D, tn), F32),
                            pltpu.VMEM((2, D, tn), BF16),
                            pltpu.SemaphoreType.DMA((2,)),
                            pltpu.SemaphoreType.DMA((2,))]),
        compiler_params=_params(("arbitrary",), VMEM_LIMIT),
        name="expert_in_swiglu",
    )(row_token, *items, h2p, w_exp_in, bias)


def _ffn2(items, act, w_exp_out, b_exp_out):
    P, F = act.shape
    n_exp, _, D = w_exp_out.shape
    n_items = items[0].shape[0]
    return pl.pallas_call(
        _ffn2_kernel,
        out_shape=jax.ShapeDtypeStruct((P, D // 2), U32),
        grid_spec=pltpu.PrefetchScalarGridSpec(
            num_scalar_prefetch=10,
            grid=(n_items,),
            in_specs=[pl.BlockSpec((EXPERT_ROWS, F), lambda q, ri, *_: (ri[q], 0)),
                      pl.BlockSpec(memory_space=pl.ANY),
                      pl.BlockSpec((1, 1, D), lambda q, ri, ro, e, *_: (e[q], 0, 0))],
            out_specs=pl.BlockSpec((EXPERT_ROWS, D // 2), lambda q, ri, ro, *_: (ro[q], 0)),
            scratch_shapes=[pltpu.VMEM((2, F, D), F32),
                            pltpu.VMEM((F, D), BF16),
                            pltpu.SemaphoreType.DMA((2,))]),
        compiler_params=_params(("arbitrary",), VMEM_LIMIT),
        name="expert_out",
    )(*items, act, w_exp_out, b_exp_out[:, None, :])


def _combine_kernel(dest_ref, y_ref, gates_ref, x1_ref, g2_ref, o_ref, buf, sem, *, tm):
    s = pl.program_id(0)
    ns = pl.num_programs(0)

    def gather(step, slot):
        def issue(t, _):
            for k in range(TOP_K):
                d = dest_ref[TOP_K * (step * tm + t) + k]
                pltpu.make_async_copy(y_ref.at[pl.ds(d, 1), :],
                                      buf.at[slot, k, pl.ds(t, 1), :], sem.at[slot]).start()
            return 0

        lax.fori_loop(0, tm, issue, 0, unroll=4)

    @pl.when(s == 0)
    def _():
        gather(0, 0)

    @pl.when(s + 1 < ns)
    def _():
        gather(s + 1, (s + 1) % 2)

    slot = s % 2
    for k in range(TOP_K):
        pltpu.make_async_copy(y_ref.at[pl.ds(0, tm), :], buf.at[slot, k], sem.at[slot]).wait()
    gates = gates_ref[...]
    y_hi = y_lo = None
    for k in range(TOP_K):
        hi, lo = _unpack_halves(buf[slot, k])
        g = gates[:, k:k + 1]
        y_hi = g * hi if y_hi is None else y_hi + g * hi
        y_lo = g * lo if y_lo is None else y_lo + g * lo
    y = jnp.concatenate([y_hi, y_lo], axis=1)
    o_ref[...] = x1_ref[...] + g2_ref[0] * y


def _combine(dest_flat, y_pad, gates_wide, x1, gate2, S):
    T, D = x1.shape
    tm = 128
    per_b = S // tm
    return pl.pallas_call(
        functools.partial(_combine_kernel, tm=tm),
        out_shape=jax.ShapeDtypeStruct((T, D), F32),
        grid_spec=pltpu.PrefetchScalarGridSpec(
            num_scalar_prefetch=1,
            grid=(T // tm,),
            in_specs=[pl.BlockSpec(memory_space=pl.ANY),
                      pl.BlockSpec((tm, LANES), lambda i, d: (i, 0)),
                      pl.BlockSpec((tm, D), lambda i, d: (i, 0)),
                      pl.BlockSpec((1, 1, D), lambda i, d: (i // per_b, 0, 0))],
            out_specs=pl.BlockSpec((tm, D), lambda i, d: (i, 0)),
            scratch_shapes=[pltpu.VMEM((2, TOP_K, tm, D // 2), U32),
                            pltpu.SemaphoreType.DMA((2,))]),
        compiler_params=_params(("arbitrary",), VMEM_LIMIT),
        name="combine_rows",
    )(dest_flat, y_pad, gates_wide, x1, gate2[:, None, :])


def kernel(x, c, norm1_w, norm2_w, w_ada, b_ada, w_in, q_norm_w, k_norm_w, w_pool, pool_scale,
           w_o, w_router, b_router, w_exp_in, b_exp_in, w_exp_out, b_exp_out):
    B, S, D = x.shape
    T = B * S
    depth = w_ada.shape[0]
    n_exp = w_router.shape[-1]
    pool_width = pool_scale.shape[-1]
    sb_width = w_o.shape[1] - pool_width
    n_heads = sb_width // HEAD_DIM
    n_blocks = (T * TOP_K + n_exp * (EXPERT_ROWS - 1)) // EXPERT_ROWS
    n_rows = n_blocks * EXPERT_ROWS

    x2 = x.reshape(T, D)
    for l in range(depth):
        mod = _adaln(c, w_ada[l], b_ada[l])
        shift1, scale1, gate1, shift2, scale2, gate2 = jnp.split(mod, 6, axis=-1)

        proj = _inproj(x2, norm1_w[l], shift1, scale1, w_in[l].astype(BF16), S)
        proj3 = proj.reshape(B, S, -1)
        o_sb = _attention(proj3, q_norm_w[l], k_norm_w[l], n_heads)
        o_pool = _pool(proj3, w_pool[l], pool_scale[l], pool_width)
        x1, h2p, gates_wide, idx_wide = _outproj(
            o_sb.reshape(T, sb_width), o_pool.reshape(T, pool_width), w_o[l].astype(BF16),
            x2, gate1, shift2, scale2, norm2_w[l], w_router[l], b_router[l], S)

        dest_wide, meta = _route(idx_wide)
        cnt = meta[0, :n_exp]
        starts = meta[1, :n_exp]
        dest_flat = dest_wide[:, :TOP_K].reshape(T * TOP_K)
        row_token = _row_tokens(dest_flat, cnt, starts, n_rows)

        F = w_exp_out.shape[2]
        act = _ffn1(_work_items(cnt, F // 1024, n_blocks), row_token, h2p, w_exp_in[l],
                    b_exp_in[l])
        y_pad = _ffn2(_work_items(cnt, 1, n_blocks), act, w_exp_out[l], b_exp_out[l])
        x2 = _combine(dest_flat, y_pad, gates_wide, x1, gate2, S)
    return x2.reshape(B, S, D)
```

```python
import functools
import math

import jax
import jax.numpy as jnp
from jax import lax
from jax.experimental import pallas as pl
from jax.experimental.pallas import tpu as pltpu

F32 = jnp.float32
BF16 = jnp.bfloat16
I32 = jnp.int32
U32 = jnp.uint32

EPS = 1e-6
HEAD_DIM = 128
POOL_WINDOWS = (2, 4, 8, 16)
TOP_K = 4
SWIGLU_ALPHA = 1.702
SWIGLU_LIMIT = 7.0

LANES = 128
EXPERT_ROWS = 256
LOG_UNDERFLOW = 104.0
VMEM_LIMIT = 56 * 1024 * 1024
WEIGHT_DMA_PRIORITY = 1


def _params(sem=None, vmem=None):
    return pltpu.CompilerParams(dimension_semantics=sem, vmem_limit_bytes=vmem)


_HIGH_HALF = 0xFFFF0000


def _pack_rows(v):
    bits = lax.bitcast_convert_type(v.astype(BF16).astype(F32), U32)
    half = v.shape[1] // 2
    return (bits[:, :half] & jnp.uint32(_HIGH_HALF)) | (bits[:, half:] >> 16)


def _unpack_halves(p):
    hi = lax.bitcast_convert_type(p & jnp.uint32(_HIGH_HALF), F32)
    lo = lax.bitcast_convert_type(p << 16, F32)
    return hi, lo


def _unpack_rows(p):
    hi, lo = _unpack_halves(p)
    return jnp.concatenate([hi.astype(BF16), lo.astype(BF16)], axis=1)


def _adaln_kernel(c_ref, w_ref, b_ref, o_ref):
    c = c_ref[...]
    ca = c / (1.0 + jnp.exp(-c))
    o_ref[...] = jnp.dot(ca.astype(BF16), w_ref[...].astype(BF16),
                         preferred_element_type=F32) + b_ref[...]


def _adaln(c, w_ada, b_ada):
    B, D = c.shape
    N = w_ada.shape[1]
    rows = 8
    tn = 1024
    cp = jnp.zeros((rows, D), F32).at[:B].set(c)
    out = pl.pallas_call(
        _adaln_kernel,
        out_shape=jax.ShapeDtypeStruct((rows, N), F32),
        grid=(N // tn,),
        in_specs=[pl.BlockSpec((rows, D), lambda j: (0, 0)),
                  pl.BlockSpec((D, tn), lambda j: (0, j)),
                  pl.BlockSpec((1, tn), lambda j: (0, j))],
        out_specs=pl.BlockSpec((rows, tn), lambda j: (0, j)),
        compiler_params=_params(("arbitrary",), VMEM_LIMIT),
        name="adaln",
    )(cp, w_ada, b_ada.reshape(1, N))
    return out[:B]


def _inproj_kernel(x_ref, nw_ref, sh_ref, sc_ref, w_ref, o_ref, h_ref, *, tm, ch):
    @pl.when(pl.program_id(1) == 0)
    def _():
        mul = nw_ref[...] * (1.0 + sc_ref[0])
        add = sh_ref[0]

        def body(c, _):
            r0 = pl.multiple_of(c * ch, ch)
            x = x_ref[pl.ds(r0, ch), :]
            inv = lax.rsqrt(jnp.mean(x * x, axis=-1, keepdims=True) + EPS)
            h_ref[pl.ds(r0, ch), :] = (x * inv * mul + add).astype(BF16)
            return 0

        lax.fori_loop(0, tm // ch, body, 0)

    o_ref[...] = jnp.dot(h_ref[...], w_ref[...],
                         preferred_element_type=F32).astype(o_ref.dtype)


def _inproj(x2, norm_w, shift, scale, w_bf, S):
    T, D = x2.shape
    N = w_bf.shape[1]
    tm, tn, ch = 1024, 1024, 128
    per_b = S // tm
    return pl.pallas_call(
        functools.partial(_inproj_kernel, tm=tm, ch=ch),
        out_shape=jax.ShapeDtypeStruct((T, N), BF16),
        grid=(T // tm, N // tn),
        in_specs=[pl.BlockSpec((tm, D), lambda i, j: (i, 0)),
                  pl.BlockSpec((1, D), lambda i, j: (0, 0)),
                  pl.BlockSpec((1, 1, D), lambda i, j: (i // per_b, 0, 0)),
                  pl.BlockSpec((1, 1, D), lambda i, j: (i // per_b, 0, 0)),
                  pl.BlockSpec((D, tn), lambda i, j: (0, j))],
        out_specs=pl.BlockSpec((tm, tn), lambda i, j: (i, j)),
        scratch_shapes=[pltpu.VMEM((tm, D), BF16)],
        compiler_params=_params(("arbitrary", "arbitrary"), VMEM_LIMIT),
        name="inproj",
    )(x2, norm_w.reshape(1, D), shift[:, None, :], scale[:, None, :], w_bf)


def _attn_kernel(q_ref, k_ref, v_ref, qw_ref, kw_ref, o_ref, kn_ref, carry_ref, acc_ref,
                 *, S, tq, hg, scale):
    i = pl.program_id(2)
    d = HEAD_DIM

    def head_norm(x, w):
        parts = []
        for h in range(hg):
            xh = x[:, h * d:(h + 1) * d]
            inv = lax.rsqrt(jnp.mean(xh * xh, axis=-1, keepdims=True) + EPS)
            parts.append(xh * inv * w)
        return parts

    @pl.when(i == 0)
    def _():
        def body(c, _):
            r0 = pl.multiple_of(c * tq, tq)
            parts = head_norm(k_ref[0, pl.ds(r0, tq), :].astype(F32), kw_ref[...])
            for h in range(hg):
                kn_ref[pl.ds(r0, tq), h * d:(h + 1) * d] = parts[h].astype(BF16)
            return 0

        lax.fori_loop(0, S // tq, body, 0)

    qb = [(p * scale).astype(BF16) for p in head_norm(q_ref[0].astype(F32), qw_ref[...])]

    row = lax.broadcasted_iota(I32, (tq, tq), 0)
    col = lax.broadcasted_iota(I32, (tq, tq), 1)
    causal = col < row
    tri = (row > col).astype(BF16)

    def scores(h, r0, mask):
        kblk = kn_ref[pl.ds(r0, tq), h * d:(h + 1) * d]
        z = lax.dot_general(qb[h], kblk, (((1,), (1,)), ((), ())), preferred_element_type=F32)
        t = jnp.log(1.0 + jnp.exp(-jnp.abs(z)))
        lsn = jnp.minimum(-z, 0.0) - t
        lsp = jnp.minimum(z, 0.0) - t
        if mask:
            lsn = jnp.where(causal, lsn, 0.0)
        hi = lsn.astype(BF16)
        lo = (lsn - hi.astype(F32)).astype(BF16)
        later = (jnp.dot(hi, tri, preferred_element_type=F32)
                 + jnp.dot(lo, tri, preferred_element_type=F32))
        return lsp + later, later[:, :1] + lsn[:, :1]

    def weighted(a, h, r0):
        vblk = v_ref[0, pl.ds(r0, tq), h * d:(h + 1) * d]
        return jnp.dot(a.astype(BF16), vblk, preferred_element_type=F32)

    has_prev = i > 0
    rd = pl.multiple_of(i * tq, tq)
    rp = pl.multiple_of(jnp.maximum(i - 1, 0) * tq, tq)
    worst = None
    for h in range(hg):
        cols = slice(h * d, (h + 1) * d)
        log_d, sum_d = scores(h, rd, True)
        log_p, sum_p = scores(h, rp, False)
        a_d = jnp.where(causal, jnp.exp(log_d), 0.0)
        a_p = jnp.where(has_prev, jnp.exp(log_p + sum_d), 0.0)
        acc_ref[:, cols] = weighted(a_d, h, rd) + weighted(a_p, h, rp)
        carry = jnp.where(has_prev, sum_d + sum_p, sum_d)
        carry_ref[h] = carry
        m = jnp.max(carry)
        worst = m if worst is None else jnp.maximum(worst, m)

    def earlier(kb):
        r0 = pl.multiple_of(kb * tq, tq)
        worst = None
        for h in range(hg):
            cols = slice(h * d, (h + 1) * d)
            log_a, row_sum = scores(h, r0, False)
            acc_ref[:, cols] += weighted(jnp.exp(log_a + carry_ref[h]), h, r0)
            carry = carry_ref[h] + row_sum
            carry_ref[h] = carry
            m = jnp.max(carry)
            worst = m if worst is None else jnp.maximum(worst, m)
        return worst

    def cond(st):
        kb, m = st
        return jnp.logical_and(kb >= 0, m > -LOG_UNDERFLOW)

    def body(st):
        kb, _ = st
        return kb - 1, earlier(kb)

    lax.while_loop(cond, body, (i - 2, worst))
    o_ref[0] = acc_ref[...].astype(o_ref.dtype)


def _attention(proj3, q_norm_w, k_norm_w, n_heads):
    B, S, _ = proj3.shape
    d = HEAD_DIM
    tq = 256
    hg = 4
    G = n_heads // hg
    w = hg * d
    return pl.pallas_call(
        functools.partial(_attn_kernel, S=S, tq=tq, hg=hg, scale=1.0 / math.sqrt(d)),
        out_shape=jax.ShapeDtypeStruct((B, S, n_heads * d), BF16),
        grid=(B, G, S // tq),
        in_specs=[pl.BlockSpec((1, tq, w), lambda b, g, i: (b, i, g)),
                  pl.BlockSpec((1, S, w), lambda b, g, i: (b, 0, G + g)),
                  pl.BlockSpec((1, S, w), lambda b, g, i: (b, 0, 2 * G + g)),
                  pl.BlockSpec((1, d), lambda b, g, i: (0, 0)),
                  pl.BlockSpec((1, d), lambda b, g, i: (0, 0))],
        out_specs=pl.BlockSpec((1, tq, w), lambda b, g, i: (b, i, g)),
        scratch_shapes=[pltpu.VMEM((S, w), BF16),
                        pltpu.VMEM((hg, tq, 1), F32),
                        pltpu.VMEM((tq, w), F32)],
        compiler_params=_params(("arbitrary", "arbitrary", "arbitrary"), VMEM_LIMIT),
        name="stickbreak_attn",
    )(proj3, proj3, proj3, q_norm_w.reshape(1, d), k_norm_w.reshape(1, d))


def _pool_kernel(u_ref, w_ref, ps_ref, o_ref, *, S, ch, gd):
    halo = 16
    for g, win in enumerate(POOL_WINDOWS):
        lo, hi = g * gd, (g + 1) * gd
        wg = w_ref[g].astype(BF16)
        sc = ps_ref[:, lo:hi]

        def body(c, _, win=win, lo=lo, hi=hi, wg=wg, sc=sc):
            r0 = pl.multiple_of(c * ch, ch)
            cur = u_ref[0, pl.ds(r0, ch), lo:hi].astype(F32)
            p0 = pl.multiple_of(jnp.maximum(r0 - halo, 0), halo)
            prev = u_ref[0, pl.ds(p0, halo), lo:hi].astype(F32)
            prev = jnp.where(c > 0, prev, 0.0)
            s = jnp.concatenate([prev, cur], axis=0)
            n = 1
            while n < win:
                s = s + pltpu.roll(s, n, 0)
                n *= 2
            s = s[halo:]
            t = r0 + lax.broadcasted_iota(I32, (ch, 1), 0)
            cnt = jnp.minimum(t + 1, win).astype(F32)
            p = s / cnt - cur
            y = jnp.dot(p.astype(BF16), wg, preferred_element_type=F32) * sc
            o_ref[0, pl.ds(r0, ch), lo:hi] = y.astype(o_ref.dtype)
            return 0

        lax.fori_loop(0, S // ch, body, 0)


def _pool(proj3, w_pool, pool_scale, pool_width):
    B, S, NP = proj3.shape
    G, gd, _ = w_pool.shape
    return pl.pallas_call(
        functools.partial(_pool_kernel, S=S, ch=256, gd=gd),
        out_shape=jax.ShapeDtypeStruct((B, S, pool_width), BF16),
        grid=(B,),
        in_specs=[pl.BlockSpec((1, S, pool_width), lambda b: (b, 0, NP // pool_width - 1)),
                  pl.BlockSpec((G, gd, gd), lambda b: (0, 0, 0)),
                  pl.BlockSpec((1, pool_width), lambda b: (0, 0))],
        out_specs=pl.BlockSpec((1, S, pool_width), lambda b: (b, 0, 0)),
        compiler_params=_params(("arbitrary",), VMEM_LIMIT),
        name="pool_mixer",
    )(proj3, w_pool, pool_scale.reshape(1, pool_width))


def _outproj_kernel(osb_ref, opool_ref, wo_ref, x_ref, g1_ref, sh_ref, sc_ref, nw_ref,
                    wr_ref, br_ref, x1_ref, h2p_ref, gates_ref, idx_ref, *, sbw, n_exp, sub):
    for r0 in range(0, x_ref.shape[0], sub):
        rows = slice(r0, r0 + sub)
        _outproj_rows(osb_ref.at[rows], opool_ref.at[rows], wo_ref, x_ref.at[rows], g1_ref,
                      sh_ref, sc_ref, nw_ref, wr_ref, br_ref, x1_ref.at[rows], h2p_ref.at[rows],
                      gates_ref.at[rows], idx_ref.at[rows], sbw=sbw, n_exp=n_exp)


def _outproj_rows(osb_ref, opool_ref, wo_ref, x_ref, g1_ref, sh_ref, sc_ref, nw_ref,
                  wr_ref, br_ref, x1_ref, h2p_ref, gates_ref, idx_ref, *, sbw, n_exp):
    tm, D = x_ref.shape
    mixed = (jnp.dot(osb_ref[...], wo_ref[:sbw, :], preferred_element_type=F32)
             + jnp.dot(opool_ref[...], wo_ref[sbw:, :], preferred_element_type=F32))
    x1 = x_ref[...] + g1_ref[0] * mixed
    x1_ref[...] = x1
    inv = lax.rsqrt(jnp.mean(x1 * x1, axis=-1, keepdims=True) + EPS)
    h2 = x1 * inv * (nw_ref[...] * (1.0 + sc_ref[0])) + sh_ref[0]
    hb = h2.astype(BF16)
    h2p_ref[...] = _pack_rows(h2)

    lane = lax.broadcasted_iota(I32, (tm, LANES), 1)
    lanef = lane.astype(F32)
    logits = jnp.dot(hb, wr_ref[...].astype(BF16), preferred_element_type=F32) + br_ref[...]
    vals = jnp.where(lane < n_exp, logits, -jnp.inf)
    tops, ids = [], []
    for _ in range(TOP_K):
        m = jnp.max(vals, axis=-1, keepdims=True)
        first = jnp.min(jnp.where(vals == m, lanef, float(LANES)), axis=-1, keepdims=True)
        tops.append(m)
        ids.append(first)
        vals = jnp.where(lanef == first, -jnp.inf, vals)
    es = [jnp.exp(m - tops[0]) for m in tops]
    den = es[0]
    for e in es[1:]:
        den = den + e
    gates = jnp.zeros((tm, LANES), F32)
    idx = jnp.zeros((tm, LANES), F32)
    for k in range(TOP_K):
        gates = jnp.where(lane == k, es[k] / den, gates)
        idx = jnp.where(lane == k, ids[k], idx)
    gates_ref[...] = gates
    idx_ref[...] = idx.astype(I32)


def _outproj(o_sb, o_pool, wo_bf, x2, gate1, shift2, scale2, norm2_w, w_router, b_router, S):
    T, D = x2.shape
    sbw = o_sb.shape[1]
    pw = o_pool.shape[1]
    n_exp = w_router.shape[1]
    tm, sub = 512, 256
    per_b = S // tm
    wr = jnp.zeros((D, LANES), F32).at[:, :n_exp].set(w_router)
    br = jnp.zeros((1, LANES), F32).at[0, :n_exp].set(b_router)
    mod_spec = pl.BlockSpec((1, 1, D), lambda i: (i // per_b, 0, 0))
    return pl.pallas_call(
        functools.partial(_outproj_kernel, sbw=sbw, n_exp=n_exp, sub=sub),
        out_shape=(jax.ShapeDtypeStruct((T, D), F32),
                   jax.ShapeDtypeStruct((T, D // 2), U32),
                   jax.ShapeDtypeStruct((T, LANES), F32),
                   jax.ShapeDtypeStruct((T, LANES), I32)),
        grid=(T // tm,),
        in_specs=[pl.BlockSpec((tm, sbw), lambda i: (i, 0)),
                  pl.BlockSpec((tm, pw), lambda i: (i, 0)),
                  pl.BlockSpec((sbw + pw, D), lambda i: (0, 0)),
                  pl.BlockSpec((tm, D), lambda i: (i, 0)),
                  mod_spec, mod_spec, mod_spec,
                  pl.BlockSpec((1, D), lambda i: (0, 0)),
                  pl.BlockSpec((D, LANES), lambda i: (0, 0)),
                  pl.BlockSpec((1, LANES), lambda i: (0, 0))],
        out_specs=(pl.BlockSpec((tm, D), lambda i: (i, 0)),
                   pl.BlockSpec((tm, D // 2), lambda i: (i, 0)),
                   pl.BlockSpec((tm, LANES), lambda i: (i, 0)),
                   pl.BlockSpec((tm, LANES), lambda i: (i, 0))),
        compiler_params=_params(("arbitrary",), VMEM_LIMIT),
        name="outproj_router",
    )(o_sb, o_pool, wo_bf, x2, gate1[:, None, :], shift2[:, None, :], scale2[:, None, :],
      norm2_w.reshape(1, D), wr, br)


def _route_kernel(idx_ref, dest_ref, meta_ref, rank_ref, *, T, ch):
    lane = lax.broadcasted_iota(I32, (ch, LANES), 1)
    row = lax.broadcasted_iota(I32, (ch, ch), 0)
    col = lax.broadcasted_iota(I32, (ch, ch), 1)
    before = (col < row).astype(BF16)

    def load(c):
        return idx_ref[pl.ds(pl.multiple_of(c * ch, ch), ch), :]

    def count(c, cnt):
        ii = load(c)
        member = lane == ii[:, 0:1]
        for k in range(1, TOP_K):
            member = jnp.logical_or(member, lane == ii[:, k:k + 1])
        mf = jnp.where(member, 1.0, 0.0)
        rank = jnp.dot(before, mf.astype(BF16), preferred_element_type=F32) + cnt
        rank_ref[pl.ds(pl.multiple_of(c * ch, ch), ch), :] = rank
        return cnt + jnp.sum(mf, axis=0, keepdims=True)

    cnt = lax.fori_loop(0, T // ch, count, jnp.zeros((1, LANES), F32))
    padded = jnp.ceil(cnt / EXPERT_ROWS) * EXPERT_ROWS
    rows = 8
    lane8 = lax.broadcasted_iota(I32, (rows, LANES), 1)
    ends = jnp.broadcast_to(padded, (rows, LANES))
    sh = 1
    while sh < LANES:
        ends = ends + jnp.where(lane8 >= sh, pltpu.roll(ends, sh, 1), 0.0)
        sh *= 2
    starts = ends - padded
    sub8 = lax.broadcasted_iota(I32, (rows, LANES), 0)
    meta = jnp.where(sub8 == 0, cnt, jnp.where(sub8 == 1, starts, padded))
    meta_ref[...] = meta.astype(I32)
    start_row = starts[0:1, :]

    def place(c, _):
        ii = load(c)
        val = rank_ref[pl.ds(pl.multiple_of(c * ch, ch), ch), :] + start_row
        out = jnp.zeros((ch, LANES), F32)
        for k in range(TOP_K):
            d = jnp.sum(jnp.where(lane == ii[:, k:k + 1], val, 0.0), axis=-1, keepdims=True)
            out = jnp.where(lane == k, d, out)
        dest_ref[pl.ds(pl.multiple_of(c * ch, ch), ch), :] = out.astype(I32)
        return 0

    lax.fori_loop(0, T // ch, place, 0)


def _route(idx_wide):
    T = idx_wide.shape[0]
    return pl.pallas_call(
        functools.partial(_route_kernel, T=T, ch=256),
        out_shape=(jax.ShapeDtypeStruct((T, LANES), I32),
                   jax.ShapeDtypeStruct((8, LANES), I32)),
        grid=(1,),
        in_specs=[pl.BlockSpec((T, LANES), lambda i: (0, 0))],
        out_specs=(pl.BlockSpec((T, LANES), lambda i: (0, 0)),
                   pl.BlockSpec((8, LANES), lambda i: (0, 0))),
        scratch_shapes=[pltpu.VMEM((T, LANES), F32)],
        compiler_params=_params(("arbitrary",), VMEM_LIMIT),
        name="route_ranks",
    )(idx_wide)


def _dispatch_kernel(dest_ref, cnt_ref, start_ref, h_ref, x_ref, z_ref, sem, zsem,
                     *, tb, n_exp, n_blocks):
    s = pl.program_id(0)

    def issue(t, _):
        for k in range(TOP_K):
            d = dest_ref[TOP_K * (s * tb + t) + k]
            pltpu.make_async_copy(h_ref.at[pl.ds(t, 1), :], x_ref.at[pl.ds(d, 1), :], sem).start()
        return 0

    lax.fori_loop(0, tb, issue, 0, unroll=4)

    @pl.when(s == 0)
    def _():
        z_ref[...] = jnp.zeros_like(z_ref)
        _dispatch_zero_fill(cnt_ref, start_ref, x_ref, z_ref, zsem, n_exp, n_blocks)

    for _ in range(TOP_K):
        pltpu.make_async_copy(h_ref, x_ref.at[pl.ds(0, tb), :], sem).wait()


def _dispatch_zero_fill(cnt_ref, start_ref, x_ref, z_ref, zsem, n_exp, n_blocks):
    def zero_fill(e, wait):
        cnt = cnt_ref[e]
        npad = (-cnt) & (EXPERT_ROWS - 1)
        off = start_ref[e] + cnt

        def one(i, _):
            cp = pltpu.make_async_copy(z_ref.at[pl.ds(0, 1), :],
                                       x_ref.at[pl.ds(off + i, 1), :], zsem)
            if wait:
                cp.wait()
            else:
                cp.start()
            return 0

        lax.fori_loop(0, npad, one, 0)
        return 0

    used = (start_ref[n_exp - 1] + cnt_ref[n_exp - 1] + EXPERT_ROWS - 1) // EXPERT_ROWS

    def tail_fill(blk, wait):
        r0 = pl.multiple_of(blk * EXPERT_ROWS, EXPERT_ROWS)
        cp = pltpu.make_async_copy(z_ref, x_ref.at[pl.ds(r0, EXPERT_ROWS), :], zsem)
        if wait:
            cp.wait()
        else:
            cp.start()
        return 0

    lax.fori_loop(0, n_exp, lambda e, _: zero_fill(e, False), 0)
    lax.fori_loop(used, n_blocks, lambda b, _: tail_fill(b, False), 0)
    lax.fori_loop(0, n_exp, lambda e, _: zero_fill(e, True), 0)
    lax.fori_loop(used, n_blocks, lambda b, _: tail_fill(b, True), 0)


def _dispatch(dest_flat, cnt, starts, h2p, n_rows):
    T, W = h2p.shape
    n_exp = cnt.shape[0]
    tb = 512
    return pl.pallas_call(
        functools.partial(_dispatch_kernel, tb=tb, n_exp=n_exp, n_blocks=n_rows // EXPERT_ROWS),
        out_shape=jax.ShapeDtypeStruct((n_rows, W), U32),
        grid_spec=pltpu.PrefetchScalarGridSpec(
            num_scalar_prefetch=3,
            grid=(T // tb,),
            in_specs=[pl.BlockSpec((tb, W), lambda s, *_: (s, 0))],
            out_specs=pl.BlockSpec(memory_space=pl.ANY),
            scratch_shapes=[pltpu.VMEM((EXPERT_ROWS, W), U32),
                            pltpu.SemaphoreType.DMA, pltpu.SemaphoreType.DMA]),
        compiler_params=_params(("arbitrary",), VMEM_LIMIT),
        name="dispatch_rows",
    )(dest_flat, cnt, starts, h2p)


def _ffn1_kernel(ri_ref, ro_ref, e_ref, j_ref, first_ref, valid_ref, slot_ref, ne_ref, nj_ref, more_ref,
                 x_ref, w_ref, b_ref, o_ref, stage, wbf, sem, *, F, tn):
    q = pl.program_id(0)

    def wcopy(e, j, slot, part):
        c0 = pl.multiple_of(part * F + j * tn, tn)
        return pltpu.make_async_copy(w_ref.at[e, :, pl.ds(c0, tn)], stage.at[slot, part],
                                     sem.at[slot])

    @pl.when(q == 0)
    def _():
        for part in range(2):
            wcopy(e_ref[0], j_ref[0], 0, part).start(priority=WEIGHT_DMA_PRIORITY)

    @pl.when(first_ref[q] == 1)
    def _():
        slot = slot_ref[q]
        for part in range(2):
            wcopy(e_ref[q], j_ref[q], slot, part).wait()

        @pl.when(more_ref[q] == 1)
        def _():
            for part in range(2):
                wcopy(ne_ref[q], nj_ref[q], 1 - slot, part).start(priority=WEIGHT_DMA_PRIORITY)

        for part in range(2):
            wbf[part] = stage[slot, part].astype(BF16)

    @pl.when(valid_ref[q] == 1)
    def _():
        xb = _unpack_rows(x_ref[...])
        g = jnp.dot(xb, wbf[0], preferred_element_type=F32) + b_ref[0, 0, 0]
        lin = jnp.dot(xb, wbf[1], preferred_element_type=F32) + b_ref[0, 1, 0]
        g = jnp.minimum(g, SWIGLU_LIMIT)
        lin = jnp.clip(lin, -SWIGLU_LIMIT, SWIGLU_LIMIT)
        act = g / (1.0 + jnp.exp(-SWIGLU_ALPHA * g)) * (lin + 1.0)
        o_ref[...] = act.astype(o_ref.dtype)

    @pl.when(valid_ref[q] == 0)
    def _():
        o_ref[...] = jnp.zeros_like(o_ref)


def _ffn2_kernel(ri_ref, ro_ref, e_ref, j_ref, first_ref, valid_ref, slot_ref, ne_ref, nj_ref, more_ref,
                 a_ref, w_ref, b_ref, o_ref, stage, wbf, sem):
    q = pl.program_id(0)

    def wcopy(e, slot):
        return pltpu.make_async_copy(w_ref.at[e], stage.at[slot], sem.at[slot])

    @pl.when(q == 0)
    def _():
        wcopy(e_ref[0], 0).start(priority=WEIGHT_DMA_PRIORITY)

    @pl.when(first_ref[q] == 1)
    def _():
        slot = slot_ref[q]
        wcopy(e_ref[q], slot).wait()

        @pl.when(more_ref[q] == 1)
        def _():
            wcopy(ne_ref[q], 1 - slot).start(priority=WEIGHT_DMA_PRIORITY)

        wbf[...] = stage[slot].astype(BF16)

    @pl.when(valid_ref[q] == 1)
    def _():
        y = jnp.dot(a_ref[...], wbf[...], preferred_element_type=F32) + b_ref[0]
        o_ref[...] = _pack_rows(y)

    @pl.when(valid_ref[q] == 0)
    def _():
        o_ref[...] = jnp.zeros_like(o_ref)


def _work_items(cnt, n_col_tiles, n_blocks):
    n_exp = cnt.shape[0]
    nblk = (cnt + EXPERT_ROWS - 1) // EXPERT_ROWS
    bstart = jnp.cumsum(nblk) - nblk
    gsize = jnp.repeat(nblk, n_col_tiles)
    gend = jnp.cumsum(gsize)
    n_groups = n_exp * n_col_tiles
    gid = jnp.arange(n_groups, dtype=I32)
    total = gend[-1]
    q = jnp.arange(n_blocks * n_col_tiles, dtype=I32)
    qc = jnp.minimum(q, total - 1)
    g = jnp.sum((gend[None, :] <= qc[:, None]).astype(I32), axis=1)
    nonempty = gsize > 0
    ordinal = jnp.cumsum(nonempty.astype(I32)) - 1
    nxt_incl = lax.cummin(jnp.where(nonempty, gid, n_groups), reverse=True)
    nxt = jnp.concatenate([nxt_incl[1:], jnp.full((1,), n_groups, I32)])
    more = nxt < n_groups
    nxt = jnp.minimum(nxt, n_groups - 1)
    per_group = jnp.stack([gend - gsize, gid // n_col_tiles, gid % n_col_tiles,
                           jnp.repeat(bstart, n_col_tiles), ordinal % 2,
                           nxt // n_col_tiles, nxt % n_col_tiles, more.astype(I32)])
    pick = (g[None, :, None] == gid[None, None, :]).astype(I32)
    gstart, e, j, brow, slot, ne, nj, more = jnp.sum(pick * per_group[:, None, :], axis=2)
    r = qc - gstart
    valid = q < total
    first = jnp.logical_and(valid, r == 0)
    over = q - total
    row_in = brow + r
    row_out = jnp.where(valid, row_in, jnp.sum(nblk) + over // n_col_tiles)
    col_out = jnp.where(valid, j, over % n_col_tiles)
    as_i32 = lambda a: a.astype(I32)
    return tuple(map(as_i32, (row_in, row_out, e, col_out, first, valid, slot, ne, nj, more)))


def _ffn1(items, x_pad, w_exp_in, b_exp_in):
    P, W = x_pad.shape
    n_exp, D, F2 = w_exp_in.shape
    F = F2 // 2
    tn = 1024
    nj = F // tn
    n_items = items[0].shape[0]
    bias = b_exp_in.reshape(n_exp, 2, nj, 1, tn)
    return pl.pallas_call(
        functools.partial(_ffn1_kernel, F=F, tn=tn),
        out_shape=jax.ShapeDtypeStruct((P, F), BF16),
        grid_spec=pltpu.PrefetchScalarGridSpec(
            num_scalar_prefetch=10,
            grid=(n_items,),
            in_specs=[pl.BlockSpec((EXPERT_ROWS, W), lambda q, ri, *_: (ri[q], 0)),
                      pl.BlockSpec(memory_space=pl.ANY),
                      pl.BlockSpec((1, 2, 1, 1, tn),
                                   lambda q, ri, ro, e, j, *_: (e[q], 0, j[q], 0, 0))],
            out_specs=pl.BlockSpec((EXPERT_ROWS, tn),
                                   lambda q, ri, ro, e, j, *_: (ro[q], j[q])),
            scratch_shapes=[pltpu.VMEM((2, 2, D, tn), F32),
                            pltpu.VMEM((2, D, tn), BF16),
                            pltpu.SemaphoreType.DMA((2,))]),
        compiler_params=_params(("arbitrary",), VMEM_LIMIT),
        name="expert_in_swiglu",
    )(*items, x_pad, w_exp_in, bias)


def _ffn2(items, act, w_exp_out, b_exp_out):
    P, F = act.shape
    n_exp, _, D = w_exp_out.shape
    n_items = items[0].shape[0]
    return pl.pallas_call(
        _ffn2_kernel,
        out_shape=jax.ShapeDtypeStruct((P, D // 2), U32),
        grid_spec=pltpu.PrefetchScalarGridSpec(
            num_scalar_prefetch=10,
            grid=(n_items,),
            in_specs=[pl.BlockSpec((EXPERT_ROWS, F), lambda q, ri, *_: (ri[q], 0)),
                      pl.BlockSpec(memory_space=pl.ANY),
                      pl.BlockSpec((1, 1, D), lambda q, ri, ro, e, *_: (e[q], 0, 0))],
            out_specs=pl.BlockSpec((EXPERT_ROWS, D // 2), lambda q, ri, ro, *_: (ro[q], 0)),
            scratch_shapes=[pltpu.VMEM((2, F, D), F32),
                            pltpu.VMEM((F, D), BF16),
                            pltpu.SemaphoreType.DMA((2,))]),
        compiler_params=_params(("arbitrary",), VMEM_LIMIT),
        name="expert_out",
    )(*items, act, w_exp_out, b_exp_out[:, None, :])


def _combine_kernel(dest_ref, y_ref, gates_ref, x1_ref, g2_ref, o_ref, buf, sem, *, tm):
    s = pl.program_id(0)
    ns = pl.num_programs(0)

    def gather(step, slot):
        def issue(t, _):
            for k in range(TOP_K):
                d = dest_ref[TOP_K * (step * tm + t) + k]
                pltpu.make_async_copy(y_ref.at[pl.ds(d, 1), :],
                                      buf.at[slot, k, pl.ds(t, 1), :], sem.at[slot]).start()
            return 0

        lax.fori_loop(0, tm, issue, 0, unroll=4)

    @pl.when(s == 0)
    def _():
        gather(0, 0)

    @pl.when(s + 1 < ns)
    def _():
        gather(s + 1, (s + 1) % 2)

    slot = s % 2
    for k in range(TOP_K):
        pltpu.make_async_copy(y_ref.at[pl.ds(0, tm), :], buf.at[slot, k], sem.at[slot]).wait()
    gates = gates_ref[...]
    y_hi = y_lo = None
    for k in range(TOP_K):
        hi, lo = _unpack_halves(buf[slot, k])
        g = gates[:, k:k + 1]
        y_hi = g * hi if y_hi is None else y_hi + g * hi
        y_lo = g * lo if y_lo is None else y_lo + g * lo
    y = jnp.concatenate([y_hi, y_lo], axis=1)
    o_ref[...] = x1_ref[...] + g2_ref[0] * y


def _combine(dest_flat, y_pad, gates_wide, x1, gate2, S):
    T, D = x1.shape
    tm = 128
    per_b = S // tm
    return pl.pallas_call(
        functools.partial(_combine_kernel, tm=tm),
        out_shape=jax.ShapeDtypeStruct((T, D), F32),
        grid_spec=pltpu.PrefetchScalarGridSpec(
            num_scalar_prefetch=1,
            grid=(T // tm,),
            in_specs=[pl.BlockSpec(memory_space=pl.ANY),
                      pl.BlockSpec((tm, LANES), lambda i, d: (i, 0)),
                      pl.BlockSpec((tm, D), lambda i, d: (i, 0)),
                      pl.BlockSpec((1, 1, D), lambda i, d: (i // per_b, 0, 0))],
            out_specs=pl.BlockSpec((tm, D), lambda i, d: (i, 0)),
            scratch_shapes=[pltpu.VMEM((2, TOP_K, tm, D // 2), U32),
                            pltpu.SemaphoreType.DMA((2,))]),
        compiler_params=_params(("arbitrary",), VMEM_LIMIT),
        name="combine_rows",
    )(dest_flat, y_pad, gates_wide, x1, gate2[:, None, :])


def kernel(x, c, norm1_w, norm2_w, w_ada, b_ada, w_in, q_norm_w, k_norm_w, w_pool, pool_scale,
           w_o, w_router, b_router, w_exp_in, b_exp_in, w_exp_out, b_exp_out):
    B, S, D = x.shape
    T = B * S
    depth = w_ada.shape[0]
    n_exp = w_router.shape[-1]
    pool_width = pool_scale.shape[-1]
    sb_width = w_o.shape[1] - pool_width
    n_heads = sb_width // HEAD_DIM
    n_blocks = (T * TOP_K + n_exp * (EXPERT_ROWS - 1)) // EXPERT_ROWS
    n_rows = n_blocks * EXPERT_ROWS

    x2 = x.reshape(T, D)
    for l in range(depth):
        mod = _adaln(c, w_ada[l], b_ada[l])
        shift1, scale1, gate1, shift2, scale2, gate2 = jnp.split(mod, 6, axis=-1)

        proj = _inproj(x2, norm1_w[l], shift1, scale1, w_in[l].astype(BF16), S)
        proj3 = proj.reshape(B, S, -1)
        o_sb = _attention(proj3, q_norm_w[l], k_norm_w[l], n_heads)
        o_pool = _pool(proj3, w_pool[l], pool_scale[l], pool_width)
        x1, h2p, gates_wide, idx_wide = _outproj(
            o_sb.reshape(T, sb_width), o_pool.reshape(T, pool_width), w_o[l].astype(BF16),
            x2, gate1, shift2, scale2, norm2_w[l], w_router[l], b_router[l], S)

        dest_wide, meta = _route(idx_wide)
        cnt = meta[0, :n_exp]
        starts = meta[1, :n_exp]
        dest_flat = dest_wide[:, :TOP_K].reshape(T * TOP_K)
        x_pad = _dispatch(dest_flat, cnt, starts, h2p, n_rows)

        F = w_exp_out.shape[2]
        act = _ffn1(_work_items(cnt, F // 1024, n_blocks), x_pad, w_exp_in[l], b_exp_in[l])
        y_pad = _ffn2(_work_items(cnt, 1, n_blocks), act, w_exp_out[l], b_exp_out[l])
        x2 = _combine(dest_flat, y_pad, gates_wide, x1, gate2, S)
    return x2.reshape(B, S, D)
```

```python
import functools
import math

import jax
import jax.numpy as jnp
from jax import lax
from jax.experimental import pallas as pl
from jax.experimental.pallas import tpu as pltpu

F32 = jnp.float32
BF16 = jnp.bfloat16
I32 = jnp.int32
U32 = jnp.uint32

EPS = 1e-6
HEAD_DIM = 128
POOL_WINDOWS = (2, 4, 8, 16)
TOP_K = 4
SWIGLU_ALPHA = 1.702
SWIGLU_LIMIT = 7.0

LANES = 128
SUBLANES = 8
EXPERT_ROWS = 256
LOG_UNDERFLOW = 104.0
VMEM_LIMIT = 56 * 1024 * 1024
WEIGHT_DMA_PRIORITY = 1


def _params(sem=None, vmem=None):
    return pltpu.CompilerParams(dimension_semantics=sem, vmem_limit_bytes=vmem)


_HIGH_HALF = 0xFFFF0000


def _pack_rows(v):
    bits = lax.bitcast_convert_type(v.astype(BF16).astype(F32), U32)
    half = v.shape[1] // 2
    return (bits[:, :half] & jnp.uint32(_HIGH_HALF)) | (bits[:, half:] >> 16)


def _unpack_halves(p):
    hi = lax.bitcast_convert_type(p & jnp.uint32(_HIGH_HALF), F32)
    lo = lax.bitcast_convert_type(p << 16, F32)
    return hi, lo


def _unpack_rows(p):
    hi, lo = _unpack_halves(p)
    return jnp.concatenate([hi.astype(BF16), lo.astype(BF16)], axis=1)


def _adaln_kernel(c_ref, w_ref, b_ref, o_ref):
    c = c_ref[...]
    ca = c / (1.0 + jnp.exp(-c))
    o_ref[...] = jnp.dot(ca.astype(BF16), w_ref[...].astype(BF16),
                         preferred_element_type=F32) + b_ref[...]


def _adaln(c, w_ada, b_ada):
    B, D = c.shape
    N = w_ada.shape[1]
    rows = 8
    tn = 1024
    cp = jnp.zeros((rows, D), F32).at[:B].set(c)
    out = pl.pallas_call(
        _adaln_kernel,
        out_shape=jax.ShapeDtypeStruct((rows, N), F32),
        grid=(N // tn,),
        in_specs=[pl.BlockSpec((rows, D), lambda j: (0, 0)),
                  pl.BlockSpec((D, tn), lambda j: (0, j)),
                  pl.BlockSpec((1, tn), lambda j: (0, j))],
        out_specs=pl.BlockSpec((rows, tn), lambda j: (0, j)),
        compiler_params=_params(("arbitrary",), VMEM_LIMIT),
        name="adaln",
    )(cp, w_ada, b_ada.reshape(1, N))
    return out[:B]


def _inproj_kernel(x_ref, nw_ref, sh_ref, sc_ref, w_ref, o_ref, h_ref, *, tm, ch):
    @pl.when(pl.program_id(1) == 0)
    def _():
        mul = nw_ref[...] * (1.0 + sc_ref[0])
        add = sh_ref[0]

        def body(c, _):
            r0 = pl.multiple_of(c * ch, ch)
            x = x_ref[pl.ds(r0, ch), :]
            inv = lax.rsqrt(jnp.mean(x * x, axis=-1, keepdims=True) + EPS)
            h_ref[pl.ds(r0, ch), :] = (x * inv * mul + add).astype(BF16)
            return 0

        lax.fori_loop(0, tm // ch, body, 0)

    o_ref[...] = jnp.dot(h_ref[...], w_ref[...],
                         preferred_element_type=F32).astype(o_ref.dtype)


def _inproj(x2, norm_w, shift, scale, w_bf, S):
    T, D = x2.shape
    N = w_bf.shape[1]
    tm, tn, ch = 1024, 1024, 128
    per_b = S // tm
    return pl.pallas_call(
        functools.partial(_inproj_kernel, tm=tm, ch=ch),
        out_shape=jax.ShapeDtypeStruct((T, N), BF16),
        grid=(T // tm, N // tn),
        in_specs=[pl.BlockSpec((tm, D), lambda i, j: (i, 0)),
                  pl.BlockSpec((1, D), lambda i, j: (0, 0)),
                  pl.BlockSpec((1, 1, D), lambda i, j: (i // per_b, 0, 0)),
                  pl.BlockSpec((1, 1, D), lambda i, j: (i // per_b, 0, 0)),
                  pl.BlockSpec((D, tn), lambda i, j: (0, j))],
        out_specs=pl.BlockSpec((tm, tn), lambda i, j: (i, j)),
        scratch_shapes=[pltpu.VMEM((tm, D), BF16)],
        compiler_params=_params(("arbitrary", "arbitrary"), VMEM_LIMIT),
        name="inproj",
    )(x2, norm_w.reshape(1, D), shift[:, None, :], scale[:, None, :], w_bf)


def _attn_kernel(q_ref, k_ref, v_ref, qw_ref, kw_ref, o_ref, kn_ref, carry_ref, acc_ref,
                 *, S, tq, hg, scale):
    i = pl.program_id(2)
    d = HEAD_DIM

    def head_norm(x, w):
        parts = []
        for h in range(hg):
            xh = x[:, h * d:(h + 1) * d]
            inv = lax.rsqrt(jnp.mean(xh * xh, axis=-1, keepdims=True) + EPS)
            parts.append(xh * inv * w)
        return parts

    @pl.when(i == 0)
    def _():
        def body(c, _):
            r0 = pl.multiple_of(c * tq, tq)
            parts = head_norm(k_ref[0, pl.ds(r0, tq), :].astype(F32), kw_ref[...])
            for h in range(hg):
                kn_ref[pl.ds(r0, tq), h * d:(h + 1) * d] = parts[h].astype(BF16)
            return 0

        lax.fori_loop(0, S // tq, body, 0)

    qb = [(p * scale).astype(BF16) for p in head_norm(q_ref[0].astype(F32), qw_ref[...])]

    row = lax.broadcasted_iota(I32, (tq, tq), 0)
    col = lax.broadcasted_iota(I32, (tq, tq), 1)
    causal = col < row
    tri = (row > col).astype(BF16)

    def scores(h, r0, mask):
        kblk = kn_ref[pl.ds(r0, tq), h * d:(h + 1) * d]
        z = lax.dot_general(qb[h], kblk, (((1,), (1,)), ((), ())), preferred_element_type=F32)
        t = jnp.log(1.0 + jnp.exp(-jnp.abs(z)))
        lsn = jnp.minimum(-z, 0.0) - t
        lsp = jnp.minimum(z, 0.0) - t
        if mask:
            lsn = jnp.where(causal, lsn, 0.0)
        hi = lsn.astype(BF16)
        lo = (lsn - hi.astype(F32)).astype(BF16)
        later = (jnp.dot(hi, tri, preferred_element_type=F32)
                 + jnp.dot(lo, tri, preferred_element_type=F32))
        return lsp + later, later[:, :1] + lsn[:, :1]

    def weighted(a, h, r0):
        vblk = v_ref[0, pl.ds(r0, tq), h * d:(h + 1) * d]
        return jnp.dot(a.astype(BF16), vblk, preferred_element_type=F32)

    has_prev = i > 0
    rd = pl.multiple_of(i * tq, tq)
    rp = pl.multiple_of(jnp.maximum(i - 1, 0) * tq, tq)
    worst = None
    for h in range(hg):
        cols = slice(h * d, (h + 1) * d)
        log_d, sum_d = scores(h, rd, True)
        log_p, sum_p = scores(h, rp, False)
        a_d = jnp.where(causal, jnp.exp(log_d), 0.0)
        a_p = jnp.where(has_prev, jnp.exp(log_p + sum_d), 0.0)
        acc_ref[:, cols] = weighted(a_d, h, rd) + weighted(a_p, h, rp)
        carry = jnp.where(has_prev, sum_d + sum_p, sum_d)
        carry_ref[h] = carry
        m = jnp.max(carry)
        worst = m if worst is None else jnp.maximum(worst, m)

    def earlier(kb):
        r0 = pl.multiple_of(kb * tq, tq)
        worst = None
        for h in range(hg):
            cols = slice(h * d, (h + 1) * d)
            log_a, row_sum = scores(h, r0, False)
            acc_ref[:, cols] += weighted(jnp.exp(log_a + carry_ref[h]), h, r0)
            carry = carry_ref[h] + row_sum
            carry_ref[h] = carry
            m = jnp.max(carry)
            worst = m if worst is None else jnp.maximum(worst, m)
        return worst

    def cond(st):
        kb, m = st
        return jnp.logical_and(kb >= 0, m > -LOG_UNDERFLOW)

    def body(st):
        kb, _ = st
        return kb - 1, earlier(kb)

    lax.while_loop(cond, body, (i - 2, worst))
    o_ref[0] = acc_ref[...].astype(o_ref.dtype)


def _attention(proj3, q_norm_w, k_norm_w, n_heads):
    B, S, _ = proj3.shape
    d = HEAD_DIM
    tq = 256
    hg = 4
    G = n_heads // hg
    w = hg * d
    return pl.pallas_call(
        functools.partial(_attn_kernel, S=S, tq=tq, hg=hg, scale=1.0 / math.sqrt(d)),
        out_shape=jax.ShapeDtypeStruct((B, S, n_heads * d), BF16),
        grid=(B, G, S // tq),
        in_specs=[pl.BlockSpec((1, tq, w), lambda b, g, i: (b, i, g)),
                  pl.BlockSpec((1, S, w), lambda b, g, i: (b, 0, G + g)),
                  pl.BlockSpec((1, S, w), lambda b, g, i: (b, 0, 2 * G + g)),
                  pl.BlockSpec((1, d), lambda b, g, i: (0, 0)),
                  pl.BlockSpec((1, d), lambda b, g, i: (0, 0))],
        out_specs=pl.BlockSpec((1, tq, w), lambda b, g, i: (b, i, g)),
        scratch_shapes=[pltpu.VMEM((S, w), BF16),
                        pltpu.VMEM((hg, tq, 1), F32),
                        pltpu.VMEM((tq, w), F32)],
        compiler_params=_params(("arbitrary", "arbitrary", "arbitrary"), VMEM_LIMIT),
        name="stickbreak_attn",
    )(proj3, proj3, proj3, q_norm_w.reshape(1, d), k_norm_w.reshape(1, d))


def _pool_kernel(u_ref, w_ref, ps_ref, o_ref, *, S, ch, gd):
    halo = 16
    for g, win in enumerate(POOL_WINDOWS):
        lo, hi = g * gd, (g + 1) * gd
        wg = w_ref[g].astype(BF16)
        sc = ps_ref[:, lo:hi]

        def body(c, _, win=win, lo=lo, hi=hi, wg=wg, sc=sc):
            r0 = pl.multiple_of(c * ch, ch)
            cur = u_ref[0, pl.ds(r0, ch), lo:hi].astype(F32)
            p0 = pl.multiple_of(jnp.maximum(r0 - halo, 0), halo)
            prev = u_ref[0, pl.ds(p0, halo), lo:hi].astype(F32)
            prev = jnp.where(c > 0, prev, 0.0)
            s = jnp.concatenate([prev, cur], axis=0)
            n = 1
            while n < win:
                s = s + pltpu.roll(s, n, 0)
                n *= 2
            s = s[halo:]
            t = r0 + lax.broadcasted_iota(I32, (ch, 1), 0)
            cnt = jnp.minimum(t + 1, win).astype(F32)
            p = s / cnt - cur
            y = jnp.dot(p.astype(BF16), wg, preferred_element_type=F32) * sc
            o_ref[0, pl.ds(r0, ch), lo:hi] = y.astype(o_ref.dtype)
            return 0

        lax.fori_loop(0, S // ch, body, 0)


def _pool(proj3, w_pool, pool_scale, pool_width):
    B, S, NP = proj3.shape
    G, gd, _ = w_pool.shape
    return pl.pallas_call(
        functools.partial(_pool_kernel, S=S, ch=256, gd=gd),
        out_shape=jax.ShapeDtypeStruct((B, S, pool_width), BF16),
        grid=(B,),
        in_specs=[pl.BlockSpec((1, S, pool_width), lambda b: (b, 0, NP // pool_width - 1)),
                  pl.BlockSpec((G, gd, gd), lambda b: (0, 0, 0)),
                  pl.BlockSpec((1, pool_width), lambda b: (0, 0))],
        out_specs=pl.BlockSpec((1, S, pool_width), lambda b: (b, 0, 0)),
        compiler_params=_params(("arbitrary",), VMEM_LIMIT),
        name="pool_mixer",
    )(proj3, w_pool, pool_scale.reshape(1, pool_width))


def _outproj_kernel(osb_ref, opool_ref, wo_ref, x_ref, g1_ref, sh_ref, sc_ref, nw_ref,
                    wr_ref, br_ref, x1_ref, h2p_ref, gates_ref, idx_ref, *, sbw, n_exp, sub):
    for r0 in range(0, x_ref.shape[0], sub):
        rows = slice(r0, r0 + sub)
        _outproj_rows(osb_ref.at[rows], opool_ref.at[rows], wo_ref, x_ref.at[rows], g1_ref,
                      sh_ref, sc_ref, nw_ref, wr_ref, br_ref, x1_ref.at[rows], h2p_ref.at[rows],
                      gates_ref.at[rows], idx_ref.at[rows], sbw=sbw, n_exp=n_exp)


def _outproj_rows(osb_ref, opool_ref, wo_ref, x_ref, g1_ref, sh_ref, sc_ref, nw_ref,
                  wr_ref, br_ref, x1_ref, h2p_ref, gates_ref, idx_ref, *, sbw, n_exp):
    tm, D = x_ref.shape
    mixed = (jnp.dot(osb_ref[...], wo_ref[:sbw, :], preferred_element_type=F32)
             + jnp.dot(opool_ref[...], wo_ref[sbw:, :], preferred_element_type=F32))
    x1 = x_ref[...] + g1_ref[0] * mixed
    x1_ref[...] = x1
    inv = lax.rsqrt(jnp.mean(x1 * x1, axis=-1, keepdims=True) + EPS)
    h2 = x1 * inv * (nw_ref[...] * (1.0 + sc_ref[0])) + sh_ref[0]
    hb = h2.astype(BF16)
    h2p_ref[...] = _pack_rows(h2)

    lane = lax.broadcasted_iota(I32, (tm, LANES), 1)
    lanef = lane.astype(F32)
    logits = jnp.dot(hb, wr_ref[...].astype(BF16), preferred_element_type=F32) + br_ref[...]
    vals = jnp.where(lane < n_exp, logits, -jnp.inf)
    tops, ids = [], []
    for _ in range(TOP_K):
        m = jnp.max(vals, axis=-1, keepdims=True)
        first = jnp.min(jnp.where(vals == m, lanef, float(LANES)), axis=-1, keepdims=True)
        tops.append(m)
        ids.append(first)
        vals = jnp.where(lanef == first, -jnp.inf, vals)
    es = [jnp.exp(m - tops[0]) for m in tops]
    den = es[0]
    for e in es[1:]:
        den = den + e
    gates = jnp.zeros((tm, LANES), F32)
    idx = jnp.zeros((tm, LANES), F32)
    for k in range(TOP_K):
        gates = jnp.where(lane == k, es[k] / den, gates)
        idx = jnp.where(lane == k, ids[k], idx)
    gates_ref[...] = gates
    idx_ref[...] = idx.astype(I32)


def _outproj(o_sb, o_pool, wo_bf, x2, gate1, shift2, scale2, norm2_w, w_router, b_router, S):
    T, D = x2.shape
    sbw = o_sb.shape[1]
    pw = o_pool.shape[1]
    n_exp = w_router.shape[1]
    tm, sub = 512, 256
    per_b = S // tm
    wr = jnp.zeros((D, LANES), F32).at[:, :n_exp].set(w_router)
    br = jnp.zeros((1, LANES), F32).at[0, :n_exp].set(b_router)
    mod_spec = pl.BlockSpec((1, 1, D), lambda i: (i // per_b, 0, 0))
    return pl.pallas_call(
        functools.partial(_outproj_kernel, sbw=sbw, n_exp=n_exp, sub=sub),
        out_shape=(jax.ShapeDtypeStruct((T, D), F32),
                   jax.ShapeDtypeStruct((T, D // 2), U32),
                   jax.ShapeDtypeStruct((T, LANES), F32),
                   jax.ShapeDtypeStruct((T, LANES), I32)),
        grid=(T // tm,),
        in_specs=[pl.BlockSpec((tm, sbw), lambda i: (i, 0)),
                  pl.BlockSpec((tm, pw), lambda i: (i, 0)),
                  pl.BlockSpec((sbw + pw, D), lambda i: (0, 0)),
                  pl.BlockSpec((tm, D), lambda i: (i, 0)),
                  mod_spec, mod_spec, mod_spec,
                  pl.BlockSpec((1, D), lambda i: (0, 0)),
                  pl.BlockSpec((D, LANES), lambda i: (0, 0)),
                  pl.BlockSpec((1, LANES), lambda i: (0, 0))],
        out_specs=(pl.BlockSpec((tm, D), lambda i: (i, 0)),
                   pl.BlockSpec((tm, D // 2), lambda i: (i, 0)),
                   pl.BlockSpec((tm, LANES), lambda i: (i, 0)),
                   pl.BlockSpec((tm, LANES), lambda i: (i, 0))),
        compiler_params=_params(("arbitrary",), VMEM_LIMIT),
        name="outproj_router",
    )(o_sb, o_pool, wo_bf, x2, gate1[:, None, :], shift2[:, None, :], scale2[:, None, :],
      norm2_w.reshape(1, D), wr, br)


def _route_kernel(idx_ref, dest_ref, meta_ref, rank_ref, *, T, ch):
    lane = lax.broadcasted_iota(I32, (ch, LANES), 1)
    row = lax.broadcasted_iota(I32, (ch, ch), 0)
    col = lax.broadcasted_iota(I32, (ch, ch), 1)
    before = (col < row).astype(BF16)

    def load(c):
        return idx_ref[pl.ds(pl.multiple_of(c * ch, ch), ch), :]

    def count(c, cnt):
        ii = load(c)
        member = lane == ii[:, 0:1]
        for k in range(1, TOP_K):
            member = jnp.logical_or(member, lane == ii[:, k:k + 1])
        mf = jnp.where(member, 1.0, 0.0)
        rank = jnp.dot(before, mf.astype(BF16), preferred_element_type=F32) + cnt
        rank_ref[pl.ds(pl.multiple_of(c * ch, ch), ch), :] = rank
        return cnt + jnp.sum(mf, axis=0, keepdims=True)

    cnt = lax.fori_loop(0, T // ch, count, jnp.zeros((1, LANES), F32))
    padded = jnp.ceil(cnt / EXPERT_ROWS) * EXPERT_ROWS
    rows = 8
    lane8 = lax.broadcasted_iota(I32, (rows, LANES), 1)
    ends = jnp.broadcast_to(padded, (rows, LANES))
    sh = 1
    while sh < LANES:
        ends = ends + jnp.where(lane8 >= sh, pltpu.roll(ends, sh, 1), 0.0)
        sh *= 2
    starts = ends - padded
    sub8 = lax.broadcasted_iota(I32, (rows, LANES), 0)
    meta = jnp.where(sub8 == 0, cnt, jnp.where(sub8 == 1, starts, padded))
    meta_ref[...] = meta.astype(I32)
    start_row = starts[0:1, :]

    def place(c, _):
        ii = load(c)
        val = rank_ref[pl.ds(pl.multiple_of(c * ch, ch), ch), :] + start_row
        out = jnp.zeros((ch, LANES), F32)
        for k in range(TOP_K):
            d = jnp.sum(jnp.where(lane == ii[:, k:k + 1], val, 0.0), axis=-1, keepdims=True)
            out = jnp.where(lane == k, d, out)
        dest_ref[pl.ds(pl.multiple_of(c * ch, ch), ch), :] = out.astype(I32)
        return 0

    lax.fori_loop(0, T // ch, place, 0)


def _route(idx_wide):
    T = idx_wide.shape[0]
    return pl.pallas_call(
        functools.partial(_route_kernel, T=T, ch=256),
        out_shape=(jax.ShapeDtypeStruct((T, LANES), I32),
                   jax.ShapeDtypeStruct((8, LANES), I32)),
        grid=(1,),
        in_specs=[pl.BlockSpec((T, LANES), lambda i: (0, 0))],
        out_specs=(pl.BlockSpec((T, LANES), lambda i: (0, 0)),
                   pl.BlockSpec((8, LANES), lambda i: (0, 0))),
        scratch_shapes=[pltpu.VMEM((T, LANES), F32)],
        compiler_params=_params(("arbitrary",), VMEM_LIMIT),
        name="route_ranks",
    )(idx_wide)


def _dispatch_kernel(dest_ref, cnt_ref, start_ref, h_ref, x_ref, z_ref, sem, zsem,
                     *, tb, n_exp, n_blocks):
    s = pl.program_id(0)

    def issue(tt, _):
        t0 = pl.multiple_of(tt * SUBLANES, SUBLANES)
        for r in range(SUBLANES):
            for k in range(TOP_K):
                d = dest_ref[TOP_K * (s * tb + t0 + r) + k]
                pltpu.make_async_copy(h_ref.at[pl.ds(t0 + r, 1), :], x_ref.at[pl.ds(d, 1), :],
                                      sem).start(priority=k % 2)
        return 0

    lax.fori_loop(0, tb // SUBLANES, issue, 0)

    @pl.when(s == 0)
    def _():
        z_ref[...] = jnp.zeros_like(z_ref)
        _dispatch_zero_fill(cnt_ref, start_ref, x_ref, z_ref, zsem, n_exp, n_blocks)

    for _ in range(TOP_K):
        pltpu.make_async_copy(h_ref, x_ref.at[pl.ds(0, tb), :], sem).wait()


def _dispatch_zero_fill(cnt_ref, start_ref, x_ref, z_ref, zsem, n_exp, n_blocks):
    def zero_fill(e, wait):
        cnt = cnt_ref[e]
        npad = (-cnt) & (EXPERT_ROWS - 1)
        off = start_ref[e] + cnt

        def one(i, _):
            cp = pltpu.make_async_copy(z_ref.at[pl.ds(0, 1), :],
                                       x_ref.at[pl.ds(off + i, 1), :], zsem)
            if wait:
                cp.wait()
            else:
                cp.start()
            return 0

        lax.fori_loop(0, npad, one, 0)
        return 0

    used = (start_ref[n_exp - 1] + cnt_ref[n_exp - 1] + EXPERT_ROWS - 1) // EXPERT_ROWS

    def tail_fill(blk, wait):
        r0 = pl.multiple_of(blk * EXPERT_ROWS, EXPERT_ROWS)
        cp = pltpu.make_async_copy(z_ref, x_ref.at[pl.ds(r0, EXPERT_ROWS), :], zsem)
        if wait:
            cp.wait()
        else:
            cp.start()
        return 0

    lax.fori_loop(0, n_exp, lambda e, _: zero_fill(e, False), 0)
    lax.fori_loop(used, n_blocks, lambda b, _: tail_fill(b, False), 0)
    lax.fori_loop(0, n_exp, lambda e, _: zero_fill(e, True), 0)
    lax.fori_loop(used, n_blocks, lambda b, _: tail_fill(b, True), 0)


def _dispatch(dest_flat, cnt, starts, h2p, n_rows):
    T, W = h2p.shape
    n_exp = cnt.shape[0]
    tb = 512
    return pl.pallas_call(
        functools.partial(_dispatch_kernel, tb=tb, n_exp=n_exp, n_blocks=n_rows // EXPERT_ROWS),
        out_shape=jax.ShapeDtypeStruct((n_rows, W), U32),
        grid_spec=pltpu.PrefetchScalarGridSpec(
            num_scalar_prefetch=3,
            grid=(T // tb,),
            in_specs=[pl.BlockSpec((tb, W), lambda s, *_: (s, 0))],
            out_specs=pl.BlockSpec(memory_space=pl.ANY),
            scratch_shapes=[pltpu.VMEM((EXPERT_ROWS, W), U32),
                            pltpu.SemaphoreType.DMA, pltpu.SemaphoreType.DMA]),
        compiler_params=_params(("arbitrary",), VMEM_LIMIT),
        name="dispatch_rows",
    )(dest_flat, cnt, starts, h2p)


def _ffn1_kernel(ri_ref, ro_ref, e_ref, j_ref, first_ref, valid_ref, slot_ref, ne_ref, nj_ref, more_ref,
                 x_ref, w_ref, b_ref, o_ref, stage, wbf, sem, *, F, tn):
    q = pl.program_id(0)

    def wcopy(e, j, slot, part):
        c0 = pl.multiple_of(part * F + j * tn, tn)
        return pltpu.make_async_copy(w_ref.at[e, :, pl.ds(c0, tn)], stage.at[slot, part],
                                     sem.at[slot])

    @pl.when(q == 0)
    def _():
        for part in range(2):
            wcopy(e_ref[0], j_ref[0], 0, part).start(priority=WEIGHT_DMA_PRIORITY)

    @pl.when(first_ref[q] == 1)
    def _():
        slot = slot_ref[q]
        for part in range(2):
            wcopy(e_ref[q], j_ref[q], slot, part).wait()

        @pl.when(more_ref[q] == 1)
        def _():
            for part in range(2):
                wcopy(ne_ref[q], nj_ref[q], 1 - slot, part).start(priority=WEIGHT_DMA_PRIORITY)

    def swiglu_block(w_gate, w_lin):
        xb = _unpack_rows(x_ref[...])
        g = jnp.dot(xb, w_gate, preferred_element_type=F32) + b_ref[0, 0, 0]
        lin = jnp.dot(xb, w_lin, preferred_element_type=F32) + b_ref[0, 1, 0]
        g = jnp.minimum(g, SWIGLU_LIMIT)
        lin = jnp.clip(lin, -SWIGLU_LIMIT, SWIGLU_LIMIT)
        act = g / (1.0 + jnp.exp(-SWIGLU_ALPHA * g)) * (lin + 1.0)
        o_ref[...] = act.astype(o_ref.dtype)

    @pl.when(first_ref[q] == 1)
    def _():
        slot = slot_ref[q]
        w16 = [stage[slot, part].astype(BF16) for part in range(2)]
        for part in range(2):
            wbf[part] = w16[part]
        swiglu_block(w16[0], w16[1])

    @pl.when(jnp.logical_and(valid_ref[q] == 1, first_ref[q] == 0))
    def _():
        swiglu_block(wbf[0], wbf[1])

    @pl.when(valid_ref[q] == 0)
    def _():
        o_ref[...] = jnp.zeros_like(o_ref)


def _ffn2_kernel(ri_ref, ro_ref, e_ref, j_ref, first_ref, valid_ref, slot_ref, ne_ref, nj_ref, more_ref,
                 a_ref, w_ref, b_ref, o_ref, stage, wbf, sem):
    q = pl.program_id(0)

    def wcopy(e, slot):
        return pltpu.make_async_copy(w_ref.at[e], stage.at[slot], sem.at[slot])

    @pl.when(q == 0)
    def _():
        wcopy(e_ref[0], 0).start(priority=WEIGHT_DMA_PRIORITY)

    @pl.when(first_ref[q] == 1)
    def _():
        slot = slot_ref[q]
        wcopy(e_ref[q], slot).wait()

        @pl.when(more_ref[q] == 1)
        def _():
            wcopy(ne_ref[q], 1 - slot).start(priority=WEIGHT_DMA_PRIORITY)

    @pl.when(first_ref[q] == 1)
    def _():
        w16 = stage[slot_ref[q]].astype(BF16)
        wbf[...] = w16
        y = jnp.dot(a_ref[...], w16, preferred_element_type=F32) + b_ref[0]
        o_ref[...] = _pack_rows(y)

    @pl.when(jnp.logical_and(valid_ref[q] == 1, first_ref[q] == 0))
    def _():
        y = jnp.dot(a_ref[...], wbf[...], preferred_element_type=F32) + b_ref[0]
        o_ref[...] = _pack_rows(y)

    @pl.when(valid_ref[q] == 0)
    def _():
        o_ref[...] = jnp.zeros_like(o_ref)


def _work_items(cnt, n_col_tiles, n_blocks):
    n_exp = cnt.shape[0]
    nblk = (cnt + EXPERT_ROWS - 1) // EXPERT_ROWS
    bstart = jnp.cumsum(nblk) - nblk
    gsize = jnp.repeat(nblk, n_col_tiles)
    gend = jnp.cumsum(gsize)
    n_groups = n_exp * n_col_tiles
    gid = jnp.arange(n_groups, dtype=I32)
    total = gend[-1]
    q = jnp.arange(n_blocks * n_col_tiles, dtype=I32)
    qc = jnp.minimum(q, total - 1)
    g = jnp.sum((gend[None, :] <= qc[:, None]).astype(I32), axis=1)
    nonempty = gsize > 0
    ordinal = jnp.cumsum(nonempty.astype(I32)) - 1
    nxt_incl = lax.cummin(jnp.where(nonempty, gid, n_groups), reverse=True)
    nxt = jnp.concatenate([nxt_incl[1:], jnp.full((1,), n_groups, I32)])
    more = nxt < n_groups
    nxt = jnp.minimum(nxt, n_groups - 1)
    per_group = jnp.stack([gend - gsize, gid // n_col_tiles, gid % n_col_tiles,
                           jnp.repeat(bstart, n_col_tiles), ordinal % 2,
                           nxt // n_col_tiles, nxt % n_col_tiles, more.astype(I32)])
    pick = (g[None, :, None] == gid[None, None, :]).astype(I32)
    gstart, e, j, brow, slot, ne, nj, more = jnp.sum(pick * per_group[:, None, :], axis=2)
    r = qc - gstart
    valid = q < total
    first = jnp.logical_and(valid, r == 0)
    over = q - total
    row_in = brow + r
    row_out = jnp.where(valid, row_in, jnp.sum(nblk) + over // n_col_tiles)
    col_out = jnp.where(valid, j, over % n_col_tiles)
    as_i32 = lambda a: a.astype(I32)
    return tuple(map(as_i32, (row_in, row_out, e, col_out, first, valid, slot, ne, nj, more)))


def _ffn1(items, x_pad, w_exp_in, b_exp_in):
    P, W = x_pad.shape
    n_exp, D, F2 = w_exp_in.shape
    F = F2 // 2
    tn = 1024
    nj = F // tn
    n_items = items[0].shape[0]
    bias = b_exp_in.reshape(n_exp, 2, nj, 1, tn)
    return pl.pallas_call(
        functools.partial(_ffn1_kernel, F=F, tn=tn),
        out_shape=jax.ShapeDtypeStruct((P, F), BF16),
        grid_spec=pltpu.PrefetchScalarGridSpec(
            num_scalar_prefetch=10,
            grid=(n_items,),
            in_specs=[pl.BlockSpec((EXPERT_ROWS, W), lambda q, ri, *_: (ri[q], 0)),
                      pl.BlockSpec(memory_space=pl.ANY),
                      pl.BlockSpec((1, 2, 1, 1, tn),
                                   lambda q, ri, ro, e, j, *_: (e[q], 0, j[q], 0, 0))],
            out_specs=pl.BlockSpec((EXPERT_ROWS, tn),
                                   lambda q, ri, ro, e, j, *_: (ro[q], j[q])),
            scratch_shapes=[pltpu.VMEM((2, 2, D, tn), F32),
                            pltpu.VMEM((2, D, tn), BF16),
                            pltpu.SemaphoreType.DMA((2,))]),
        compiler_params=_params(("arbitrary",), VMEM_LIMIT),
        name="expert_in_swiglu",
    )(*items, x_pad, w_exp_in, bias)


def _ffn2(items, act, w_exp_out, b_exp_out):
    P, F = act.shape
    n_exp, _, D = w_exp_out.shape
    n_items = items[0].shape[0]
    return pl.pallas_call(
        _ffn2_kernel,
        out_shape=jax.ShapeDtypeStruct((P, D // 2), U32),
        grid_spec=pltpu.PrefetchScalarGridSpec(
            num_scalar_prefetch=10,
            grid=(n_items,),
            in_specs=[pl.BlockSpec((EXPERT_ROWS, F), lambda q, ri, *_: (ri[q], 0)),
                      pl.BlockSpec(memory_space=pl.ANY),
                      pl.BlockSpec((1, 1, D), lambda q, ri, ro, e, *_: (e[q], 0, 0))],
            out_specs=pl.BlockSpec((EXPERT_ROWS, D // 2), lambda q, ri, ro, *_: (ro[q], 0)),
            scratch_shapes=[pltpu.VMEM((2, F, D), F32),
                            pltpu.VMEM((F, D), BF16),
                            pltpu.SemaphoreType.DMA((2,))]),
        compiler_params=_params(("arbitrary",), VMEM_LIMIT),
        name="expert_out",
    )(*items, act, w_exp_out, b_exp_out[:, None, :])


def _combine_kernel(dest_ref, y_ref, gates_ref, x1_ref, g2_ref, o_ref, buf, sem, *, tm):
    s = pl.program_id(0)
    ns = pl.num_programs(0)

    def gather(step, slot):
        def issue(tt, _):
            t0 = pl.multiple_of(tt * SUBLANES, SUBLANES)
            for r in range(SUBLANES):
                for k in range(TOP_K):
                    d = dest_ref[TOP_K * (step * tm + t0 + r) + k]
                    pltpu.make_async_copy(y_ref.at[pl.ds(d, 1), :],
                                          buf.at[slot, k, pl.ds(t0 + r, 1), :],
                                          sem.at[slot]).start(priority=k % 2)
            return 0

        lax.fori_loop(0, tm // SUBLANES, issue, 0)

    @pl.when(s == 0)
    def _():
        gather(0, 0)

    @pl.when(s + 1 < ns)
    def _():
        gather(s + 1, (s + 1) % 2)

    slot = s % 2
    for k in range(TOP_K):
        pltpu.make_async_copy(y_ref.at[pl.ds(0, tm), :], buf.at[slot, k], sem.at[slot]).wait()
    gates = gates_ref[...]
    y_hi = y_lo = None
    for k in range(TOP_K):
        hi, lo = _unpack_halves(buf[slot, k])
        g = gates[:, k:k + 1]
        y_hi = g * hi if y_hi is None else y_hi + g * hi
        y_lo = g * lo if y_lo is None else y_lo + g * lo
    y = jnp.concatenate([y_hi, y_lo], axis=1)
    o_ref[...] = x1_ref[...] + g2_ref[0] * y


def _combine(dest_flat, y_pad, gates_wide, x1, gate2, S):
    T, D = x1.shape
    tm = 128
    per_b = S // tm
    return pl.pallas_call(
        functools.partial(_combine_kernel, tm=tm),
        out_shape=jax.ShapeDtypeStruct((T, D), F32),
        grid_spec=pltpu.PrefetchScalarGridSpec(
            num_scalar_prefetch=1,
            grid=(T // tm,),
            in_specs=[pl.BlockSpec(memory_space=pl.ANY),
                      pl.BlockSpec((tm, LANES), lambda i, d: (i, 0)),
                      pl.BlockSpec((tm, D), lambda i, d: (i, 0)),
                      pl.BlockSpec((1, 1, D), lambda i, d: (i // per_b, 0, 0))],
            out_specs=pl.BlockSpec((tm, D), lambda i, d: (i, 0)),
            scratch_shapes=[pltpu.VMEM((2, TOP_K, tm, D // 2), U32),
                            pltpu.SemaphoreType.DMA((2,))]),
        compiler_params=_params(("arbitrary",), VMEM_LIMIT),
        name="combine_rows",
    )(dest_flat, y_pad, gates_wide, x1, gate2[:, None, :])


def kernel(x, c, norm1_w, norm2_w, w_ada, b_ada, w_in, q_norm_w, k_norm_w, w_pool, pool_scale,
           w_o, w_router, b_router, w_exp_in, b_exp_in, w_exp_out, b_exp_out):
    B, S, D = x.shape
    T = B * S
    depth = w_ada.shape[0]
    n_exp = w_router.shape[-1]
    pool_width = pool_scale.shape[-1]
    sb_width = w_o.shape[1] - pool_width
    n_heads = sb_width // HEAD_DIM
    n_blocks = (T * TOP_K + n_exp * (EXPERT_ROWS - 1)) // EXPERT_ROWS
    n_rows = n_blocks * EXPERT_ROWS

    x2 = x.reshape(T, D)
    for l in range(depth):
        mod = _adaln(c, w_ada[l], b_ada[l])
        shift1, scale1, gate1, shift2, scale2, gate2 = jnp.split(mod, 6, axis=-1)

        proj = _inproj(x2, norm1_w[l], shift1, scale1, w_in[l].astype(BF16), S)
        proj3 = proj.reshape(B, S, -1)
        o_sb = _attention(proj3, q_norm_w[l], k_norm_w[l], n_heads)
        o_pool = _pool(proj3, w_pool[l], pool_scale[l], pool_width)
        x1, h2p, gates_wide, idx_wide = _outproj(
            o_sb.reshape(T, sb_width), o_pool.reshape(T, pool_width), w_o[l].astype(BF16),
            x2, gate1, shift2, scale2, norm2_w[l], w_router[l], b_router[l], S)

        dest_wide, meta = _route(idx_wide)
        cnt = meta[0, :n_exp]
        starts = meta[1, :n_exp]
        dest_flat = dest_wide[:, :TOP_K].reshape(T * TOP_K)
        x_pad = _dispatch(dest_flat, cnt, starts, h2p, n_rows)

        F = w_exp_out.shape[2]
        act = _ffn1(_work_items(cnt, F // 1024, n_blocks), x_pad, w_exp_in[l], b_exp_in[l])
        y_pad = _ffn2(_work_items(cnt, 1, n_blocks), act, w_exp_out[l], b_exp_out[l])
        x2 = _combine(dest_flat, y_pad, gates_wide, x1, gate2, S)
    return x2.reshape(B, S, D)
```

```python
import functools
import math

import jax
import jax.numpy as jnp
from jax import lax
from jax.experimental import pallas as pl
from jax.experimental.pallas import tpu as pltpu

F32 = jnp.float32
BF16 = jnp.bfloat16
I32 = jnp.int32
U32 = jnp.uint32

EPS = 1e-6
HEAD_DIM = 128
POOL_WINDOWS = (2, 4, 8, 16)
TOP_K = 4
SWIGLU_ALPHA = 1.702
SWIGLU_LIMIT = 7.0

LANES = 128
SUBLANES = 8
EXPERT_ROWS = 256
LOG_UNDERFLOW = 104.0
VMEM_LIMIT = 56 * 1024 * 1024
WEIGHT_DMA_PRIORITY = 1


def _params(sem=None, vmem=None):
    return pltpu.CompilerParams(dimension_semantics=sem, vmem_limit_bytes=vmem)


_HIGH_HALF = 0xFFFF0000


def _pack_rows(v):
    bits = lax.bitcast_convert_type(v.astype(BF16).astype(F32), U32)
    half = v.shape[1] // 2
    return (bits[:, :half] & jnp.uint32(_HIGH_HALF)) | (bits[:, half:] >> 16)


def _unpack_halves(p):
    hi = lax.bitcast_convert_type(p & jnp.uint32(_HIGH_HALF), F32)
    lo = lax.bitcast_convert_type(p << 16, F32)
    return hi, lo


def _unpack_rows(p):
    hi, lo = _unpack_halves(p)
    return jnp.concatenate([hi.astype(BF16), lo.astype(BF16)], axis=1)


def _adaln_kernel(c_ref, w_ref, b_ref, o_ref):
    c = c_ref[...]
    ca = c / (1.0 + jnp.exp(-c))
    o_ref[...] = jnp.dot(ca.astype(BF16), w_ref[...].astype(BF16),
                         preferred_element_type=F32) + b_ref[...]


def _adaln(c, w_ada, b_ada):
    B, D = c.shape
    N = w_ada.shape[1]
    rows = 8
    tn = 1024
    cp = jnp.zeros((rows, D), F32).at[:B].set(c)
    out = pl.pallas_call(
        _adaln_kernel,
        out_shape=jax.ShapeDtypeStruct((rows, N), F32),
        grid=(N // tn,),
        in_specs=[pl.BlockSpec((rows, D), lambda j: (0, 0)),
                  pl.BlockSpec((D, tn), lambda j: (0, j)),
                  pl.BlockSpec((1, tn), lambda j: (0, j))],
        out_specs=pl.BlockSpec((rows, tn), lambda j: (0, j)),
        compiler_params=_params(("arbitrary",), VMEM_LIMIT),
        name="adaln",
    )(cp, w_ada, b_ada.reshape(1, N))
    return out[:B]


def _inproj_kernel(x_ref, nw_ref, sh_ref, sc_ref, w_ref, o_ref, h_ref, *, tm, ch):
    @pl.when(pl.program_id(1) == 0)
    def _():
        mul = nw_ref[...] * (1.0 + sc_ref[0])
        add = sh_ref[0]

        def body(c, _):
            r0 = pl.multiple_of(c * ch, ch)
            x = x_ref[pl.ds(r0, ch), :]
            inv = lax.rsqrt(jnp.mean(x * x, axis=-1, keepdims=True) + EPS)
            h_ref[pl.ds(r0, ch), :] = (x * inv * mul + add).astype(BF16)
            return 0

        lax.fori_loop(0, tm // ch, body, 0)

    o_ref[...] = jnp.dot(h_ref[...], w_ref[...],
                         preferred_element_type=F32).astype(o_ref.dtype)


def _inproj(x2, norm_w, shift, scale, w_bf, S):
    T, D = x2.shape
    N = w_bf.shape[1]
    tm, tn, ch = 1024, 1024, 128
    per_b = S // tm
    return pl.pallas_call(
        functools.partial(_inproj_kernel, tm=tm, ch=ch),
        out_shape=jax.ShapeDtypeStruct((T, N), BF16),
        grid=(T // tm, N // tn),
        in_specs=[pl.BlockSpec((tm, D), lambda i, j: (i, 0)),
                  pl.BlockSpec((1, D), lambda i, j: (0, 0)),
                  pl.BlockSpec((1, 1, D), lambda i, j: (i // per_b, 0, 0)),
                  pl.BlockSpec((1, 1, D), lambda i, j: (i // per_b, 0, 0)),
                  pl.BlockSpec((D, tn), lambda i, j: (0, j))],
        out_specs=pl.BlockSpec((tm, tn), lambda i, j: (i, j)),
        scratch_shapes=[pltpu.VMEM((tm, D), BF16)],
        compiler_params=_params(("arbitrary", "arbitrary"), VMEM_LIMIT),
        name="inproj",
    )(x2, norm_w.reshape(1, D), shift[:, None, :], scale[:, None, :], w_bf)


def _attn_kernel(q_ref, k_ref, v_ref, qw_ref, kw_ref, o_ref, kn_ref, carry_ref, acc_ref,
                 *, S, tq, hg, scale):
    i = pl.program_id(2)
    d = HEAD_DIM

    def head_norm(x, w):
        parts = []
        for h in range(hg):
            xh = x[:, h * d:(h + 1) * d]
            inv = lax.rsqrt(jnp.mean(xh * xh, axis=-1, keepdims=True) + EPS)
            parts.append(xh * inv * w)
        return parts

    @pl.when(i == 0)
    def _():
        def body(c, _):
            r0 = pl.multiple_of(c * tq, tq)
            parts = head_norm(k_ref[0, pl.ds(r0, tq), :].astype(F32), kw_ref[...])
            for h in range(hg):
                kn_ref[pl.ds(r0, tq), h * d:(h + 1) * d] = parts[h].astype(BF16)
            return 0

        lax.fori_loop(0, S // tq, body, 0)

    qb = [(p * scale).astype(BF16) for p in head_norm(q_ref[0].astype(F32), qw_ref[...])]

    row = lax.broadcasted_iota(I32, (tq, tq), 0)
    col = lax.broadcasted_iota(I32, (tq, tq), 1)
    causal = col < row
    tri = (row > col).astype(BF16)

    def scores(h, r0, mask):
        kblk = kn_ref[pl.ds(r0, tq), h * d:(h + 1) * d]
        z = lax.dot_general(qb[h], kblk, (((1,), (1,)), ((), ())), preferred_element_type=F32)
        t = jnp.log(1.0 + jnp.exp(-jnp.abs(z)))
        lsn = jnp.minimum(-z, 0.0) - t
        lsp = jnp.minimum(z, 0.0) - t
        if mask:
            lsn = jnp.where(causal, lsn, 0.0)
        hi = lsn.astype(BF16)
        lo = (lsn - hi.astype(F32)).astype(BF16)
        later = (jnp.dot(hi, tri, preferred_element_type=F32)
                 + jnp.dot(lo, tri, preferred_element_type=F32))
        return lsp + later, later[:, :1] + lsn[:, :1]

    def weighted(a, h, r0):
        vblk = v_ref[0, pl.ds(r0, tq), h * d:(h + 1) * d]
        return jnp.dot(a.astype(BF16), vblk, preferred_element_type=F32)

    has_prev = i > 0
    rd = pl.multiple_of(i * tq, tq)
    rp = pl.multiple_of(jnp.maximum(i - 1, 0) * tq, tq)
    worst = None
    for h in range(hg):
        cols = slice(h * d, (h + 1) * d)
        log_d, sum_d = scores(h, rd, True)
        log_p, sum_p = scores(h, rp, False)
        a_d = jnp.where(causal, jnp.exp(log_d), 0.0)
        a_p = jnp.where(has_prev, jnp.exp(log_p + sum_d), 0.0)
        acc_ref[:, cols] = weighted(a_d, h, rd) + weighted(a_p, h, rp)
        carry = jnp.where(has_prev, sum_d + sum_p, sum_d)
        carry_ref[h] = carry
        m = jnp.max(carry)
        worst = m if worst is None else jnp.maximum(worst, m)

    def earlier(kb):
        r0 = pl.multiple_of(kb * tq, tq)
        worst = None
        for h in range(hg):
            cols = slice(h * d, (h + 1) * d)
            log_a, row_sum = scores(h, r0, False)
            acc_ref[:, cols] += weighted(jnp.exp(log_a + carry_ref[h]), h, r0)
            carry = carry_ref[h] + row_sum
            carry_ref[h] = carry
            m = jnp.max(carry)
            worst = m if worst is None else jnp.maximum(worst, m)
        return worst

    def cond(st):
        kb, m = st
        return jnp.logical_and(kb >= 0, m > -LOG_UNDERFLOW)

    def body(st):
        kb, _ = st
        return kb - 1, earlier(kb)

    lax.while_loop(cond, body, (i - 2, worst))
    o_ref[0] = acc_ref[...].astype(o_ref.dtype)


def _attention(proj3, q_norm_w, k_norm_w, n_heads):
    B, S, _ = proj3.shape
    d = HEAD_DIM
    tq = 256
    hg = 4
    G = n_heads // hg
    w = hg * d
    return pl.pallas_call(
        functools.partial(_attn_kernel, S=S, tq=tq, hg=hg, scale=1.0 / math.sqrt(d)),
        out_shape=jax.ShapeDtypeStruct((B, S, n_heads * d), BF16),
        grid=(B, G, S // tq),
        in_specs=[pl.BlockSpec((1, tq, w), lambda b, g, i: (b, i, g)),
                  pl.BlockSpec((1, S, w), lambda b, g, i: (b, 0, G + g)),
                  pl.BlockSpec((1, S, w), lambda b, g, i: (b, 0, 2 * G + g)),
                  pl.BlockSpec((1, d), lambda b, g, i: (0, 0)),
                  pl.BlockSpec((1, d), lambda b, g, i: (0, 0))],
        out_specs=pl.BlockSpec((1, tq, w), lambda b, g, i: (b, i, g)),
        scratch_shapes=[pltpu.VMEM((S, w), BF16),
                        pltpu.VMEM((hg, tq, 1), F32),
                        pltpu.VMEM((tq, w), F32)],
        compiler_params=_params(("arbitrary", "arbitrary", "arbitrary"), VMEM_LIMIT),
        name="stickbreak_attn",
    )(proj3, proj3, proj3, q_norm_w.reshape(1, d), k_norm_w.reshape(1, d))


def _pool_kernel(u_ref, w_ref, ps_ref, o_ref, *, S, ch, gd):
    halo = 16
    for g, win in enumerate(POOL_WINDOWS):
        lo, hi = g * gd, (g + 1) * gd
        wg = w_ref[g].astype(BF16)
        sc = ps_ref[:, lo:hi]

        def body(c, _, win=win, lo=lo, hi=hi, wg=wg, sc=sc):
            r0 = pl.multiple_of(c * ch, ch)
            cur = u_ref[0, pl.ds(r0, ch), lo:hi].astype(F32)
            p0 = pl.multiple_of(jnp.maximum(r0 - halo, 0), halo)
            prev = u_ref[0, pl.ds(p0, halo), lo:hi].astype(F32)
            prev = jnp.where(c > 0, prev, 0.0)
            s = jnp.concatenate([prev, cur], axis=0)
            n = 1
            while n < win:
                s = s + pltpu.roll(s, n, 0)
                n *= 2
            s = s[halo:]
            t = r0 + lax.broadcasted_iota(I32, (ch, 1), 0)
            cnt = jnp.minimum(t + 1, win).astype(F32)
            p = s / cnt - cur
            y = jnp.dot(p.astype(BF16), wg, preferred_element_type=F32) * sc
            o_ref[0, pl.ds(r0, ch), lo:hi] = y.astype(o_ref.dtype)
            return 0

        lax.fori_loop(0, S // ch, body, 0)


def _pool(proj3, w_pool, pool_scale, pool_width):
    B, S, NP = proj3.shape
    G, gd, _ = w_pool.shape
    return pl.pallas_call(
        functools.partial(_pool_kernel, S=S, ch=256, gd=gd),
        out_shape=jax.ShapeDtypeStruct((B, S, pool_width), BF16),
        grid=(B,),
        in_specs=[pl.BlockSpec((1, S, pool_width), lambda b: (b, 0, NP // pool_width - 1)),
                  pl.BlockSpec((G, gd, gd), lambda b: (0, 0, 0)),
                  pl.BlockSpec((1, pool_width), lambda b: (0, 0))],
        out_specs=pl.BlockSpec((1, S, pool_width), lambda b: (b, 0, 0)),
        compiler_params=_params(("arbitrary",), VMEM_LIMIT),
        name="pool_mixer",
    )(proj3, w_pool, pool_scale.reshape(1, pool_width))


def _outproj_kernel(osb_ref, opool_ref, wo_ref, x_ref, g1_ref, sh_ref, sc_ref, nw_ref,
                    wr_ref, br_ref, x1_ref, h2p_ref, gates_ref, idx_ref, *, sbw, n_exp, sub):
    for r0 in range(0, x_ref.shape[0], sub):
        rows = slice(r0, r0 + sub)
        _outproj_rows(osb_ref.at[rows], opool_ref.at[rows], wo_ref, x_ref.at[rows], g1_ref,
                      sh_ref, sc_ref, nw_ref, wr_ref, br_ref, x1_ref.at[rows], h2p_ref.at[rows],
                      gates_ref.at[rows], idx_ref.at[rows], sbw=sbw, n_exp=n_exp)


def _outproj_rows(osb_ref, opool_ref, wo_ref, x_ref, g1_ref, sh_ref, sc_ref, nw_ref,
                  wr_ref, br_ref, x1_ref, h2p_ref, gates_ref, idx_ref, *, sbw, n_exp):
    tm, D = x_ref.shape
    mixed = (jnp.dot(osb_ref[...], wo_ref[:sbw, :], preferred_element_type=F32)
             + jnp.dot(opool_ref[...], wo_ref[sbw:, :], preferred_element_type=F32))
    x1 = x_ref[...] + g1_ref[0] * mixed
    x1_ref[...] = x1
    inv = lax.rsqrt(jnp.mean(x1 * x1, axis=-1, keepdims=True) + EPS)
    h2 = x1 * inv * (nw_ref[...] * (1.0 + sc_ref[0])) + sh_ref[0]
    hb = h2.astype(BF16)
    h2p_ref[...] = _pack_rows(h2)

    lane = lax.broadcasted_iota(I32, (tm, LANES), 1)
    lanef = lane.astype(F32)
    logits = jnp.dot(hb, wr_ref[...].astype(BF16), preferred_element_type=F32) + br_ref[...]
    vals = jnp.where(lane < n_exp, logits, -jnp.inf)
    tops, ids = [], []
    for _ in range(TOP_K):
        m = jnp.max(vals, axis=-1, keepdims=True)
        first = jnp.min(jnp.where(vals == m, lanef, float(LANES)), axis=-1, keepdims=True)
        tops.append(m)
        ids.append(first)
        vals = jnp.where(lanef == first, -jnp.inf, vals)
    es = [jnp.exp(m - tops[0]) for m in tops]
    den = es[0]
    for e in es[1:]:
        den = den + e
    gates = jnp.zeros((tm, LANES), F32)
    idx = jnp.zeros((tm, LANES), F32)
    for k in range(TOP_K):
        gates = jnp.where(lane == k, es[k] / den, gates)
        idx = jnp.where(lane == k, ids[k], idx)
    gates_ref[...] = gates
    idx_ref[...] = idx.astype(I32)


def _outproj(o_sb, o_pool, wo_bf, x2, gate1, shift2, scale2, norm2_w, w_router, b_router, S):
    T, D = x2.shape
    sbw = o_sb.shape[1]
    pw = o_pool.shape[1]
    n_exp = w_router.shape[1]
    tm, sub = 512, 256
    per_b = S // tm
    wr = jnp.zeros((D, LANES), F32).at[:, :n_exp].set(w_router)
    br = jnp.zeros((1, LANES), F32).at[0, :n_exp].set(b_router)
    mod_spec = pl.BlockSpec((1, 1, D), lambda i: (i // per_b, 0, 0))
    return pl.pallas_call(
        functools.partial(_outproj_kernel, sbw=sbw, n_exp=n_exp, sub=sub),
        out_shape=(jax.ShapeDtypeStruct((T, D), F32),
                   jax.ShapeDtypeStruct((T, D // 2), U32),
                   jax.ShapeDtypeStruct((T, LANES), F32),
                   jax.ShapeDtypeStruct((T, LANES), I32)),
        grid=(T // tm,),
        in_specs=[pl.BlockSpec((tm, sbw), lambda i: (i, 0)),
                  pl.BlockSpec((tm, pw), lambda i: (i, 0)),
                  pl.BlockSpec((sbw + pw, D), lambda i: (0, 0)),
                  pl.BlockSpec((tm, D), lambda i: (i, 0)),
                  mod_spec, mod_spec, mod_spec,
                  pl.BlockSpec((1, D), lambda i: (0, 0)),
                  pl.BlockSpec((D, LANES), lambda i: (0, 0)),
                  pl.BlockSpec((1, LANES), lambda i: (0, 0))],
        out_specs=(pl.BlockSpec((tm, D), lambda i: (i, 0)),
                   pl.BlockSpec((tm, D // 2), lambda i: (i, 0)),
                   pl.BlockSpec((tm, LANES), lambda i: (i, 0)),
                   pl.BlockSpec((tm, LANES), lambda i: (i, 0))),
        compiler_params=_params(("arbitrary",), VMEM_LIMIT),
        name="outproj_router",
    )(o_sb, o_pool, wo_bf, x2, gate1[:, None, :], shift2[:, None, :], scale2[:, None, :],
      norm2_w.reshape(1, D), wr, br)


def _route_kernel(idx_ref, dest_ref, meta_ref, rank_ref, *, T, ch):
    lane = lax.broadcasted_iota(I32, (ch, LANES), 1)
    row = lax.broadcasted_iota(I32, (ch, ch), 0)
    col = lax.broadcasted_iota(I32, (ch, ch), 1)
    before = (col < row).astype(BF16)

    def load(c):
        return idx_ref[pl.ds(pl.multiple_of(c * ch, ch), ch), :]

    def count(c, cnt):
        ii = load(c)
        member = lane == ii[:, 0:1]
        for k in range(1, TOP_K):
            member = jnp.logical_or(member, lane == ii[:, k:k + 1])
        mf = jnp.where(member, 1.0, 0.0)
        rank = jnp.dot(before, mf.astype(BF16), preferred_element_type=F32) + cnt
        rank_ref[pl.ds(pl.multiple_of(c * ch, ch), ch), :] = rank
        return cnt + jnp.sum(mf, axis=0, keepdims=True)

    cnt = lax.fori_loop(0, T // ch, count, jnp.zeros((1, LANES), F32))
    padded = jnp.ceil(cnt / EXPERT_ROWS) * EXPERT_ROWS
    rows = 8
    lane8 = lax.broadcasted_iota(I32, (rows, LANES), 1)
    ends = jnp.broadcast_to(padded, (rows, LANES))
    sh = 1
    while sh < LANES:
        ends = ends + jnp.where(lane8 >= sh, pltpu.roll(ends, sh, 1), 0.0)
        sh *= 2
    starts = ends - padded
    sub8 = lax.broadcasted_iota(I32, (rows, LANES), 0)
    meta = jnp.where(sub8 == 0, cnt, jnp.where(sub8 == 1, starts, padded))
    meta_ref[...] = meta.astype(I32)
    start_row = starts[0:1, :]

    def place(c, _):
        ii = load(c)
        val = rank_ref[pl.ds(pl.multiple_of(c * ch, ch), ch), :] + start_row
        out = jnp.zeros((ch, LANES), F32)
        for k in range(TOP_K):
            d = jnp.sum(jnp.where(lane == ii[:, k:k + 1], val, 0.0), axis=-1, keepdims=True)
            out = jnp.where(lane == k, d, out)
        dest_ref[pl.ds(pl.multiple_of(c * ch, ch), ch), :] = out.astype(I32)
        return 0

    lax.fori_loop(0, T // ch, place, 0)


def _route(idx_wide):
    T = idx_wide.shape[0]
    return pl.pallas_call(
        functools.partial(_route_kernel, T=T, ch=256),
        out_shape=(jax.ShapeDtypeStruct((T, LANES), I32),
                   jax.ShapeDtypeStruct((8, LANES), I32)),
        grid=(1,),
        in_specs=[pl.BlockSpec((T, LANES), lambda i: (0, 0))],
        out_specs=(pl.BlockSpec((T, LANES), lambda i: (0, 0)),
                   pl.BlockSpec((8, LANES), lambda i: (0, 0))),
        scratch_shapes=[pltpu.VMEM((T, LANES), F32)],
        compiler_params=_params(("arbitrary",), VMEM_LIMIT),
        name="route_ranks",
    )(idx_wide)


def _dispatch_kernel(dest_ref, cnt_ref, start_ref, h_ref, x_ref, z_ref, sem, zsem,
                     *, tb, n_exp, n_blocks):
    s = pl.program_id(0)

    def issue(tt, _):
        t0 = pl.multiple_of(tt * SUBLANES, SUBLANES)
        for r in range(SUBLANES):
            for k in range(TOP_K):
                d = dest_ref[TOP_K * (s * tb + t0 + r) + k]
                pltpu.make_async_copy(h_ref.at[pl.ds(t0 + r, 1), :], x_ref.at[pl.ds(d, 1), :],
                                      sem).start(priority=k % 2)
        return 0

    lax.fori_loop(0, tb // SUBLANES, issue, 0)

    @pl.when(s == 0)
    def _():
        z_ref[...] = jnp.zeros_like(z_ref)
        _dispatch_zero_fill(cnt_ref, start_ref, x_ref, z_ref, zsem, n_exp, n_blocks)

    for _ in range(TOP_K):
        pltpu.make_async_copy(h_ref, x_ref.at[pl.ds(0, tb), :], sem).wait()


def _dispatch_zero_fill(cnt_ref, start_ref, x_ref, z_ref, zsem, n_exp, n_blocks):
    def zero_fill(e, wait):
        cnt = cnt_ref[e]
        npad = (-cnt) & (EXPERT_ROWS - 1)
        off = start_ref[e] + cnt

        def one(i, _):
            cp = pltpu.make_async_copy(z_ref.at[pl.ds(0, 1), :],
                                       x_ref.at[pl.ds(off + i, 1), :], zsem)
            if wait:
                cp.wait()
            else:
                cp.start()
            return 0

        lax.fori_loop(0, npad, one, 0)
        return 0

    used = (start_ref[n_exp - 1] + cnt_ref[n_exp - 1] + EXPERT_ROWS - 1) // EXPERT_ROWS

    def tail_fill(blk, wait):
        r0 = pl.multiple_of(blk * EXPERT_ROWS, EXPERT_ROWS)
        cp = pltpu.make_async_copy(z_ref, x_ref.at[pl.ds(r0, EXPERT_ROWS), :], zsem)
        if wait:
            cp.wait()
        else:
            cp.start()
        return 0

    lax.fori_loop(0, n_exp, lambda e, _: zero_fill(e, False), 0)
    lax.fori_loop(used, n_blocks, lambda b, _: tail_fill(b, False), 0)
    lax.fori_loop(0, n_exp, lambda e, _: zero_fill(e, True), 0)
    lax.fori_loop(used, n_blocks, lambda b, _: tail_fill(b, True), 0)


def _dispatch(dest_flat, cnt, starts, h2p, n_rows):
    T, W = h2p.shape
    n_exp = cnt.shape[0]
    tb = 512
    return pl.pallas_call(
        functools.partial(_dispatch_kernel, tb=tb, n_exp=n_exp, n_blocks=n_rows // EXPERT_ROWS),
        out_shape=jax.ShapeDtypeStruct((n_rows, W), U32),
        grid_spec=pltpu.PrefetchScalarGridSpec(
            num_scalar_prefetch=3,
            grid=(T // tb,),
            in_specs=[pl.BlockSpec((tb, W), lambda s, *_: (s, 0))],
            out_specs=pl.BlockSpec(memory_space=pl.ANY),
            scratch_shapes=[pltpu.VMEM((EXPERT_ROWS, W), U32),
                            pltpu.SemaphoreType.DMA, pltpu.SemaphoreType.DMA]),
        compiler_params=_params(("arbitrary",), VMEM_LIMIT),
        name="dispatch_rows",
    )(dest_flat, cnt, starts, h2p)


ROW_RING = 3


def _row_ring(src_ref, ri_ref, ring, rsem):
    q = pl.program_id(0)
    last = pl.num_programs(0) - 1

    def copy(item, slot):
        r0 = pl.multiple_of(ri_ref[item] * EXPERT_ROWS, EXPERT_ROWS)
        return pltpu.make_async_copy(src_ref.at[pl.ds(r0, EXPERT_ROWS), :], ring.at[slot],
                                     rsem.at[slot])

    @pl.when(q == 0)
    def _():
        for i in range(ROW_RING - 1):
            copy(i, i).start()

    ahead = q + ROW_RING - 1

    @pl.when(ahead <= last)
    def _():
        copy(ahead, ahead % ROW_RING).start()

    slot = q % ROW_RING
    copy(q, slot).wait()
    return slot


def _ffn1_kernel(ri_ref, ro_ref, e_ref, j_ref, first_ref, valid_ref, slot_ref, ne_ref, nj_ref, more_ref,
                 x_ref, w_ref, b_ref, o_ref, ring, rsem, stage, wbf, sem, *, F, tn):
    q = pl.program_id(0)
    rows = _row_ring(x_ref, ri_ref, ring, rsem)

    def wcopy(e, j, slot, part):
        c0 = pl.multiple_of(part * F + j * tn, tn)
        return pltpu.make_async_copy(w_ref.at[e, :, pl.ds(c0, tn)], stage.at[slot, part],
                                     sem.at[slot])

    @pl.when(q == 0)
    def _():
        for part in range(2):
            wcopy(e_ref[0], j_ref[0], 0, part).start(priority=WEIGHT_DMA_PRIORITY)

    @pl.when(first_ref[q] == 1)
    def _():
        slot = slot_ref[q]
        for part in range(2):
            wcopy(e_ref[q], j_ref[q], slot, part).wait()

        @pl.when(more_ref[q] == 1)
        def _():
            for part in range(2):
                wcopy(ne_ref[q], nj_ref[q], 1 - slot, part).start(priority=WEIGHT_DMA_PRIORITY)

    def swiglu_block(w_gate, w_lin):
        xb = _unpack_rows(ring[rows])
        g = jnp.dot(xb, w_gate, preferred_element_type=F32) + b_ref[0, 0, 0]
        lin = jnp.dot(xb, w_lin, preferred_element_type=F32) + b_ref[0, 1, 0]
        g = jnp.minimum(g, SWIGLU_LIMIT)
        lin = jnp.clip(lin, -SWIGLU_LIMIT, SWIGLU_LIMIT)
        act = g / (1.0 + jnp.exp(-SWIGLU_ALPHA * g)) * (lin + 1.0)
        o_ref[...] = act.astype(o_ref.dtype)

    @pl.when(first_ref[q] == 1)
    def _():
        slot = slot_ref[q]
        w16 = [stage[slot, part].astype(BF16) for part in range(2)]
        for part in range(2):
            wbf[part] = w16[part]
        swiglu_block(w16[0], w16[1])

    @pl.when(jnp.logical_and(valid_ref[q] == 1, first_ref[q] == 0))
    def _():
        swiglu_block(wbf[0], wbf[1])

    @pl.when(valid_ref[q] == 0)
    def _():
        o_ref[...] = jnp.zeros_like(o_ref)


def _ffn2_kernel(ri_ref, ro_ref, e_ref, j_ref, first_ref, valid_ref, slot_ref, ne_ref, nj_ref, more_ref,
                 a_ref, w_ref, b_ref, o_ref, ring, rsem, stage, wbf, sem):
    q = pl.program_id(0)
    rows = _row_ring(a_ref, ri_ref, ring, rsem)

    def wcopy(e, slot):
        return pltpu.make_async_copy(w_ref.at[e], stage.at[slot], sem.at[slot])

    @pl.when(q == 0)
    def _():
        wcopy(e_ref[0], 0).start(priority=WEIGHT_DMA_PRIORITY)

    @pl.when(first_ref[q] == 1)
    def _():
        slot = slot_ref[q]
        wcopy(e_ref[q], slot).wait()

        @pl.when(more_ref[q] == 1)
        def _():
            wcopy(ne_ref[q], 1 - slot).start(priority=WEIGHT_DMA_PRIORITY)

    @pl.when(first_ref[q] == 1)
    def _():
        w16 = stage[slot_ref[q]].astype(BF16)
        wbf[...] = w16
        y = jnp.dot(ring[rows],w16, preferred_element_type=F32) + b_ref[0]
        o_ref[...] = _pack_rows(y)

    @pl.when(jnp.logical_and(valid_ref[q] == 1, first_ref[q] == 0))
    def _():
        y = jnp.dot(ring[rows],wbf[...], preferred_element_type=F32) + b_ref[0]
        o_ref[...] = _pack_rows(y)

    @pl.when(valid_ref[q] == 0)
    def _():
        o_ref[...] = jnp.zeros_like(o_ref)


def _work_items(cnt, n_col_tiles, n_blocks):
    n_exp = cnt.shape[0]
    nblk = (cnt + EXPERT_ROWS - 1) // EXPERT_ROWS
    bstart = jnp.cumsum(nblk) - nblk
    gsize = jnp.repeat(nblk, n_col_tiles)
    gend = jnp.cumsum(gsize)
    n_groups = n_exp * n_col_tiles
    gid = jnp.arange(n_groups, dtype=I32)
    total = gend[-1]
    q = jnp.arange(n_blocks * n_col_tiles, dtype=I32)
    qc = jnp.minimum(q, total - 1)
    g = jnp.sum((gend[None, :] <= qc[:, None]).astype(I32), axis=1)
    nonempty = gsize > 0
    ordinal = jnp.cumsum(nonempty.astype(I32)) - 1
    nxt_incl = lax.cummin(jnp.where(nonempty, gid, n_groups), reverse=True)
    nxt = jnp.concatenate([nxt_incl[1:], jnp.full((1,), n_groups, I32)])
    more = nxt < n_groups
    nxt = jnp.minimum(nxt, n_groups - 1)
    per_group = jnp.stack([gend - gsize, gid // n_col_tiles, gid % n_col_tiles,
                           jnp.repeat(bstart, n_col_tiles), ordinal % 2,
                           nxt // n_col_tiles, nxt % n_col_tiles, more.astype(I32)])
    pick = (g[None, :, None] == gid[None, None, :]).astype(I32)
    gstart, e, j, brow, slot, ne, nj, more = jnp.sum(pick * per_group[:, None, :], axis=2)
    r = qc - gstart
    valid = q < total
    first = jnp.logical_and(valid, r == 0)
    over = q - total
    row_in = brow + r
    row_out = jnp.where(valid, row_in, jnp.sum(nblk) + over // n_col_tiles)
    col_out = jnp.where(valid, j, over % n_col_tiles)
    as_i32 = lambda a: a.astype(I32)
    return tuple(map(as_i32, (row_in, row_out, e, col_out, first, valid, slot, ne, nj, more)))


def _ffn1(items, x_pad, w_exp_in, b_exp_in):
    P, W = x_pad.shape
    n_exp, D, F2 = w_exp_in.shape
    F = F2 // 2
    tn = 1024
    nj = F // tn
    n_items = items[0].shape[0]
    bias = b_exp_in.reshape(n_exp, 2, nj, 1, tn)
    return pl.pallas_call(
        functools.partial(_ffn1_kernel, F=F, tn=tn),
        out_shape=jax.ShapeDtypeStruct((P, F), BF16),
        grid_spec=pltpu.PrefetchScalarGridSpec(
            num_scalar_prefetch=10,
            grid=(n_items,),
            in_specs=[pl.BlockSpec(memory_space=pl.ANY),
                      pl.BlockSpec(memory_space=pl.ANY),
                      pl.BlockSpec((1, 2, 1, 1, tn),
                                   lambda q, ri, ro, e, j, *_: (e[q], 0, j[q], 0, 0))],
            out_specs=pl.BlockSpec((EXPERT_ROWS, tn),
                                   lambda q, ri, ro, e, j, *_: (ro[q], j[q])),
            scratch_shapes=[pltpu.VMEM((ROW_RING, EXPERT_ROWS, W), U32),
                            pltpu.SemaphoreType.DMA((ROW_RING,)),
                            pltpu.VMEM((2, 2, D, tn), F32),
                            pltpu.VMEM((2, D, tn), BF16),
                            pltpu.SemaphoreType.DMA((2,))]),
        compiler_params=_params(("arbitrary",), VMEM_LIMIT),
        name="expert_in_swiglu",
    )(*items, x_pad, w_exp_in, bias)


def _ffn2(items, act, w_exp_out, b_exp_out):
    P, F = act.shape
    n_exp, _, D = w_exp_out.shape
    n_items = items[0].shape[0]
    return pl.pallas_call(
        _ffn2_kernel,
        out_shape=jax.ShapeDtypeStruct((P, D // 2), U32),
        grid_spec=pltpu.PrefetchScalarGridSpec(
            num_scalar_prefetch=10,
            grid=(n_items,),
            in_specs=[pl.BlockSpec(memory_space=pl.ANY),
                      pl.BlockSpec(memory_space=pl.ANY),
                      pl.BlockSpec((1, 1, D), lambda q, ri, ro, e, *_: (e[q], 0, 0))],
            out_specs=pl.BlockSpec((EXPERT_ROWS, D // 2), lambda q, ri, ro, *_: (ro[q], 0)),
            scratch_shapes=[pltpu.VMEM((ROW_RING, EXPERT_ROWS, F), BF16),
                            pltpu.SemaphoreType.DMA((ROW_RING,)),
                            pltpu.VMEM((2, F, D), F32),
                            pltpu.VMEM((F, D), BF16),
                            pltpu.SemaphoreType.DMA((2,))]),
        compiler_params=_params(("arbitrary",), VMEM_LIMIT),
        name="expert_out",
    )(*items, act, w_exp_out, b_exp_out[:, None, :])


def _combine_kernel(dest_ref, y_ref, gates_ref, x1_ref, g2_ref, o_ref, buf, sem, *, tm):
    s = pl.program_id(0)
    ns = pl.num_programs(0)

    def gather(step, slot):
        def issue(tt, _):
            t0 = pl.multiple_of(tt * SUBLANES, SUBLANES)
            for r in range(SUBLANES):
                for k in range(TOP_K):
                    d = dest_ref[TOP_K * (step * tm + t0 + r) + k]
                    pltpu.make_async_copy(y_ref.at[pl.ds(d, 1), :],
                                          buf.at[slot, k, pl.ds(t0 + r, 1), :],
                                          sem.at[slot]).start(priority=k % 2)
            return 0

        lax.fori_loop(0, tm // SUBLANES, issue, 0)

    @pl.when(s == 0)
    def _():
        gather(0, 0)

    @pl.when(s + 1 < ns)
    def _():
        gather(s + 1, (s + 1) % 2)

    slot = s % 2
    for k in range(TOP_K):
        pltpu.make_async_copy(y_ref.at[pl.ds(0, tm), :], buf.at[slot, k], sem.at[slot]).wait()
    gates = gates_ref[...]
    y_hi = y_lo = None
    for k in range(TOP_K):
        hi, lo = _unpack_halves(buf[slot, k])
        g = gates[:, k:k + 1]
        y_hi = g * hi if y_hi is None else y_hi + g * hi
        y_lo = g * lo if y_lo is None else y_lo + g * lo
    y = jnp.concatenate([y_hi, y_lo], axis=1)
    o_ref[...] = x1_ref[...] + g2_ref[0] * y


def _combine(dest_flat, y_pad, gates_wide, x1, gate2, S):
    T, D = x1.shape
    tm = 128
    per_b = S // tm
    return pl.pallas_call(
        functools.partial(_combine_kernel, tm=tm),
        out_shape=jax.ShapeDtypeStruct((T, D), F32),
        grid_spec=pltpu.PrefetchScalarGridSpec(
            num_scalar_prefetch=1,
            grid=(T // tm,),
            in_specs=[pl.BlockSpec(memory_space=pl.ANY),
                      pl.BlockSpec((tm, LANES), lambda i, d: (i, 0)),
                      pl.BlockSpec((tm, D), lambda i, d: (i, 0)),
                      pl.BlockSpec((1, 1, D), lambda i, d: (i // per_b, 0, 0))],
            out_specs=pl.BlockSpec((tm, D), lambda i, d: (i, 0)),
            scratch_shapes=[pltpu.VMEM((2, TOP_K, tm, D // 2), U32),
                            pltpu.SemaphoreType.DMA((2,))]),
        compiler_params=_params(("arbitrary",), VMEM_LIMIT),
        name="combine_rows",
    )(dest_flat, y_pad, gates_wide, x1, gate2[:, None, :])


def kernel(x, c, norm1_w, norm2_w, w_ada, b_ada, w_in, q_norm_w, k_norm_w, w_pool, pool_scale,
           w_o, w_router, b_router, w_exp_in, b_exp_in, w_exp_out, b_exp_out):
    B, S, D = x.shape
    T = B * S
    depth = w_ada.shape[0]
    n_exp = w_router.shape[-1]
    pool_width = pool_scale.shape[-1]
    sb_width = w_o.shape[1] - pool_width
    n_heads = sb_width // HEAD_DIM
    n_blocks = (T * TOP_K + n_exp * (EXPERT_ROWS - 1)) // EXPERT_ROWS
    n_rows = n_blocks * EXPERT_ROWS

    x2 = x.reshape(T, D)
    for l in range(depth):
        mod = _adaln(c, w_ada[l], b_ada[l])
        shift1, scale1, gate1, shift2, scale2, gate2 = jnp.split(mod, 6, axis=-1)

        proj = _inproj(x2, norm1_w[l], shift1, scale1, w_in[l].astype(BF16), S)
        proj3 = proj.reshape(B, S, -1)
        o_sb = _attention(proj3, q_norm_w[l], k_norm_w[l], n_heads)
        o_pool = _pool(proj3, w_pool[l], pool_scale[l], pool_width)
        x1, h2p, gates_wide, idx_wide = _outproj(
            o_sb.reshape(T, sb_width), o_pool.reshape(T, pool_width), w_o[l].astype(BF16),
            x2, gate1, shift2, scale2, norm2_w[l], w_router[l], b_router[l], S)

        dest_wide, meta = _route(idx_wide)
        cnt = meta[0, :n_exp]
        starts = meta[1, :n_exp]
        dest_flat = dest_wide[:, :TOP_K].reshape(T * TOP_K)
        x_pad = _dispatch(dest_flat, cnt, starts, h2p, n_rows)

        F = w_exp_out.shape[2]
        act = _ffn1(_work_items(cnt, F // 1024, n_blocks), x_pad, w_exp_in[l], b_exp_in[l])
        y_pad = _ffn2(_work_items(cnt, 1, n_blocks), act, w_exp_out[l], b_exp_out[l])
        x2 = _combine(dest_flat, y_pad, gates_wide, x1, gate2, S)
    return x2.reshape(B, S, D)
```

```python
import functools
import math

import jax
import jax.numpy as jnp
from jax import lax
from jax.experimental import pallas as pl
from jax.experimental.pallas import tpu as pltpu

F32 = jnp.float32
BF16 = jnp.bfloat16
I32 = jnp.int32
U32 = jnp.uint32

EPS = 1e-6
HEAD_DIM = 128
POOL_WINDOWS = (2, 4, 8, 16)
TOP_K = 4
SWIGLU_ALPHA = 1.702
SWIGLU_LIMIT = 7.0

LANES = 128
SUBLANES = 8
EXPERT_ROWS = 256
LOG_UNDERFLOW = 104.0
VMEM_LIMIT = 56 * 1024 * 1024
WEIGHT_DMA_PRIORITY = 1


def _params(sem=None, vmem=None):
    return pltpu.CompilerParams(dimension_semantics=sem, vmem_limit_bytes=vmem)


_HIGH_HALF = 0xFFFF0000


def _pack_rows(v):
    bits = lax.bitcast_convert_type(v.astype(BF16).astype(F32), U32)
    half = v.shape[1] // 2
    return (bits[:, :half] & jnp.uint32(_HIGH_HALF)) | (bits[:, half:] >> 16)


def _unpack_halves(p):
    hi = lax.bitcast_convert_type(p & jnp.uint32(_HIGH_HALF), F32)
    lo = lax.bitcast_convert_type(p << 16, F32)
    return hi, lo


def _unpack_rows(p):
    hi, lo = _unpack_halves(p)
    return jnp.concatenate([hi.astype(BF16), lo.astype(BF16)], axis=1)


def _adaln_kernel(c_ref, w_ref, b_ref, o_ref):
    c = c_ref[...]
    ca = c / (1.0 + jnp.exp(-c))
    o_ref[...] = jnp.dot(ca.astype(BF16), w_ref[...].astype(BF16),
                         preferred_element_type=F32) + b_ref[...]


def _adaln(c, w_ada, b_ada):
    B, D = c.shape
    N = w_ada.shape[1]
    rows = 8
    tn = 1024
    cp = jnp.zeros((rows, D), F32).at[:B].set(c)
    out = pl.pallas_call(
        _adaln_kernel,
        out_shape=jax.ShapeDtypeStruct((rows, N), F32),
        grid=(N // tn,),
        in_specs=[pl.BlockSpec((rows, D), lambda j: (0, 0)),
                  pl.BlockSpec((D, tn), lambda j: (0, j)),
                  pl.BlockSpec((1, tn), lambda j: (0, j))],
        out_specs=pl.BlockSpec((rows, tn), lambda j: (0, j)),
        compiler_params=_params(("arbitrary",), VMEM_LIMIT),
        name="adaln",
    )(cp, w_ada, b_ada.reshape(1, N))
    return out[:B]


def _inproj_kernel(x_ref, nw_ref, sh_ref, sc_ref, w_ref, o_ref, h_ref, *, tm, ch):
    @pl.when(pl.program_id(1) == 0)
    def _():
        mul = nw_ref[...] * (1.0 + sc_ref[0])
        add = sh_ref[0]

        def body(c, _):
            r0 = pl.multiple_of(c * ch, ch)
            x = x_ref[pl.ds(r0, ch), :]
            inv = lax.rsqrt(jnp.mean(x * x, axis=-1, keepdims=True) + EPS)
            h_ref[pl.ds(r0, ch), :] = (x * inv * mul + add).astype(BF16)
            return 0

        lax.fori_loop(0, tm // ch, body, 0)

    o_ref[...] = jnp.dot(h_ref[...], w_ref[...],
                         preferred_element_type=F32).astype(o_ref.dtype)


def _inproj(x2, norm_w, shift, scale, w_bf, S):
    T, D = x2.shape
    N = w_bf.shape[1]
    tm, tn, ch = 1024, 1024, 128
    per_b = S // tm
    return pl.pallas_call(
        functools.partial(_inproj_kernel, tm=tm, ch=ch),
        out_shape=jax.ShapeDtypeStruct((T, N), BF16),
        grid=(T // tm, N // tn),
        in_specs=[pl.BlockSpec((tm, D), lambda i, j: (i, 0)),
                  pl.BlockSpec((1, D), lambda i, j: (0, 0)),
                  pl.BlockSpec((1, 1, D), lambda i, j: (i // per_b, 0, 0)),
                  pl.BlockSpec((1, 1, D), lambda i, j: (i // per_b, 0, 0)),
                  pl.BlockSpec((D, tn), lambda i, j: (0, j))],
        out_specs=pl.BlockSpec((tm, tn), lambda i, j: (i, j)),
        scratch_shapes=[pltpu.VMEM((tm, D), BF16)],
        compiler_params=_params(("arbitrary", "arbitrary"), VMEM_LIMIT),
        name="inproj",
    )(x2, norm_w.reshape(1, D), shift[:, None, :], scale[:, None, :], w_bf)


def _attn_kernel(q_ref, k_ref, v_ref, qw_ref, kw_ref, o_ref, kn_ref, carry_ref, acc_ref,
                 *, S, tq, hg, scale):
    i = pl.program_id(2)
    d = HEAD_DIM

    def head_norm(x, w):
        parts = []
        for h in range(hg):
            xh = x[:, h * d:(h + 1) * d]
            inv = lax.rsqrt(jnp.mean(xh * xh, axis=-1, keepdims=True) + EPS)
            parts.append(xh * inv * w)
        return parts

    @pl.when(i == 0)
    def _():
        def body(c, _):
            r0 = pl.multiple_of(c * tq, tq)
            parts = head_norm(k_ref[0, pl.ds(r0, tq), :].astype(F32), kw_ref[...])
            for h in range(hg):
                kn_ref[pl.ds(r0, tq), h * d:(h + 1) * d] = parts[h].astype(BF16)
            return 0

        lax.fori_loop(0, S // tq, body, 0)

    qb = [(p * scale).astype(BF16) for p in head_norm(q_ref[0].astype(F32), qw_ref[...])]

    row = lax.broadcasted_iota(I32, (tq, tq), 0)
    col = lax.broadcasted_iota(I32, (tq, tq), 1)
    causal = col < row
    tri = (row > col).astype(BF16)

    def scores(h, r0, mask):
        kblk = kn_ref[pl.ds(r0, tq), h * d:(h + 1) * d]
        z = lax.dot_general(qb[h], kblk, (((1,), (1,)), ((), ())), preferred_element_type=F32)
        t = jnp.log(1.0 + jnp.exp(-jnp.abs(z)))
        lsn = jnp.minimum(-z, 0.0) - t
        lsp = jnp.minimum(z, 0.0) - t
        if mask:
            lsn = jnp.where(causal, lsn, 0.0)
        later = jnp.dot(lsn.astype(BF16), tri, preferred_element_type=F32)
        return lsp + later, later[:, :1] + lsn[:, :1]

    def weighted(a, h, r0):
        vblk = v_ref[0, pl.ds(r0, tq), h * d:(h + 1) * d]
        return jnp.dot(a.astype(BF16), vblk, preferred_element_type=F32)

    has_prev = i > 0
    rd = pl.multiple_of(i * tq, tq)
    rp = pl.multiple_of(jnp.maximum(i - 1, 0) * tq, tq)
    worst = None
    for h in range(hg):
        cols = slice(h * d, (h + 1) * d)
        log_d, sum_d = scores(h, rd, True)
        log_p, sum_p = scores(h, rp, False)
        a_d = jnp.where(causal, jnp.exp(log_d), 0.0)
        a_p = jnp.where(has_prev, jnp.exp(log_p + sum_d), 0.0)
        acc_ref[:, cols] = weighted(a_d, h, rd) + weighted(a_p, h, rp)
        carry = jnp.where(has_prev, sum_d + sum_p, sum_d)
        carry_ref[h] = carry
        m = jnp.max(carry)
        worst = m if worst is None else jnp.maximum(worst, m)

    def earlier(kb):
        r0 = pl.multiple_of(kb * tq, tq)
        worst = None
        for h in range(hg):
            cols = slice(h * d, (h + 1) * d)
            log_a, row_sum = scores(h, r0, False)
            acc_ref[:, cols] += weighted(jnp.exp(log_a + carry_ref[h]), h, r0)
            carry = carry_ref[h] + row_sum
            carry_ref[h] = carry
            m = jnp.max(carry)
            worst = m if worst is None else jnp.maximum(worst, m)
        return worst

    def cond(st):
        kb, m = st
        return jnp.logical_and(kb >= 0, m > -LOG_UNDERFLOW)

    def body(st):
        kb, _ = st
        return kb - 1, earlier(kb)

    lax.while_loop(cond, body, (i - 2, worst))
    o_ref[0] = acc_ref[...].astype(o_ref.dtype)


def _attention(proj3, q_norm_w, k_norm_w, n_heads):
    B, S, _ = proj3.shape
    d = HEAD_DIM
    tq = 256
    hg = 4
    G = n_heads // hg
    w = hg * d
    return pl.pallas_call(
        functools.partial(_attn_kernel, S=S, tq=tq, hg=hg, scale=1.0 / math.sqrt(d)),
        out_shape=jax.ShapeDtypeStruct((B, S, n_heads * d), BF16),
        grid=(B, G, S // tq),
        in_specs=[pl.BlockSpec((1, tq, w), lambda b, g, i: (b, i, g)),
                  pl.BlockSpec((1, S, w), lambda b, g, i: (b, 0, G + g)),
                  pl.BlockSpec((1, S, w), lambda b, g, i: (b, 0, 2 * G + g)),
                  pl.BlockSpec((1, d), lambda b, g, i: (0, 0)),
                  pl.BlockSpec((1, d), lambda b, g, i: (0, 0))],
        out_specs=pl.BlockSpec((1, tq, w), lambda b, g, i: (b, i, g)),
        scratch_shapes=[pltpu.VMEM((S, w), BF16),
                        pltpu.VMEM((hg, tq, 1), F32),
                        pltpu.VMEM((tq, w), F32)],
        compiler_params=_params(("arbitrary", "arbitrary", "arbitrary"), VMEM_LIMIT),
        name="stickbreak_attn",
    )(proj3, proj3, proj3, q_norm_w.reshape(1, d), k_norm_w.reshape(1, d))


def _pool_kernel(u_ref, w_ref, ps_ref, o_ref, *, S, ch, gd):
    halo = 16
    for g, win in enumerate(POOL_WINDOWS):
        lo, hi = g * gd, (g + 1) * gd
        wg = w_ref[g].astype(BF16)
        sc = ps_ref[:, lo:hi]

        def body(c, _, win=win, lo=lo, hi=hi, wg=wg, sc=sc):
            r0 = pl.multiple_of(c * ch, ch)
            cur = u_ref[0, pl.ds(r0, ch), lo:hi].astype(F32)
            p0 = pl.multiple_of(jnp.maximum(r0 - halo, 0), halo)
            prev = u_ref[0, pl.ds(p0, halo), lo:hi].astype(F32)
            prev = jnp.where(c > 0, prev, 0.0)
            s = jnp.concatenate([prev, cur], axis=0)
            n = 1
            while n < win:
                s = s + pltpu.roll(s, n, 0)
                n *= 2
            s = s[halo:]
            t = r0 + lax.broadcasted_iota(I32, (ch, 1), 0)
            cnt = jnp.minimum(t + 1, win).astype(F32)
            p = s / cnt - cur
            y = jnp.dot(p.astype(BF16), wg, preferred_element_type=F32) * sc
            o_ref[0, pl.ds(r0, ch), lo:hi] = y.astype(o_ref.dtype)
            return 0

        lax.fori_loop(0, S // ch, body, 0)


def _pool(proj3, w_pool, pool_scale, pool_width):
    B, S, NP = proj3.shape
    G, gd, _ = w_pool.shape
    return pl.pallas_call(
        functools.partial(_pool_kernel, S=S, ch=256, gd=gd),
        out_shape=jax.ShapeDtypeStruct((B, S, pool_width), BF16),
        grid=(B,),
        in_specs=[pl.BlockSpec((1, S, pool_width), lambda b: (b, 0, NP // pool_width - 1)),
                  pl.BlockSpec((G, gd, gd), lambda b: (0, 0, 0)),
                  pl.BlockSpec((1, pool_width), lambda b: (0, 0))],
        out_specs=pl.BlockSpec((1, S, pool_width), lambda b: (b, 0, 0)),
        compiler_params=_params(("arbitrary",), VMEM_LIMIT),
        name="pool_mixer",
    )(proj3, w_pool, pool_scale.reshape(1, pool_width))


def _outproj_kernel(osb_ref, opool_ref, wo_ref, x_ref, g1_ref, sh_ref, sc_ref, nw_ref,
                    wr_ref, br_ref, x1_ref, h2p_ref, gates_ref, idx_ref, *, sbw, n_exp, sub):
    for r0 in range(0, x_ref.shape[0], sub):
        rows = slice(r0, r0 + sub)
        _outproj_rows(osb_ref.at[rows], opool_ref.at[rows], wo_ref, x_ref.at[rows], g1_ref,
                      sh_ref, sc_ref, nw_ref, wr_ref, br_ref, x1_ref.at[rows], h2p_ref.at[rows],
                      gates_ref.at[rows], idx_ref.at[rows], sbw=sbw, n_exp=n_exp)


def _outproj_rows(osb_ref, opool_ref, wo_ref, x_ref, g1_ref, sh_ref, sc_ref, nw_ref,
                  wr_ref, br_ref, x1_ref, h2p_ref, gates_ref, idx_ref, *, sbw, n_exp):
    tm, D = x_ref.shape
    mixed = (jnp.dot(osb_ref[...], wo_ref[:sbw, :], preferred_element_type=F32)
             + jnp.dot(opool_ref[...], wo_ref[sbw:, :], preferred_element_type=F32))
    x1 = x_ref[...] + g1_ref[0] * mixed
    x1_ref[...] = x1
    inv = lax.rsqrt(jnp.mean(x1 * x1, axis=-1, keepdims=True) + EPS)
    h2 = x1 * inv * (nw_ref[...] * (1.0 + sc_ref[0])) + sh_ref[0]
    hb = h2.astype(BF16)
    h2p_ref[...] = _pack_rows(h2)

    lane = lax.broadcasted_iota(I32, (tm, LANES), 1)
    lanef = lane.astype(F32)
    logits = jnp.dot(hb, wr_ref[...].astype(BF16), preferred_element_type=F32) + br_ref[...]
    vals = jnp.where(lane < n_exp, logits, -jnp.inf)
    tops, ids = [], []
    for _ in range(TOP_K):
        m = jnp.max(vals, axis=-1, keepdims=True)
        first = jnp.min(jnp.where(vals == m, lanef, float(LANES)), axis=-1, keepdims=True)
        tops.append(m)
        ids.append(first)
        vals = jnp.where(lanef == first, -jnp.inf, vals)
    es = [jnp.exp(m - tops[0]) for m in tops]
    den = es[0]
    for e in es[1:]:
        den = den + e
    gates = jnp.zeros((tm, LANES), F32)
    idx = jnp.zeros((tm, LANES), F32)
    for k in range(TOP_K):
        gates = jnp.where(lane == k, es[k] / den, gates)
        idx = jnp.where(lane == k, ids[k], idx)
    gates_ref[...] = gates
    idx_ref[...] = idx.astype(I32)


def _outproj(o_sb, o_pool, wo_bf, x2, gate1, shift2, scale2, norm2_w, w_router, b_router, S):
    T, D = x2.shape
    sbw = o_sb.shape[1]
    pw = o_pool.shape[1]
    n_exp = w_router.shape[1]
    tm, sub = 512, 256
    per_b = S // tm
    wr = jnp.zeros((D, LANES), F32).at[:, :n_exp].set(w_router)
    br = jnp.zeros((1, LANES), F32).at[0, :n_exp].set(b_router)
    mod_spec = pl.BlockSpec((1, 1, D), lambda i: (i // per_b, 0, 0))
    return pl.pallas_call(
        functools.partial(_outproj_kernel, sbw=sbw, n_exp=n_exp, sub=sub),
        out_shape=(jax.ShapeDtypeStruct((T, D), F32),
                   jax.ShapeDtypeStruct((T, D // 2), U32),
                   jax.ShapeDtypeStruct((T, LANES), F32),
                   jax.ShapeDtypeStruct((T, LANES), I32)),
        grid=(T // tm,),
        in_specs=[pl.BlockSpec((tm, sbw), lambda i: (i, 0)),
                  pl.BlockSpec((tm, pw), lambda i: (i, 0)),
                  pl.BlockSpec((sbw + pw, D), lambda i: (0, 0)),
                  pl.BlockSpec((tm, D), lambda i: (i, 0)),
                  mod_spec, mod_spec, mod_spec,
                  pl.BlockSpec((1, D), lambda i: (0, 0)),
                  pl.BlockSpec((D, LANES), lambda i: (0, 0)),
                  pl.BlockSpec((1, LANES), lambda i: (0, 0))],
        out_specs=(pl.BlockSpec((tm, D), lambda i: (i, 0)),
                   pl.BlockSpec((tm, D // 2), lambda i: (i, 0)),
                   pl.BlockSpec((tm, LANES), lambda i: (i, 0)),
                   pl.BlockSpec((tm, LANES), lambda i: (i, 0))),
        compiler_params=_params(("arbitrary",), VMEM_LIMIT),
        name="outproj_router",
    )(o_sb, o_pool, wo_bf, x2, gate1[:, None, :], shift2[:, None, :], scale2[:, None, :],
      norm2_w.reshape(1, D), wr, br)


def _route_kernel(idx_ref, dest_ref, meta_ref, rank_ref, *, T, ch):
    lane = lax.broadcasted_iota(I32, (ch, LANES), 1)
    row = lax.broadcasted_iota(I32, (ch, ch), 0)
    col = lax.broadcasted_iota(I32, (ch, ch), 1)
    before = (col < row).astype(BF16)

    def load(c):
        return idx_ref[pl.ds(pl.multiple_of(c * ch, ch), ch), :]

    def count(c, cnt):
        ii = load(c)
        member = lane == ii[:, 0:1]
        for k in range(1, TOP_K):
            member = jnp.logical_or(member, lane == ii[:, k:k + 1])
        mf = jnp.where(member, 1.0, 0.0)
        rank = jnp.dot(before, mf.astype(BF16), preferred_element_type=F32) + cnt
        rank_ref[pl.ds(pl.multiple_of(c * ch, ch), ch), :] = rank
        return cnt + jnp.sum(mf, axis=0, keepdims=True)

    cnt = lax.fori_loop(0, T // ch, count, jnp.zeros((1, LANES), F32))
    padded = jnp.ceil(cnt / EXPERT_ROWS) * EXPERT_ROWS
    rows = 8
    lane8 = lax.broadcasted_iota(I32, (rows, LANES), 1)
    ends = jnp.broadcast_to(padded, (rows, LANES))
    sh = 1
    while sh < LANES:
        ends = ends + jnp.where(lane8 >= sh, pltpu.roll(ends, sh, 1), 0.0)
        sh *= 2
    starts = ends - padded
    sub8 = lax.broadcasted_iota(I32, (rows, LANES), 0)
    meta = jnp.where(sub8 == 0, cnt, jnp.where(sub8 == 1, starts, padded))
    meta_ref[...] = meta.astype(I32)
    start_row = starts[0:1, :]

    def place(c, _):
        ii = load(c)
        val = rank_ref[pl.ds(pl.multiple_of(c * ch, ch), ch), :] + start_row
        out = jnp.zeros((ch, LANES), F32)
        for k in range(TOP_K):
            d = jnp.sum(jnp.where(lane == ii[:, k:k + 1], val, 0.0), axis=-1, keepdims=True)
            out = jnp.where(lane == k, d, out)
        dest_ref[pl.ds(pl.multiple_of(c * ch, ch), ch), :] = out.astype(I32)
        return 0

    lax.fori_loop(0, T // ch, place, 0)


def _route(idx_wide):
    T = idx_wide.shape[0]
    return pl.pallas_call(
        functools.partial(_route_kernel, T=T, ch=256),
        out_shape=(jax.ShapeDtypeStruct((T, LANES), I32),
                   jax.ShapeDtypeStruct((8, LANES), I32)),
        grid=(1,),
        in_specs=[pl.BlockSpec((T, LANES), lambda i: (0, 0))],
        out_specs=(pl.BlockSpec((T, LANES), lambda i: (0, 0)),
                   pl.BlockSpec((8, LANES), lambda i: (0, 0))),
        scratch_shapes=[pltpu.VMEM((T, LANES), F32)],
        compiler_params=_params(("arbitrary",), VMEM_LIMIT),
        name="route_ranks",
    )(idx_wide)


def _dispatch_kernel(dest_ref, cnt_ref, start_ref, h_ref, x_ref, z_ref, sem, zsem,
                     *, tb, n_exp, n_blocks):
    s = pl.program_id(0)

    def issue(tt, _):
        t0 = pl.multiple_of(tt * SUBLANES, SUBLANES)
        for r in range(SUBLANES):
            for k in range(TOP_K):
                d = dest_ref[TOP_K * (s * tb + t0 + r) + k]
                pltpu.make_async_copy(h_ref.at[pl.ds(t0 + r, 1), :], x_ref.at[pl.ds(d, 1), :],
                                      sem).start(priority=k % 2)
        return 0

    lax.fori_loop(0, tb // SUBLANES, issue, 0)

    @pl.when(s == 0)
    def _():
        z_ref[...] = jnp.zeros_like(z_ref)
        _dispatch_zero_fill(cnt_ref, start_ref, x_ref, z_ref, zsem, n_exp, n_blocks)

    for _ in range(TOP_K):
        pltpu.make_async_copy(h_ref, x_ref.at[pl.ds(0, tb), :], sem).wait()


def _dispatch_zero_fill(cnt_ref, start_ref, x_ref, z_ref, zsem, n_exp, n_blocks):
    def zero_fill(e, wait):
        cnt = cnt_ref[e]
        npad = (-cnt) & (EXPERT_ROWS - 1)
        off = start_ref[e] + cnt

        def one(i, _):
            cp = pltpu.make_async_copy(z_ref.at[pl.ds(0, 1), :],
                                       x_ref.at[pl.ds(off + i, 1), :], zsem)
            if wait:
                cp.wait()
            else:
                cp.start()
            return 0

        lax.fori_loop(0, npad, one, 0)
        return 0

    used = (start_ref[n_exp - 1] + cnt_ref[n_exp - 1] + EXPERT_ROWS - 1) // EXPERT_ROWS

    def tail_fill(blk, wait):
        r0 = pl.multiple_of(blk * EXPERT_ROWS, EXPERT_ROWS)
        cp = pltpu.make_async_copy(z_ref, x_ref.at[pl.ds(r0, EXPERT_ROWS), :], zsem)
        if wait:
            cp.wait()
        else:
            cp.start()
        return 0

    lax.fori_loop(0, n_exp, lambda e, _: zero_fill(e, False), 0)
    lax.fori_loop(used, n_blocks, lambda b, _: tail_fill(b, False), 0)
    lax.fori_loop(0, n_exp, lambda e, _: zero_fill(e, True), 0)
    lax.fori_loop(used, n_blocks, lambda b, _: tail_fill(b, True), 0)


def _dispatch(dest_flat, cnt, starts, h2p, n_rows):
    T, W = h2p.shape
    n_exp = cnt.shape[0]
    tb = 1024
    return pl.pallas_call(
        functools.partial(_dispatch_kernel, tb=tb, n_exp=n_exp, n_blocks=n_rows // EXPERT_ROWS),
        out_shape=jax.ShapeDtypeStruct((n_rows, W), U32),
        grid_spec=pltpu.PrefetchScalarGridSpec(
            num_scalar_prefetch=3,
            grid=(T // tb,),
            in_specs=[pl.BlockSpec((tb, W), lambda s, *_: (s, 0))],
            out_specs=pl.BlockSpec(memory_space=pl.ANY),
            scratch_shapes=[pltpu.VMEM((EXPERT_ROWS, W), U32),
                            pltpu.SemaphoreType.DMA, pltpu.SemaphoreType.DMA]),
        compiler_params=_params(("arbitrary",), VMEM_LIMIT),
        name="dispatch_rows",
    )(dest_flat, cnt, starts, h2p)


def _ffn1_kernel(ri_ref, ro_ref, e_ref, j_ref, first_ref, valid_ref, slot_ref, ne_ref, nj_ref, more_ref, half_ref,
                 x_ref, w_ref, b_ref, o_ref, stage, wbf, sem, *, F, tn):
    q = pl.program_id(0)

    def wcopy(e, j, slot, part):
        c0 = pl.multiple_of(part * F + j * tn, tn)
        return pltpu.make_async_copy(w_ref.at[e, :, pl.ds(c0, tn)], stage.at[slot, part],
                                     sem.at[slot])

    @pl.when(q == 0)
    def _():
        for part in range(2):
            wcopy(e_ref[0], j_ref[0], 0, part).start(priority=WEIGHT_DMA_PRIORITY)

    @pl.when(first_ref[q] == 1)
    def _():
        slot = slot_ref[q]
        for part in range(2):
            wcopy(e_ref[q], j_ref[q], slot, part).wait()

        @pl.when(more_ref[q] == 1)
        def _():
            for part in range(2):
                wcopy(ne_ref[q], nj_ref[q], 1 - slot, part).start(priority=WEIGHT_DMA_PRIORITY)

    def swiglu_block(w_gate, w_lin, rows):
        xb = _unpack_rows(x_ref[:rows, :])
        g = jnp.dot(xb, w_gate, preferred_element_type=F32) + b_ref[0, 0, 0]
        lin = jnp.dot(xb, w_lin, preferred_element_type=F32) + b_ref[0, 1, 0]
        g = jnp.minimum(g, SWIGLU_LIMIT)
        lin = jnp.clip(lin, -SWIGLU_LIMIT, SWIGLU_LIMIT)
        act = g / (1.0 + jnp.exp(-SWIGLU_ALPHA * g)) * (lin + 1.0)
        o_ref[:rows, :] = act.astype(o_ref.dtype)
        if rows < EXPERT_ROWS:
            o_ref[rows:, :] = jnp.zeros((EXPERT_ROWS - rows, o_ref.shape[1]), o_ref.dtype)

    @pl.when(first_ref[q] == 1)
    def _():
        slot = slot_ref[q]
        w16 = [stage[slot, part].astype(BF16) for part in range(2)]
        for part in range(2):
            wbf[part] = w16[part]
        swiglu_block(w16[0], w16[1], EXPERT_ROWS)

    later = jnp.logical_and(valid_ref[q] == 1, first_ref[q] == 0)

    @pl.when(jnp.logical_and(later, half_ref[q] == 0))
    def _():
        swiglu_block(wbf[0], wbf[1], EXPERT_ROWS)

    @pl.when(jnp.logical_and(later, half_ref[q] == 1))
    def _():
        swiglu_block(wbf[0], wbf[1], EXPERT_ROWS // 2)

    @pl.when(valid_ref[q] == 0)
    def _():
        o_ref[...] = jnp.zeros_like(o_ref)


def _ffn2_kernel(ri_ref, ro_ref, e_ref, j_ref, first_ref, valid_ref, slot_ref, ne_ref, nj_ref, more_ref, half_ref,
                 a_ref, w_ref, b_ref, o_ref, stage, wbf, sem):
    q = pl.program_id(0)

    def wcopy(e, slot):
        return pltpu.make_async_copy(w_ref.at[e], stage.at[slot], sem.at[slot])

    @pl.when(q == 0)
    def _():
        wcopy(e_ref[0], 0).start(priority=WEIGHT_DMA_PRIORITY)

    @pl.when(first_ref[q] == 1)
    def _():
        slot = slot_ref[q]
        wcopy(e_ref[q], slot).wait()

        @pl.when(more_ref[q] == 1)
        def _():
            wcopy(ne_ref[q], 1 - slot).start(priority=WEIGHT_DMA_PRIORITY)

    def out_block(w, rows):
        y = jnp.dot(a_ref[:rows, :], w, preferred_element_type=F32) + b_ref[0]
        o_ref[:rows, :] = _pack_rows(y)
        if rows < EXPERT_ROWS:
            o_ref[rows:, :] = jnp.zeros((EXPERT_ROWS - rows, o_ref.shape[1]), o_ref.dtype)

    @pl.when(first_ref[q] == 1)
    def _():
        w16 = stage[slot_ref[q]].astype(BF16)
        wbf[...] = w16
        out_block(w16, EXPERT_ROWS)

    later = jnp.logical_and(valid_ref[q] == 1, first_ref[q] == 0)

    @pl.when(jnp.logical_and(later, half_ref[q] == 0))
    def _():
        out_block(wbf[...], EXPERT_ROWS)

    @pl.when(jnp.logical_and(later, half_ref[q] == 1))
    def _():
        out_block(wbf[...], EXPERT_ROWS // 2)

    @pl.when(valid_ref[q] == 0)
    def _():
        o_ref[...] = jnp.zeros_like(o_ref)


def _work_items(cnt, n_col_tiles, n_blocks):
    n_exp = cnt.shape[0]
    nblk = (cnt + EXPERT_ROWS - 1) // EXPERT_ROWS
    bstart = jnp.cumsum(nblk) - nblk
    gsize = jnp.repeat(nblk, n_col_tiles)
    gend = jnp.cumsum(gsize)
    n_groups = n_exp * n_col_tiles
    gid = jnp.arange(n_groups, dtype=I32)
    total = gend[-1]
    q = jnp.arange(n_blocks * n_col_tiles, dtype=I32)
    qc = jnp.minimum(q, total - 1)
    g = jnp.sum((gend[None, :] <= qc[:, None]).astype(I32), axis=1)
    nonempty = gsize > 0
    ordinal = jnp.cumsum(nonempty.astype(I32)) - 1
    nxt_incl = lax.cummin(jnp.where(nonempty, gid, n_groups), reverse=True)
    nxt = jnp.concatenate([nxt_incl[1:], jnp.full((1,), n_groups, I32)])
    more = nxt < n_groups
    nxt = jnp.minimum(nxt, n_groups - 1)
    per_group = jnp.stack([gend - gsize, gid // n_col_tiles, gid % n_col_tiles,
                           jnp.repeat(bstart, n_col_tiles), ordinal % 2,
                           nxt // n_col_tiles, nxt % n_col_tiles, more.astype(I32),
                           jnp.repeat(cnt, n_col_tiles)])
    pick = (g[None, :, None] == gid[None, None, :]).astype(I32)
    gstart, e, j, brow, slot, ne, nj, more, rows = jnp.sum(pick * per_group[:, None, :], axis=2)
    r = qc - gstart
    valid = q < total
    first = jnp.logical_and(valid, r == 0)
    half = jnp.logical_and(valid, rows - r * EXPERT_ROWS <= EXPERT_ROWS // 2)
    over = q - total
    row_in = brow + r
    row_out = jnp.where(valid, row_in, jnp.sum(nblk) + over // n_col_tiles)
    col_out = jnp.where(valid, j, over % n_col_tiles)
    as_i32 = lambda a: a.astype(I32)
    return tuple(map(as_i32, (row_in, row_out, e, col_out, first, valid, slot, ne, nj, more,
                              half)))


def _ffn1(items, x_pad, w_exp_in, b_exp_in):
    P, W = x_pad.shape
    n_exp, D, F2 = w_exp_in.shape
    F = F2 // 2
    tn = 1024
    nj = F // tn
    n_items = items[0].shape[0]
    bias = b_exp_in.reshape(n_exp, 2, nj, 1, tn)
    return pl.pallas_call(
        functools.partial(_ffn1_kernel, F=F, tn=tn),
        out_shape=jax.ShapeDtypeStruct((P, F), BF16),
        grid_spec=pltpu.PrefetchScalarGridSpec(
            num_scalar_prefetch=11,
            grid=(n_items,),
            in_specs=[pl.BlockSpec((EXPERT_ROWS, W), lambda q, ri, *_: (ri[q], 0)),
                      pl.BlockSpec(memory_space=pl.ANY),
                      pl.BlockSpec((1, 2, 1, 1, tn),
                                   lambda q, ri, ro, e, j, *_: (e[q], 0, j[q], 0, 0))],
            out_specs=pl.BlockSpec((EXPERT_ROWS, tn),
                                   lambda q, ri, ro, e, j, *_: (ro[q], j[q])),
            scratch_shapes=[pltpu.VMEM((2, 2, D, tn), F32),
                            pltpu.VMEM((2, D, tn), BF16),
                            pltpu.SemaphoreType.DMA((2,))]),
        compiler_params=_params(("arbitrary",), VMEM_LIMIT),
        name="expert_in_swiglu",
    )(*items, x_pad, w_exp_in, bias)


def _ffn2(items, act, w_exp_out, b_exp_out):
    P, F = act.shape
    n_exp, _, D = w_exp_out.shape
    n_items = items[0].shape[0]
    return pl.pallas_call(
        _ffn2_kernel,
        out_shape=jax.ShapeDtypeStruct((P, D // 2), U32),
        grid_spec=pltpu.PrefetchScalarGridSpec(
            num_scalar_prefetch=11,
            grid=(n_items,),
            in_specs=[pl.BlockSpec((EXPERT_ROWS, F), lambda q, ri, *_: (ri[q], 0)),
                      pl.BlockSpec(memory_space=pl.ANY),
                      pl.BlockSpec((1, 1, D), lambda q, ri, ro, e, *_: (e[q], 0, 0))],
            out_specs=pl.BlockSpec((EXPERT_ROWS, D // 2), lambda q, ri, ro, *_: (ro[q], 0)),
            scratch_shapes=[pltpu.VMEM((2, F, D), F32),
                            pltpu.VMEM((F, D), BF16),
                            pltpu.SemaphoreType.DMA((2,))]),
        compiler_params=_params(("arbitrary",), VMEM_LIMIT),
        name="expert_out",
    )(*items, act, w_exp_out, b_exp_out[:, None, :])


def _combine_kernel(dest_ref, y_ref, gates_ref, x1_ref, g2_ref, o_ref, buf, sem, *, tm):
    s = pl.program_id(0)
    ns = pl.num_programs(0)

    def gather(step, slot):
        def issue(tt, _):
            t0 = pl.multiple_of(tt * SUBLANES, SUBLANES)
            for r in range(SUBLANES):
                for k in range(TOP_K):
                    d = dest_ref[TOP_K * (step * tm + t0 + r) + k]
                    pltpu.make_async_copy(y_ref.at[pl.ds(d, 1), :],
                                          buf.at[slot, k, pl.ds(t0 + r, 1), :],
                                          sem.at[slot]).start(priority=k % 2)
            return 0

        lax.fori_loop(0, tm // SUBLANES, issue, 0)

    @pl.when(s == 0)
    def _():
        gather(0, 0)

    @pl.when(s + 1 < ns)
    def _():
        gather(s + 1, (s + 1) % 2)

    slot = s % 2
    for k in range(TOP_K):
        pltpu.make_async_copy(y_ref.at[pl.ds(0, tm), :], buf.at[slot, k], sem.at[slot]).wait()
    gates = gates_ref[...]
    y_hi = y_lo = None
    for k in range(TOP_K):
        hi, lo = _unpack_halves(buf[slot, k])
        g = gates[:, k:k + 1]
        y_hi = g * hi if y_hi is None else y_hi + g * hi
        y_lo = g * lo if y_lo is None else y_lo + g * lo
    y = jnp.concatenate([y_hi, y_lo], axis=1)
    o_ref[...] = x1_ref[...] + g2_ref[0] * y


def _combine(dest_flat, y_pad, gates_wide, x1, gate2, S):
    T, D = x1.shape
    tm = 128
    per_b = S // tm
    return pl.pallas_call(
        functools.partial(_combine_kernel, tm=tm),
        out_shape=jax.ShapeDtypeStruct((T, D), F32),
        grid_spec=pltpu.PrefetchScalarGridSpec(
            num_scalar_prefetch=1,
            grid=(T // tm,),
            in_specs=[pl.BlockSpec(memory_space=pl.ANY),
                      pl.BlockSpec((tm, LANES), lambda i, d: (i, 0)),
                      pl.BlockSpec((tm, D), lambda i, d: (i, 0)),
                      pl.BlockSpec((1, 1, D), lambda i, d: (i // per_b, 0, 0))],
            out_specs=pl.BlockSpec((tm, D), lambda i, d: (i, 0)),
            scratch_shapes=[pltpu.VMEM((2, TOP_K, tm, D // 2), U32),
                            pltpu.SemaphoreType.DMA((2,))]),
        compiler_params=_params(("arbitrary",), VMEM_LIMIT),
        name="combine_rows",
    )(dest_flat, y_pad, gates_wide, x1, gate2[:, None, :])


def kernel(x, c, norm1_w, norm2_w, w_ada, b_ada, w_in, q_norm_w, k_norm_w, w_pool, pool_scale,
           w_o, w_router, b_router, w_exp_in, b_exp_in, w_exp_out, b_exp_out):
    B, S, D = x.shape
    T = B * S
    depth = w_ada.shape[0]
    n_exp = w_router.shape[-1]
    pool_width = pool_scale.shape[-1]
    sb_width = w_o.shape[1] - pool_width
    n_heads = sb_width // HEAD_DIM
    n_blocks = (T * TOP_K + n_exp * (EXPERT_ROWS - 1)) // EXPERT_ROWS
    n_rows = n_blocks * EXPERT_ROWS

    x2 = x.reshape(T, D)
    for l in range(depth):
        mod = _adaln(c, w_ada[l], b_ada[l])
        shift1, scale1, gate1, shift2, scale2, gate2 = jnp.split(mod, 6, axis=-1)

        proj = _inproj(x2, norm1_w[l], shift1, scale1, w_in[l].astype(BF16), S)
        proj3 = proj.reshape(B, S, -1)
        o_sb = _attention(proj3, q_norm_w[l], k_norm_w[l], n_heads)
        o_pool = _pool(proj3, w_pool[l], pool_scale[l], pool_width)
        x1, h2p, gates_wide, idx_wide = _outproj(
            o_sb.reshape(T, sb_width), o_pool.reshape(T, pool_width), w_o[l].astype(BF16),
            x2, gate1, shift2, scale2, norm2_w[l], w_router[l], b_router[l], S)

        dest_wide, meta = _route(idx_wide)
        cnt = meta[0, :n_exp]
        starts = meta[1, :n_exp]
        dest_flat = dest_wide[:, :TOP_K].reshape(T * TOP_K)
        x_pad = _dispatch(dest_flat, cnt, starts, h2p, n_rows)

        F = w_exp_out.shape[2]
        act = _ffn1(_work_items(cnt, F // 1024, n_blocks), x_pad, w_exp_in[l], b_exp_in[l])
        y_pad = _ffn2(_work_items(cnt, 1, n_blocks), act, w_exp_out[l], b_exp_out[l])
        x2 = _combine(dest_flat, y_pad, gates_wide, x1, gate2, S)
    return x2.reshape(B, S, D)
```

```python
import functools
import math

import jax
import jax.numpy as jnp
from jax import lax
from jax.experimental import pallas as pl
from jax.experimental.pallas import tpu as pltpu

F32 = jnp.float32
BF16 = jnp.bfloat16
I32 = jnp.int32
U32 = jnp.uint32

EPS = 1e-6
HEAD_DIM = 128
POOL_WINDOWS = (2, 4, 8, 16)
TOP_K = 4
SWIGLU_ALPHA = 1.702
SWIGLU_LIMIT = 7.0

LANES = 128
SUBLANES = 8
EXPERT_ROWS = 256
TAIL_ROWS = 64
MIN_TAIL_PIECES = 2
LOG_UNDERFLOW = 104.0
VMEM_LIMIT = 56 * 1024 * 1024
WEIGHT_DMA_PRIORITY = 1


def _params(sem=None, vmem=None):
    return pltpu.CompilerParams(dimension_semantics=sem, vmem_limit_bytes=vmem)


_HIGH_HALF = 0xFFFF0000


def _pack_rows(v):
    bits = lax.bitcast_convert_type(v.astype(BF16).astype(F32), U32)
    half = v.shape[1] // 2
    return (bits[:, :half] & jnp.uint32(_HIGH_HALF)) | (bits[:, half:] >> 16)


def _unpack_halves(p):
    hi = lax.bitcast_convert_type(p & jnp.uint32(_HIGH_HALF), F32)
    lo = lax.bitcast_convert_type(p << 16, F32)
    return hi, lo


def _unpack_rows(p):
    hi, lo = _unpack_halves(p)
    return jnp.concatenate([hi.astype(BF16), lo.astype(BF16)], axis=1)


def _adaln_kernel(c_ref, w_ref, b_ref, o_ref):
    c = c_ref[...]
    ca = c / (1.0 + jnp.exp(-c))
    o_ref[...] = jnp.dot(ca.astype(BF16), w_ref[...].astype(BF16),
                         preferred_element_type=F32) + b_ref[...]


def _adaln(c, w_ada, b_ada):
    B, D = c.shape
    N = w_ada.shape[1]
    rows = 8
    tn = 1024
    cp = jnp.zeros((rows, D), F32).at[:B].set(c)
    out = pl.pallas_call(
        _adaln_kernel,
        out_shape=jax.ShapeDtypeStruct((rows, N), F32),
        grid=(N // tn,),
        in_specs=[pl.BlockSpec((rows, D), lambda j: (0, 0)),
                  pl.BlockSpec((D, tn), lambda j: (0, j)),
                  pl.BlockSpec((1, tn), lambda j: (0, j))],
        out_specs=pl.BlockSpec((rows, tn), lambda j: (0, j)),
        compiler_params=_params(("arbitrary",), VMEM_LIMIT),
        name="adaln",
    )(cp, w_ada, b_ada.reshape(1, N))
    return out[:B]


def _inproj_kernel(x_ref, nw_ref, sh_ref, sc_ref, w_ref, o_ref, h_ref, *, tm, ch):
    @pl.when(pl.program_id(1) == 0)
    def _():
        mul = nw_ref[...] * (1.0 + sc_ref[0])
        add = sh_ref[0]

        def body(c, _):
            r0 = pl.multiple_of(c * ch, ch)
            x = x_ref[pl.ds(r0, ch), :]
            inv = lax.rsqrt(jnp.mean(x * x, axis=-1, keepdims=True) + EPS)
            h_ref[pl.ds(r0, ch), :] = (x * inv * mul + add).astype(BF16)
            return 0

        lax.fori_loop(0, tm // ch, body, 0)

    o_ref[...] = jnp.dot(h_ref[...], w_ref[...],
                         preferred_element_type=F32).astype(o_ref.dtype)


def _inproj(x2, norm_w, shift, scale, w_bf, S):
    T, D = x2.shape
    N = w_bf.shape[1]
    tm, tn, ch = 1024, 1024, 128
    per_b = S // tm
    return pl.pallas_call(
        functools.partial(_inproj_kernel, tm=tm, ch=ch),
        out_shape=jax.ShapeDtypeStruct((T, N), BF16),
        grid=(T // tm, N // tn),
        in_specs=[pl.BlockSpec((tm, D), lambda i, j: (i, 0)),
                  pl.BlockSpec((1, D), lambda i, j: (0, 0)),
                  pl.BlockSpec((1, 1, D), lambda i, j: (i // per_b, 0, 0)),
                  pl.BlockSpec((1, 1, D), lambda i, j: (i // per_b, 0, 0)),
                  pl.BlockSpec((D, tn), lambda i, j: (0, j))],
        out_specs=pl.BlockSpec((tm, tn), lambda i, j: (i, j)),
        scratch_shapes=[pltpu.VMEM((tm, D), BF16)],
        compiler_params=_params(("arbitrary", "arbitrary"), VMEM_LIMIT),
        name="inproj",
    )(x2, norm_w.reshape(1, D), shift[:, None, :], scale[:, None, :], w_bf)


def _attn_kernel(q_ref, k_ref, v_ref, qw_ref, kw_ref, o_ref, kn_ref, carry_ref, acc_ref,
                 *, S, tq, hg, scale):
    i = pl.program_id(2)
    d = HEAD_DIM

    def head_norm(x, w):
        parts = []
        for h in range(hg):
            xh = x[:, h * d:(h + 1) * d]
            inv = lax.rsqrt(jnp.mean(xh * xh, axis=-1, keepdims=True) + EPS)
            parts.append(xh * inv * w)
        return parts

    @pl.when(i == 0)
    def _():
        def body(c, _):
            r0 = pl.multiple_of(c * tq, tq)
            parts = head_norm(k_ref[0, pl.ds(r0, tq), :].astype(F32), kw_ref[...])
            for h in range(hg):
                kn_ref[pl.ds(r0, tq), h * d:(h + 1) * d] = parts[h].astype(BF16)
            return 0

        lax.fori_loop(0, S // tq, body, 0)

    qb = [(p * scale).astype(BF16) for p in head_norm(q_ref[0].astype(F32), qw_ref[...])]

    row = lax.broadcasted_iota(I32, (tq, tq), 0)
    col = lax.broadcasted_iota(I32, (tq, tq), 1)
    causal = col < row
    tri = (row > col).astype(BF16)

    def scores(h, r0, mask):
        kblk = kn_ref[pl.ds(r0, tq), h * d:(h + 1) * d]
        z = lax.dot_general(qb[h], kblk, (((1,), (1,)), ((), ())), preferred_element_type=F32)
        t = jnp.log(1.0 + jnp.exp(-jnp.abs(z)))
        lsn = jnp.minimum(-z, 0.0) - t
        lsp = lsn + z
        if mask:
            lsn = jnp.where(causal, lsn, 0.0)
        later = jnp.dot(lsn.astype(BF16), tri, preferred_element_type=F32)
        return lsp + later, later[:, :1] + lsn[:, :1]

    def weighted(a, h, r0):
        vblk = v_ref[0, pl.ds(r0, tq), h * d:(h + 1) * d]
        return jnp.dot(a.astype(BF16), vblk, preferred_element_type=F32)

    has_prev = i > 0
    rd = pl.multiple_of(i * tq, tq)
    rp = pl.multiple_of(jnp.maximum(i - 1, 0) * tq, tq)
    worst = None
    for h in range(hg):
        cols = slice(h * d, (h + 1) * d)
        log_d, sum_d = scores(h, rd, True)
        log_p, sum_p = scores(h, rp, False)
        a_d = jnp.where(causal, jnp.exp(log_d), 0.0)
        a_p = jnp.where(has_prev, jnp.exp(log_p + sum_d), 0.0)
        acc_ref[:, cols] = weighted(a_d, h, rd) + weighted(a_p, h, rp)
        carry = jnp.where(has_prev, sum_d + sum_p, sum_d)
        carry_ref[h] = carry
        m = jnp.max(carry)
        worst = m if worst is None else jnp.maximum(worst, m)

    def earlier(kb):
        r0 = pl.multiple_of(kb * tq, tq)
        worst = None
        for h in range(hg):
            cols = slice(h * d, (h + 1) * d)
            log_a, row_sum = scores(h, r0, False)
            acc_ref[:, cols] += weighted(jnp.exp(log_a + carry_ref[h]), h, r0)
            carry = carry_ref[h] + row_sum
            carry_ref[h] = carry
            m = jnp.max(carry)
            worst = m if worst is None else jnp.maximum(worst, m)
        return worst

    def cond(st):
        kb, m = st
        return jnp.logical_and(kb >= 0, m > -LOG_UNDERFLOW)

    def body(st):
        kb, _ = st
        return kb - 1, earlier(kb)

    lax.while_loop(cond, body, (i - 2, worst))
    o_ref[0] = acc_ref[...].astype(o_ref.dtype)


def _attention(proj3, q_norm_w, k_norm_w, n_heads):
    B, S, _ = proj3.shape
    d = HEAD_DIM
    tq = 256
    hg = 4
    G = n_heads // hg
    w = hg * d
    return pl.pallas_call(
        functools.partial(_attn_kernel, S=S, tq=tq, hg=hg, scale=1.0 / math.sqrt(d)),
        out_shape=jax.ShapeDtypeStruct((B, S, n_heads * d), BF16),
        grid=(B, G, S // tq),
        in_specs=[pl.BlockSpec((1, tq, w), lambda b, g, i: (b, i, g)),
                  pl.BlockSpec((1, S, w), lambda b, g, i: (b, 0, G + g)),
                  pl.BlockSpec((1, S, w), lambda b, g, i: (b, 0, 2 * G + g)),
                  pl.BlockSpec((1, d), lambda b, g, i: (0, 0)),
                  pl.BlockSpec((1, d), lambda b, g, i: (0, 0))],
        out_specs=pl.BlockSpec((1, tq, w), lambda b, g, i: (b, i, g)),
        scratch_shapes=[pltpu.VMEM((S, w), BF16),
                        pltpu.VMEM((hg, tq, 1), F32),
                        pltpu.VMEM((tq, w), F32)],
        compiler_params=_params(("arbitrary", "arbitrary", "arbitrary"), VMEM_LIMIT),
        name="stickbreak_attn",
    )(proj3, proj3, proj3, q_norm_w.reshape(1, d), k_norm_w.reshape(1, d))


def _pool_kernel(u_ref, w_ref, ps_ref, o_ref, *, S, ch, gd):
    halo = 16
    for g, win in enumerate(POOL_WINDOWS):
        lo, hi = g * gd, (g + 1) * gd
        wg = w_ref[g].astype(BF16)
        sc = ps_ref[:, lo:hi]

        def body(c, _, win=win, lo=lo, hi=hi, wg=wg, sc=sc):
            r0 = pl.multiple_of(c * ch, ch)
            cur = u_ref[0, pl.ds(r0, ch), lo:hi].astype(F32)
            p0 = pl.multiple_of(jnp.maximum(r0 - halo, 0), halo)
            prev = u_ref[0, pl.ds(p0, halo), lo:hi].astype(F32)
            prev = jnp.where(c > 0, prev, 0.0)
            s = jnp.concatenate([prev, cur], axis=0)
            n = 1
            while n < win:
                s = s + pltpu.roll(s, n, 0)
                n *= 2
            s = s[halo:]
            t = r0 + lax.broadcasted_iota(I32, (ch, 1), 0)
            cnt = jnp.minimum(t + 1, win).astype(F32)
            p = s / cnt - cur
            y = jnp.dot(p.astype(BF16), wg, preferred_element_type=F32) * sc
            o_ref[0, pl.ds(r0, ch), lo:hi] = y.astype(o_ref.dtype)
            return 0

        lax.fori_loop(0, S // ch, body, 0)


def _pool(proj3, w_pool, pool_scale, pool_width):
    B, S, NP = proj3.shape
    G, gd, _ = w_pool.shape
    return pl.pallas_call(
        functools.partial(_pool_kernel, S=S, ch=256, gd=gd),
        out_shape=jax.ShapeDtypeStruct((B, S, pool_width), BF16),
        grid=(B,),
        in_specs=[pl.BlockSpec((1, S, pool_width), lambda b: (b, 0, NP // pool_width - 1)),
                  pl.BlockSpec((G, gd, gd), lambda b: (0, 0, 0)),
                  pl.BlockSpec((1, pool_width), lambda b: (0, 0))],
        out_specs=pl.BlockSpec((1, S, pool_width), lambda b: (b, 0, 0)),
        compiler_params=_params(("arbitrary",), VMEM_LIMIT),
        name="pool_mixer",
    )(proj3, w_pool, pool_scale.reshape(1, pool_width))


def _outproj_kernel(osb_ref, opool_ref, wo_ref, x_ref, g1_ref, sh_ref, sc_ref, nw_ref,
                    wr_ref, br_ref, x1_ref, h2p_ref, gates_ref, idx_ref, *, sbw, n_exp, sub):
    for r0 in range(0, x_ref.shape[0], sub):
        rows = slice(r0, r0 + sub)
        _outproj_rows(osb_ref.at[rows], opool_ref.at[rows], wo_ref, x_ref.at[rows], g1_ref,
                      sh_ref, sc_ref, nw_ref, wr_ref, br_ref, x1_ref.at[rows], h2p_ref.at[rows],
                      gates_ref.at[rows], idx_ref.at[rows], sbw=sbw, n_exp=n_exp)


def _outproj_rows(osb_ref, opool_ref, wo_ref, x_ref, g1_ref, sh_ref, sc_ref, nw_ref,
                  wr_ref, br_ref, x1_ref, h2p_ref, gates_ref, idx_ref, *, sbw, n_exp):
    tm, D = x_ref.shape
    mixed = (jnp.dot(osb_ref[...], wo_ref[:sbw, :], preferred_element_type=F32)
             + jnp.dot(opool_ref[...], wo_ref[sbw:, :], preferred_element_type=F32))
    x1 = x_ref[...] + g1_ref[0] * mixed
    x1_ref[...] = x1
    inv = lax.rsqrt(jnp.mean(x1 * x1, axis=-1, keepdims=True) + EPS)
    h2 = x1 * inv * (nw_ref[...] * (1.0 + sc_ref[0])) + sh_ref[0]
    hb = h2.astype(BF16)
    h2p_ref[...] = _pack_rows(h2)

    lane = lax.broadcasted_iota(I32, (tm, LANES), 1)
    lanef = lane.astype(F32)
    logits = jnp.dot(hb, wr_ref[...].astype(BF16), preferred_element_type=F32) + br_ref[...]
    vals = jnp.where(lane < n_exp, logits, -jnp.inf)
    tops, ids = [], []
    for _ in range(TOP_K):
        m = jnp.max(vals, axis=-1, keepdims=True)
        first = jnp.min(jnp.where(vals == m, lanef, float(LANES)), axis=-1, keepdims=True)
        tops.append(m)
        ids.append(first)
        vals = jnp.where(lanef == first, -jnp.inf, vals)
    es = [jnp.exp(m - tops[0]) for m in tops]
    den = es[0]
    for e in es[1:]:
        den = den + e
    gates = jnp.zeros((tm, LANES), F32)
    idx = jnp.zeros((tm, LANES), F32)
    for k in range(TOP_K):
        gates = jnp.where(lane == k, es[k] / den, gates)
        idx = jnp.where(lane == k, ids[k], idx)
    gates_ref[...] = gates
    idx_ref[...] = idx.astype(I32)


def _outproj(o_sb, o_pool, wo_bf, x2, gate1, shift2, scale2, norm2_w, w_router, b_router, S):
    T, D = x2.shape
    sbw = o_sb.shape[1]
    pw = o_pool.shape[1]
    n_exp = w_router.shape[1]
    tm, sub = 512, 256
    per_b = S // tm
    wr = jnp.zeros((D, LANES), F32).at[:, :n_exp].set(w_router)
    br = jnp.zeros((1, LANES), F32).at[0, :n_exp].set(b_router)
    mod_spec = pl.BlockSpec((1, 1, D), lambda i: (i // per_b, 0, 0))
    return pl.pallas_call(
        functools.partial(_outproj_kernel, sbw=sbw, n_exp=n_exp, sub=sub),
        out_shape=(jax.ShapeDtypeStruct((T, D), F32),
                   jax.ShapeDtypeStruct((T, D // 2), U32),
                   jax.ShapeDtypeStruct((T, LANES), F32),
                   jax.ShapeDtypeStruct((T, LANES), I32)),
        grid=(T // tm,),
        in_specs=[pl.BlockSpec((tm, sbw), lambda i: (i, 0)),
                  pl.BlockSpec((tm, pw), lambda i: (i, 0)),
                  pl.BlockSpec((sbw + pw, D), lambda i: (0, 0)),
                  pl.BlockSpec((tm, D), lambda i: (i, 0)),
                  mod_spec, mod_spec, mod_spec,
                  pl.BlockSpec((1, D), lambda i: (0, 0)),
                  pl.BlockSpec((D, LANES), lambda i: (0, 0)),
                  pl.BlockSpec((1, LANES), lambda i: (0, 0))],
        out_specs=(pl.BlockSpec((tm, D), lambda i: (i, 0)),
                   pl.BlockSpec((tm, D // 2), lambda i: (i, 0)),
                   pl.BlockSpec((tm, LANES), lambda i: (i, 0)),
                   pl.BlockSpec((tm, LANES), lambda i: (i, 0))),
        compiler_params=_params(("arbitrary",), VMEM_LIMIT),
        name="outproj_router",
    )(o_sb, o_pool, wo_bf, x2, gate1[:, None, :], shift2[:, None, :], scale2[:, None, :],
      norm2_w.reshape(1, D), wr, br)


def _route_kernel(idx_ref, dest_ref, meta_ref, rank_ref, *, T, ch):
    lane = lax.broadcasted_iota(I32, (ch, LANES), 1)
    row = lax.broadcasted_iota(I32, (ch, ch), 0)
    col = lax.broadcasted_iota(I32, (ch, ch), 1)
    before = (col < row).astype(BF16)

    def load(c):
        return idx_ref[pl.ds(pl.multiple_of(c * ch, ch), ch), :]

    def count(c, cnt):
        ii = load(c)
        member = lane == ii[:, 0:1]
        for k in range(1, TOP_K):
            member = jnp.logical_or(member, lane == ii[:, k:k + 1])
        mf = jnp.where(member, 1.0, 0.0)
        rank = jnp.dot(before, mf.astype(BF16), preferred_element_type=F32) + cnt
        rank_ref[pl.ds(pl.multiple_of(c * ch, ch), ch), :] = rank
        return cnt + jnp.sum(mf, axis=0, keepdims=True)

    cnt = lax.fori_loop(0, T // ch, count, jnp.zeros((1, LANES), F32))
    padded = jnp.ceil(cnt / EXPERT_ROWS) * EXPERT_ROWS
    rows = 8
    lane8 = lax.broadcasted_iota(I32, (rows, LANES), 1)
    ends = jnp.broadcast_to(padded, (rows, LANES))
    sh = 1
    while sh < LANES:
        ends = ends + jnp.where(lane8 >= sh, pltpu.roll(ends, sh, 1), 0.0)
        sh *= 2
    starts = ends - padded
    sub8 = lax.broadcasted_iota(I32, (rows, LANES), 0)
    meta = jnp.where(sub8 == 0, cnt, jnp.where(sub8 == 1, starts, padded))
    meta_ref[...] = meta.astype(I32)
    start_row = starts[0:1, :]

    def place(c, _):
        ii = load(c)
        val = rank_ref[pl.ds(pl.multiple_of(c * ch, ch), ch), :] + start_row
        out = jnp.zeros((ch, LANES), F32)
        for k in range(TOP_K):
            d = jnp.sum(jnp.where(lane == ii[:, k:k + 1], val, 0.0), axis=-1, keepdims=True)
            out = jnp.where(lane == k, d, out)
        dest_ref[pl.ds(pl.multiple_of(c * ch, ch), ch), :] = out.astype(I32)
        return 0

    lax.fori_loop(0, T // ch, place, 0)


def _route(idx_wide):
    T = idx_wide.shape[0]
    return pl.pallas_call(
        functools.partial(_route_kernel, T=T, ch=256),
        out_shape=(jax.ShapeDtypeStruct((T, LANES), I32),
                   jax.ShapeDtypeStruct((8, LANES), I32)),
        grid=(1,),
        in_specs=[pl.BlockSpec((T, LANES), lambda i: (0, 0))],
        out_specs=(pl.BlockSpec((T, LANES), lambda i: (0, 0)),
                   pl.BlockSpec((8, LANES), lambda i: (0, 0))),
        scratch_shapes=[pltpu.VMEM((T, LANES), F32)],
        compiler_params=_params(("arbitrary",), VMEM_LIMIT),
        name="route_ranks",
    )(idx_wide)


def _dispatch_kernel(dest_ref, cnt_ref, start_ref, h_ref, x_ref, z_ref, sem, zsem,
                     *, tb, n_exp, n_blocks):
    s = pl.program_id(0)

    def issue(tt, _):
        t0 = pl.multiple_of(tt * SUBLANES, SUBLANES)
        for r in range(SUBLANES):
            for k in range(TOP_K):
                d = dest_ref[TOP_K * (s * tb + t0 + r) + k]
                pltpu.make_async_copy(h_ref.at[pl.ds(t0 + r, 1), :], x_ref.at[pl.ds(d, 1), :],
                                      sem).start(priority=k % 2)
        return 0

    lax.fori_loop(0, tb // SUBLANES, issue, 0)

    @pl.when(s == 0)
    def _():
        z_ref[...] = jnp.zeros_like(z_ref)
        _dispatch_zero_fill(cnt_ref, start_ref, x_ref, z_ref, zsem, n_exp, n_blocks)

    for _ in range(TOP_K):
        pltpu.make_async_copy(h_ref, x_ref.at[pl.ds(0, tb), :], sem).wait()


def _dispatch_zero_fill(cnt_ref, start_ref, x_ref, z_ref, zsem, n_exp, n_blocks):
    def zero_fill(e, wait):
        cnt = cnt_ref[e]
        npad = (-cnt) & (EXPERT_ROWS - 1)
        off = start_ref[e] + cnt

        def one(i, _):
            cp = pltpu.make_async_copy(z_ref.at[pl.ds(0, 1), :],
                                       x_ref.at[pl.ds(off + i, 1), :], zsem)
            if wait:
                cp.wait()
            else:
                cp.start()
            return 0

        lax.fori_loop(0, npad, one, 0)
        return 0

    used = (start_ref[n_exp - 1] + cnt_ref[n_exp - 1] + EXPERT_ROWS - 1) // EXPERT_ROWS

    def tail_fill(blk, wait):
        r0 = pl.multiple_of(blk * EXPERT_ROWS, EXPERT_ROWS)
        cp = pltpu.make_async_copy(z_ref, x_ref.at[pl.ds(r0, EXPERT_ROWS), :], zsem)
        if wait:
            cp.wait()
        else:
            cp.start()
        return 0

    lax.fori_loop(0, n_exp, lambda e, _: zero_fill(e, False), 0)
    lax.fori_loop(used, n_blocks, lambda b, _: tail_fill(b, False), 0)
    lax.fori_loop(0, n_exp, lambda e, _: zero_fill(e, True), 0)
    lax.fori_loop(used, n_blocks, lambda b, _: tail_fill(b, True), 0)


def _dispatch(dest_flat, cnt, starts, h2p, n_rows):
    T, W = h2p.shape
    n_exp = cnt.shape[0]
    tb = 1024
    return pl.pallas_call(
        functools.partial(_dispatch_kernel, tb=tb, n_exp=n_exp, n_blocks=n_rows // EXPERT_ROWS),
        out_shape=jax.ShapeDtypeStruct((n_rows, W), U32),
        grid_spec=pltpu.PrefetchScalarGridSpec(
            num_scalar_prefetch=3,
            grid=(T // tb,),
            in_specs=[pl.BlockSpec((tb, W), lambda s, *_: (s, 0))],
            out_specs=pl.BlockSpec(memory_space=pl.ANY),
            scratch_shapes=[pltpu.VMEM((EXPERT_ROWS, W), U32),
                            pltpu.SemaphoreType.DMA, pltpu.SemaphoreType.DMA]),
        compiler_params=_params(("arbitrary",), VMEM_LIMIT),
        name="dispatch_rows",
    )(dest_flat, cnt, starts, h2p)


def _ffn1_kernel(ri_ref, ro_ref, e_ref, j_ref, first_ref, valid_ref, slot_ref, ne_ref, nj_ref, more_ref, pieces_ref,
                 x_ref, w_ref, b_ref, o_ref, stage, wbf, sem, *, F, tn):
    q = pl.program_id(0)

    def wcopy(e, j, slot, part):
        c0 = pl.multiple_of(part * F + j * tn, tn)
        return pltpu.make_async_copy(w_ref.at[e, :, pl.ds(c0, tn)], stage.at[slot, part],
                                     sem.at[slot])

    @pl.when(q == 0)
    def _():
        for part in range(2):
            wcopy(e_ref[0], j_ref[0], 0, part).start(priority=WEIGHT_DMA_PRIORITY)

    @pl.when(first_ref[q] == 1)
    def _():
        slot = slot_ref[q]
        for part in range(2):
            wcopy(e_ref[q], j_ref[q], slot, part).wait()

        @pl.when(more_ref[q] == 1)
        def _():
            for part in range(2):
                wcopy(ne_ref[q], nj_ref[q], 1 - slot, part).start(priority=WEIGHT_DMA_PRIORITY)

    def swiglu_block(w_gate, w_lin, rows):
        xb = _unpack_rows(x_ref[:rows, :])
        g = jnp.dot(xb, w_gate, preferred_element_type=F32) + b_ref[0, 0, 0]
        lin = jnp.dot(xb, w_lin, preferred_element_type=F32) + b_ref[0, 1, 0]
        g = jnp.minimum(g, SWIGLU_LIMIT)
        lin = jnp.clip(lin, -SWIGLU_LIMIT, SWIGLU_LIMIT)
        act = g / (1.0 + jnp.exp(-SWIGLU_ALPHA * g)) * (lin + 1.0)
        o_ref[:rows, :] = act.astype(o_ref.dtype)
        if rows < EXPERT_ROWS:
            o_ref[rows:, :] = jnp.zeros((EXPERT_ROWS - rows, o_ref.shape[1]), o_ref.dtype)

    @pl.when(first_ref[q] == 1)
    def _():
        slot = slot_ref[q]
        w16 = [stage[slot, part].astype(BF16) for part in range(2)]
        for part in range(2):
            wbf[part] = w16[part]
        swiglu_block(w16[0], w16[1], EXPERT_ROWS)

    later = jnp.logical_and(valid_ref[q] == 1, first_ref[q] == 0)

    for pieces in range(MIN_TAIL_PIECES, EXPERT_ROWS // TAIL_ROWS + 1):
        @pl.when(jnp.logical_and(later, pieces_ref[q] == pieces))
        def _(pieces=pieces):
            swiglu_block(wbf[0], wbf[1], pieces * TAIL_ROWS)

    @pl.when(valid_ref[q] == 0)
    def _():
        o_ref[...] = jnp.zeros_like(o_ref)


def _ffn2_kernel(ri_ref, ro_ref, e_ref, j_ref, first_ref, valid_ref, slot_ref, ne_ref, nj_ref, more_ref, pieces_ref,
                 a_ref, w_ref, b_ref, o_ref, stage, wbf, sem):
    q = pl.program_id(0)

    def wcopy(e, slot):
        return pltpu.make_async_copy(w_ref.at[e], stage.at[slot], sem.at[slot])

    @pl.when(q == 0)
    def _():
        wcopy(e_ref[0], 0).start(priority=WEIGHT_DMA_PRIORITY)

    @pl.when(first_ref[q] == 1)
    def _():
        slot = slot_ref[q]
        wcopy(e_ref[q], slot).wait()

        @pl.when(more_ref[q] == 1)
        def _():
            wcopy(ne_ref[q], 1 - slot).start(priority=WEIGHT_DMA_PRIORITY)

    def out_block(w, rows):
        y = jnp.dot(a_ref[:rows, :], w, preferred_element_type=F32) + b_ref[0]
        o_ref[:rows, :] = _pack_rows(y)
        if rows < EXPERT_ROWS:
            o_ref[rows:, :] = jnp.zeros((EXPERT_ROWS - rows, o_ref.shape[1]), o_ref.dtype)

    @pl.when(first_ref[q] == 1)
    def _():
        w16 = stage[slot_ref[q]].astype(BF16)
        wbf[...] = w16
        out_block(w16, EXPERT_ROWS)

    later = jnp.logical_and(valid_ref[q] == 1, first_ref[q] == 0)

    for pieces in range(MIN_TAIL_PIECES, EXPERT_ROWS // TAIL_ROWS + 1):
        @pl.when(jnp.logical_and(later, pieces_ref[q] == pieces))
        def _(pieces=pieces):
            out_block(wbf[...], pieces * TAIL_ROWS)

    @pl.when(valid_ref[q] == 0)
    def _():
        o_ref[...] = jnp.zeros_like(o_ref)


def _work_items(cnt, n_col_tiles, n_blocks):
    n_exp = cnt.shape[0]
    nblk = (cnt + EXPERT_ROWS - 1) // EXPERT_ROWS
    bstart = jnp.cumsum(nblk) - nblk
    gsize = jnp.repeat(nblk, n_col_tiles)
    gend = jnp.cumsum(gsize)
    n_groups = n_exp * n_col_tiles
    gid = jnp.arange(n_groups, dtype=I32)
    total = gend[-1]
    q = jnp.arange(n_blocks * n_col_tiles, dtype=I32)
    qc = jnp.minimum(q, total - 1)
    g = jnp.sum((gend[None, :] <= qc[:, None]).astype(I32), axis=1)
    nonempty = gsize > 0
    ordinal = jnp.cumsum(nonempty.astype(I32)) - 1
    nxt_incl = lax.cummin(jnp.where(nonempty, gid, n_groups), reverse=True)
    nxt = jnp.concatenate([nxt_incl[1:], jnp.full((1,), n_groups, I32)])
    more = nxt < n_groups
    nxt = jnp.minimum(nxt, n_groups - 1)
    per_group = jnp.stack([gend - gsize, gid // n_col_tiles, gid % n_col_tiles,
                           jnp.repeat(bstart, n_col_tiles), ordinal % 2,
                           nxt // n_col_tiles, nxt % n_col_tiles, more.astype(I32),
                           jnp.repeat(cnt, n_col_tiles)])
    pick = (g[None, :, None] == gid[None, None, :]).astype(I32)
    gstart, e, j, brow, slot, ne, nj, more, rows = jnp.sum(pick * per_group[:, None, :], axis=2)
    r = qc - gstart
    valid = q < total
    first = jnp.logical_and(valid, r == 0)
    pieces = jnp.clip((rows - r * EXPERT_ROWS + TAIL_ROWS - 1) // TAIL_ROWS, MIN_TAIL_PIECES,
                      EXPERT_ROWS // TAIL_ROWS)
    over = q - total
    row_in = brow + r
    row_out = jnp.where(valid, row_in, jnp.sum(nblk) + over // n_col_tiles)
    col_out = jnp.where(valid, j, over % n_col_tiles)
    as_i32 = lambda a: a.astype(I32)
    return tuple(map(as_i32, (row_in, row_out, e, col_out, first, valid, slot, ne, nj, more,
                              pieces)))


def _ffn1(items, x_pad, w_exp_in, b_exp_in):
    P, W = x_pad.shape
    n_exp, D, F2 = w_exp_in.shape
    F = F2 // 2
    tn = 1024
    nj = F // tn
    n_items = items[0].shape[0]
    bias = b_exp_in.reshape(n_exp, 2, nj, 1, tn)
    return pl.pallas_call(
        functools.partial(_ffn1_kernel, F=F, tn=tn),
        out_shape=jax.ShapeDtypeStruct((P, F), BF16),
        grid_spec=pltpu.PrefetchScalarGridSpec(
            num_scalar_prefetch=11,
            grid=(n_items,),
            in_specs=[pl.BlockSpec((EXPERT_ROWS, W), lambda q, ri, *_: (ri[q], 0)),
                      pl.BlockSpec(memory_space=pl.ANY),
                      pl.BlockSpec((1, 2, 1, 1, tn),
                                   lambda q, ri, ro, e, j, *_: (e[q], 0, j[q], 0, 0))],
            out_specs=pl.BlockSpec((EXPERT_ROWS, tn),
                                   lambda q, ri, ro, e, j, *_: (ro[q], j[q])),
            scratch_shapes=[pltpu.VMEM((2, 2, D, tn), F32),
                            pltpu.VMEM((2, D, tn), BF16),
                            pltpu.SemaphoreType.DMA((2,))]),
        compiler_params=_params(("arbitrary",), VMEM_LIMIT),
        name="expert_in_swiglu",
    )(*items, x_pad, w_exp_in, bias)


def _ffn2(items, act, w_exp_out, b_exp_out):
    P, F = act.shape
    n_exp, _, D = w_exp_out.shape
    n_items = items[0].shape[0]
    return pl.pallas_call(
        _ffn2_kernel,
        out_shape=jax.ShapeDtypeStruct((P, D // 2), U32),
        grid_spec=pltpu.PrefetchScalarGridSpec(
            num_scalar_prefetch=11,
            grid=(n_items,),
            in_specs=[pl.BlockSpec((EXPERT_ROWS, F), lambda q, ri, *_: (ri[q], 0)),
                      pl.BlockSpec(memory_space=pl.ANY),
                      pl.BlockSpec((1, 1, D), lambda q, ri, ro, e, *_: (e[q], 0, 0))],
            out_specs=pl.BlockSpec((EXPERT_ROWS, D // 2), lambda q, ri, ro, *_: (ro[q], 0)),
            scratch_shapes=[pltpu.VMEM((2, F, D), F32),
                            pltpu.VMEM((F, D), BF16),
                            pltpu.SemaphoreType.DMA((2,))]),
        compiler_params=_params(("arbitrary",), VMEM_LIMIT),
        name="expert_out",
    )(*items, act, w_exp_out, b_exp_out[:, None, :])


def _combine_kernel(dest_ref, y_ref, gates_ref, x1_ref, g2_ref, o_ref, buf, sem, *, tm):
    s = pl.program_id(0)
    ns = pl.num_programs(0)

    def gather(step, slot):
        def issue(tt, _):
            t0 = pl.multiple_of(tt * SUBLANES, SUBLANES)
            for r in range(SUBLANES):
                for k in range(TOP_K):
                    d = dest_ref[TOP_K * (step * tm + t0 + r) + k]
                    pltpu.make_async_copy(y_ref.at[pl.ds(d, 1), :],
                                          buf.at[slot, k, pl.ds(t0 + r, 1), :],
                                          sem.at[slot]).start(priority=k % 2)
            return 0

        lax.fori_loop(0, tm // SUBLANES, issue, 0)

    @pl.when(s == 0)
    def _():
        gather(0, 0)

    @pl.when(s + 1 < ns)
    def _():
        gather(s + 1, (s + 1) % 2)

    slot = s % 2
    for k in range(TOP_K):
        pltpu.make_async_copy(y_ref.at[pl.ds(0, tm), :], buf.at[slot, k], sem.at[slot]).wait()
    gates = gates_ref[...]
    y_hi = y_lo = None
    for k in range(TOP_K):
        hi, lo = _unpack_halves(buf[slot, k])
        g = gates[:, k:k + 1]
        y_hi = g * hi if y_hi is None else y_hi + g * hi
        y_lo = g * lo if y_lo is None else y_lo + g * lo
    y = jnp.concatenate([y_hi, y_lo], axis=1)
    o_ref[...] = x1_ref[...] + g2_ref[0] * y


def _combine(dest_flat, y_pad, gates_wide, x1, gate2, S):
    T, D = x1.shape
    tm = 128
    per_b = S // tm
    return pl.pallas_call(
        functools.partial(_combine_kernel, tm=tm),
        out_shape=jax.ShapeDtypeStruct((T, D), F32),
        grid_spec=pltpu.PrefetchScalarGridSpec(
            num_scalar_prefetch=1,
            grid=(T // tm,),
            in_specs=[pl.BlockSpec(memory_space=pl.ANY),
                      pl.BlockSpec((tm, LANES), lambda i, d: (i, 0)),
                      pl.BlockSpec((tm, D), lambda i, d: (i, 0)),
                      pl.BlockSpec((1, 1, D), lambda i, d: (i // per_b, 0, 0))],
            out_specs=pl.BlockSpec((tm, D), lambda i, d: (i, 0)),
            scratch_shapes=[pltpu.VMEM((2, TOP_K, tm, D // 2), U32),
                            pltpu.SemaphoreType.DMA((2,))]),
        compiler_params=_params(("arbitrary",), VMEM_LIMIT),
        name="combine_rows",
    )(dest_flat, y_pad, gates_wide, x1, gate2[:, None, :])


def kernel(x, c, norm1_w, norm2_w, w_ada, b_ada, w_in, q_norm_w, k_norm_w, w_pool, pool_scale,
           w_o, w_router, b_router, w_exp_in, b_exp_in, w_exp_out, b_exp_out):
    B, S, D = x.shape
    T = B * S
    depth = w_ada.shape[0]
    n_exp = w_router.shape[-1]
    pool_width = pool_scale.shape[-1]
    sb_width = w_o.shape[1] - pool_width
    n_heads = sb_width // HEAD_DIM
    n_blocks = (T * TOP_K + n_exp * (EXPERT_ROWS - 1)) // EXPERT_ROWS
    n_rows = n_blocks * EXPERT_ROWS

    x2 = x.reshape(T, D)
    for l in range(depth):
        mod = _adaln(c, w_ada[l], b_ada[l])
        shift1, scale1, gate1, shift2, scale2, gate2 = jnp.split(mod, 6, axis=-1)

        proj = _inproj(x2, norm1_w[l], shift1, scale1, w_in[l].astype(BF16), S)
        proj3 = proj.reshape(B, S, -1)
        o_sb = _attention(proj3, q_norm_w[l], k_norm_w[l], n_heads)
        o_pool = _pool(proj3, w_pool[l], pool_scale[l], pool_width)
        x1, h2p, gates_wide, idx_wide = _outproj(
            o_sb.reshape(T, sb_width), o_pool.reshape(T, pool_width), w_o[l].astype(BF16),
            x2, gate1, shift2, scale2, norm2_w[l], w_router[l], b_router[l], S)

        dest_wide, meta = _route(idx_wide)
        cnt = meta[0, :n_exp]
        starts = meta[1, :n_exp]
        dest_flat = dest_wide[:, :TOP_K].reshape(T * TOP_K)
        x_pad = _dispatch(dest_flat, cnt, starts, h2p, n_rows)

        F = w_exp_out.shape[2]
        act = _ffn1(_work_items(cnt, F // 1024, n_blocks), x_pad, w_exp_in[l], b_exp_in[l])
        y_pad = _ffn2(_work_items(cnt, 1, n_blocks), act, w_exp_out[l], b_exp_out[l])
        x2 = _combine(dest_flat, y_pad, gates_wide, x1, gate2, S)
    return x2.reshape(B, S, D)
```

```python
import functools
import math

import jax
import jax.numpy as jnp
from jax import lax
from jax.experimental import pallas as pl
from jax.experimental.pallas import tpu as pltpu

F32 = jnp.float32
BF16 = jnp.bfloat16
I32 = jnp.int32
U32 = jnp.uint32

EPS = 1e-6
HEAD_DIM = 128
POOL_WINDOWS = (2, 4, 8, 16)
TOP_K = 4
SWIGLU_ALPHA = 1.702
SWIGLU_LIMIT = 7.0

LANES = 128
SUBLANES = 8
EXPERT_ROWS = 256
TAIL_ROWS = 64
MIN_TAIL_PIECES = 2
LOG_UNDERFLOW = 104.0
VMEM_LIMIT = 56 * 1024 * 1024
WEIGHT_DMA_PRIORITY = 1


def _params(sem=None, vmem=None):
    return pltpu.CompilerParams(dimension_semantics=sem, vmem_limit_bytes=vmem)


_HIGH_HALF = 0xFFFF0000


def _pack_rows(v):
    bits = lax.bitcast_convert_type(v.astype(BF16).astype(F32), U32)
    half = v.shape[1] // 2
    return (bits[:, :half] & jnp.uint32(_HIGH_HALF)) | (bits[:, half:] >> 16)


def _unpack_halves(p):
    hi = lax.bitcast_convert_type(p & jnp.uint32(_HIGH_HALF), F32)
    lo = lax.bitcast_convert_type(p << 16, F32)
    return hi, lo


def _unpack_rows(p):
    hi, lo = _unpack_halves(p)
    return jnp.concatenate([hi.astype(BF16), lo.astype(BF16)], axis=1)


def _adaln_kernel(c_ref, w_ref, b_ref, o_ref):
    c = c_ref[...]
    ca = c / (1.0 + jnp.exp(-c))
    o_ref[...] = jnp.dot(ca.astype(BF16), w_ref[...].astype(BF16),
                         preferred_element_type=F32) + b_ref[...]


def _adaln(c, w_ada, b_ada):
    B, D = c.shape
    N = w_ada.shape[1]
    rows = 8
    tn = 1024
    cp = jnp.zeros((rows, D), F32).at[:B].set(c)
    out = pl.pallas_call(
        _adaln_kernel,
        out_shape=jax.ShapeDtypeStruct((rows, N), F32),
        grid=(N // tn,),
        in_specs=[pl.BlockSpec((rows, D), lambda j: (0, 0)),
                  pl.BlockSpec((D, tn), lambda j: (0, j)),
                  pl.BlockSpec((1, tn), lambda j: (0, j))],
        out_specs=pl.BlockSpec((rows, tn), lambda j: (0, j)),
        compiler_params=_params(("arbitrary",), VMEM_LIMIT),
        name="adaln",
    )(cp, w_ada, b_ada.reshape(1, N))
    return out[:B]


def _inproj_kernel(x_ref, nw_ref, sh_ref, sc_ref, w_ref, o_ref, h_ref, *, tm, ch):
    @pl.when(pl.program_id(1) == 0)
    def _():
        mul = nw_ref[...] * (1.0 + sc_ref[0])
        add = sh_ref[0]

        def body(c, _):
            r0 = pl.multiple_of(c * ch, ch)
            x = x_ref[pl.ds(r0, ch), :]
            inv = lax.rsqrt(jnp.mean(x * x, axis=-1, keepdims=True) + EPS)
            h_ref[pl.ds(r0, ch), :] = (x * inv * mul + add).astype(BF16)
            return 0

        lax.fori_loop(0, tm // ch, body, 0)

    o_ref[...] = jnp.dot(h_ref[...], w_ref[...],
                         preferred_element_type=F32).astype(o_ref.dtype)


def _inproj(x2, norm_w, shift, scale, w_bf, S):
    T, D = x2.shape
    N = w_bf.shape[1]
    tm, tn, ch = 1024, 1024, 128
    per_b = S // tm
    return pl.pallas_call(
        functools.partial(_inproj_kernel, tm=tm, ch=ch),
        out_shape=jax.ShapeDtypeStruct((T, N), BF16),
        grid=(T // tm, N // tn),
        in_specs=[pl.BlockSpec((tm, D), lambda i, j: (i, 0)),
                  pl.BlockSpec((1, D), lambda i, j: (0, 0)),
                  pl.BlockSpec((1, 1, D), lambda i, j: (i // per_b, 0, 0)),
                  pl.BlockSpec((1, 1, D), lambda i, j: (i // per_b, 0, 0)),
                  pl.BlockSpec((D, tn), lambda i, j: (0, j))],
        out_specs=pl.BlockSpec((tm, tn), lambda i, j: (i, j)),
        scratch_shapes=[pltpu.VMEM((tm, D), BF16)],
        compiler_params=_params(("arbitrary", "arbitrary"), VMEM_LIMIT),
        name="inproj",
    )(x2, norm_w.reshape(1, D), shift[:, None, :], scale[:, None, :], w_bf)


def _attn_kernel(q_ref, k_ref, v_ref, qw_ref, kw_ref, o_ref, kn_ref, carry_ref, acc_ref,
                 *, S, tq, hg, scale):
    i = pl.program_id(2)
    d = HEAD_DIM

    def head_norm(x, w):
        parts = []
        for h in range(hg):
            xh = x[:, h * d:(h + 1) * d]
            inv = lax.rsqrt(jnp.mean(xh * xh, axis=-1, keepdims=True) + EPS)
            parts.append(xh * inv * w)
        return parts

    @pl.when(i == 0)
    def _():
        def body(c, _):
            r0 = pl.multiple_of(c * tq, tq)
            parts = head_norm(k_ref[0, pl.ds(r0, tq), :].astype(F32), kw_ref[...])
            for h in range(hg):
                kn_ref[pl.ds(r0, tq), h * d:(h + 1) * d] = parts[h].astype(BF16)
            return 0

        lax.fori_loop(0, S // tq, body, 0)

    qb = [(p * scale).astype(BF16) for p in head_norm(q_ref[0].astype(F32), qw_ref[...])]

    row = lax.broadcasted_iota(I32, (tq, tq), 0)
    col = lax.broadcasted_iota(I32, (tq, tq), 1)
    causal = col < row
    tri = (row > col).astype(BF16)

    def scores(h, r0, mask):
        kblk = kn_ref[pl.ds(r0, tq), h * d:(h + 1) * d]
        z = lax.dot_general(qb[h], kblk, (((1,), (1,)), ((), ())), preferred_element_type=F32)
        t = jnp.log(1.0 + jnp.exp(-jnp.abs(z)))
        lsn = jnp.minimum(-z, 0.0) - t
        lsp = lsn + z
        if mask:
            lsn = jnp.where(causal, lsn, 0.0)
        later = jnp.dot(lsn.astype(BF16), tri, preferred_element_type=F32)
        return lsp + later, later[:, :1] + lsn[:, :1]

    def weighted(a, h, r0):
        vblk = v_ref[0, pl.ds(r0, tq), h * d:(h + 1) * d]
        return jnp.dot(a.astype(BF16), vblk, preferred_element_type=F32)

    has_prev = i > 0
    rd = pl.multiple_of(i * tq, tq)
    rp = pl.multiple_of(jnp.maximum(i - 1, 0) * tq, tq)
    worst = None
    for h in range(hg):
        cols = slice(h * d, (h + 1) * d)
        log_d, sum_d = scores(h, rd, True)
        log_p, sum_p = scores(h, rp, False)
        a_d = jnp.where(causal, jnp.exp(log_d), 0.0)
        a_p = jnp.where(has_prev, jnp.exp(log_p + sum_d), 0.0)
        acc_ref[:, cols] = weighted(a_d, h, rd) + weighted(a_p, h, rp)
        carry = jnp.where(has_prev, sum_d + sum_p, sum_d)
        carry_ref[h] = carry
        m = jnp.max(carry)
        worst = m if worst is None else jnp.maximum(worst, m)

    def earlier(kb):
        r0 = pl.multiple_of(kb * tq, tq)
        worst = None
        for h in range(hg):
            cols = slice(h * d, (h + 1) * d)
            log_a, row_sum = scores(h, r0, False)
            acc_ref[:, cols] += weighted(jnp.exp(log_a + carry_ref[h]), h, r0)
            carry = carry_ref[h] + row_sum
            carry_ref[h] = carry
            m = jnp.max(carry)
            worst = m if worst is None else jnp.maximum(worst, m)
        return worst

    def cond(st):
        kb, m = st
        return jnp.logical_and(kb >= 0, m > -LOG_UNDERFLOW)

    def body(st):
        kb, _ = st
        return kb - 1, earlier(kb)

    lax.while_loop(cond, body, (i - 2, worst))
    o_ref[0] = acc_ref[...].astype(o_ref.dtype)


def _attention(proj3, q_norm_w, k_norm_w, n_heads):
    B, S, _ = proj3.shape
    d = HEAD_DIM
    tq = 256
    hg = 8
    G = n_heads // hg
    w = hg * d
    return pl.pallas_call(
        functools.partial(_attn_kernel, S=S, tq=tq, hg=hg, scale=1.0 / math.sqrt(d)),
        out_shape=jax.ShapeDtypeStruct((B, S, n_heads * d), BF16),
        grid=(B, G, S // tq),
        in_specs=[pl.BlockSpec((1, tq, w), lambda b, g, i: (b, i, g)),
                  pl.BlockSpec((1, S, w), lambda b, g, i: (b, 0, G + g)),
                  pl.BlockSpec((1, S, w), lambda b, g, i: (b, 0, 2 * G + g)),
                  pl.BlockSpec((1, d), lambda b, g, i: (0, 0)),
                  pl.BlockSpec((1, d), lambda b, g, i: (0, 0))],
        out_specs=pl.BlockSpec((1, tq, w), lambda b, g, i: (b, i, g)),
        scratch_shapes=[pltpu.VMEM((S, w), BF16),
                        pltpu.VMEM((hg, tq, 1), F32),
                        pltpu.VMEM((tq, w), F32)],
        compiler_params=_params(("arbitrary", "arbitrary", "arbitrary"), VMEM_LIMIT),
        name="stickbreak_attn",
    )(proj3, proj3, proj3, q_norm_w.reshape(1, d), k_norm_w.reshape(1, d))


def _pool_kernel(u_ref, w_ref, ps_ref, o_ref, *, S, ch, gd):
    halo = 16
    for g, win in enumerate(POOL_WINDOWS):
        lo, hi = g * gd, (g + 1) * gd
        wg = w_ref[g].astype(BF16)
        sc = ps_ref[:, lo:hi]

        def body(c, _, win=win, lo=lo, hi=hi, wg=wg, sc=sc):
            r0 = pl.multiple_of(c * ch, ch)
            cur = u_ref[0, pl.ds(r0, ch), lo:hi].astype(F32)
            p0 = pl.multiple_of(jnp.maximum(r0 - halo, 0), halo)
            prev = u_ref[0, pl.ds(p0, halo), lo:hi].astype(F32)
            prev = jnp.where(c > 0, prev, 0.0)
            s = jnp.concatenate([prev, cur], axis=0)
            n = 1
            while n < win:
                s = s + pltpu.roll(s, n, 0)
                n *= 2
            s = s[halo:]
            t = r0 + lax.broadcasted_iota(I32, (ch, 1), 0)
            cnt = jnp.minimum(t + 1, win).astype(F32)
            p = s / cnt - cur
            y = jnp.dot(p.astype(BF16), wg, preferred_element_type=F32) * sc
            o_ref[0, pl.ds(r0, ch), lo:hi] = y.astype(o_ref.dtype)
            return 0

        lax.fori_loop(0, S // ch, body, 0)


def _pool(proj3, w_pool, pool_scale, pool_width):
    B, S, NP = proj3.shape
    G, gd, _ = w_pool.shape
    return pl.pallas_call(
        functools.partial(_pool_kernel, S=S, ch=256, gd=gd),
        out_shape=jax.ShapeDtypeStruct((B, S, pool_width), BF16),
        grid=(B,),
        in_specs=[pl.BlockSpec((1, S, pool_width), lambda b: (b, 0, NP // pool_width - 1)),
                  pl.BlockSpec((G, gd, gd), lambda b: (0, 0, 0)),
                  pl.BlockSpec((1, pool_width), lambda b: (0, 0))],
        out_specs=pl.BlockSpec((1, S, pool_width), lambda b: (b, 0, 0)),
        compiler_params=_params(("arbitrary",), VMEM_LIMIT),
        name="pool_mixer",
    )(proj3, w_pool, pool_scale.reshape(1, pool_width))


def _outproj_kernel(osb_ref, opool_ref, wo_ref, x_ref, g1_ref, sh_ref, sc_ref, nw_ref,
                    wr_ref, br_ref, x1_ref, h2p_ref, gates_ref, idx_ref, *, sbw, n_exp, sub):
    for r0 in range(0, x_ref.shape[0], sub):
        rows = slice(r0, r0 + sub)
        _outproj_rows(osb_ref.at[rows], opool_ref.at[rows], wo_ref, x_ref.at[rows], g1_ref,
                      sh_ref, sc_ref, nw_ref, wr_ref, br_ref, x1_ref.at[rows], h2p_ref.at[rows],
                      gates_ref.at[rows], idx_ref.at[rows], sbw=sbw, n_exp=n_exp)


def _outproj_rows(osb_ref, opool_ref, wo_ref, x_ref, g1_ref, sh_ref, sc_ref, nw_ref,
                  wr_ref, br_ref, x1_ref, h2p_ref, gates_ref, idx_ref, *, sbw, n_exp):
    tm, D = x_ref.shape
    mixed = (jnp.dot(osb_ref[...], wo_ref[:sbw, :], preferred_element_type=F32)
             + jnp.dot(opool_ref[...], wo_ref[sbw:, :], preferred_element_type=F32))
    x1 = x_ref[...] + g1_ref[0] * mixed
    x1_ref[...] = x1
    inv = lax.rsqrt(jnp.mean(x1 * x1, axis=-1, keepdims=True) + EPS)
    h2 = x1 * inv * (nw_ref[...] * (1.0 + sc_ref[0])) + sh_ref[0]
    hb = h2.astype(BF16)
    h2p_ref[...] = _pack_rows(h2)

    lane = lax.broadcasted_iota(I32, (tm, LANES), 1)
    lanef = lane.astype(F32)
    logits = jnp.dot(hb, wr_ref[...].astype(BF16), preferred_element_type=F32) + br_ref[...]
    vals = jnp.where(lane < n_exp, logits, -jnp.inf)
    tops, ids = [], []
    for _ in range(TOP_K):
        m = jnp.max(vals, axis=-1, keepdims=True)
        first = jnp.min(jnp.where(vals == m, lanef, float(LANES)), axis=-1, keepdims=True)
        tops.append(m)
        ids.append(first)
        vals = jnp.where(lanef == first, -jnp.inf, vals)
    es = [jnp.exp(m - tops[0]) for m in tops]
    den = es[0]
    for e in es[1:]:
        den = den + e
    gates = jnp.zeros((tm, LANES), F32)
    idx = jnp.zeros((tm, LANES), F32)
    for k in range(TOP_K):
        gates = jnp.where(lane == k, es[k] / den, gates)
        idx = jnp.where(lane == k, ids[k], idx)
    gates_ref[...] = gates
    idx_ref[...] = idx.astype(I32)


def _outproj(o_sb, o_pool, wo_bf, x2, gate1, shift2, scale2, norm2_w, w_router, b_router, S):
    T, D = x2.shape
    sbw = o_sb.shape[1]
    pw = o_pool.shape[1]
    n_exp = w_router.shape[1]
    tm, sub = 512, 256
    per_b = S // tm
    wr = jnp.zeros((D, LANES), F32).at[:, :n_exp].set(w_router)
    br = jnp.zeros((1, LANES), F32).at[0, :n_exp].set(b_router)
    mod_spec = pl.BlockSpec((1, 1, D), lambda i: (i // per_b, 0, 0))
    return pl.pallas_call(
        functools.partial(_outproj_kernel, sbw=sbw, n_exp=n_exp, sub=sub),
        out_shape=(jax.ShapeDtypeStruct((T, D), F32),
                   jax.ShapeDtypeStruct((T, D // 2), U32),
                   jax.ShapeDtypeStruct((T, LANES), F32),
                   jax.ShapeDtypeStruct((T, LANES), I32)),
        grid=(T // tm,),
        in_specs=[pl.BlockSpec((tm, sbw), lambda i: (i, 0)),
                  pl.BlockSpec((tm, pw), lambda i: (i, 0)),
                  pl.BlockSpec((sbw + pw, D), lambda i: (0, 0)),
                  pl.BlockSpec((tm, D), lambda i: (i, 0)),
                  mod_spec, mod_spec, mod_spec,
                  pl.BlockSpec((1, D), lambda i: (0, 0)),
                  pl.BlockSpec((D, LANES), lambda i: (0, 0)),
                  pl.BlockSpec((1, LANES), lambda i: (0, 0))],
        out_specs=(pl.BlockSpec((tm, D), lambda i: (i, 0)),
                   pl.BlockSpec((tm, D // 2), lambda i: (i, 0)),
                   pl.BlockSpec((tm, LANES), lambda i: (i, 0)),
                   pl.BlockSpec((tm, LANES), lambda i: (i, 0))),
        compiler_params=_params(("arbitrary",), VMEM_LIMIT),
        name="outproj_router",
    )(o_sb, o_pool, wo_bf, x2, gate1[:, None, :], shift2[:, None, :], scale2[:, None, :],
      norm2_w.reshape(1, D), wr, br)


def _route_kernel(idx_ref, dest_ref, meta_ref, rank_ref, *, T, ch):
    lane = lax.broadcasted_iota(I32, (ch, LANES), 1)
    row = lax.broadcasted_iota(I32, (ch, ch), 0)
    col = lax.broadcasted_iota(I32, (ch, ch), 1)
    before = (col < row).astype(BF16)

    def load(c):
        return idx_ref[pl.ds(pl.multiple_of(c * ch, ch), ch), :]

    def count(c, cnt):
        ii = load(c)
        member = lane == ii[:, 0:1]
        for k in range(1, TOP_K):
            member = jnp.logical_or(member, lane == ii[:, k:k + 1])
        mf = jnp.where(member, 1.0, 0.0)
        rank = jnp.dot(before, mf.astype(BF16), preferred_element_type=F32) + cnt
        rank_ref[pl.ds(pl.multiple_of(c * ch, ch), ch), :] = rank
        return cnt + jnp.sum(mf, axis=0, keepdims=True)

    cnt = lax.fori_loop(0, T // ch, count, jnp.zeros((1, LANES), F32))
    padded = jnp.ceil(cnt / EXPERT_ROWS) * EXPERT_ROWS
    rows = 8
    lane8 = lax.broadcasted_iota(I32, (rows, LANES), 1)
    ends = jnp.broadcast_to(padded, (rows, LANES))
    sh = 1
    while sh < LANES:
        ends = ends + jnp.where(lane8 >= sh, pltpu.roll(ends, sh, 1), 0.0)
        sh *= 2
    starts = ends - padded
    sub8 = lax.broadcasted_iota(I32, (rows, LANES), 0)
    meta = jnp.where(sub8 == 0, cnt, jnp.where(sub8 == 1, starts, padded))
    meta_ref[...] = meta.astype(I32)
    start_row = starts[0:1, :]

    def place(c, _):
        ii = load(c)
        val = rank_ref[pl.ds(pl.multiple_of(c * ch, ch), ch), :] + start_row
        out = jnp.zeros((ch, LANES), F32)
        for k in range(TOP_K):
            d = jnp.sum(jnp.where(lane == ii[:, k:k + 1], val, 0.0), axis=-1, keepdims=True)
            out = jnp.where(lane == k, d, out)
        dest_ref[pl.ds(pl.multiple_of(c * ch, ch), ch), :] = out.astype(I32)
        return 0

    lax.fori_loop(0, T // ch, place, 0)


def _route(idx_wide):
    T = idx_wide.shape[0]
    return pl.pallas_call(
        functools.partial(_route_kernel, T=T, ch=256),
        out_shape=(jax.ShapeDtypeStruct((T, LANES), I32),
                   jax.ShapeDtypeStruct((8, LANES), I32)),
        grid=(1,),
        in_specs=[pl.BlockSpec((T, LANES), lambda i: (0, 0))],
        out_specs=(pl.BlockSpec((T, LANES), lambda i: (0, 0)),
                   pl.BlockSpec((8, LANES), lambda i: (0, 0))),
        scratch_shapes=[pltpu.VMEM((T, LANES), F32)],
        compiler_params=_params(("arbitrary",), VMEM_LIMIT),
        name="route_ranks",
    )(idx_wide)


def _dispatch_kernel(dest_ref, cnt_ref, start_ref, h_ref, x_ref, z_ref, sem, zsem,
                     *, tb, n_exp, n_blocks):
    s = pl.program_id(0)

    @pl.when(s == 0)
    def _():
        z_ref[...] = jnp.zeros_like(z_ref)
        _dispatch_zero_fill(cnt_ref, start_ref, x_ref, z_ref, zsem, n_exp, n_blocks)

    def issue(tt, _):
        t0 = pl.multiple_of(tt * SUBLANES, SUBLANES)
        for r in range(SUBLANES):
            for k in range(TOP_K):
                d = dest_ref[TOP_K * (s * tb + t0 + r) + k]
                pltpu.make_async_copy(h_ref.at[pl.ds(t0 + r, 1), :], x_ref.at[pl.ds(d, 1), :],
                                      sem).start(priority=k % 2)
        return 0

    lax.fori_loop(0, tb // SUBLANES, issue, 0)

    for _ in range(TOP_K):
        pltpu.make_async_copy(h_ref, x_ref.at[pl.ds(0, tb), :], sem).wait()


def _dispatch_zero_fill(cnt_ref, start_ref, x_ref, z_ref, zsem, n_exp, n_blocks):
    def block_fill(blk, wait):
        r0 = pl.multiple_of(blk * EXPERT_ROWS, EXPERT_ROWS)
        cp = pltpu.make_async_copy(z_ref, x_ref.at[pl.ds(r0, EXPERT_ROWS), :], zsem)
        if wait:
            cp.wait()
        else:
            cp.start()

    def zero_fill(e, wait):
        cnt = cnt_ref[e]

        @pl.when((cnt & (EXPERT_ROWS - 1)) != 0)
        def _():
            block_fill((start_ref[e] + cnt) // EXPERT_ROWS, wait)

        return 0

    used = (start_ref[n_exp - 1] + cnt_ref[n_exp - 1] + EXPERT_ROWS - 1) // EXPERT_ROWS

    def tail_fill(blk, wait):
        block_fill(blk, wait)
        return 0

    lax.fori_loop(0, n_exp, lambda e, _: zero_fill(e, False), 0)
    lax.fori_loop(used, n_blocks, lambda b, _: tail_fill(b, False), 0)
    lax.fori_loop(0, n_exp, lambda e, _: zero_fill(e, True), 0)
    lax.fori_loop(used, n_blocks, lambda b, _: tail_fill(b, True), 0)


def _dispatch(dest_flat, cnt, starts, h2p, n_rows):
    T, W = h2p.shape
    n_exp = cnt.shape[0]
    tb = 1024
    return pl.pallas_call(
        functools.partial(_dispatch_kernel, tb=tb, n_exp=n_exp, n_blocks=n_rows // EXPERT_ROWS),
        out_shape=jax.ShapeDtypeStruct((n_rows, W), U32),
        grid_spec=pltpu.PrefetchScalarGridSpec(
            num_scalar_prefetch=3,
            grid=(T // tb,),
            in_specs=[pl.BlockSpec((tb, W), lambda s, *_: (s, 0))],
            out_specs=pl.BlockSpec(memory_space=pl.ANY),
            scratch_shapes=[pltpu.VMEM((EXPERT_ROWS, W), U32),
                            pltpu.SemaphoreType.DMA, pltpu.SemaphoreType.DMA]),
        compiler_params=_params(("arbitrary",), VMEM_LIMIT),
        name="dispatch_rows",
    )(dest_flat, cnt, starts, h2p)


def _ffn1_kernel(ri_ref, ro_ref, e_ref, j_ref, first_ref, valid_ref, slot_ref, ne_ref, nj_ref, more_ref, pieces_ref,
                 x_ref, w_ref, b_ref, o_ref, stage, wbf, sem, *, F, tn):
    q = pl.program_id(0)

    def wcopy(e, j, slot, part):
        c0 = pl.multiple_of(part * F + j * tn, tn)
        return pltpu.make_async_copy(w_ref.at[e, :, pl.ds(c0, tn)], stage.at[slot, part],
                                     sem.at[slot])

    @pl.when(q == 0)
    def _():
        for part in range(2):
            wcopy(e_ref[0], j_ref[0], 0, part).start(priority=WEIGHT_DMA_PRIORITY)

    @pl.when(first_ref[q] == 1)
    def _():
        slot = slot_ref[q]
        for part in range(2):
            wcopy(e_ref[q], j_ref[q], slot, part).wait()

        @pl.when(more_ref[q] == 1)
        def _():
            for part in range(2):
                wcopy(ne_ref[q], nj_ref[q], 1 - slot, part).start(priority=WEIGHT_DMA_PRIORITY)

    def swiglu_block(w_gate, w_lin, rows):
        xb = _unpack_rows(x_ref[:rows, :])
        g = jnp.dot(xb, w_gate, preferred_element_type=F32) + b_ref[0, 0, 0]
        lin = jnp.dot(xb, w_lin, preferred_element_type=F32) + b_ref[0, 1, 0]
        g = jnp.minimum(g, SWIGLU_LIMIT)
        lin = jnp.clip(lin, -SWIGLU_LIMIT, SWIGLU_LIMIT)
        act = g / (1.0 + jnp.exp(-SWIGLU_ALPHA * g)) * (lin + 1.0)
        o_ref[:rows, :] = act.astype(o_ref.dtype)
        if rows < EXPERT_ROWS:
            o_ref[rows:, :] = jnp.zeros((EXPERT_ROWS - rows, o_ref.shape[1]), o_ref.dtype)

    @pl.when(first_ref[q] == 1)
    def _():
        slot = slot_ref[q]
        w16 = [stage[slot, part].astype(BF16) for part in range(2)]
        for part in range(2):
            wbf[part] = w16[part]
        swiglu_block(w16[0], w16[1], EXPERT_ROWS)

    later = jnp.logical_and(valid_ref[q] == 1, first_ref[q] == 0)

    for pieces in range(MIN_TAIL_PIECES, EXPERT_ROWS // TAIL_ROWS + 1):
        @pl.when(jnp.logical_and(later, pieces_ref[q] == pieces))
        def _(pieces=pieces):
            swiglu_block(wbf[0], wbf[1], pieces * TAIL_ROWS)

    @pl.when(valid_ref[q] == 0)
    def _():
        o_ref[...] = jnp.zeros_like(o_ref)


def _ffn2_kernel(ri_ref, ro_ref, e_ref, j_ref, first_ref, valid_ref, slot_ref, ne_ref, nj_ref, more_ref, pieces_ref,
                 a_ref, w_ref, b_ref, o_ref, stage, wbf, sem):
    q = pl.program_id(0)

    def wcopy(e, slot):
        return pltpu.make_async_copy(w_ref.at[e], stage.at[slot], sem.at[slot])

    @pl.when(q == 0)
    def _():
        wcopy(e_ref[0], 0).start(priority=WEIGHT_DMA_PRIORITY)

    @pl.when(first_ref[q] == 1)
    def _():
        slot = slot_ref[q]
        wcopy(e_ref[q], slot).wait()

        @pl.when(more_ref[q] == 1)
        def _():
            wcopy(ne_ref[q], 1 - slot).start(priority=WEIGHT_DMA_PRIORITY)

    def out_block(w, rows):
        y = jnp.dot(a_ref[:rows, :], w, preferred_element_type=F32) + b_ref[0]
        o_ref[:rows, :] = _pack_rows(y)
        if rows < EXPERT_ROWS:
            o_ref[rows:, :] = jnp.zeros((EXPERT_ROWS - rows, o_ref.shape[1]), o_ref.dtype)

    @pl.when(first_ref[q] == 1)
    def _():
        w16 = stage[slot_ref[q]].astype(BF16)
        wbf[...] = w16
        out_block(w16, EXPERT_ROWS)

    later = jnp.logical_and(valid_ref[q] == 1, first_ref[q] == 0)

    for pieces in range(MIN_TAIL_PIECES, EXPERT_ROWS // TAIL_ROWS + 1):
        @pl.when(jnp.logical_and(later, pieces_ref[q] == pieces))
        def _(pieces=pieces):
            out_block(wbf[...], pieces * TAIL_ROWS)

    @pl.when(valid_ref[q] == 0)
    def _():
        o_ref[...] = jnp.zeros_like(o_ref)


def _work_items(cnt, n_col_tiles, n_blocks):
    n_exp = cnt.shape[0]
    nblk = (cnt + EXPERT_ROWS - 1) // EXPERT_ROWS
    bstart = jnp.cumsum(nblk) - nblk
    gsize = jnp.repeat(nblk, n_col_tiles)
    gend = jnp.cumsum(gsize)
    n_groups = n_exp * n_col_tiles
    gid = jnp.arange(n_groups, dtype=I32)
    total = gend[-1]
    q = jnp.arange(n_blocks * n_col_tiles, dtype=I32)
    qc = jnp.minimum(q, total - 1)
    g = jnp.sum((gend[None, :] <= qc[:, None]).astype(I32), axis=1)
    nonempty = gsize > 0
    ordinal = jnp.cumsum(nonempty.astype(I32)) - 1
    nxt_incl = lax.cummin(jnp.where(nonempty, gid, n_groups), reverse=True)
    nxt = jnp.concatenate([nxt_incl[1:], jnp.full((1,), n_groups, I32)])
    more = nxt < n_groups
    nxt = jnp.minimum(nxt, n_groups - 1)
    per_group = jnp.stack([gend - gsize, gid // n_col_tiles, gid % n_col_tiles,
                           jnp.repeat(bstart, n_col_tiles), ordinal % 2,
                           nxt // n_col_tiles, nxt % n_col_tiles, more.astype(I32),
                           jnp.repeat(cnt, n_col_tiles)])
    pick = (g[None, :, None] == gid[None, None, :]).astype(I32)
    gstart, e, j, brow, slot, ne, nj, more, rows = jnp.sum(pick * per_group[:, None, :], axis=2)
    r = qc - gstart
    valid = q < total
    first = jnp.logical_and(valid, r == 0)
    pieces = jnp.clip((rows - r * EXPERT_ROWS + TAIL_ROWS - 1) // TAIL_ROWS, MIN_TAIL_PIECES,
                      EXPERT_ROWS // TAIL_ROWS)
    over = q - total
    row_in = brow + r
    row_out = jnp.where(valid, row_in, jnp.sum(nblk) + over // n_col_tiles)
    col_out = jnp.where(valid, j, over % n_col_tiles)
    as_i32 = lambda a: a.astype(I32)
    return tuple(map(as_i32, (row_in, row_out, e, col_out, first, valid, slot, ne, nj, more,
                              pieces)))


def _ffn1(items, x_pad, w_exp_in, b_exp_in):
    P, W = x_pad.shape
    n_exp, D, F2 = w_exp_in.shape
    F = F2 // 2
    tn = 1024
    nj = F // tn
    n_items = items[0].shape[0]
    bias = b_exp_in.reshape(n_exp, 2, nj, 1, tn)
    return pl.pallas_call(
        functools.partial(_ffn1_kernel, F=F, tn=tn),
        out_shape=jax.ShapeDtypeStruct((P, F), BF16),
        grid_spec=pltpu.PrefetchScalarGridSpec(
            num_scalar_prefetch=11,
            grid=(n_items,),
            in_specs=[pl.BlockSpec((EXPERT_ROWS, W), lambda q, ri, *_: (ri[q], 0)),
                      pl.BlockSpec(memory_space=pl.ANY),
                      pl.BlockSpec((1, 2, 1, 1, tn),
                                   lambda q, ri, ro, e, j, *_: (e[q], 0, j[q], 0, 0))],
            out_specs=pl.BlockSpec((EXPERT_ROWS, tn),
                                   lambda q, ri, ro, e, j, *_: (ro[q], j[q])),
            scratch_shapes=[pltpu.VMEM((2, 2, D, tn), F32),
                            pltpu.VMEM((2, D, tn), BF16),
                            pltpu.SemaphoreType.DMA((2,))]),
        compiler_params=_params(("arbitrary",), VMEM_LIMIT),
        name="expert_in_swiglu",
    )(*items, x_pad, w_exp_in, bias)


def _ffn2(items, act, w_exp_out, b_exp_out):
    P, F = act.shape
    n_exp, _, D = w_exp_out.shape
    n_items = items[0].shape[0]
    return pl.pallas_call(
        _ffn2_kernel,
        out_shape=jax.ShapeDtypeStruct((P, D // 2), U32),
        grid_spec=pltpu.PrefetchScalarGridSpec(
            num_scalar_prefetch=11,
            grid=(n_items,),
            in_specs=[pl.BlockSpec((EXPERT_ROWS, F), lambda q, ri, *_: (ri[q], 0)),
                      pl.BlockSpec(memory_space=pl.ANY),
                      pl.BlockSpec((1, 1, D), lambda q, ri, ro, e, *_: (e[q], 0, 0))],
            out_specs=pl.BlockSpec((EXPERT_ROWS, D // 2), lambda q, ri, ro, *_: (ro[q], 0)),
            scratch_shapes=[pltpu.VMEM((2, F, D), F32),
                            pltpu.VMEM((F, D), BF16),
                            pltpu.SemaphoreType.DMA((2,))]),
        compiler_params=_params(("arbitrary",), VMEM_LIMIT),
        name="expert_out",
    )(*items, act, w_exp_out, b_exp_out[:, None, :])


def _combine_kernel(dest_ref, y_ref, gates_ref, x1_ref, g2_ref, o_ref, buf, sem, *, tm):
    s = pl.program_id(0)
    ns = pl.num_programs(0)

    def gather(step, slot):
        def issue(tt, _):
            t0 = pl.multiple_of(tt * SUBLANES, SUBLANES)
            for r in range(SUBLANES):
                for k in range(TOP_K):
                    d = dest_ref[TOP_K * (step * tm + t0 + r) + k]
                    pltpu.make_async_copy(y_ref.at[pl.ds(d, 1), :],
                                          buf.at[slot, k, pl.ds(t0 + r, 1), :],
                                          sem.at[slot]).start(priority=k % 2)
            return 0

        lax.fori_loop(0, tm // SUBLANES, issue, 0)

    @pl.when(s == 0)
    def _():
        gather(0, 0)

    @pl.when(s + 1 < ns)
    def _():
        gather(s + 1, (s + 1) % 2)

    slot = s % 2
    for k in range(TOP_K):
        pltpu.make_async_copy(y_ref.at[pl.ds(0, tm), :], buf.at[slot, k], sem.at[slot]).wait()
    gates = gates_ref[...]
    y_hi = y_lo = None
    for k in range(TOP_K):
        hi, lo = _unpack_halves(buf[slot, k])
        g = gates[:, k:k + 1]
        y_hi = g * hi if y_hi is None else y_hi + g * hi
        y_lo = g * lo if y_lo is None else y_lo + g * lo
    y = jnp.concatenate([y_hi, y_lo], axis=1)
    o_ref[...] = x1_ref[...] + g2_ref[0] * y


def _combine(dest_flat, y_pad, gates_wide, x1, gate2, S):
    T, D = x1.shape
    tm = 256
    per_b = S // tm
    return pl.pallas_call(
        functools.partial(_combine_kernel, tm=tm),
        out_shape=jax.ShapeDtypeStruct((T, D), F32),
        grid_spec=pltpu.PrefetchScalarGridSpec(
            num_scalar_prefetch=1,
            grid=(T // tm,),
            in_specs=[pl.BlockSpec(memory_space=pl.ANY),
                      pl.BlockSpec((tm, LANES), lambda i, d: (i, 0)),
                      pl.BlockSpec((tm, D), lambda i, d: (i, 0)),
                      pl.BlockSpec((1, 1, D), lambda i, d: (i // per_b, 0, 0))],
            out_specs=pl.BlockSpec((tm, D), lambda i, d: (i, 0)),
            scratch_shapes=[pltpu.VMEM((2, TOP_K, tm, D // 2), U32),
                            pltpu.SemaphoreType.DMA((2,))]),
        compiler_params=_params(("arbitrary",), VMEM_LIMIT),
        name="combine_rows",
    )(dest_flat, y_pad, gates_wide, x1, gate2[:, None, :])


def kernel(x, c, norm1_w, norm2_w, w_ada, b_ada, w_in, q_norm_w, k_norm_w, w_pool, pool_scale,
           w_o, w_router, b_router, w_exp_in, b_exp_in, w_exp_out, b_exp_out):
    B, S, D = x.shape
    T = B * S
    depth = w_ada.shape[0]
    n_exp = w_router.shape[-1]
    pool_width = pool_scale.shape[-1]
    sb_width = w_o.shape[1] - pool_width
    n_heads = sb_width // HEAD_DIM
    n_blocks = (T * TOP_K + n_exp * (EXPERT_ROWS - 1)) // EXPERT_ROWS
    n_rows = n_blocks * EXPERT_ROWS

    x2 = x.reshape(T, D)
    for l in range(depth):
        mod = _adaln(c, w_ada[l], b_ada[l])
        shift1, scale1, gate1, shift2, scale2, gate2 = jnp.split(mod, 6, axis=-1)

        proj = _inproj(x2, norm1_w[l], shift1, scale1, w_in[l].astype(BF16), S)
        proj3 = proj.reshape(B, S, -1)
        o_sb = _attention(proj3, q_norm_w[l], k_norm_w[l], n_heads)
        o_pool = _pool(proj3, w_pool[l], pool_scale[l], pool_width)
        x1, h2p, gates_wide, idx_wide = _outproj(
            o_sb.reshape(T, sb_width), o_pool.reshape(T, pool_width), w_o[l].astype(BF16),
            x2, gate1, shift2, scale2, norm2_w[l], w_router[l], b_router[l], S)

        dest_wide, meta = _route(idx_wide)
        cnt = meta[0, :n_exp]
        starts = meta[1, :n_exp]
        dest_flat = dest_wide[:, :TOP_K].reshape(T * TOP_K)
        x_pad = _dispatch(dest_flat, cnt, starts, h2p, n_rows)

        F = w_exp_out.shape[2]
        act = _ffn1(_work_items(cnt, F // 1024, n_blocks), x_pad, w_exp_in[l], b_exp_in[l])
        y_pad = _ffn2(_work_items(cnt, 1, n_blocks), act, w_exp_out[l], b_exp_out[l])
        x2 = _combine(dest_flat, y_pad, gates_wide, x1, gate2, S)
    return x2.reshape(B, S, D)
```

```python
import functools
import math

import jax
import jax.numpy as jnp
from jax import lax
from jax.experimental import pallas as pl
from jax.experimental.pallas import tpu as pltpu

F32 = jnp.float32
BF16 = jnp.bfloat16
I32 = jnp.int32
U32 = jnp.uint32

EPS = 1e-6
HEAD_DIM = 128
POOL_WINDOWS = (2, 4, 8, 16)
TOP_K = 4
SWIGLU_ALPHA = 1.702
SWIGLU_LIMIT = 7.0

LANES = 128
SUBLANES = 8
EXPERT_ROWS = 256
TAIL_ROWS = 64
MIN_TAIL_PIECES = 2
LOG_UNDERFLOW = 104.0
VMEM_LIMIT = 56 * 1024 * 1024
WEIGHT_DMA_PRIORITY = 1


def _params(sem=None, vmem=None):
    return pltpu.CompilerParams(dimension_semantics=sem, vmem_limit_bytes=vmem)


_HIGH_HALF = 0xFFFF0000


def _pack_rows(v):
    bits = lax.bitcast_convert_type(v.astype(BF16).astype(F32), U32)
    half = v.shape[1] // 2
    return (bits[:, :half] & jnp.uint32(_HIGH_HALF)) | (bits[:, half:] >> 16)


def _unpack_halves(p):
    hi = lax.bitcast_convert_type(p & jnp.uint32(_HIGH_HALF), F32)
    lo = lax.bitcast_convert_type(p << 16, F32)
    return hi, lo


def _unpack_rows(p):
    hi, lo = _unpack_halves(p)
    return jnp.concatenate([hi.astype(BF16), lo.astype(BF16)], axis=1)


def _adaln_kernel(c_ref, w_ref, b_ref, o_ref):
    c = c_ref[...]
    ca = c / (1.0 + jnp.exp(-c))
    o_ref[...] = jnp.dot(ca.astype(BF16), w_ref[...].astype(BF16),
                         preferred_element_type=F32) + b_ref[...]


def _adaln(c, w_ada, b_ada):
    B, D = c.shape
    N = w_ada.shape[1]
    rows = 8
    tn = 1024
    cp = jnp.zeros((rows, D), F32).at[:B].set(c)
    out = pl.pallas_call(
        _adaln_kernel,
        out_shape=jax.ShapeDtypeStruct((rows, N), F32),
        grid=(N // tn,),
        in_specs=[pl.BlockSpec((rows, D), lambda j: (0, 0)),
                  pl.BlockSpec((D, tn), lambda j: (0, j)),
                  pl.BlockSpec((1, tn), lambda j: (0, j))],
        out_specs=pl.BlockSpec((rows, tn), lambda j: (0, j)),
        compiler_params=_params(("arbitrary",), VMEM_LIMIT),
        name="adaln",
    )(cp, w_ada, b_ada.reshape(1, N))
    return out[:B]


def _inproj_kernel(x_ref, nw_ref, sh_ref, sc_ref, w_ref, o_ref, h_ref, *, tm, ch):
    @pl.when(pl.program_id(1) == 0)
    def _():
        mul = nw_ref[...] * (1.0 + sc_ref[0])
        add = sh_ref[0]

        def body(c, _):
            r0 = pl.multiple_of(c * ch, ch)
            x = x_ref[pl.ds(r0, ch), :]
            inv = lax.rsqrt(jnp.mean(x * x, axis=-1, keepdims=True) + EPS)
            h_ref[pl.ds(r0, ch), :] = (x * inv * mul + add).astype(BF16)
            return 0

        lax.fori_loop(0, tm // ch, body, 0)

    o_ref[...] = jnp.dot(h_ref[...], w_ref[...],
                         preferred_element_type=F32).astype(o_ref.dtype)


def _inproj(x2, norm_w, shift, scale, w_bf, S):
    T, D = x2.shape
    N = w_bf.shape[1]
    tm, tn, ch = 1024, 1024, 128
    per_b = S // tm
    return pl.pallas_call(
        functools.partial(_inproj_kernel, tm=tm, ch=ch),
        out_shape=jax.ShapeDtypeStruct((T, N), BF16),
        grid=(T // tm, N // tn),
        in_specs=[pl.BlockSpec((tm, D), lambda i, j: (i, 0)),
                  pl.BlockSpec((1, D), lambda i, j: (0, 0)),
                  pl.BlockSpec((1, 1, D), lambda i, j: (i // per_b, 0, 0)),
                  pl.BlockSpec((1, 1, D), lambda i, j: (i // per_b, 0, 0)),
                  pl.BlockSpec((D, tn), lambda i, j: (0, j))],
        out_specs=pl.BlockSpec((tm, tn), lambda i, j: (i, j)),
        scratch_shapes=[pltpu.VMEM((tm, D), BF16)],
        compiler_params=_params(("arbitrary", "arbitrary"), VMEM_LIMIT),
        name="inproj",
    )(x2, norm_w.reshape(1, D), shift[:, None, :], scale[:, None, :], w_bf)


def _attn_kernel(q_ref, k_ref, v_ref, qw_ref, kw_ref, o_ref, kn_ref, carry_ref, acc_ref,
                 *, S, tq, hg, scale):
    i = pl.program_id(2)
    d = HEAD_DIM

    def head_norm(x, w):
        parts = []
        for h in range(hg):
            xh = x[:, h * d:(h + 1) * d]
            inv = lax.rsqrt(jnp.mean(xh * xh, axis=-1, keepdims=True) + EPS)
            parts.append(xh * inv * w)
        return parts

    @pl.when(i == 0)
    def _():
        def body(c, _):
            r0 = pl.multiple_of(c * tq, tq)
            parts = head_norm(k_ref[0, pl.ds(r0, tq), :].astype(F32), kw_ref[...])
            for h in range(hg):
                kn_ref[pl.ds(r0, tq), h * d:(h + 1) * d] = parts[h].astype(BF16)
            return 0

        lax.fori_loop(0, S // tq, body, 0)

    qb = [(p * scale).astype(BF16) for p in head_norm(q_ref[0].astype(F32), qw_ref[...])]

    row = lax.broadcasted_iota(I32, (tq, tq), 0)
    col = lax.broadcasted_iota(I32, (tq, tq), 1)
    causal = col < row
    tri = (row > col).astype(BF16)

    def scores(h, r0, mask):
        kblk = kn_ref[pl.ds(r0, tq), h * d:(h + 1) * d]
        z = lax.dot_general(qb[h], kblk, (((1,), (1,)), ((), ())), preferred_element_type=F32)
        t = jnp.log(1.0 + jnp.exp(-jnp.abs(z)))
        lsn = jnp.minimum(-z, 0.0) - t
        lsp = lsn + z
        if mask:
            lsn = jnp.where(causal, lsn, 0.0)
        later = jnp.dot(lsn.astype(BF16), tri, preferred_element_type=F32)
        return lsp + later, later[:, :1] + lsn[:, :1]

    def weighted(a, h, r0):
        vblk = v_ref[0, pl.ds(r0, tq), h * d:(h + 1) * d]
        return jnp.dot(a.astype(BF16), vblk, preferred_element_type=F32)

    has_prev = i > 0
    rd = pl.multiple_of(i * tq, tq)
    rp = pl.multiple_of(jnp.maximum(i - 1, 0) * tq, tq)
    worst = None
    for h in range(hg):
        cols = slice(h * d, (h + 1) * d)
        log_d, sum_d = scores(h, rd, True)
        log_p, sum_p = scores(h, rp, False)
        a_d = jnp.where(causal, jnp.exp(log_d), 0.0)
        a_p = jnp.where(has_prev, jnp.exp(log_p + sum_d), 0.0)
        acc_ref[:, cols] = weighted(a_d, h, rd) + weighted(a_p, h, rp)
        carry = jnp.where(has_prev, sum_d + sum_p, sum_d)
        carry_ref[h] = carry
        m = jnp.max(carry)
        worst = m if worst is None else jnp.maximum(worst, m)

    def earlier(kb):
        r0 = pl.multiple_of(kb * tq, tq)
        worst = None
        for h in range(hg):
            cols = slice(h * d, (h + 1) * d)
            log_a, row_sum = scores(h, r0, False)
            acc_ref[:, cols] += weighted(jnp.exp(log_a + carry_ref[h]), h, r0)
            carry = carry_ref[h] + row_sum
            carry_ref[h] = carry
            m = jnp.max(carry)
            worst = m if worst is None else jnp.maximum(worst, m)
        return worst

    def cond(st):
        kb, m = st
        return jnp.logical_and(kb >= 0, m > -LOG_UNDERFLOW)

    def body(st):
        kb, _ = st
        return kb - 1, earlier(kb)

    lax.while_loop(cond, body, (i - 2, worst))
    o_ref[0] = acc_ref[...].astype(o_ref.dtype)


def _attention(proj3, q_norm_w, k_norm_w, n_heads):
    B, S, _ = proj3.shape
    d = HEAD_DIM
    tq = 256
    hg = 8
    G = n_heads // hg
    w = hg * d
    return pl.pallas_call(
        functools.partial(_attn_kernel, S=S, tq=tq, hg=hg, scale=1.0 / math.sqrt(d)),
        out_shape=jax.ShapeDtypeStruct((B, S, n_heads * d), BF16),
        grid=(B, G, S // tq),
        in_specs=[pl.BlockSpec((1, tq, w), lambda b, g, i: (b, i, g)),
                  pl.BlockSpec((1, S, w), lambda b, g, i: (b, 0, G + g)),
                  pl.BlockSpec((1, S, w), lambda b, g, i: (b, 0, 2 * G + g)),
                  pl.BlockSpec((1, d), lambda b, g, i: (0, 0)),
                  pl.BlockSpec((1, d), lambda b, g, i: (0, 0))],
        out_specs=pl.BlockSpec((1, tq, w), lambda b, g, i: (b, i, g)),
        scratch_shapes=[pltpu.VMEM((S, w), BF16),
                        pltpu.VMEM((hg, tq, 1), F32),
                        pltpu.VMEM((tq, w), F32)],
        compiler_params=_params(("arbitrary", "arbitrary", "arbitrary"), VMEM_LIMIT),
        name="stickbreak_attn",
    )(proj3, proj3, proj3, q_norm_w.reshape(1, d), k_norm_w.reshape(1, d))


def _pool_kernel(u_ref, w_ref, ps_ref, o_ref, *, S, ch, gd):
    halo = 16
    wgs = [w_ref[g].astype(BF16) for g in range(len(POOL_WINDOWS))]

    def body(c, _):
        r0 = pl.multiple_of(c * ch, ch)
        p0 = pl.multiple_of(jnp.maximum(r0 - halo, 0), halo)
        t = r0 + lax.broadcasted_iota(I32, (ch, 1), 0)
        for g, win in enumerate(POOL_WINDOWS):
            lo, hi = g * gd, (g + 1) * gd
            cur = u_ref[0, pl.ds(r0, ch), lo:hi].astype(F32)
            prev = u_ref[0, pl.ds(p0, halo), lo:hi].astype(F32)
            prev = jnp.where(c > 0, prev, 0.0)
            s = jnp.concatenate([prev, cur], axis=0)
            n = 1
            while n < win:
                s = s + pltpu.roll(s, n, 0)
                n *= 2
            s = s[halo:]
            cnt = jnp.minimum(t + 1, win).astype(F32)
            p = s / cnt - cur
            y = jnp.dot(p.astype(BF16), wgs[g], preferred_element_type=F32) * ps_ref[:, lo:hi]
            o_ref[0, pl.ds(r0, ch), lo:hi] = y.astype(o_ref.dtype)
        return 0

    lax.fori_loop(0, S // ch, body, 0)


def _pool(proj3, w_pool, pool_scale, pool_width):
    B, S, NP = proj3.shape
    G, gd, _ = w_pool.shape
    return pl.pallas_call(
        functools.partial(_pool_kernel, S=S, ch=256, gd=gd),
        out_shape=jax.ShapeDtypeStruct((B, S, pool_width), BF16),
        grid=(B,),
        in_specs=[pl.BlockSpec((1, S, pool_width), lambda b: (b, 0, NP // pool_width - 1)),
                  pl.BlockSpec((G, gd, gd), lambda b: (0, 0, 0)),
                  pl.BlockSpec((1, pool_width), lambda b: (0, 0))],
        out_specs=pl.BlockSpec((1, S, pool_width), lambda b: (b, 0, 0)),
        compiler_params=_params(("arbitrary",), VMEM_LIMIT),
        name="pool_mixer",
    )(proj3, w_pool, pool_scale.reshape(1, pool_width))


def _outproj_kernel(osb_ref, opool_ref, wo_ref, x_ref, g1_ref, sh_ref, sc_ref, nw_ref,
                    wr_ref, br_ref, x1_ref, h2p_ref, gates_ref, idx_ref, *, sbw, n_exp, sub):
    for r0 in range(0, x_ref.shape[0], sub):
        rows = slice(r0, r0 + sub)
        _outproj_rows(osb_ref.at[rows], opool_ref.at[rows], wo_ref, x_ref.at[rows], g1_ref,
                      sh_ref, sc_ref, nw_ref, wr_ref, br_ref, x1_ref.at[rows], h2p_ref.at[rows],
                      gates_ref.at[rows], idx_ref.at[rows], sbw=sbw, n_exp=n_exp)


def _outproj_rows(osb_ref, opool_ref, wo_ref, x_ref, g1_ref, sh_ref, sc_ref, nw_ref,
                  wr_ref, br_ref, x1_ref, h2p_ref, gates_ref, idx_ref, *, sbw, n_exp):
    tm, D = x_ref.shape
    mixed = (jnp.dot(osb_ref[...], wo_ref[:sbw, :], preferred_element_type=F32)
             + jnp.dot(opool_ref[...], wo_ref[sbw:, :], preferred_element_type=F32))
    x1 = x_ref[...] + g1_ref[0] * mixed
    x1_ref[...] = x1
    inv = lax.rsqrt(jnp.mean(x1 * x1, axis=-1, keepdims=True) + EPS)
    h2 = x1 * inv * (nw_ref[...] * (1.0 + sc_ref[0])) + sh_ref[0]
    hb = h2.astype(BF16)
    h2p_ref[...] = _pack_rows(h2)

    vals = lax.dot_general(wr_ref[...].astype(BF16), hb, (((1,), (1,)), ((), ())),
                           preferred_element_type=F32) + br_ref[...]
    expert = lax.broadcasted_iota(I32, (n_exp, tm), 0).astype(F32)
    tops, ids = [], []
    for _ in range(TOP_K):
        m = jnp.max(vals, axis=0, keepdims=True)
        first = jnp.min(jnp.where(vals == m, expert, float(n_exp)), axis=0, keepdims=True)
        tops.append(m)
        ids.append(first)
        vals = jnp.where(expert == first, -jnp.inf, vals)
    es = [jnp.exp(m - tops[0]) for m in tops]
    den = es[0]
    for e in es[1:]:
        den = den + e
    slot = lax.broadcasted_iota(I32, (LANES, tm), 0)
    gates = jnp.zeros((LANES, tm), F32)
    idx = jnp.zeros((LANES, tm), F32)
    for k in range(TOP_K):
        gates = jnp.where(slot == k, es[k] / den, gates)
        idx = jnp.where(slot == k, ids[k], idx)
    gates_ref[...] = gates.T
    idx_ref[...] = idx.T.astype(I32)


def _outproj(o_sb, o_pool, wo_bf, x2, gate1, shift2, scale2, norm2_w, w_router, b_router, S):
    T, D = x2.shape
    sbw = o_sb.shape[1]
    pw = o_pool.shape[1]
    n_exp = w_router.shape[1]
    tm, sub = 512, 256
    per_b = S // tm
    wr = w_router.T
    br = b_router.reshape(n_exp, 1)
    mod_spec = pl.BlockSpec((1, 1, D), lambda i: (i // per_b, 0, 0))
    return pl.pallas_call(
        functools.partial(_outproj_kernel, sbw=sbw, n_exp=n_exp, sub=sub),
        out_shape=(jax.ShapeDtypeStruct((T, D), F32),
                   jax.ShapeDtypeStruct((T, D // 2), U32),
                   jax.ShapeDtypeStruct((T, LANES), F32),
                   jax.ShapeDtypeStruct((T, LANES), I32)),
        grid=(T // tm,),
        in_specs=[pl.BlockSpec((tm, sbw), lambda i: (i, 0)),
                  pl.BlockSpec((tm, pw), lambda i: (i, 0)),
                  pl.BlockSpec((sbw + pw, D), lambda i: (0, 0)),
                  pl.BlockSpec((tm, D), lambda i: (i, 0)),
                  mod_spec, mod_spec, mod_spec,
                  pl.BlockSpec((1, D), lambda i: (0, 0)),
                  pl.BlockSpec((n_exp, D), lambda i: (0, 0)),
                  pl.BlockSpec((n_exp, 1), lambda i: (0, 0))],
        out_specs=(pl.BlockSpec((tm, D), lambda i: (i, 0)),
                   pl.BlockSpec((tm, D // 2), lambda i: (i, 0)),
                   pl.BlockSpec((tm, LANES), lambda i: (i, 0)),
                   pl.BlockSpec((tm, LANES), lambda i: (i, 0))),
        compiler_params=_params(("arbitrary",), VMEM_LIMIT),
        name="outproj_router",
    )(o_sb, o_pool, wo_bf, x2, gate1[:, None, :], shift2[:, None, :], scale2[:, None, :],
      norm2_w.reshape(1, D), wr, br)


def _route_kernel(idx_ref, dest_ref, meta_ref, rank_ref, *, T, ch):
    lane = lax.broadcasted_iota(I32, (ch, LANES), 1)
    row = lax.broadcasted_iota(I32, (ch, ch), 0)
    col = lax.broadcasted_iota(I32, (ch, ch), 1)
    before = (col < row).astype(BF16)

    def load(c):
        return idx_ref[pl.ds(pl.multiple_of(c * ch, ch), ch), :]

    def count(c, cnt):
        ii = load(c)
        member = lane == ii[:, 0:1]
        for k in range(1, TOP_K):
            member = jnp.logical_or(member, lane == ii[:, k:k + 1])
        mf = jnp.where(member, 1.0, 0.0)
        rank = jnp.dot(before, mf.astype(BF16), preferred_element_type=F32) + cnt
        rank_ref[pl.ds(pl.multiple_of(c * ch, ch), ch), :] = rank
        return cnt + jnp.sum(mf, axis=0, keepdims=True)

    cnt = lax.fori_loop(0, T // ch, count, jnp.zeros((1, LANES), F32), unroll=2)
    padded = jnp.ceil(cnt / EXPERT_ROWS) * EXPERT_ROWS
    rows = 8
    lane8 = lax.broadcasted_iota(I32, (rows, LANES), 1)
    ends = jnp.broadcast_to(padded, (rows, LANES))
    sh = 1
    while sh < LANES:
        ends = ends + jnp.where(lane8 >= sh, pltpu.roll(ends, sh, 1), 0.0)
        sh *= 2
    starts = ends - padded
    sub8 = lax.broadcasted_iota(I32, (rows, LANES), 0)
    meta = jnp.where(sub8 == 0, cnt, jnp.where(sub8 == 1, starts, padded))
    meta_ref[...] = meta.astype(I32)
    start_row = starts[0:1, :]

    def place(c, _):
        ii = load(c)
        val = rank_ref[pl.ds(pl.multiple_of(c * ch, ch), ch), :] + start_row
        out = jnp.zeros((ch, LANES), F32)
        for k in range(TOP_K):
            d = jnp.sum(jnp.where(lane == ii[:, k:k + 1], val, 0.0), axis=-1, keepdims=True)
            out = jnp.where(lane == k, d, out)
        dest_ref[pl.ds(pl.multiple_of(c * ch, ch), ch), :] = out.astype(I32)
        return 0

    lax.fori_loop(0, T // ch, place, 0, unroll=2)


def _route(idx_wide):
    T = idx_wide.shape[0]
    return pl.pallas_call(
        functools.partial(_route_kernel, T=T, ch=256),
        out_shape=(jax.ShapeDtypeStruct((T, LANES), I32),
                   jax.ShapeDtypeStruct((8, LANES), I32)),
        grid=(1,),
        in_specs=[pl.BlockSpec((T, LANES), lambda i: (0, 0))],
        out_specs=(pl.BlockSpec((T, LANES), lambda i: (0, 0)),
                   pl.BlockSpec((8, LANES), lambda i: (0, 0))),
        scratch_shapes=[pltpu.VMEM((T, LANES), F32)],
        compiler_params=_params(("arbitrary",), VMEM_LIMIT),
        name="route_ranks",
    )(idx_wide)


def _dispatch_kernel(dest_ref, cnt_ref, start_ref, h_ref, x_ref, z_ref, sem, zsem,
                     *, tb, n_exp, n_blocks):
    s = pl.program_id(0)

    @pl.when(s == 0)
    def _():
        z_ref[...] = jnp.zeros_like(z_ref)
        _dispatch_zero_fill(cnt_ref, start_ref, x_ref, z_ref, zsem, n_exp, n_blocks)

    def issue(tt, _):
        t0 = pl.multiple_of(tt * SUBLANES, SUBLANES)
        for r in range(SUBLANES):
            for k in range(TOP_K):
                d = dest_ref[TOP_K * (s * tb + t0 + r) + k]
                pltpu.make_async_copy(h_ref.at[pl.ds(t0 + r, 1), :], x_ref.at[pl.ds(d, 1), :],
                                      sem).start(priority=k % 2)
        return 0

    lax.fori_loop(0, tb // SUBLANES, issue, 0)

    for _ in range(TOP_K):
        pltpu.make_async_copy(h_ref, x_ref.at[pl.ds(0, tb), :], sem).wait()


def _dispatch_zero_fill(cnt_ref, start_ref, x_ref, z_ref, zsem, n_exp, n_blocks):
    def block_fill(blk, wait):
        r0 = pl.multiple_of(blk * EXPERT_ROWS, EXPERT_ROWS)
        cp = pltpu.make_async_copy(z_ref, x_ref.at[pl.ds(r0, EXPERT_ROWS), :], zsem)
        if wait:
            cp.wait()
        else:
            cp.start()

    def zero_fill(e, wait):
        cnt = cnt_ref[e]

        @pl.when((cnt & (EXPERT_ROWS - 1)) != 0)
        def _():
            block_fill((start_ref[e] + cnt) // EXPERT_ROWS, wait)

        return 0

    used = (start_ref[n_exp - 1] + cnt_ref[n_exp - 1] + EXPERT_ROWS - 1) // EXPERT_ROWS

    def tail_fill(blk, wait):
        block_fill(blk, wait)
        return 0

    lax.fori_loop(0, n_exp, lambda e, _: zero_fill(e, False), 0)
    lax.fori_loop(used, n_blocks, lambda b, _: tail_fill(b, False), 0)
    lax.fori_loop(0, n_exp, lambda e, _: zero_fill(e, True), 0)
    lax.fori_loop(used, n_blocks, lambda b, _: tail_fill(b, True), 0)


def _dispatch(dest_flat, cnt, starts, h2p, n_rows):
    T, W = h2p.shape
    n_exp = cnt.shape[0]
    tb = 1024
    return pl.pallas_call(
        functools.partial(_dispatch_kernel, tb=tb, n_exp=n_exp, n_blocks=n_rows // EXPERT_ROWS),
        out_shape=jax.ShapeDtypeStruct((n_rows, W), U32),
        grid_spec=pltpu.PrefetchScalarGridSpec(
            num_scalar_prefetch=3,
            grid=(T // tb,),
            in_specs=[pl.BlockSpec((tb, W), lambda s, *_: (s, 0))],
            out_specs=pl.BlockSpec(memory_space=pl.ANY),
            scratch_shapes=[pltpu.VMEM((EXPERT_ROWS, W), U32),
                            pltpu.SemaphoreType.DMA, pltpu.SemaphoreType.DMA]),
        compiler_params=_params(("arbitrary",), VMEM_LIMIT),
        name="dispatch_rows",
    )(dest_flat, cnt, starts, h2p)


def _ffn1_kernel(ri_ref, ro_ref, e_ref, j_ref, first_ref, valid_ref, slot_ref, ne_ref, nj_ref, more_ref, pieces_ref,
                 x_ref, w_ref, b_ref, o_ref, stage, wbf, sem, *, F, tn):
    q = pl.program_id(0)

    def wcopy(e, j, slot, part):
        c0 = pl.multiple_of(part * F + j * tn, tn)
        return pltpu.make_async_copy(w_ref.at[e, :, pl.ds(c0, tn)], stage.at[slot, part],
                                     sem.at[slot])

    @pl.when(q == 0)
    def _():
        for part in range(2):
            wcopy(e_ref[0], j_ref[0], 0, part).start(priority=WEIGHT_DMA_PRIORITY)

    @pl.when(first_ref[q] == 1)
    def _():
        slot = slot_ref[q]
        for part in range(2):
            wcopy(e_ref[q], j_ref[q], slot, part).wait()

        @pl.when(more_ref[q] == 1)
        def _():
            for part in range(2):
                wcopy(ne_ref[q], nj_ref[q], 1 - slot, part).start(priority=WEIGHT_DMA_PRIORITY)

    def swiglu_block(w_gate, w_lin, rows):
        xb = _unpack_rows(x_ref[:rows, :])
        g = jnp.dot(xb, w_gate, preferred_element_type=F32) + b_ref[0, 0, 0]
        lin = jnp.dot(xb, w_lin, preferred_element_type=F32) + b_ref[0, 1, 0]
        g = jnp.minimum(g, SWIGLU_LIMIT)
        lin = jnp.clip(lin, -SWIGLU_LIMIT, SWIGLU_LIMIT)
        act = g / (1.0 + jnp.exp(-SWIGLU_ALPHA * g)) * (lin + 1.0)
        o_ref[:rows, :] = act.astype(o_ref.dtype)
        if rows < EXPERT_ROWS:
            o_ref[rows:, :] = jnp.zeros((EXPERT_ROWS - rows, o_ref.shape[1]), o_ref.dtype)

    @pl.when(first_ref[q] == 1)
    def _():
        slot = slot_ref[q]
        w16 = [stage[slot, part].astype(BF16) for part in range(2)]
        for part in range(2):
            wbf[part] = w16[part]
        swiglu_block(w16[0], w16[1], EXPERT_ROWS)

    later = jnp.logical_and(valid_ref[q] == 1, first_ref[q] == 0)

    for pieces in range(MIN_TAIL_PIECES, EXPERT_ROWS // TAIL_ROWS + 1):
        @pl.when(jnp.logical_and(later, pieces_ref[q] == pieces))
        def _(pieces=pieces):
            swiglu_block(wbf[0], wbf[1], pieces * TAIL_ROWS)

    @pl.when(valid_ref[q] == 0)
    def _():
        o_ref[...] = jnp.zeros_like(o_ref)


def _ffn2_kernel(ri_ref, ro_ref, e_ref, j_ref, first_ref, valid_ref, slot_ref, ne_ref, nj_ref, more_ref, pieces_ref,
                 a_ref, w_ref, b_ref, o_ref, stage, wbf, sem):
    q = pl.program_id(0)

    def wcopy(e, slot):
        return pltpu.make_async_copy(w_ref.at[e], stage.at[slot], sem.at[slot])

    @pl.when(q == 0)
    def _():
        wcopy(e_ref[0], 0).start(priority=WEIGHT_DMA_PRIORITY)

    @pl.when(first_ref[q] == 1)
    def _():
        slot = slot_ref[q]
        wcopy(e_ref[q], slot).wait()

        @pl.when(more_ref[q] == 1)
        def _():
            wcopy(ne_ref[q], 1 - slot).start(priority=WEIGHT_DMA_PRIORITY)

    def out_block(w, rows):
        y = jnp.dot(a_ref[:rows, :], w, preferred_element_type=F32) + b_ref[0]
        o_ref[:rows, :] = _pack_rows(y)
        if rows < EXPERT_ROWS:
            o_ref[rows:, :] = jnp.zeros((EXPERT_ROWS - rows, o_ref.shape[1]), o_ref.dtype)

    @pl.when(first_ref[q] == 1)
    def _():
        w16 = stage[slot_ref[q]].astype(BF16)
        wbf[...] = w16
        out_block(w16, EXPERT_ROWS)

    later = jnp.logical_and(valid_ref[q] == 1, first_ref[q] == 0)

    for pieces in range(MIN_TAIL_PIECES, EXPERT_ROWS // TAIL_ROWS + 1):
        @pl.when(jnp.logical_and(later, pieces_ref[q] == pieces))
        def _(pieces=pieces):
            out_block(wbf[...], pieces * TAIL_ROWS)

    @pl.when(valid_ref[q] == 0)
    def _():
        o_ref[...] = jnp.zeros_like(o_ref)


def _work_items(cnt, n_col_tiles, n_blocks):
    n_exp = cnt.shape[0]
    nblk = (cnt + EXPERT_ROWS - 1) // EXPERT_ROWS
    bstart = jnp.cumsum(nblk) - nblk
    gsize = jnp.repeat(nblk, n_col_tiles)
    gend = jnp.cumsum(gsize)
    n_groups = n_exp * n_col_tiles
    gid = jnp.arange(n_groups, dtype=I32)
    total = gend[-1]
    q = jnp.arange(n_blocks * n_col_tiles, dtype=I32)
    qc = jnp.minimum(q, total - 1)
    g = jnp.sum((gend[None, :] <= qc[:, None]).astype(I32), axis=1)
    nonempty = gsize > 0
    ordinal = jnp.cumsum(nonempty.astype(I32)) - 1
    nxt_incl = lax.cummin(jnp.where(nonempty, gid, n_groups), reverse=True)
    nxt = jnp.concatenate([nxt_incl[1:], jnp.full((1,), n_groups, I32)])
    more = nxt < n_groups
    nxt = jnp.minimum(nxt, n_groups - 1)
    per_group = jnp.stack([gend - gsize, gid // n_col_tiles, gid % n_col_tiles,
                           jnp.repeat(bstart, n_col_tiles), ordinal % 2,
                           nxt // n_col_tiles, nxt % n_col_tiles, more.astype(I32),
                           jnp.repeat(cnt, n_col_tiles)])
    pick = (g[None, :, None] == gid[None, None, :]).astype(I32)
    gstart, e, j, brow, slot, ne, nj, more, rows = jnp.sum(pick * per_group[:, None, :], axis=2)
    r = qc - gstart
    valid = q < total
    first = jnp.logical_and(valid, r == 0)
    pieces = jnp.clip((rows - r * EXPERT_ROWS + TAIL_ROWS - 1) // TAIL_ROWS, MIN_TAIL_PIECES,
                      EXPERT_ROWS // TAIL_ROWS)
    over = q - total
    row_in = brow + r
    row_out = jnp.where(valid, row_in, jnp.sum(nblk) + over // n_col_tiles)
    col_out = jnp.where(valid, j, over % n_col_tiles)
    as_i32 = lambda a: a.astype(I32)
    return tuple(map(as_i32, (row_in, row_out, e, col_out, first, valid, slot, ne, nj, more,
                              pieces)))


def _ffn1(items, x_pad, w_exp_in, b_exp_in):
    P, W = x_pad.shape
    n_exp, D, F2 = w_exp_in.shape
    F = F2 // 2
    tn = 1024
    nj = F // tn
    n_items = items[0].shape[0]
    bias = b_exp_in.reshape(n_exp, 2, nj, 1, tn)
    return pl.pallas_call(
        functools.partial(_ffn1_kernel, F=F, tn=tn),
        out_shape=jax.ShapeDtypeStruct((P, F), BF16),
        grid_spec=pltpu.PrefetchScalarGridSpec(
            num_scalar_prefetch=11,
            grid=(n_items,),
            in_specs=[pl.BlockSpec((EXPERT_ROWS, W), lambda q, ri, *_: (ri[q], 0)),
                      pl.BlockSpec(memory_space=pl.ANY),
                      pl.BlockSpec((1, 2, 1, 1, tn),
                                   lambda q, ri, ro, e, j, *_: (e[q], 0, j[q], 0, 0))],
            out_specs=pl.BlockSpec((EXPERT_ROWS, tn),
                                   lambda q, ri, ro, e, j, *_: (ro[q], j[q])),
            scratch_shapes=[pltpu.VMEM((2, 2, D, tn), F32),
                            pltpu.VMEM((2, D, tn), BF16),
                            pltpu.SemaphoreType.DMA((2,))]),
        compiler_params=_params(("arbitrary",), VMEM_LIMIT),
        name="expert_in_swiglu",
    )(*items, x_pad, w_exp_in, bias)


def _ffn2(items, act, w_exp_out, b_exp_out):
    P, F = act.shape
    n_exp, _, D = w_exp_out.shape
    n_items = items[0].shape[0]
    return pl.pallas_call(
        _ffn2_kernel,
        out_shape=jax.ShapeDtypeStruct((P, D // 2), U32),
        grid_spec=pltpu.PrefetchScalarGridSpec(
            num_scalar_prefetch=11,
            grid=(n_items,),
            in_specs=[pl.BlockSpec((EXPERT_ROWS, F), lambda q, ri, *_: (ri[q], 0)),
                      pl.BlockSpec(memory_space=pl.ANY),
                      pl.BlockSpec((1, 1, D), lambda q, ri, ro, e, *_: (e[q], 0, 0))],
            out_specs=pl.BlockSpec((EXPERT_ROWS, D // 2), lambda q, ri, ro, *_: (ro[q], 0)),
            scratch_shapes=[pltpu.VMEM((2, F, D), F32),
                            pltpu.VMEM((F, D), BF16),
                            pltpu.SemaphoreType.DMA((2,))]),
        compiler_params=_params(("arbitrary",), VMEM_LIMIT),
        name="expert_out",
    )(*items, act, w_exp_out, b_exp_out[:, None, :])


def _combine_kernel(dest_ref, y_ref, gates_ref, x1_ref, g2_ref, o_ref, buf, sem, *, tm):
    s = pl.program_id(0)
    ns = pl.num_programs(0)

    def gather(step, slot):
        def issue(tt, _):
            t0 = pl.multiple_of(tt * SUBLANES, SUBLANES)
            for r in range(SUBLANES):
                for k in range(TOP_K):
                    d = dest_ref[TOP_K * (step * tm + t0 + r) + k]
                    pltpu.make_async_copy(y_ref.at[pl.ds(d, 1), :],
                                          buf.at[slot, k, pl.ds(t0 + r, 1), :],
                                          sem.at[slot]).start(priority=k % 2)
            return 0

        lax.fori_loop(0, tm // SUBLANES, issue, 0)

    @pl.when(s == 0)
    def _():
        gather(0, 0)

    @pl.when(s + 1 < ns)
    def _():
        gather(s + 1, (s + 1) % 2)

    slot = s % 2
    for k in range(TOP_K):
        pltpu.make_async_copy(y_ref.at[pl.ds(0, tm), :], buf.at[slot, k], sem.at[slot]).wait()
    gates = gates_ref[...]
    y_hi = y_lo = None
    for k in range(TOP_K):
        hi, lo = _unpack_halves(buf[slot, k])
        g = gates[:, k:k + 1]
        y_hi = g * hi if y_hi is None else y_hi + g * hi
        y_lo = g * lo if y_lo is None else y_lo + g * lo
    y = jnp.concatenate([y_hi, y_lo], axis=1)
    o_ref[...] = x1_ref[...] + g2_ref[0] * y


def _combine(dest_flat, y_pad, gates_wide, x1, gate2, S):
    T, D = x1.shape
    tm = 256
    per_b = S // tm
    return pl.pallas_call(
        functools.partial(_combine_kernel, tm=tm),
        out_shape=jax.ShapeDtypeStruct((T, D), F32),
        grid_spec=pltpu.PrefetchScalarGridSpec(
            num_scalar_prefetch=1,
            grid=(T // tm,),
            in_specs=[pl.BlockSpec(memory_space=pl.ANY),
                      pl.BlockSpec((tm, LANES), lambda i, d: (i, 0)),
                      pl.BlockSpec((tm, D), lambda i, d: (i, 0)),
                      pl.BlockSpec((1, 1, D), lambda i, d: (i // per_b, 0, 0))],
            out_specs=pl.BlockSpec((tm, D), lambda i, d: (i, 0)),
            scratch_shapes=[pltpu.VMEM((2, TOP_K, tm, D // 2), U32),
                            pltpu.SemaphoreType.DMA((2,))]),
        compiler_params=_params(("arbitrary",), VMEM_LIMIT),
        name="combine_rows",
    )(dest_flat, y_pad, gates_wide, x1, gate2[:, None, :])


def kernel(x, c, norm1_w, norm2_w, w_ada, b_ada, w_in, q_norm_w, k_norm_w, w_pool, pool_scale,
           w_o, w_router, b_router, w_exp_in, b_exp_in, w_exp_out, b_exp_out):
    B, S, D = x.shape
    T = B * S
    depth = w_ada.shape[0]
    n_exp = w_router.shape[-1]
    pool_width = pool_scale.shape[-1]
    sb_width = w_o.shape[1] - pool_width
    n_heads = sb_width // HEAD_DIM
    n_blocks = (T * TOP_K + n_exp * (EXPERT_ROWS - 1)) // EXPERT_ROWS
    n_rows = n_blocks * EXPERT_ROWS

    x2 = x.reshape(T, D)
    for l in range(depth):
        mod = _adaln(c, w_ada[l], b_ada[l])
        shift1, scale1, gate1, shift2, scale2, gate2 = jnp.split(mod, 6, axis=-1)

        proj = _inproj(x2, norm1_w[l], shift1, scale1, w_in[l].astype(BF16), S)
        proj3 = proj.reshape(B, S, -1)
        o_sb = _attention(proj3, q_norm_w[l], k_norm_w[l], n_heads)
        o_pool = _pool(proj3, w_pool[l], pool_scale[l], pool_width)
        x1, h2p, gates_wide, idx_wide = _outproj(
            o_sb.reshape(T, sb_width), o_pool.reshape(T, pool_width), w_o[l].astype(BF16),
            x2, gate1, shift2, scale2, norm2_w[l], w_router[l], b_router[l], S)

        dest_wide, meta = _route(idx_wide)
        cnt = meta[0, :n_exp]
        starts = meta[1, :n_exp]
        dest_flat = dest_wide[:, :TOP_K].reshape(T * TOP_K)
        x_pad = _dispatch(dest_flat, cnt, starts, h2p, n_rows)

        F = w_exp_out.shape[2]
        act = _ffn1(_work_items(cnt, F // 1024, n_blocks), x_pad, w_exp_in[l], b_exp_in[l])
        y_pad = _ffn2(_work_items(cnt, 1, n_blocks), act, w_exp_out[l], b_exp_out[l])
        x2 = _combine(dest_flat, y_pad, gates_wide, x1, gate2, S)
    return x2.reshape(B, S, D)
```

```python
import functools
import math

import jax
import jax.numpy as jnp
from jax import lax
from jax.experimental import pallas as pl
from jax.experimental.pallas import tpu as pltpu

F32 = jnp.float32
BF16 = jnp.bfloat16
I32 = jnp.int32
U32 = jnp.uint32

EPS = 1e-6
HEAD_DIM = 128
POOL_WINDOWS = (2, 4, 8, 16)
TOP_K = 4
SWIGLU_ALPHA = 1.702
SWIGLU_LIMIT = 7.0

LANES = 128
SUBLANES = 8
EXPERT_ROWS = 256
TAIL_ROWS = 64
MIN_TAIL_PIECES = 2
LOG_UNDERFLOW = 104.0
VMEM_LIMIT = 56 * 1024 * 1024
WEIGHT_DMA_PRIORITY = 1


def _params(sem=None, vmem=None):
    return pltpu.CompilerParams(dimension_semantics=sem, vmem_limit_bytes=vmem)


_HIGH_HALF = 0xFFFF0000


def _pack_rows(v):
    bits = lax.bitcast_convert_type(v.astype(BF16).astype(F32), U32)
    half = v.shape[1] // 2
    return (bits[:, :half] & jnp.uint32(_HIGH_HALF)) | (bits[:, half:] >> 16)


def _unpack_halves(p):
    hi = lax.bitcast_convert_type(p & jnp.uint32(_HIGH_HALF), F32)
    lo = lax.bitcast_convert_type(p << 16, F32)
    return hi, lo


def _unpack_rows(p):
    hi, lo = _unpack_halves(p)
    return jnp.concatenate([hi.astype(BF16), lo.astype(BF16)], axis=1)


def _adaln_kernel(c_ref, w_ref, b_ref, o_ref):
    c = c_ref[...]
    ca = c / (1.0 + jnp.exp(-c))
    o_ref[...] = jnp.dot(ca.astype(BF16), w_ref[...].astype(BF16),
                         preferred_element_type=F32) + b_ref[...]


def _adaln(c, w_ada, b_ada):
    B, D = c.shape
    N = w_ada.shape[1]
    rows = 8
    tn = 1024
    cp = jnp.zeros((rows, D), F32).at[:B].set(c)
    out = pl.pallas_call(
        _adaln_kernel,
        out_shape=jax.ShapeDtypeStruct((rows, N), F32),
        grid=(N // tn,),
        in_specs=[pl.BlockSpec((rows, D), lambda j: (0, 0)),
                  pl.BlockSpec((D, tn), lambda j: (0, j)),
                  pl.BlockSpec((1, tn), lambda j: (0, j))],
        out_specs=pl.BlockSpec((rows, tn), lambda j: (0, j)),
        compiler_params=_params(("arbitrary",), VMEM_LIMIT),
        name="adaln",
    )(cp, w_ada, b_ada.reshape(1, N))
    return out[:B]


def _inproj_kernel(x_ref, nw_ref, sh_ref, sc_ref, w_ref, o_ref, h_ref, *, tm, ch):
    @pl.when(pl.program_id(1) == 0)
    def _():
        mul = nw_ref[...] * (1.0 + sc_ref[0])
        add = sh_ref[0]

        def body(c, _):
            r0 = pl.multiple_of(c * ch, ch)
            x = x_ref[pl.ds(r0, ch), :]
            inv = lax.rsqrt(jnp.mean(x * x, axis=-1, keepdims=True) + EPS)
            h_ref[pl.ds(r0, ch), :] = (x * inv * mul + add).astype(BF16)
            return 0

        lax.fori_loop(0, tm // ch, body, 0)

    o_ref[...] = jnp.dot(h_ref[...], w_ref[...],
                         preferred_element_type=F32).astype(o_ref.dtype)


def _inproj(x2, norm_w, shift, scale, w_bf, S):
    T, D = x2.shape
    N = w_bf.shape[1]
    tm, tn, ch = 1024, 1024, 128
    per_b = S // tm
    return pl.pallas_call(
        functools.partial(_inproj_kernel, tm=tm, ch=ch),
        out_shape=jax.ShapeDtypeStruct((T, N), BF16),
        grid=(T // tm, N // tn),
        in_specs=[pl.BlockSpec((tm, D), lambda i, j: (i, 0)),
                  pl.BlockSpec((1, D), lambda i, j: (0, 0)),
                  pl.BlockSpec((1, 1, D), lambda i, j: (i // per_b, 0, 0)),
                  pl.BlockSpec((1, 1, D), lambda i, j: (i // per_b, 0, 0)),
                  pl.BlockSpec((D, tn), lambda i, j: (0, j))],
        out_specs=pl.BlockSpec((tm, tn), lambda i, j: (i, j)),
        scratch_shapes=[pltpu.VMEM((tm, D), BF16)],
        compiler_params=_params(("arbitrary", "arbitrary"), VMEM_LIMIT),
        name="inproj",
    )(x2, norm_w.reshape(1, D), shift[:, None, :], scale[:, None, :], w_bf)


def _attn_kernel(q_ref, k_ref, v_ref, qw_ref, kw_ref, o_ref, kn_ref, carry_ref, acc_ref,
                 *, S, tq, hg, scale):
    i = pl.program_id(2)
    d = HEAD_DIM

    def head_norm(x, w):
        parts = []
        for h in range(hg):
            xh = x[:, h * d:(h + 1) * d]
            inv = lax.rsqrt(jnp.mean(xh * xh, axis=-1, keepdims=True) + EPS)
            parts.append(xh * inv * w)
        return parts

    @pl.when(i == 0)
    def _():
        def body(c, _):
            r0 = pl.multiple_of(c * tq, tq)
            parts = head_norm(k_ref[0, pl.ds(r0, tq), :].astype(F32), kw_ref[...])
            for h in range(hg):
                kn_ref[pl.ds(r0, tq), h * d:(h + 1) * d] = parts[h].astype(BF16)
            return 0

        lax.fori_loop(0, S // tq, body, 0)

    qb = [(p * scale).astype(BF16) for p in head_norm(q_ref[0].astype(F32), qw_ref[...])]

    row = lax.broadcasted_iota(I32, (tq, tq), 0)
    col = lax.broadcasted_iota(I32, (tq, tq), 1)
    causal = col < row
    tri = (row > col).astype(BF16)

    def scores(h, r0, mask):
        kblk = kn_ref[pl.ds(r0, tq), h * d:(h + 1) * d]
        z = lax.dot_general(qb[h], kblk, (((1,), (1,)), ((), ())), preferred_element_type=F32)
        t = jnp.log(1.0 + jnp.exp(-jnp.abs(z)))
        lsn = jnp.minimum(-z, 0.0) - t
        lsp = lsn + z
        if mask:
            lsn = jnp.where(causal, lsn, 0.0)
        later = jnp.dot(lsn.astype(BF16), tri, preferred_element_type=F32)
        return lsp + later, later[:, :1] + lsn[:, :1]

    def weighted(a, h, r0):
        vblk = v_ref[0, pl.ds(r0, tq), h * d:(h + 1) * d]
        return jnp.dot(a.astype(BF16), vblk, preferred_element_type=F32)

    has_prev = i > 0
    rd = pl.multiple_of(i * tq, tq)
    rp = pl.multiple_of(jnp.maximum(i - 1, 0) * tq, tq)
    worst = None
    for h in range(hg):
        cols = slice(h * d, (h + 1) * d)
        log_d, sum_d = scores(h, rd, True)
        log_p, sum_p = scores(h, rp, False)
        a_d = jnp.where(causal, jnp.exp(log_d), 0.0)
        a_p = jnp.where(has_prev, jnp.exp(log_p + sum_d), 0.0)
        acc_ref[:, cols] = weighted(a_d, h, rd) + weighted(a_p, h, rp)
        carry = jnp.where(has_prev, sum_d + sum_p, sum_d)
        carry_ref[h] = carry
        m = jnp.max(carry)
        worst = m if worst is None else jnp.maximum(worst, m)

    def earlier(kb):
        r0 = pl.multiple_of(kb * tq, tq)
        worst = None
        for h in range(hg):
            cols = slice(h * d, (h + 1) * d)
            log_a, row_sum = scores(h, r0, False)
            acc_ref[:, cols] += weighted(jnp.exp(log_a + carry_ref[h]), h, r0)
            carry = carry_ref[h] + row_sum
            carry_ref[h] = carry
            m = jnp.max(carry)
            worst = m if worst is None else jnp.maximum(worst, m)
        return worst

    def cond(st):
        kb, m = st
        return jnp.logical_and(kb >= 0, m > -LOG_UNDERFLOW)

    def body(st):
        kb, _ = st
        return kb - 1, earlier(kb)

    lax.while_loop(cond, body, (i - 2, worst))
    o_ref[0] = acc_ref[...].astype(o_ref.dtype)


def _attention(proj3, q_norm_w, k_norm_w, n_heads):
    B, S, _ = proj3.shape
    d = HEAD_DIM
    tq = 256
    hg = 8
    G = n_heads // hg
    w = hg * d
    return pl.pallas_call(
        functools.partial(_attn_kernel, S=S, tq=tq, hg=hg, scale=1.0 / math.sqrt(d)),
        out_shape=jax.ShapeDtypeStruct((B, S, n_heads * d), BF16),
        grid=(B, G, S // tq),
        in_specs=[pl.BlockSpec((1, tq, w), lambda b, g, i: (b, i, g)),
                  pl.BlockSpec((1, S, w), lambda b, g, i: (b, 0, G + g)),
                  pl.BlockSpec((1, S, w), lambda b, g, i: (b, 0, 2 * G + g)),
                  pl.BlockSpec((1, d), lambda b, g, i: (0, 0)),
                  pl.BlockSpec((1, d), lambda b, g, i: (0, 0))],
        out_specs=pl.BlockSpec((1, tq, w), lambda b, g, i: (b, i, g)),
        scratch_shapes=[pltpu.VMEM((S, w), BF16),
                        pltpu.VMEM((hg, tq, 1), F32),
                        pltpu.VMEM((tq, w), F32)],
        compiler_params=_params(("arbitrary", "arbitrary", "arbitrary"), VMEM_LIMIT),
        name="stickbreak_attn",
    )(proj3, proj3, proj3, q_norm_w.reshape(1, d), k_norm_w.reshape(1, d))


def _pool_kernel(u_ref, w_ref, ps_ref, o_ref, *, S, ch, gd):
    halo = 16
    wgs = [w_ref[g].astype(BF16) for g in range(len(POOL_WINDOWS))]

    def body(c, _):
        r0 = pl.multiple_of(c * ch, ch)
        p0 = pl.multiple_of(jnp.maximum(r0 - halo, 0), halo)
        t = r0 + lax.broadcasted_iota(I32, (ch, 1), 0)
        for g, win in enumerate(POOL_WINDOWS):
            lo, hi = g * gd, (g + 1) * gd
            cur = u_ref[0, pl.ds(r0, ch), lo:hi].astype(F32)
            prev = u_ref[0, pl.ds(p0, halo), lo:hi].astype(F32)
            prev = jnp.where(c > 0, prev, 0.0)
            s = jnp.concatenate([prev, cur], axis=0)
            n = 1
            while n < win:
                s = s + pltpu.roll(s, n, 0)
                n *= 2
            s = s[halo:]
            cnt = jnp.minimum(t + 1, win).astype(F32)
            p = s / cnt - cur
            y = jnp.dot(p.astype(BF16), wgs[g], preferred_element_type=F32) * ps_ref[:, lo:hi]
            o_ref[0, pl.ds(r0, ch), lo:hi] = y.astype(o_ref.dtype)
        return 0

    lax.fori_loop(0, S // ch, body, 0)


def _pool(proj3, w_pool, pool_scale, pool_width):
    B, S, NP = proj3.shape
    G, gd, _ = w_pool.shape
    return pl.pallas_call(
        functools.partial(_pool_kernel, S=S, ch=256, gd=gd),
        out_shape=jax.ShapeDtypeStruct((B, S, pool_width), BF16),
        grid=(B,),
        in_specs=[pl.BlockSpec((1, S, pool_width), lambda b: (b, 0, NP // pool_width - 1)),
                  pl.BlockSpec((G, gd, gd), lambda b: (0, 0, 0)),
                  pl.BlockSpec((1, pool_width), lambda b: (0, 0))],
        out_specs=pl.BlockSpec((1, S, pool_width), lambda b: (b, 0, 0)),
        compiler_params=_params(("arbitrary",), VMEM_LIMIT),
        name="pool_mixer",
    )(proj3, w_pool, pool_scale.reshape(1, pool_width))


def _outproj_kernel(osb_ref, opool_ref, wo_ref, x_ref, g1_ref, sh_ref, sc_ref, nw_ref,
                    wr_ref, br_ref, x1_ref, h2p_ref, gates_ref, idx_ref, *, sbw, n_exp, sub):
    for r0 in range(0, x_ref.shape[0], sub):
        rows = slice(r0, r0 + sub)
        _outproj_rows(osb_ref.at[rows], opool_ref.at[rows], wo_ref, x_ref.at[rows], g1_ref,
                      sh_ref, sc_ref, nw_ref, wr_ref, br_ref, x1_ref.at[rows], h2p_ref.at[rows],
                      gates_ref.at[rows], idx_ref.at[rows], sbw=sbw, n_exp=n_exp)


def _outproj_rows(osb_ref, opool_ref, wo_ref, x_ref, g1_ref, sh_ref, sc_ref, nw_ref,
                  wr_ref, br_ref, x1_ref, h2p_ref, gates_ref, idx_ref, *, sbw, n_exp):
    tm, D = x_ref.shape
    mixed = (jnp.dot(osb_ref[...], wo_ref[:sbw, :], preferred_element_type=F32)
             + jnp.dot(opool_ref[...], wo_ref[sbw:, :], preferred_element_type=F32))
    x1 = x_ref[...] + g1_ref[0] * mixed
    x1_ref[...] = x1
    inv = lax.rsqrt(jnp.mean(x1 * x1, axis=-1, keepdims=True) + EPS)
    h2 = x1 * inv * (nw_ref[...] * (1.0 + sc_ref[0])) + sh_ref[0]
    hb = h2.astype(BF16)
    h2p_ref[...] = _pack_rows(h2)

    vals = lax.dot_general(wr_ref[...].astype(BF16), hb, (((1,), (1,)), ((), ())),
                           preferred_element_type=F32) + br_ref[...]
    expert = lax.broadcasted_iota(I32, (n_exp, tm), 0).astype(F32)
    tops, ids = [], []
    for _ in range(TOP_K):
        m = jnp.max(vals, axis=0, keepdims=True)
        first = jnp.min(jnp.where(vals == m, expert, float(n_exp)), axis=0, keepdims=True)
        tops.append(m)
        ids.append(first)
        vals = jnp.where(expert == first, -jnp.inf, vals)
    es = [jnp.exp(m - tops[0]) for m in tops]
    den = es[0]
    for e in es[1:]:
        den = den + e
    slot = lax.broadcasted_iota(I32, (LANES, tm), 0)
    gates = jnp.zeros((LANES, tm), F32)
    idx = jnp.zeros((LANES, tm), F32)
    for k in range(TOP_K):
        gates = jnp.where(slot == k, es[k] / den, gates)
        idx = jnp.where(slot == k, ids[k], idx)
    gates_ref[...] = gates.T
    idx_ref[...] = idx.T.astype(I32)


def _outproj(o_sb, o_pool, wo_bf, x2, gate1, shift2, scale2, norm2_w, w_router, b_router, S):
    T, D = x2.shape
    sbw = o_sb.shape[1]
    pw = o_pool.shape[1]
    n_exp = w_router.shape[1]
    tm, sub = 512, 256
    per_b = S // tm
    wr = w_router.T
    br = b_router.reshape(n_exp, 1)
    mod_spec = pl.BlockSpec((1, 1, D), lambda i: (i // per_b, 0, 0))
    return pl.pallas_call(
        functools.partial(_outproj_kernel, sbw=sbw, n_exp=n_exp, sub=sub),
        out_shape=(jax.ShapeDtypeStruct((T, D), F32),
                   jax.ShapeDtypeStruct((T, D // 2), U32),
                   jax.ShapeDtypeStruct((T, LANES), F32),
                   jax.ShapeDtypeStruct((T, LANES), I32)),
        grid=(T // tm,),
        in_specs=[pl.BlockSpec((tm, sbw), lambda i: (i, 0)),
                  pl.BlockSpec((tm, pw), lambda i: (i, 0)),
                  pl.BlockSpec((sbw + pw, D), lambda i: (0, 0)),
                  pl.BlockSpec((tm, D), lambda i: (i, 0)),
                  mod_spec, mod_spec, mod_spec,
                  pl.BlockSpec((1, D), lambda i: (0, 0)),
                  pl.BlockSpec((n_exp, D), lambda i: (0, 0)),
                  pl.BlockSpec((n_exp, 1), lambda i: (0, 0))],
        out_specs=(pl.BlockSpec((tm, D), lambda i: (i, 0)),
                   pl.BlockSpec((tm, D // 2), lambda i: (i, 0)),
                   pl.BlockSpec((tm, LANES), lambda i: (i, 0)),
                   pl.BlockSpec((tm, LANES), lambda i: (i, 0))),
        compiler_params=_params(("arbitrary",), VMEM_LIMIT),
        name="outproj_router",
    )(o_sb, o_pool, wo_bf, x2, gate1[:, None, :], shift2[:, None, :], scale2[:, None, :],
      norm2_w.reshape(1, D), wr, br)


def _route_kernel(idx_ref, dest_ref, meta_ref, rank_ref, *, T, ch):
    lane = lax.broadcasted_iota(I32, (ch, LANES), 1)
    row = lax.broadcasted_iota(I32, (ch, ch), 0)
    col = lax.broadcasted_iota(I32, (ch, ch), 1)
    before = (col < row).astype(BF16)

    def load(c):
        return idx_ref[pl.ds(pl.multiple_of(c * ch, ch), ch), :]

    def count(c, cnt):
        ii = load(c)
        member = lane == ii[:, 0:1]
        for k in range(1, TOP_K):
            member = jnp.logical_or(member, lane == ii[:, k:k + 1])
        mf = jnp.where(member, 1.0, 0.0)
        rank = jnp.dot(before, mf.astype(BF16), preferred_element_type=F32) + cnt
        rank_ref[pl.ds(pl.multiple_of(c * ch, ch), ch), :] = rank
        return cnt + jnp.sum(mf, axis=0, keepdims=True)

    cnt = lax.fori_loop(0, T // ch, count, jnp.zeros((1, LANES), F32), unroll=2)
    padded = jnp.ceil(cnt / EXPERT_ROWS) * EXPERT_ROWS
    rows = 8
    lane8 = lax.broadcasted_iota(I32, (rows, LANES), 1)
    ends = jnp.broadcast_to(padded, (rows, LANES))
    sh = 1
    while sh < LANES:
        ends = ends + jnp.where(lane8 >= sh, pltpu.roll(ends, sh, 1), 0.0)
        sh *= 2
    starts = ends - padded
    sub8 = lax.broadcasted_iota(I32, (rows, LANES), 0)
    meta = jnp.where(sub8 == 0, cnt, jnp.where(sub8 == 1, starts, padded))
    meta_ref[...] = meta.astype(I32)
    start_row = starts[0:1, :]

    def place(c, _):
        ii = load(c)
        val = rank_ref[pl.ds(pl.multiple_of(c * ch, ch), ch), :] + start_row
        out = jnp.zeros((ch, LANES), F32)
        for k in range(TOP_K):
            d = jnp.sum(jnp.where(lane == ii[:, k:k + 1], val, 0.0), axis=-1, keepdims=True)
            out = jnp.where(lane == k, d, out)
        dest_ref[pl.ds(pl.multiple_of(c * ch, ch), ch), :] = out.astype(I32)
        return 0

    lax.fori_loop(0, T // ch, place, 0, unroll=2)


def _route(idx_wide):
    T = idx_wide.shape[0]
    return pl.pallas_call(
        functools.partial(_route_kernel, T=T, ch=256),
        out_shape=(jax.ShapeDtypeStruct((T, LANES), I32),
                   jax.ShapeDtypeStruct((8, LANES), I32)),
        grid=(1,),
        in_specs=[pl.BlockSpec((T, LANES), lambda i: (0, 0))],
        out_specs=(pl.BlockSpec((T, LANES), lambda i: (0, 0)),
                   pl.BlockSpec((8, LANES), lambda i: (0, 0))),
        scratch_shapes=[pltpu.VMEM((T, LANES), F32)],
        compiler_params=_params(("arbitrary",), VMEM_LIMIT),
        name="route_ranks",
    )(idx_wide)


def _dispatch_kernel(dest_ref, cnt_ref, start_ref, h_ref, x_ref, z_ref, sem, zsem,
                     *, tb, n_exp, n_blocks):
    s = pl.program_id(0)

    @pl.when(s == 0)
    def _():
        z_ref[...] = jnp.zeros_like(z_ref)
        _dispatch_zero_fill(cnt_ref, start_ref, x_ref, z_ref, zsem, n_exp, n_blocks)

    base = TOP_K * s * tb
    for t in range(tb):
        for k in range(TOP_K):
            d = dest_ref[base + (TOP_K * t + k)]
            pltpu.make_async_copy(h_ref.at[pl.ds(t, 1), :], x_ref.at[pl.ds(d, 1), :],
                                  sem).start(priority=k % 2)

    for _ in range(TOP_K):
        pltpu.make_async_copy(h_ref, x_ref.at[pl.ds(0, tb), :], sem).wait()


def _dispatch_zero_fill(cnt_ref, start_ref, x_ref, z_ref, zsem, n_exp, n_blocks):
    def block_fill(blk, wait):
        r0 = pl.multiple_of(blk * EXPERT_ROWS, EXPERT_ROWS)
        cp = pltpu.make_async_copy(z_ref, x_ref.at[pl.ds(r0, EXPERT_ROWS), :], zsem)
        if wait:
            cp.wait()
        else:
            cp.start()

    def zero_fill(e, wait):
        cnt = cnt_ref[e]

        @pl.when((cnt & (EXPERT_ROWS - 1)) != 0)
        def _():
            block_fill((start_ref[e] + cnt) // EXPERT_ROWS, wait)

        return 0

    used = (start_ref[n_exp - 1] + cnt_ref[n_exp - 1] + EXPERT_ROWS - 1) // EXPERT_ROWS

    def tail_fill(blk, wait):
        block_fill(blk, wait)
        return 0

    lax.fori_loop(0, n_exp, lambda e, _: zero_fill(e, False), 0)
    lax.fori_loop(used, n_blocks, lambda b, _: tail_fill(b, False), 0)
    lax.fori_loop(0, n_exp, lambda e, _: zero_fill(e, True), 0)
    lax.fori_loop(used, n_blocks, lambda b, _: tail_fill(b, True), 0)


def _dispatch(dest_flat, cnt, starts, h2p, n_rows):
    T, W = h2p.shape
    n_exp = cnt.shape[0]
    tb = 256
    return pl.pallas_call(
        functools.partial(_dispatch_kernel, tb=tb, n_exp=n_exp, n_blocks=n_rows // EXPERT_ROWS),
        out_shape=jax.ShapeDtypeStruct((n_rows, W), U32),
        grid_spec=pltpu.PrefetchScalarGridSpec(
            num_scalar_prefetch=3,
            grid=(T // tb,),
            in_specs=[pl.BlockSpec((tb, W), lambda s, *_: (s, 0))],
            out_specs=pl.BlockSpec(memory_space=pl.ANY),
            scratch_shapes=[pltpu.VMEM((EXPERT_ROWS, W), U32),
                            pltpu.SemaphoreType.DMA, pltpu.SemaphoreType.DMA]),
        compiler_params=_params(("arbitrary",), VMEM_LIMIT),
        name="dispatch_rows",
    )(dest_flat, cnt, starts, h2p)


def _ffn1_kernel(ri_ref, ro_ref, e_ref, j_ref, first_ref, valid_ref, slot_ref, ne_ref, nj_ref, more_ref, pieces_ref,
                 x_ref, w_ref, b_ref, o_ref, stage, wbf, sem, *, F, tn):
    q = pl.program_id(0)

    def wcopy(e, j, slot, part):
        c0 = pl.multiple_of(part * F + j * tn, tn)
        return pltpu.make_async_copy(w_ref.at[e, :, pl.ds(c0, tn)], stage.at[slot, part],
                                     sem.at[slot])

    @pl.when(q == 0)
    def _():
        for part in range(2):
            wcopy(e_ref[0], j_ref[0], 0, part).start(priority=WEIGHT_DMA_PRIORITY)

    @pl.when(first_ref[q] == 1)
    def _():
        slot = slot_ref[q]
        for part in range(2):
            wcopy(e_ref[q], j_ref[q], slot, part).wait()

        @pl.when(more_ref[q] == 1)
        def _():
            for part in range(2):
                wcopy(ne_ref[q], nj_ref[q], 1 - slot, part).start(priority=WEIGHT_DMA_PRIORITY)

    def swiglu_block(w_gate, w_lin, rows):
        xb = _unpack_rows(x_ref[:rows, :])
        g = jnp.dot(xb, w_gate, preferred_element_type=F32) + b_ref[0, 0, 0]
        lin = jnp.dot(xb, w_lin, preferred_element_type=F32) + b_ref[0, 1, 0]
        g = jnp.minimum(g, SWIGLU_LIMIT)
        lin = jnp.clip(lin, -SWIGLU_LIMIT, SWIGLU_LIMIT)
        act = g / (1.0 + jnp.exp(-SWIGLU_ALPHA * g)) * (lin + 1.0)
        o_ref[:rows, :] = act.astype(o_ref.dtype)
        if rows < EXPERT_ROWS:
            o_ref[rows:, :] = jnp.zeros((EXPERT_ROWS - rows, o_ref.shape[1]), o_ref.dtype)

    @pl.when(first_ref[q] == 1)
    def _():
        slot = slot_ref[q]
        w16 = [stage[slot, part].astype(BF16) for part in range(2)]
        for part in range(2):
            wbf[part] = w16[part]
        swiglu_block(w16[0], w16[1], EXPERT_ROWS)

    later = jnp.logical_and(valid_ref[q] == 1, first_ref[q] == 0)

    for pieces in range(MIN_TAIL_PIECES, EXPERT_ROWS // TAIL_ROWS + 1):
        @pl.when(jnp.logical_and(later, pieces_ref[q] == pieces))
        def _(pieces=pieces):
            swiglu_block(wbf[0], wbf[1], pieces * TAIL_ROWS)

    @pl.when(valid_ref[q] == 0)
    def _():
        o_ref[...] = jnp.zeros_like(o_ref)


def _ffn2_kernel(ri_ref, ro_ref, e_ref, j_ref, first_ref, valid_ref, slot_ref, ne_ref, nj_ref, more_ref, pieces_ref,
                 a_ref, w_ref, b_ref, o_ref, stage, wbf, sem):
    q = pl.program_id(0)

    def wcopy(e, slot):
        return pltpu.make_async_copy(w_ref.at[e], stage.at[slot], sem.at[slot])

    @pl.when(q == 0)
    def _():
        wcopy(e_ref[0], 0).start(priority=WEIGHT_DMA_PRIORITY)

    @pl.when(first_ref[q] == 1)
    def _():
        slot = slot_ref[q]
        wcopy(e_ref[q], slot).wait()

        @pl.when(more_ref[q] == 1)
        def _():
            wcopy(ne_ref[q], 1 - slot).start(priority=WEIGHT_DMA_PRIORITY)

    def out_block(w, rows):
        y = jnp.dot(a_ref[:rows, :], w, preferred_element_type=F32) + b_ref[0]
        o_ref[:rows, :] = _pack_rows(y)
        if rows < EXPERT_ROWS:
            o_ref[rows:, :] = jnp.zeros((EXPERT_ROWS - rows, o_ref.shape[1]), o_ref.dtype)

    @pl.when(first_ref[q] == 1)
    def _():
        w16 = stage[slot_ref[q]].astype(BF16)
        wbf[...] = w16
        out_block(w16, EXPERT_ROWS)

    later = jnp.logical_and(valid_ref[q] == 1, first_ref[q] == 0)

    for pieces in range(MIN_TAIL_PIECES, EXPERT_ROWS // TAIL_ROWS + 1):
        @pl.when(jnp.logical_and(later, pieces_ref[q] == pieces))
        def _(pieces=pieces):
            out_block(wbf[...], pieces * TAIL_ROWS)

    @pl.when(valid_ref[q] == 0)
    def _():
        o_ref[...] = jnp.zeros_like(o_ref)


def _work_items(cnt, n_col_tiles, n_blocks):
    n_exp = cnt.shape[0]
    nblk = (cnt + EXPERT_ROWS - 1) // EXPERT_ROWS
    bstart = jnp.cumsum(nblk) - nblk
    gsize = jnp.repeat(nblk, n_col_tiles)
    gend = jnp.cumsum(gsize)
    n_groups = n_exp * n_col_tiles
    gid = jnp.arange(n_groups, dtype=I32)
    total = gend[-1]
    q = jnp.arange(n_blocks * n_col_tiles, dtype=I32)
    qc = jnp.minimum(q, total - 1)
    g = jnp.sum((gend[None, :] <= qc[:, None]).astype(I32), axis=1)
    nonempty = gsize > 0
    ordinal = jnp.cumsum(nonempty.astype(I32)) - 1
    nxt_incl = lax.cummin(jnp.where(nonempty, gid, n_groups), reverse=True)
    nxt = jnp.concatenate([nxt_incl[1:], jnp.full((1,), n_groups, I32)])
    more = nxt < n_groups
    nxt = jnp.minimum(nxt, n_groups - 1)
    per_group = jnp.stack([gend - gsize, gid // n_col_tiles, gid % n_col_tiles,
                           jnp.repeat(bstart, n_col_tiles), ordinal % 2,
                           nxt // n_col_tiles, nxt % n_col_tiles, more.astype(I32),
                           jnp.repeat(cnt, n_col_tiles)])
    pick = (g[None, :, None] == gid[None, None, :]).astype(I32)
    gstart, e, j, brow, slot, ne, nj, more, rows = jnp.sum(pick * per_group[:, None, :], axis=2)
    r = qc - gstart
    valid = q < total
    first = jnp.logical_and(valid, r == 0)
    pieces = jnp.clip((rows - r * EXPERT_ROWS + TAIL_ROWS - 1) // TAIL_ROWS, MIN_TAIL_PIECES,
                      EXPERT_ROWS // TAIL_ROWS)
    over = q - total
    row_in = brow + r
    row_out = jnp.where(valid, row_in, jnp.sum(nblk) + over // n_col_tiles)
    col_out = jnp.where(valid, j, over % n_col_tiles)
    as_i32 = lambda a: a.astype(I32)
    return tuple(map(as_i32, (row_in, row_out, e, col_out, first, valid, slot, ne, nj, more,
                              pieces)))


def _ffn1(items, x_pad, w_exp_in, b_exp_in):
    P, W = x_pad.shape
    n_exp, D, F2 = w_exp_in.shape
    F = F2 // 2
    tn = 1024
    nj = F // tn
    n_items = items[0].shape[0]
    bias = b_exp_in.reshape(n_exp, 2, nj, 1, tn)
    return pl.pallas_call(
        functools.partial(_ffn1_kernel, F=F, tn=tn),
        out_shape=jax.ShapeDtypeStruct((P, F), BF16),
        grid_spec=pltpu.PrefetchScalarGridSpec(
            num_scalar_prefetch=11,
            grid=(n_items,),
            in_specs=[pl.BlockSpec((EXPERT_ROWS, W), lambda q, ri, *_: (ri[q], 0)),
                      pl.BlockSpec(memory_space=pl.ANY),
                      pl.BlockSpec((1, 2, 1, 1, tn),
                                   lambda q, ri, ro, e, j, *_: (e[q], 0, j[q], 0, 0))],
            out_specs=pl.BlockSpec((EXPERT_ROWS, tn),
                                   lambda q, ri, ro, e, j, *_: (ro[q], j[q])),
            scratch_shapes=[pltpu.VMEM((2, 2, D, tn), F32),
                            pltpu.VMEM((2, D, tn), BF16),
                            pltpu.SemaphoreType.DMA((2,))]),
        compiler_params=_params(("arbitrary",), VMEM_LIMIT),
        name="expert_in_swiglu",
    )(*items, x_pad, w_exp_in, bias)


def _ffn2(items, act, w_exp_out, b_exp_out):
    P, F = act.shape
    n_exp, _, D = w_exp_out.shape
    n_items = items[0].shape[0]
    return pl.pallas_call(
        _ffn2_kernel,
        out_shape=jax.ShapeDtypeStruct((P, D // 2), U32),
        grid_spec=pltpu.PrefetchScalarGridSpec(
            num_scalar_prefetch=11,
            grid=(n_items,),
            in_specs=[pl.BlockSpec((EXPERT_ROWS, F), lambda q, ri, *_: (ri[q], 0)),
                      pl.BlockSpec(memory_space=pl.ANY),
                      pl.BlockSpec((1, 1, D), lambda q, ri, ro, e, *_: (e[q], 0, 0))],
            out_specs=pl.BlockSpec((EXPERT_ROWS, D // 2), lambda q, ri, ro, *_: (ro[q], 0)),
            scratch_shapes=[pltpu.VMEM((2, F, D), F32),
                            pltpu.VMEM((F, D), BF16),
                            pltpu.SemaphoreType.DMA((2,))]),
        compiler_params=_params(("arbitrary",), VMEM_LIMIT),
        name="expert_out",
    )(*items, act, w_exp_out, b_exp_out[:, None, :])


def _combine_kernel(dest_ref, y_ref, gates_ref, x1_ref, g2_ref, o_ref, buf, sem, *, tm):
    s = pl.program_id(0)
    ns = pl.num_programs(0)

    def gather(step, slot, unrolled):
        base = TOP_K * step * tm

        def row(t, k):
            d = dest_ref[base + (TOP_K * t + k)]
            pltpu.make_async_copy(y_ref.at[pl.ds(d, 1), :], buf.at[slot, k, pl.ds(t, 1), :],
                                  sem.at[slot]).start(priority=k % 2)

        if unrolled:
            for t in range(tm):
                for k in range(TOP_K):
                    row(t, k)
        else:
            def issue(t, _):
                for k in range(TOP_K):
                    row(t, k)
                return 0

            lax.fori_loop(0, tm, issue, 0)

    @pl.when(s == 0)
    def _():
        gather(0, 0, False)

    for par in range(2):
        @pl.when(jnp.logical_and(s + 1 < ns, (s + 1) % 2 == par))
        def _(par=par):
            gather(s + 1, par, True)

    slot = s % 2
    for k in range(TOP_K):
        pltpu.make_async_copy(y_ref.at[pl.ds(0, tm), :], buf.at[slot, k], sem.at[slot]).wait()
    gates = gates_ref[...]
    y_hi = y_lo = None
    for k in range(TOP_K):
        hi, lo = _unpack_halves(buf[slot, k])
        g = gates[:, k:k + 1]
        y_hi = g * hi if y_hi is None else y_hi + g * hi
        y_lo = g * lo if y_lo is None else y_lo + g * lo
    y = jnp.concatenate([y_hi, y_lo], axis=1)
    o_ref[...] = x1_ref[...] + g2_ref[0] * y


def _combine(dest_flat, y_pad, gates_wide, x1, gate2, S):
    T, D = x1.shape
    tm = 256
    per_b = S // tm
    return pl.pallas_call(
        functools.partial(_combine_kernel, tm=tm),
        out_shape=jax.ShapeDtypeStruct((T, D), F32),
        grid_spec=pltpu.PrefetchScalarGridSpec(
            num_scalar_prefetch=1,
            grid=(T // tm,),
            in_specs=[pl.BlockSpec(memory_space=pl.ANY),
                      pl.BlockSpec((tm, LANES), lambda i, d: (i, 0)),
                      pl.BlockSpec((tm, D), lambda i, d: (i, 0)),
                      pl.BlockSpec((1, 1, D), lambda i, d: (i // per_b, 0, 0))],
            out_specs=pl.BlockSpec((tm, D), lambda i, d: (i, 0)),
            scratch_shapes=[pltpu.VMEM((2, TOP_K, tm, D // 2), U32),
                            pltpu.SemaphoreType.DMA((2,))]),
        compiler_params=_params(("arbitrary",), VMEM_LIMIT),
        name="combine_rows",
    )(dest_flat, y_pad, gates_wide, x1, gate2[:, None, :])


def kernel(x, c, norm1_w, norm2_w, w_ada, b_ada, w_in, q_norm_w, k_norm_w, w_pool, pool_scale,
           w_o, w_router, b_router, w_exp_in, b_exp_in, w_exp_out, b_exp_out):
    B, S, D = x.shape
    T = B * S
    depth = w_ada.shape[0]
    n_exp = w_router.shape[-1]
    pool_width = pool_scale.shape[-1]
    sb_width = w_o.shape[1] - pool_width
    n_heads = sb_width // HEAD_DIM
    n_blocks = (T * TOP_K + n_exp * (EXPERT_ROWS - 1)) // EXPERT_ROWS
    n_rows = n_blocks * EXPERT_ROWS

    x2 = x.reshape(T, D)
    for l in range(depth):
        mod = _adaln(c, w_ada[l], b_ada[l])
        shift1, scale1, gate1, shift2, scale2, gate2 = jnp.split(mod, 6, axis=-1)

        proj = _inproj(x2, norm1_w[l], shift1, scale1, w_in[l].astype(BF16), S)
        proj3 = proj.reshape(B, S, -1)
        o_sb = _attention(proj3, q_norm_w[l], k_norm_w[l], n_heads)
        o_pool = _pool(proj3, w_pool[l], pool_scale[l], pool_width)
        x1, h2p, gates_wide, idx_wide = _outproj(
            o_sb.reshape(T, sb_width), o_pool.reshape(T, pool_width), w_o[l].astype(BF16),
            x2, gate1, shift2, scale2, norm2_w[l], w_router[l], b_router[l], S)

        dest_wide, meta = _route(idx_wide)
        cnt = meta[0, :n_exp]
        starts = meta[1, :n_exp]
        dest_flat = dest_wide[:, :TOP_K].reshape(T * TOP_K)
        x_pad = _dispatch(dest_flat, cnt, starts, h2p, n_rows)

        F = w_exp_out.shape[2]
        act = _ffn1(_work_items(cnt, F // 1024, n_blocks), x_pad, w_exp_in[l], b_exp_in[l])
        y_pad = _ffn2(_work_items(cnt, 1, n_blocks), act, w_exp_out[l], b_exp_out[l])
        x2 = _combine(dest_flat, y_pad, gates_wide, x1, gate2, S)
    return x2.reshape(B, S, D)
```

```python
import functools
import math

import jax
import jax.numpy as jnp
from jax import lax
from jax.experimental import pallas as pl
from jax.experimental.pallas import tpu as pltpu

F32 = jnp.float32
BF16 = jnp.bfloat16
I32 = jnp.int32
U32 = jnp.uint32

EPS = 1e-6
HEAD_DIM = 128
POOL_WINDOWS = (2, 4, 8, 16)
TOP_K = 4
SWIGLU_ALPHA = 1.702
SWIGLU_LIMIT = 7.0

LANES = 128
SUBLANES = 8
EXPERT_ROWS = 256
TAIL_ROWS = 64
MIN_TAIL_PIECES = 2
LOG_UNDERFLOW = 104.0
VMEM_LIMIT = 56 * 1024 * 1024
WEIGHT_DMA_PRIORITY = 1


def _params(sem=None, vmem=None):
    return pltpu.CompilerParams(dimension_semantics=sem, vmem_limit_bytes=vmem)


_HIGH_HALF = 0xFFFF0000


def _pack_rows(v):
    bits = lax.bitcast_convert_type(v.astype(BF16).astype(F32), U32)
    half = v.shape[1] // 2
    return (bits[:, :half] & jnp.uint32(_HIGH_HALF)) | (bits[:, half:] >> 16)


def _unpack_halves(p):
    hi = lax.bitcast_convert_type(p & jnp.uint32(_HIGH_HALF), F32)
    lo = lax.bitcast_convert_type(p << 16, F32)
    return hi, lo


def _unpack_rows(p):
    hi, lo = _unpack_halves(p)
    return jnp.concatenate([hi.astype(BF16), lo.astype(BF16)], axis=1)


def _adaln_kernel(c_ref, w_ref, b_ref, o_ref):
    c = c_ref[...]
    ca = c / (1.0 + jnp.exp(-c))
    o_ref[...] = jnp.dot(ca.astype(BF16), w_ref[...].astype(BF16),
                         preferred_element_type=F32) + b_ref[...]


def _adaln(c, w_ada, b_ada):
    B, D = c.shape
    N = w_ada.shape[1]
    rows = 8
    tn = 1024
    cp = jnp.zeros((rows, D), F32).at[:B].set(c)
    out = pl.pallas_call(
        _adaln_kernel,
        out_shape=jax.ShapeDtypeStruct((rows, N), F32),
        grid=(N // tn,),
        in_specs=[pl.BlockSpec((rows, D), lambda j: (0, 0)),
                  pl.BlockSpec((D, tn), lambda j: (0, j)),
                  pl.BlockSpec((1, tn), lambda j: (0, j))],
        out_specs=pl.BlockSpec((rows, tn), lambda j: (0, j)),
        compiler_params=_params(("arbitrary",), VMEM_LIMIT),
        name="adaln",
    )(cp, w_ada, b_ada.reshape(1, N))
    return out[:B]


def _inproj_kernel(x_ref, nw_ref, sh_ref, sc_ref, w_ref, o_ref, h_ref, *, tm, ch):
    @pl.when(pl.program_id(1) == 0)
    def _():
        mul = nw_ref[...] * (1.0 + sc_ref[0])
        add = sh_ref[0]

        def body(c, _):
            r0 = pl.multiple_of(c * ch, ch)
            x = x_ref[pl.ds(r0, ch), :]
            inv = lax.rsqrt(jnp.mean(x * x, axis=-1, keepdims=True) + EPS)
            h_ref[pl.ds(r0, ch), :] = (x * inv * mul + add).astype(BF16)
            return 0

        lax.fori_loop(0, tm // ch, body, 0)

    o_ref[...] = jnp.dot(h_ref[...], w_ref[...],
                         preferred_element_type=F32).astype(o_ref.dtype)


def _inproj(x2, norm_w, shift, scale, w_bf, S):
    T, D = x2.shape
    N = w_bf.shape[1]
    tm, tn, ch = 1024, 2048, 128
    per_b = S // tm
    return pl.pallas_call(
        functools.partial(_inproj_kernel, tm=tm, ch=ch),
        out_shape=jax.ShapeDtypeStruct((T, N), BF16),
        grid=(T // tm, N // tn),
        in_specs=[pl.BlockSpec((tm, D), lambda i, j: (i, 0)),
                  pl.BlockSpec((1, D), lambda i, j: (0, 0)),
                  pl.BlockSpec((1, 1, D), lambda i, j: (i // per_b, 0, 0)),
                  pl.BlockSpec((1, 1, D), lambda i, j: (i // per_b, 0, 0)),
                  pl.BlockSpec((D, tn), lambda i, j: (0, j))],
        out_specs=pl.BlockSpec((tm, tn), lambda i, j: (i, j)),
        scratch_shapes=[pltpu.VMEM((tm, D), BF16)],
        compiler_params=_params(("arbitrary", "arbitrary"), VMEM_LIMIT),
        name="inproj",
    )(x2, norm_w.reshape(1, D), shift[:, None, :], scale[:, None, :], w_bf)


def _attn_kernel(q_ref, k_ref, v_ref, qw_ref, kw_ref, o_ref, kn_ref, carry_ref, acc_ref,
                 *, S, tq, hg, scale):
    i = pl.program_id(2)
    d = HEAD_DIM

    def head_norm(x, w):
        parts = []
        for h in range(hg):
            xh = x[:, h * d:(h + 1) * d]
            inv = lax.rsqrt(jnp.mean(xh * xh, axis=-1, keepdims=True) + EPS)
            parts.append(xh * inv * w)
        return parts

    @pl.when(i == 0)
    def _():
        def body(c, _):
            r0 = pl.multiple_of(c * tq, tq)
            parts = head_norm(k_ref[0, pl.ds(r0, tq), :].astype(F32), kw_ref[...])
            for h in range(hg):
                kn_ref[pl.ds(r0, tq), h * d:(h + 1) * d] = parts[h].astype(BF16)
            return 0

        lax.fori_loop(0, S // tq, body, 0)

    qb = [(p * scale).astype(BF16) for p in head_norm(q_ref[0].astype(F32), qw_ref[...])]

    row = lax.broadcasted_iota(I32, (tq, tq), 0)
    col = lax.broadcasted_iota(I32, (tq, tq), 1)
    causal = col < row
    tri = (row > col).astype(BF16)

    def scores(h, rows, r0, nk, mask=None):
        kblk = kn_ref[pl.ds(r0, nk), h * d:(h + 1) * d]
        z = lax.dot_general(qb[h][rows], kblk, (((1,), (1,)), ((), ())),
                            preferred_element_type=F32)
        t = jnp.log(1.0 + jnp.exp(-jnp.abs(z)))
        lsn = jnp.minimum(-z, 0.0) - t
        lsp = lsn + z
        if mask is not None:
            lsn = jnp.where(mask, lsn, 0.0)
        later = jnp.dot(lsn.astype(BF16), tri[:nk, :nk], preferred_element_type=F32)
        return lsp + later, later[:, :1] + lsn[:, :1]

    def weighted(a, h, r0, nk):
        vblk = v_ref[0, pl.ds(r0, nk), h * d:(h + 1) * d]
        return jnp.dot(a.astype(BF16), vblk, preferred_element_type=F32)

    has_prev = i > 0
    rd = pl.multiple_of(i * tq, tq)
    rp = pl.multiple_of(jnp.maximum(i - 1, 0) * tq, tq)
    half = tq // 2
    top, bottom, every = slice(0, half), slice(half, tq), slice(0, tq)
    worst = None
    for h in range(hg):
        cols = slice(h * d, (h + 1) * d)
        log_t, sum_t = scores(h, top, rd, half, causal[top, top])
        log_b, sum_b = scores(h, bottom, rd, tq, causal[bottom, :])
        log_p, sum_p = scores(h, every, rp, tq)
        sum_d = jnp.concatenate([sum_t, sum_b], axis=0)
        a_t = jnp.where(causal[top, top], jnp.exp(log_t), 0.0)
        a_b = jnp.where(causal[bottom, :], jnp.exp(log_b), 0.0)
        a_p = jnp.where(has_prev, jnp.exp(log_p + sum_d), 0.0)
        from_prev = weighted(a_p, h, rp, tq)
        acc_ref[top, cols] = weighted(a_t, h, rd, half) + from_prev[top]
        acc_ref[bottom, cols] = weighted(a_b, h, rd, tq) + from_prev[bottom]
        carry = jnp.where(has_prev, sum_d + sum_p, sum_d)
        carry_ref[h] = carry
        m = jnp.max(carry)
        worst = m if worst is None else jnp.maximum(worst, m)

    def earlier(kb):
        r0 = pl.multiple_of(kb * tq, tq)
        worst = None
        for h in range(hg):
            cols = slice(h * d, (h + 1) * d)
            log_a, row_sum = scores(h, every, r0, tq)
            acc_ref[:, cols] += weighted(jnp.exp(log_a + carry_ref[h]), h, r0, tq)
            carry = carry_ref[h] + row_sum
            carry_ref[h] = carry
            m = jnp.max(carry)
            worst = m if worst is None else jnp.maximum(worst, m)
        return worst

    def cond(st):
        kb, m = st
        return jnp.logical_and(kb >= 0, m > -LOG_UNDERFLOW)

    def body(st):
        kb, _ = st
        return kb - 1, earlier(kb)

    lax.while_loop(cond, body, (i - 2, worst))
    o_ref[0] = acc_ref[...].astype(o_ref.dtype)


def _attention(proj3, q_norm_w, k_norm_w, n_heads):
    B, S, _ = proj3.shape
    d = HEAD_DIM
    tq = 256
    hg = 8
    G = n_heads // hg
    w = hg * d
    return pl.pallas_call(
        functools.partial(_attn_kernel, S=S, tq=tq, hg=hg, scale=1.0 / math.sqrt(d)),
        out_shape=jax.ShapeDtypeStruct((B, S, n_heads * d), BF16),
        grid=(B, G, S // tq),
        in_specs=[pl.BlockSpec((1, tq, w), lambda b, g, i: (b, i, g)),
                  pl.BlockSpec((1, S, w), lambda b, g, i: (b, 0, G + g)),
                  pl.BlockSpec((1, S, w), lambda b, g, i: (b, 0, 2 * G + g)),
                  pl.BlockSpec((1, d), lambda b, g, i: (0, 0)),
                  pl.BlockSpec((1, d), lambda b, g, i: (0, 0))],
        out_specs=pl.BlockSpec((1, tq, w), lambda b, g, i: (b, i, g)),
        scratch_shapes=[pltpu.VMEM((S, w), BF16),
                        pltpu.VMEM((hg, tq, 1), F32),
                        pltpu.VMEM((tq, w), F32)],
        compiler_params=_params(("arbitrary", "arbitrary", "arbitrary"), VMEM_LIMIT),
        name="stickbreak_attn",
    )(proj3, proj3, proj3, q_norm_w.reshape(1, d), k_norm_w.reshape(1, d))


def _pool_kernel(u_ref, w_ref, ps_ref, o_ref, *, S, ch, gd):
    halo = 16
    wgs = [w_ref[g].astype(BF16) for g in range(len(POOL_WINDOWS))]

    def body(c, _):
        r0 = pl.multiple_of(c * ch, ch)
        p0 = pl.multiple_of(jnp.maximum(r0 - halo, 0), halo)
        t = r0 + lax.broadcasted_iota(I32, (ch, 1), 0)
        for g, win in enumerate(POOL_WINDOWS):
            lo, hi = g * gd, (g + 1) * gd
            cur = u_ref[0, pl.ds(r0, ch), lo:hi].astype(F32)
            prev = u_ref[0, pl.ds(p0, halo), lo:hi].astype(F32)
            prev = jnp.where(c > 0, prev, 0.0)
            s = jnp.concatenate([prev, cur], axis=0)
            n = 1
            while n < win:
                s = s + pltpu.roll(s, n, 0)
                n *= 2
            s = s[halo:]
            cnt = jnp.minimum(t + 1, win).astype(F32)
            p = s / cnt - cur
            y = jnp.dot(p.astype(BF16), wgs[g], preferred_element_type=F32) * ps_ref[:, lo:hi]
            o_ref[0, pl.ds(r0, ch), lo:hi] = y.astype(o_ref.dtype)
        return 0

    lax.fori_loop(0, S // ch, body, 0)


def _pool(proj3, w_pool, pool_scale, pool_width):
    B, S, NP = proj3.shape
    G, gd, _ = w_pool.shape
    return pl.pallas_call(
        functools.partial(_pool_kernel, S=S, ch=256, gd=gd),
        out_shape=jax.ShapeDtypeStruct((B, S, pool_width), BF16),
        grid=(B,),
        in_specs=[pl.BlockSpec((1, S, pool_width), lambda b: (b, 0, NP // pool_width - 1)),
                  pl.BlockSpec((G, gd, gd), lambda b: (0, 0, 0)),
                  pl.BlockSpec((1, pool_width), lambda b: (0, 0))],
        out_specs=pl.BlockSpec((1, S, pool_width), lambda b: (b, 0, 0)),
        compiler_params=_params(("arbitrary",), VMEM_LIMIT),
        name="pool_mixer",
    )(proj3, w_pool, pool_scale.reshape(1, pool_width))


def _outproj_kernel(osb_ref, opool_ref, wo_ref, x_ref, g1_ref, sh_ref, sc_ref, nw_ref,
                    wr_ref, br_ref, x1_ref, h2p_ref, gates_ref, idx_ref, *, sbw, n_exp, sub):
    for r0 in range(0, x_ref.shape[0], sub):
        rows = slice(r0, r0 + sub)
        _outproj_rows(osb_ref.at[rows], opool_ref.at[rows], wo_ref, x_ref.at[rows], g1_ref,
                      sh_ref, sc_ref, nw_ref, wr_ref, br_ref, x1_ref.at[rows], h2p_ref.at[rows],
                      gates_ref.at[rows], idx_ref.at[rows], sbw=sbw, n_exp=n_exp)


def _outproj_rows(osb_ref, opool_ref, wo_ref, x_ref, g1_ref, sh_ref, sc_ref, nw_ref,
                  wr_ref, br_ref, x1_ref, h2p_ref, gates_ref, idx_ref, *, sbw, n_exp):
    tm, D = x_ref.shape
    mixed = (jnp.dot(osb_ref[...], wo_ref[:sbw, :], preferred_element_type=F32)
             + jnp.dot(opool_ref[...], wo_ref[sbw:, :], preferred_element_type=F32))
    x1 = x_ref[...] + g1_ref[0] * mixed
    x1_ref[...] = x1
    inv = lax.rsqrt(jnp.mean(x1 * x1, axis=-1, keepdims=True) + EPS)
    h2 = x1 * inv * (nw_ref[...] * (1.0 + sc_ref[0])) + sh_ref[0]
    hb = h2.astype(BF16)
    h2p_ref[...] = _pack_rows(h2)

    vals = lax.dot_general(wr_ref[...].astype(BF16), hb, (((1,), (1,)), ((), ())),
                           preferred_element_type=F32) + br_ref[...]
    expert = lax.broadcasted_iota(I32, (n_exp, tm), 0).astype(F32)
    tops, ids = [], []
    for _ in range(TOP_K):
        m = jnp.max(vals, axis=0, keepdims=True)
        first = jnp.min(jnp.where(vals == m, expert, float(n_exp)), axis=0, keepdims=True)
        tops.append(m)
        ids.append(first)
        vals = jnp.where(expert == first, -jnp.inf, vals)
    es = [jnp.exp(m - tops[0]) for m in tops]
    den = es[0]
    for e in es[1:]:
        den = den + e
    slot = lax.broadcasted_iota(I32, (LANES, tm), 0)
    gates = jnp.zeros((LANES, tm), F32)
    idx = jnp.zeros((LANES, tm), F32)
    for k in range(TOP_K):
        gates = jnp.where(slot == k, es[k] / den, gates)
        idx = jnp.where(slot == k, ids[k], idx)
    gates_ref[...] = gates.T
    idx_ref[...] = idx.T.astype(I32)


def _outproj(o_sb, o_pool, wo_bf, x2, gate1, shift2, scale2, norm2_w, w_router, b_router, S):
    T, D = x2.shape
    sbw = o_sb.shape[1]
    pw = o_pool.shape[1]
    n_exp = w_router.shape[1]
    tm, sub = 512, 256
    per_b = S // tm
    wr = w_router.T
    br = b_router.reshape(n_exp, 1)
    mod_spec = pl.BlockSpec((1, 1, D), lambda i: (i // per_b, 0, 0))
    return pl.pallas_call(
        functools.partial(_outproj_kernel, sbw=sbw, n_exp=n_exp, sub=sub),
        out_shape=(jax.ShapeDtypeStruct((T, D), F32),
                   jax.ShapeDtypeStruct((T, D // 2), U32),
                   jax.ShapeDtypeStruct((T, LANES), F32),
                   jax.ShapeDtypeStruct((T, LANES), I32)),
        grid=(T // tm,),
        in_specs=[pl.BlockSpec((tm, sbw), lambda i: (i, 0)),
                  pl.BlockSpec((tm, pw), lambda i: (i, 0)),
                  pl.BlockSpec((sbw + pw, D), lambda i: (0, 0)),
                  pl.BlockSpec((tm, D), lambda i: (i, 0)),
                  mod_spec, mod_spec, mod_spec,
                  pl.BlockSpec((1, D), lambda i: (0, 0)),
                  pl.BlockSpec((n_exp, D), lambda i: (0, 0)),
                  pl.BlockSpec((n_exp, 1), lambda i: (0, 0))],
        out_specs=(pl.BlockSpec((tm, D), lambda i: (i, 0)),
                   pl.BlockSpec((tm, D // 2), lambda i: (i, 0)),
                   pl.BlockSpec((tm, LANES), lambda i: (i, 0)),
                   pl.BlockSpec((tm, LANES), lambda i: (i, 0))),
        compiler_params=_params(("arbitrary",), VMEM_LIMIT),
        name="outproj_router",
    )(o_sb, o_pool, wo_bf, x2, gate1[:, None, :], shift2[:, None, :], scale2[:, None, :],
      norm2_w.reshape(1, D), wr, br)


def _route_kernel(idx_ref, dest_ref, meta_ref, rank_ref, *, T, ch):
    lane = lax.broadcasted_iota(I32, (ch, LANES), 1)
    row = lax.broadcasted_iota(I32, (ch, ch), 0)
    col = lax.broadcasted_iota(I32, (ch, ch), 1)
    before = (col < row).astype(BF16)

    def load(c):
        return idx_ref[pl.ds(pl.multiple_of(c * ch, ch), ch), :]

    def count(c, cnt):
        ii = load(c)
        member = lane == ii[:, 0:1]
        for k in range(1, TOP_K):
            member = jnp.logical_or(member, lane == ii[:, k:k + 1])
        mf = jnp.where(member, 1.0, 0.0)
        rank = jnp.dot(before, mf.astype(BF16), preferred_element_type=F32) + cnt
        rank_ref[pl.ds(pl.multiple_of(c * ch, ch), ch), :] = rank
        return cnt + jnp.sum(mf, axis=0, keepdims=True)

    cnt = lax.fori_loop(0, T // ch, count, jnp.zeros((1, LANES), F32), unroll=2)
    padded = jnp.ceil(cnt / EXPERT_ROWS) * EXPERT_ROWS
    rows = 8
    lane8 = lax.broadcasted_iota(I32, (rows, LANES), 1)
    ends = jnp.broadcast_to(padded, (rows, LANES))
    sh = 1
    while sh < LANES:
        ends = ends + jnp.where(lane8 >= sh, pltpu.roll(ends, sh, 1), 0.0)
        sh *= 2
    starts = ends - padded
    sub8 = lax.broadcasted_iota(I32, (rows, LANES), 0)
    meta = jnp.where(sub8 == 0, cnt, jnp.where(sub8 == 1, starts, padded))
    meta_ref[...] = meta.astype(I32)
    start_row = starts[0:1, :]

    def place(c, _):
        ii = load(c)
        val = rank_ref[pl.ds(pl.multiple_of(c * ch, ch), ch), :] + start_row
        out = jnp.zeros((ch, LANES), F32)
        for k in range(TOP_K):
            d = jnp.sum(jnp.where(lane == ii[:, k:k + 1], val, 0.0), axis=-1, keepdims=True)
            out = jnp.where(lane == k, d, out)
        dest_ref[pl.ds(pl.multiple_of(c * ch, ch), ch), :] = out.astype(I32)
        return 0

    lax.fori_loop(0, T // ch, place, 0, unroll=2)


def _route(idx_wide):
    T = idx_wide.shape[0]
    return pl.pallas_call(
        functools.partial(_route_kernel, T=T, ch=256),
        out_shape=(jax.ShapeDtypeStruct((T, LANES), I32),
                   jax.ShapeDtypeStruct((8, LANES), I32)),
        grid=(1,),
        in_specs=[pl.BlockSpec((T, LANES), lambda i: (0, 0))],
        out_specs=(pl.BlockSpec((T, LANES), lambda i: (0, 0)),
                   pl.BlockSpec((8, LANES), lambda i: (0, 0))),
        scratch_shapes=[pltpu.VMEM((T, LANES), F32)],
        compiler_params=_params(("arbitrary",), VMEM_LIMIT),
        name="route_ranks",
    )(idx_wide)


def _dispatch_kernel(dest_ref, cnt_ref, start_ref, h_ref, x_ref, z_ref, sem, zsem,
                     *, tb, n_exp, n_blocks):
    s = pl.program_id(0)

    @pl.when(s == 0)
    def _():
        z_ref[...] = jnp.zeros_like(z_ref)
        _dispatch_zero_fill(cnt_ref, start_ref, x_ref, z_ref, zsem, n_exp, n_blocks)

    base = TOP_K * s * tb
    for t in range(tb):
        for k in range(TOP_K):
            d = dest_ref[base + (TOP_K * t + k)]
            pltpu.make_async_copy(h_ref.at[pl.ds(t, 1), :], x_ref.at[pl.ds(d, 1), :],
                                  sem).start(priority=k % 2)

    for _ in range(TOP_K):
        pltpu.make_async_copy(h_ref, x_ref.at[pl.ds(0, tb), :], sem).wait()


def _dispatch_zero_fill(cnt_ref, start_ref, x_ref, z_ref, zsem, n_exp, n_blocks):
    def block_fill(blk, wait):
        r0 = pl.multiple_of(blk * EXPERT_ROWS, EXPERT_ROWS)
        cp = pltpu.make_async_copy(z_ref, x_ref.at[pl.ds(r0, EXPERT_ROWS), :], zsem)
        if wait:
            cp.wait()
        else:
            cp.start()

    def zero_fill(e, wait):
        cnt = cnt_ref[e]

        @pl.when((cnt & (EXPERT_ROWS - 1)) != 0)
        def _():
            block_fill((start_ref[e] + cnt) // EXPERT_ROWS, wait)

        return 0

    used = (start_ref[n_exp - 1] + cnt_ref[n_exp - 1] + EXPERT_ROWS - 1) // EXPERT_ROWS

    def tail_fill(blk, wait):
        block_fill(blk, wait)
        return 0

    lax.fori_loop(0, n_exp, lambda e, _: zero_fill(e, False), 0)
    lax.fori_loop(used, n_blocks, lambda b, _: tail_fill(b, False), 0)
    lax.fori_loop(0, n_exp, lambda e, _: zero_fill(e, True), 0)
    lax.fori_loop(used, n_blocks, lambda b, _: tail_fill(b, True), 0)


def _dispatch(dest_flat, cnt, starts, h2p, n_rows):
    T, W = h2p.shape
    n_exp = cnt.shape[0]
    tb = 256
    return pl.pallas_call(
        functools.partial(_dispatch_kernel, tb=tb, n_exp=n_exp, n_blocks=n_rows // EXPERT_ROWS),
        out_shape=jax.ShapeDtypeStruct((n_rows, W), U32),
        grid_spec=pltpu.PrefetchScalarGridSpec(
            num_scalar_prefetch=3,
            grid=(T // tb,),
            in_specs=[pl.BlockSpec((tb, W), lambda s, *_: (s, 0))],
            out_specs=pl.BlockSpec(memory_space=pl.ANY),
            scratch_shapes=[pltpu.VMEM((EXPERT_ROWS, W), U32),
                            pltpu.SemaphoreType.DMA, pltpu.SemaphoreType.DMA]),
        compiler_params=_params(("arbitrary",), VMEM_LIMIT),
        name="dispatch_rows",
    )(dest_flat, cnt, starts, h2p)


def _ffn1_kernel(ri_ref, ro_ref, e_ref, j_ref, first_ref, valid_ref, slot_ref, ne_ref, nj_ref, more_ref, pieces_ref,
                 x_ref, w_ref, b_ref, o_ref, stage, wbf, sem, *, F, tn):
    q = pl.program_id(0)

    def wcopy(e, j, slot, part):
        c0 = pl.multiple_of(part * F + j * tn, tn)
        return pltpu.make_async_copy(w_ref.at[e, :, pl.ds(c0, tn)], stage.at[slot, part],
                                     sem.at[slot])

    @pl.when(q == 0)
    def _():
        for part in range(2):
            wcopy(e_ref[0], j_ref[0], 0, part).start(priority=WEIGHT_DMA_PRIORITY)

    @pl.when(first_ref[q] == 1)
    def _():
        slot = slot_ref[q]
        for part in range(2):
            wcopy(e_ref[q], j_ref[q], slot, part).wait()

        @pl.when(more_ref[q] == 1)
        def _():
            for part in range(2):
                wcopy(ne_ref[q], nj_ref[q], 1 - slot, part).start(priority=WEIGHT_DMA_PRIORITY)

    def swiglu_block(w_gate, w_lin, rows):
        xb = _unpack_rows(x_ref[:rows, :])
        g = jnp.dot(xb, w_gate, preferred_element_type=F32) + b_ref[0, 0, 0]
        lin = jnp.dot(xb, w_lin, preferred_element_type=F32) + b_ref[0, 1, 0]
        g = jnp.minimum(g, SWIGLU_LIMIT)
        lin = jnp.clip(lin, -SWIGLU_LIMIT, SWIGLU_LIMIT)
        act = g / (1.0 + jnp.exp(-SWIGLU_ALPHA * g)) * (lin + 1.0)
        o_ref[:rows, :] = act.astype(o_ref.dtype)
        if rows < EXPERT_ROWS:
            o_ref[rows:, :] = jnp.zeros((EXPERT_ROWS - rows, o_ref.shape[1]), o_ref.dtype)

    @pl.when(first_ref[q] == 1)
    def _():
        slot = slot_ref[q]
        w16 = [stage[slot, part].astype(BF16) for part in range(2)]
        for part in range(2):
            wbf[part] = w16[part]
        swiglu_block(w16[0], w16[1], EXPERT_ROWS)

    later = jnp.logical_and(valid_ref[q] == 1, first_ref[q] == 0)

    for pieces in range(MIN_TAIL_PIECES, EXPERT_ROWS // TAIL_ROWS + 1):
        @pl.when(jnp.logical_and(later, pieces_ref[q] == pieces))
        def _(pieces=pieces):
            swiglu_block(wbf[0], wbf[1], pieces * TAIL_ROWS)

    @pl.when(valid_ref[q] == 0)
    def _():
        o_ref[...] = jnp.zeros_like(o_ref)


def _ffn2_kernel(ri_ref, ro_ref, e_ref, j_ref, first_ref, valid_ref, slot_ref, ne_ref, nj_ref, more_ref, pieces_ref,
                 a_ref, w_ref, b_ref, o_ref, stage, wbf, sem):
    q = pl.program_id(0)

    def wcopy(e, slot):
        return pltpu.make_async_copy(w_ref.at[e], stage.at[slot], sem.at[slot])

    @pl.when(q == 0)
    def _():
        wcopy(e_ref[0], 0).start(priority=WEIGHT_DMA_PRIORITY)

    @pl.when(first_ref[q] == 1)
    def _():
        slot = slot_ref[q]
        wcopy(e_ref[q], slot).wait()

        @pl.when(more_ref[q] == 1)
        def _():
            wcopy(ne_ref[q], 1 - slot).start(priority=WEIGHT_DMA_PRIORITY)

    def out_block(w, rows):
        y = jnp.dot(a_ref[:rows, :], w, preferred_element_type=F32) + b_ref[0]
        o_ref[:rows, :] = _pack_rows(y)
        if rows < EXPERT_ROWS:
            o_ref[rows:, :] = jnp.zeros((EXPERT_ROWS - rows, o_ref.shape[1]), o_ref.dtype)

    @pl.when(first_ref[q] == 1)
    def _():
        w16 = stage[slot_ref[q]].astype(BF16)
        wbf[...] = w16
        out_block(w16, EXPERT_ROWS)

    later = jnp.logical_and(valid_ref[q] == 1, first_ref[q] == 0)

    for pieces in range(MIN_TAIL_PIECES, EXPERT_ROWS // TAIL_ROWS + 1):
        @pl.when(jnp.logical_and(later, pieces_ref[q] == pieces))
        def _(pieces=pieces):
            out_block(wbf[...], pieces * TAIL_ROWS)

    @pl.when(valid_ref[q] == 0)
    def _():
        o_ref[...] = jnp.zeros_like(o_ref)


def _work_items(cnt, n_col_tiles, n_blocks):
    n_exp = cnt.shape[0]
    nblk = (cnt + EXPERT_ROWS - 1) // EXPERT_ROWS
    bstart = jnp.cumsum(nblk) - nblk
    gsize = jnp.repeat(nblk, n_col_tiles)
    gend = jnp.cumsum(gsize)
    n_groups = n_exp * n_col_tiles
    gid = jnp.arange(n_groups, dtype=I32)
    total = gend[-1]
    q = jnp.arange(n_blocks * n_col_tiles, dtype=I32)
    qc = jnp.minimum(q, total - 1)
    g = jnp.sum((gend[None, :] <= qc[:, None]).astype(I32), axis=1)
    nonempty = gsize > 0
    ordinal = jnp.cumsum(nonempty.astype(I32)) - 1
    nxt_incl = lax.cummin(jnp.where(nonempty, gid, n_groups), reverse=True)
    nxt = jnp.concatenate([nxt_incl[1:], jnp.full((1,), n_groups, I32)])
    more = nxt < n_groups
    nxt = jnp.minimum(nxt, n_groups - 1)
    per_group = jnp.stack([gend - gsize, gid // n_col_tiles, gid % n_col_tiles,
                           jnp.repeat(bstart, n_col_tiles), ordinal % 2,
                           nxt // n_col_tiles, nxt % n_col_tiles, more.astype(I32),
                           jnp.repeat(cnt, n_col_tiles)])
    pick = (g[None, :, None] == gid[None, None, :]).astype(I32)
    gstart, e, j, brow, slot, ne, nj, more, rows = jnp.sum(pick * per_group[:, None, :], axis=2)
    r = qc - gstart
    valid = q < total
    first = jnp.logical_and(valid, r == 0)
    pieces = jnp.clip((rows - r * EXPERT_ROWS + TAIL_ROWS - 1) // TAIL_ROWS, MIN_TAIL_PIECES,
                      EXPERT_ROWS // TAIL_ROWS)
    over = q - total
    row_in = brow + r
    row_out = jnp.where(valid, row_in, jnp.sum(nblk) + over // n_col_tiles)
    col_out = jnp.where(valid, j, over % n_col_tiles)
    as_i32 = lambda a: a.astype(I32)
    return tuple(map(as_i32, (row_in, row_out, e, col_out, first, valid, slot, ne, nj, more,
                              pieces)))


def _ffn1(items, x_pad, w_exp_in, b_exp_in):
    P, W = x_pad.shape
    n_exp, D, F2 = w_exp_in.shape
    F = F2 // 2
    tn = 1024
    nj = F // tn
    n_items = items[0].shape[0]
    bias = b_exp_in.reshape(n_exp, 2, nj, 1, tn)
    return pl.pallas_call(
        functools.partial(_ffn1_kernel, F=F, tn=tn),
        out_shape=jax.ShapeDtypeStruct((P, F), BF16),
        grid_spec=pltpu.PrefetchScalarGridSpec(
            num_scalar_prefetch=11,
            grid=(n_items,),
            in_specs=[pl.BlockSpec((EXPERT_ROWS, W), lambda q, ri, *_: (ri[q], 0)),
                      pl.BlockSpec(memory_space=pl.ANY),
                      pl.BlockSpec((1, 2, 1, 1, tn),
                                   lambda q, ri, ro, e, j, *_: (e[q], 0, j[q], 0, 0))],
            out_specs=pl.BlockSpec((EXPERT_ROWS, tn),
                                   lambda q, ri, ro, e, j, *_: (ro[q], j[q])),
            scratch_shapes=[pltpu.VMEM((2, 2, D, tn), F32),
                            pltpu.VMEM((2, D, tn), BF16),
                            pltpu.SemaphoreType.DMA((2,))]),
        compiler_params=_params(("arbitrary",), VMEM_LIMIT),
        name="expert_in_swiglu",
    )(*items, x_pad, w_exp_in, bias)


def _ffn2(items, act, w_exp_out, b_exp_out):
    P, F = act.shape
    n_exp, _, D = w_exp_out.shape
    n_items = items[0].shape[0]
    return pl.pallas_call(
        _ffn2_kernel,
        out_shape=jax.ShapeDtypeStruct((P, D // 2), U32),
        grid_spec=pltpu.PrefetchScalarGridSpec(
            num_scalar_prefetch=11,
            grid=(n_items,),
            in_specs=[pl.BlockSpec((EXPERT_ROWS, F), lambda q, ri, *_: (ri[q], 0)),
                      pl.BlockSpec(memory_space=pl.ANY),
                      pl.BlockSpec((1, 1, D), lambda q, ri, ro, e, *_: (e[q], 0, 0))],
            out_specs=pl.BlockSpec((EXPERT_ROWS, D // 2), lambda q, ri, ro, *_: (ro[q], 0)),
            scratch_shapes=[pltpu.VMEM((2, F, D), F32),
                            pltpu.VMEM((F, D), BF16),
                            pltpu.SemaphoreType.DMA((2,))]),
        compiler_params=_params(("arbitrary",), VMEM_LIMIT),
        name="expert_out",
    )(*items, act, w_exp_out, b_exp_out[:, None, :])


def _combine_kernel(dest_ref, y_ref, gates_ref, x1_ref, g2_ref, o_ref, buf, sem, *, tm):
    s = pl.program_id(0)
    ns = pl.num_programs(0)

    def gather(step, slot, unrolled):
        base = TOP_K * step * tm

        def row(t, k):
            d = dest_ref[base + (TOP_K * t + k)]
            pltpu.make_async_copy(y_ref.at[pl.ds(d, 1), :], buf.at[slot, k, pl.ds(t, 1), :],
                                  sem.at[slot]).start(priority=k % 2)

        if unrolled:
            for t in range(tm):
                for k in range(TOP_K):
                    row(t, k)
        else:
            def issue(t, _):
                for k in range(TOP_K):
                    row(t, k)
                return 0

            lax.fori_loop(0, tm, issue, 0)

    @pl.when(s == 0)
    def _():
        gather(0, 0, False)

    for par in range(2):
        @pl.when(jnp.logical_and(s + 1 < ns, (s + 1) % 2 == par))
        def _(par=par):
            gather(s + 1, par, True)

    slot = s % 2
    for k in range(TOP_K):
        pltpu.make_async_copy(y_ref.at[pl.ds(0, tm), :], buf.at[slot, k], sem.at[slot]).wait()
    gates = gates_ref[...]
    y_hi = y_lo = None
    for k in range(TOP_K):
        hi, lo = _unpack_halves(buf[slot, k])
        g = gates[:, k:k + 1]
        y_hi = g * hi if y_hi is None else y_hi + g * hi
        y_lo = g * lo if y_lo is None else y_lo + g * lo
    y = jnp.concatenate([y_hi, y_lo], axis=1)
    o_ref[...] = x1_ref[...] + g2_ref[0] * y


def _combine(dest_flat, y_pad, gates_wide, x1, gate2, S):
    T, D = x1.shape
    tm = 256
    per_b = S // tm
    return pl.pallas_call(
        functools.partial(_combine_kernel, tm=tm),
        out_shape=jax.ShapeDtypeStruct((T, D), F32),
        grid_spec=pltpu.PrefetchScalarGridSpec(
            num_scalar_prefetch=1,
            grid=(T // tm,),
            in_specs=[pl.BlockSpec(memory_space=pl.ANY),
                      pl.BlockSpec((tm, LANES), lambda i, d: (i, 0)),
                      pl.BlockSpec((tm, D), lambda i, d: (i, 0)),
                      pl.BlockSpec((1, 1, D), lambda i, d: (i // per_b, 0, 0))],
            out_specs=pl.BlockSpec((tm, D), lambda i, d: (i, 0)),
            scratch_shapes=[pltpu.VMEM((2, TOP_K, tm, D // 2), U32),
                            pltpu.SemaphoreType.DMA((2,))]),
        compiler_params=_params(("arbitrary",), VMEM_LIMIT),
        name="combine_rows",
    )(dest_flat, y_pad, gates_wide, x1, gate2[:, None, :])


def kernel(x, c, norm1_w, norm2_w, w_ada, b_ada, w_in, q_norm_w, k_norm_w, w_pool, pool_scale,
           w_o, w_router, b_router, w_exp_in, b_exp_in, w_exp_out, b_exp_out):
    B, S, D = x.shape
    T = B * S
    depth = w_ada.shape[0]
    n_exp = w_router.shape[-1]
    pool_width = pool_scale.shape[-1]
    sb_width = w_o.shape[1] - pool_width
    n_heads = sb_width // HEAD_DIM
    n_blocks = (T * TOP_K + n_exp * (EXPERT_ROWS - 1)) // EXPERT_ROWS
    n_rows = n_blocks * EXPERT_ROWS

    x2 = x.reshape(T, D)
    for l in range(depth):
        mod = _adaln(c, w_ada[l], b_ada[l])
        shift1, scale1, gate1, shift2, scale2, gate2 = jnp.split(mod, 6, axis=-1)

        proj = _inproj(x2, norm1_w[l], shift1, scale1, w_in[l].astype(BF16), S)
        proj3 = proj.reshape(B, S, -1)
        o_sb = _attention(proj3, q_norm_w[l], k_norm_w[l], n_heads)
        o_pool = _pool(proj3, w_pool[l], pool_scale[l], pool_width)
        x1, h2p, gates_wide, idx_wide = _outproj(
            o_sb.reshape(T, sb_width), o_pool.reshape(T, pool_width), w_o[l].astype(BF16),
            x2, gate1, shift2, scale2, norm2_w[l], w_router[l], b_router[l], S)

        dest_wide, meta = _route(idx_wide)
        cnt = meta[0, :n_exp]
        starts = meta[1, :n_exp]
        dest_flat = dest_wide[:, :TOP_K].reshape(T * TOP_K)
        x_pad = _dispatch(dest_flat, cnt, starts, h2p, n_rows)

        F = w_exp_out.shape[2]
        act = _ffn1(_work_items(cnt, F // 1024, n_blocks), x_pad, w_exp_in[l], b_exp_in[l])
        y_pad = _ffn2(_work_items(cnt, 1, n_blocks), act, w_exp_out[l], b_exp_out[l])
        x2 = _combine(dest_flat, y_pad, gates_wide, x1, gate2, S)
    return x2.reshape(B, S, D)
```

```python
import functools
import math

import jax
import jax.numpy as jnp
from jax import lax
from jax.experimental import pallas as pl
from jax.experimental.pallas import tpu as pltpu

F32 = jnp.float32
BF16 = jnp.bfloat16
I32 = jnp.int32
U32 = jnp.uint32

EPS = 1e-6
HEAD_DIM = 128
POOL_WINDOWS = (2, 4, 8, 16)
TOP_K = 4
SWIGLU_ALPHA = 1.702
SWIGLU_LIMIT = 7.0

LANES = 128
SUBLANES = 8
EXPERT_ROWS = 256
TAIL_ROWS = 64
MIN_TAIL_PIECES = 2
LOG_UNDERFLOW = 104.0
VMEM_LIMIT = 56 * 1024 * 1024
WEIGHT_DMA_PRIORITY = 1


def _params(sem=None, vmem=None):
    return pltpu.CompilerParams(dimension_semantics=sem, vmem_limit_bytes=vmem)


_HIGH_HALF = 0xFFFF0000


def _pack_rows(v):
    bits = lax.bitcast_convert_type(v.astype(BF16).astype(F32), U32)
    half = v.shape[1] // 2
    return (bits[:, :half] & jnp.uint32(_HIGH_HALF)) | (bits[:, half:] >> 16)


def _unpack_halves(p):
    hi = lax.bitcast_convert_type(p & jnp.uint32(_HIGH_HALF), F32)
    lo = lax.bitcast_convert_type(p << 16, F32)
    return hi, lo


def _unpack_rows(p):
    hi, lo = _unpack_halves(p)
    return jnp.concatenate([hi.astype(BF16), lo.astype(BF16)], axis=1)


def _adaln_kernel(c_ref, w_ref, b_ref, o_ref):
    c = c_ref[...]
    ca = c / (1.0 + jnp.exp(-c))
    o_ref[...] = jnp.dot(ca.astype(BF16), w_ref[...].astype(BF16),
                         preferred_element_type=F32) + b_ref[...]


def _adaln(c, w_ada, b_ada):
    B, D = c.shape
    N = w_ada.shape[1]
    rows = 8
    tn = 1024
    cp = jnp.zeros((rows, D), F32).at[:B].set(c)
    out = pl.pallas_call(
        _adaln_kernel,
        out_shape=jax.ShapeDtypeStruct((rows, N), F32),
        grid=(N // tn,),
        in_specs=[pl.BlockSpec((rows, D), lambda j: (0, 0)),
                  pl.BlockSpec((D, tn), lambda j: (0, j)),
                  pl.BlockSpec((1, tn), lambda j: (0, j))],
        out_specs=pl.BlockSpec((rows, tn), lambda j: (0, j)),
        compiler_params=_params(("arbitrary",), VMEM_LIMIT),
        name="adaln",
    )(cp, w_ada, b_ada.reshape(1, N))
    return out[:B]


def _inproj_kernel(x_ref, nw_ref, sh_ref, sc_ref, w_ref, o_ref, h_ref, *, tm, ch):
    @pl.when(pl.program_id(1) == 0)
    def _():
        mul = nw_ref[...] * (1.0 + sc_ref[0])
        add = sh_ref[0]

        def body(c, _):
            r0 = pl.multiple_of(c * ch, ch)
            x = x_ref[pl.ds(r0, ch), :]
            inv = lax.rsqrt(jnp.mean(x * x, axis=-1, keepdims=True) + EPS)
            h_ref[pl.ds(r0, ch), :] = (x * inv * mul + add).astype(BF16)
            return 0

        lax.fori_loop(0, tm // ch, body, 0)

    o_ref[...] = jnp.dot(h_ref[...], w_ref[...],
                         preferred_element_type=F32).astype(o_ref.dtype)


def _inproj(x2, norm_w, shift, scale, w_bf, S):
    T, D = x2.shape
    N = w_bf.shape[1]
    tm, tn, ch = 1024, 2048, 128
    per_b = S // tm
    return pl.pallas_call(
        functools.partial(_inproj_kernel, tm=tm, ch=ch),
        out_shape=jax.ShapeDtypeStruct((T, N), BF16),
        grid=(T // tm, N // tn),
        in_specs=[pl.BlockSpec((tm, D), lambda i, j: (i, 0)),
                  pl.BlockSpec((1, D), lambda i, j: (0, 0)),
                  pl.BlockSpec((1, 1, D), lambda i, j: (i // per_b, 0, 0)),
                  pl.BlockSpec((1, 1, D), lambda i, j: (i // per_b, 0, 0)),
                  pl.BlockSpec((D, tn), lambda i, j: (0, j))],
        out_specs=pl.BlockSpec((tm, tn), lambda i, j: (i, j)),
        scratch_shapes=[pltpu.VMEM((tm, D), BF16)],
        compiler_params=_params(("arbitrary", "arbitrary"), VMEM_LIMIT),
        name="inproj",
    )(x2, norm_w.reshape(1, D), shift[:, None, :], scale[:, None, :], w_bf)


def _attn_kernel(q_ref, k_ref, v_ref, qw_ref, kw_ref, o_ref, kn_ref, carry_ref, acc_ref,
                 *, S, tq, hg, scale):
    i = pl.program_id(2)
    d = HEAD_DIM

    def head_norm(x, w):
        parts = []
        for h in range(hg):
            xh = x[:, h * d:(h + 1) * d]
            inv = lax.rsqrt(jnp.mean(xh * xh, axis=-1, keepdims=True) + EPS)
            parts.append(xh * inv * w)
        return parts

    @pl.when(i == 0)
    def _():
        def body(c, _):
            r0 = pl.multiple_of(c * tq, tq)
            parts = head_norm(k_ref[0, pl.ds(r0, tq), :].astype(F32), kw_ref[...])
            for h in range(hg):
                kn_ref[pl.ds(r0, tq), h * d:(h + 1) * d] = parts[h].astype(BF16)
            return 0

        lax.fori_loop(0, S // tq, body, 0)

    qb = [(p * scale).astype(BF16) for p in head_norm(q_ref[0].astype(F32), qw_ref[...])]

    row = lax.broadcasted_iota(I32, (tq, tq), 0)
    col = lax.broadcasted_iota(I32, (tq, tq), 1)
    causal = col < row
    tri = (row > col).astype(BF16)

    def scores(h, rows, r0, nk, mask=None):
        kblk = kn_ref[pl.ds(r0, nk), h * d:(h + 1) * d]
        z = lax.dot_general(qb[h][rows], kblk, (((1,), (1,)), ((), ())),
                            preferred_element_type=F32)
        t = jnp.log(1.0 + jnp.exp(-jnp.abs(z)))
        lsn = jnp.minimum(-z, 0.0) - t
        lsp = lsn + z
        if mask is not None:
            lsn = jnp.where(mask, lsn, 0.0)
        later = jnp.dot(lsn.astype(BF16), tri[:nk, :nk], preferred_element_type=F32)
        return lsp + later, later[:, :1] + lsn[:, :1]

    def weighted(a, h, r0, nk):
        vblk = v_ref[0, pl.ds(r0, nk), h * d:(h + 1) * d]
        return jnp.dot(a.astype(BF16), vblk, preferred_element_type=F32)

    has_prev = i > 0
    rd = pl.multiple_of(i * tq, tq)
    rp = pl.multiple_of(jnp.maximum(i - 1, 0) * tq, tq)
    half = tq // 2
    top, bottom, every = slice(0, half), slice(half, tq), slice(0, tq)
    worst = None
    for h in range(hg):
        cols = slice(h * d, (h + 1) * d)
        log_t, sum_t = scores(h, top, rd, half, causal[top, top])
        log_b, sum_b = scores(h, bottom, rd, tq, causal[bottom, :])
        log_p, sum_p = scores(h, every, rp, tq)
        sum_d = jnp.concatenate([sum_t, sum_b], axis=0)
        a_t = jnp.where(causal[top, top], jnp.exp(log_t), 0.0)
        a_b = jnp.where(causal[bottom, :], jnp.exp(log_b), 0.0)
        a_p = jnp.where(has_prev, jnp.exp(log_p + sum_d), 0.0)
        from_prev = weighted(a_p, h, rp, tq)
        acc_ref[top, cols] = weighted(a_t, h, rd, half) + from_prev[top]
        acc_ref[bottom, cols] = weighted(a_b, h, rd, tq) + from_prev[bottom]
        carry = jnp.where(has_prev, sum_d + sum_p, sum_d)
        carry_ref[h] = carry
        m = jnp.max(carry)
        worst = m if worst is None else jnp.maximum(worst, m)

    def earlier(kb):
        r0 = pl.multiple_of(kb * tq, tq)
        worst = None
        for h in range(hg):
            cols = slice(h * d, (h + 1) * d)
            log_a, row_sum = scores(h, every, r0, tq)
            acc_ref[:, cols] += weighted(jnp.exp(log_a + carry_ref[h]), h, r0, tq)
            carry = carry_ref[h] + row_sum
            carry_ref[h] = carry
            m = jnp.max(carry)
            worst = m if worst is None else jnp.maximum(worst, m)
        return worst

    def cond(st):
        kb, m = st
        return jnp.logical_and(kb >= 0, m > -LOG_UNDERFLOW)

    def body(st):
        kb, _ = st
        return kb - 1, earlier(kb)

    lax.while_loop(cond, body, (i - 2, worst))
    o_ref[0] = acc_ref[...].astype(o_ref.dtype)


def _attention(proj3, q_norm_w, k_norm_w, n_heads):
    B, S, _ = proj3.shape
    d = HEAD_DIM
    tq = 256
    hg = 8
    G = n_heads // hg
    w = hg * d
    return pl.pallas_call(
        functools.partial(_attn_kernel, S=S, tq=tq, hg=hg, scale=1.0 / math.sqrt(d)),
        out_shape=jax.ShapeDtypeStruct((B, S, n_heads * d), BF16),
        grid=(B, G, S // tq),
        in_specs=[pl.BlockSpec((1, tq, w), lambda b, g, i: (b, i, g)),
                  pl.BlockSpec((1, S, w), lambda b, g, i: (b, 0, G + g)),
                  pl.BlockSpec((1, S, w), lambda b, g, i: (b, 0, 2 * G + g)),
                  pl.BlockSpec((1, d), lambda b, g, i: (0, 0)),
                  pl.BlockSpec((1, d), lambda b, g, i: (0, 0))],
        out_specs=pl.BlockSpec((1, tq, w), lambda b, g, i: (b, i, g)),
        scratch_shapes=[pltpu.VMEM((S, w), BF16),
                        pltpu.VMEM((hg, tq, 1), F32),
                        pltpu.VMEM((tq, w), F32)],
        compiler_params=_params(("arbitrary", "arbitrary", "arbitrary"), VMEM_LIMIT),
        name="stickbreak_attn",
    )(proj3, proj3, proj3, q_norm_w.reshape(1, d), k_norm_w.reshape(1, d))


def _pool_kernel(u_ref, w_ref, ps_ref, o_ref, *, S, ch, gd):
    halo = 16
    wgs = [w_ref[g].astype(BF16) for g in range(len(POOL_WINDOWS))]

    def body(c, _):
        r0 = pl.multiple_of(c * ch, ch)
        p0 = pl.multiple_of(jnp.maximum(r0 - halo, 0), halo)
        t = r0 + lax.broadcasted_iota(I32, (ch, 1), 0)
        for g, win in enumerate(POOL_WINDOWS):
            lo, hi = g * gd, (g + 1) * gd
            cur = u_ref[0, pl.ds(r0, ch), lo:hi].astype(F32)
            prev = u_ref[0, pl.ds(p0, halo), lo:hi].astype(F32)
            prev = jnp.where(c > 0, prev, 0.0)
            s = jnp.concatenate([prev, cur], axis=0)
            n = 1
            while n < win:
                s = s + pltpu.roll(s, n, 0)
                n *= 2
            s = s[halo:]
            cnt = jnp.minimum(t + 1, win).astype(F32)
            p = s / cnt - cur
            y = jnp.dot(p.astype(BF16), wgs[g], preferred_element_type=F32) * ps_ref[:, lo:hi]
            o_ref[0, pl.ds(r0, ch), lo:hi] = y.astype(o_ref.dtype)
        return 0

    lax.fori_loop(0, S // ch, body, 0)


def _pool(proj3, w_pool, pool_scale, pool_width):
    B, S, NP = proj3.shape
    G, gd, _ = w_pool.shape
    return pl.pallas_call(
        functools.partial(_pool_kernel, S=S, ch=256, gd=gd),
        out_shape=jax.ShapeDtypeStruct((B, S, pool_width), BF16),
        grid=(B,),
        in_specs=[pl.BlockSpec((1, S, pool_width), lambda b: (b, 0, NP // pool_width - 1)),
                  pl.BlockSpec((G, gd, gd), lambda b: (0, 0, 0)),
                  pl.BlockSpec((1, pool_width), lambda b: (0, 0))],
        out_specs=pl.BlockSpec((1, S, pool_width), lambda b: (b, 0, 0)),
        compiler_params=_params(("arbitrary",), VMEM_LIMIT),
        name="pool_mixer",
    )(proj3, w_pool, pool_scale.reshape(1, pool_width))


def _outproj_kernel(osb_ref, opool_ref, wo_ref, x_ref, g1_ref, sh_ref, sc_ref, nw_ref,
                    wr_ref, br_ref, x1_ref, h2p_ref, gates_ref, idx_ref, *, sbw, n_exp, sub):
    for r0 in range(0, x_ref.shape[0], sub):
        rows = slice(r0, r0 + sub)
        _outproj_rows(osb_ref.at[rows], opool_ref.at[rows], wo_ref, x_ref.at[rows], g1_ref,
                      sh_ref, sc_ref, nw_ref, wr_ref, br_ref, x1_ref.at[rows], h2p_ref.at[rows],
                      gates_ref.at[rows], idx_ref.at[:, rows], sbw=sbw, n_exp=n_exp)


def _outproj_rows(osb_ref, opool_ref, wo_ref, x_ref, g1_ref, sh_ref, sc_ref, nw_ref,
                  wr_ref, br_ref, x1_ref, h2p_ref, gates_ref, idx_ref, *, sbw, n_exp):
    tm, D = x_ref.shape
    mixed = (jnp.dot(osb_ref[...], wo_ref[:sbw, :], preferred_element_type=F32)
             + jnp.dot(opool_ref[...], wo_ref[sbw:, :], preferred_element_type=F32))
    x1 = x_ref[...] + g1_ref[0] * mixed
    x1_ref[...] = x1
    inv = lax.rsqrt(jnp.mean(x1 * x1, axis=-1, keepdims=True) + EPS)
    h2 = x1 * inv * (nw_ref[...] * (1.0 + sc_ref[0])) + sh_ref[0]
    hb = h2.astype(BF16)
    h2p_ref[...] = _pack_rows(h2)

    vals = lax.dot_general(wr_ref[...].astype(BF16), hb, (((1,), (1,)), ((), ())),
                           preferred_element_type=F32) + br_ref[...]
    expert = lax.broadcasted_iota(I32, (n_exp, tm), 0).astype(F32)
    tops, ids = [], []
    for _ in range(TOP_K):
        m = jnp.max(vals, axis=0, keepdims=True)
        first = jnp.min(jnp.where(vals == m, expert, float(n_exp)), axis=0, keepdims=True)
        tops.append(m)
        ids.append(first)
        vals = jnp.where(expert == first, -jnp.inf, vals)
    es = [jnp.exp(m - tops[0]) for m in tops]
    den = es[0]
    for e in es[1:]:
        den = den + e
    slot = lax.broadcasted_iota(I32, (LANES, tm), 0)
    gates = jnp.zeros((LANES, tm), F32)
    for k in range(TOP_K):
        gates = jnp.where(slot == k, es[k] / den, gates)
    gates_ref[...] = gates.T
    slot8 = lax.broadcasted_iota(I32, (SUBLANES, tm), 0)
    idx = jnp.zeros((SUBLANES, tm), F32)
    for k in range(TOP_K):
        idx = jnp.where(slot8 == k, ids[k], idx)
    idx_ref[...] = idx.astype(I32)


def _outproj(o_sb, o_pool, wo_bf, x2, gate1, shift2, scale2, norm2_w, w_router, b_router, S):
    T, D = x2.shape
    sbw = o_sb.shape[1]
    pw = o_pool.shape[1]
    n_exp = w_router.shape[1]
    tm, sub = 512, 256
    per_b = S // tm
    wr = w_router.T
    br = b_router.reshape(n_exp, 1)
    mod_spec = pl.BlockSpec((1, 1, D), lambda i: (i // per_b, 0, 0))
    return pl.pallas_call(
        functools.partial(_outproj_kernel, sbw=sbw, n_exp=n_exp, sub=sub),
        out_shape=(jax.ShapeDtypeStruct((T, D), F32),
                   jax.ShapeDtypeStruct((T, D // 2), U32),
                   jax.ShapeDtypeStruct((T, LANES), F32),
                   jax.ShapeDtypeStruct((SUBLANES, T), I32)),
        grid=(T // tm,),
        in_specs=[pl.BlockSpec((tm, sbw), lambda i: (i, 0)),
                  pl.BlockSpec((tm, pw), lambda i: (i, 0)),
                  pl.BlockSpec((sbw + pw, D), lambda i: (0, 0)),
                  pl.BlockSpec((tm, D), lambda i: (i, 0)),
                  mod_spec, mod_spec, mod_spec,
                  pl.BlockSpec((1, D), lambda i: (0, 0)),
                  pl.BlockSpec((n_exp, D), lambda i: (0, 0)),
                  pl.BlockSpec((n_exp, 1), lambda i: (0, 0))],
        out_specs=(pl.BlockSpec((tm, D), lambda i: (i, 0)),
                   pl.BlockSpec((tm, D // 2), lambda i: (i, 0)),
                   pl.BlockSpec((tm, LANES), lambda i: (i, 0)),
                   pl.BlockSpec((SUBLANES, tm), lambda i: (0, i))),
        compiler_params=_params(("arbitrary",), VMEM_LIMIT),
        name="outproj_router",
    )(o_sb, o_pool, wo_bf, x2, gate1[:, None, :], shift2[:, None, :], scale2[:, None, :],
      norm2_w.reshape(1, D), wr, br)


def _route_kernel(idx_ref, dest_ref, meta_ref, rank_ref, *, T, ch, n_exp):
    expert = lax.broadcasted_iota(I32, (n_exp, ch), 0)
    row = lax.broadcasted_iota(I32, (ch, ch), 0)
    col = lax.broadcasted_iota(I32, (ch, ch), 1)
    earlier = (row < col).astype(BF16)

    def chunk(c):
        return slice(c * ch, (c + 1) * ch)

    def count(c, cnt):
        member = expert == idx_ref[0:1, chunk(c)]
        for k in range(1, TOP_K):
            member = jnp.logical_or(member, expert == idx_ref[k:k + 1, chunk(c)])
        mf = jnp.where(member, 1.0, 0.0)
        rank_ref[:, chunk(c)] = jnp.dot(mf.astype(BF16), earlier,
                                        preferred_element_type=F32) + cnt
        return cnt + jnp.sum(mf, axis=1, keepdims=True)

    cnt = jnp.zeros((n_exp, 1), F32)
    for c in range(T // ch):
        cnt = count(c, cnt)
    padded = jnp.broadcast_to(jnp.ceil(cnt / EXPERT_ROWS) * EXPERT_ROWS, (n_exp, LANES))
    sub = lax.broadcasted_iota(I32, (n_exp, LANES), 0)
    ends = padded
    sh = 1
    while sh < n_exp:
        ends = ends + jnp.where(sub >= sh, pltpu.roll(ends, sh, 0), 0.0)
        sh *= 2
    starts_wide = ends - padded
    starts = starts_wide[:, :1]
    lane = lax.broadcasted_iota(I32, (n_exp, LANES), 1)
    meta = jnp.where(lane == 0, jnp.broadcast_to(cnt, (n_exp, LANES)),
                     jnp.where(lane == 1, starts_wide, 0.0))
    meta_ref[...] = meta.astype(I32)

    slot = lax.broadcasted_iota(I32, (SUBLANES, ch), 0)

    def place(c, _):
        val = rank_ref[:, chunk(c)] + starts
        out = jnp.zeros((SUBLANES, ch), F32)
        for k in range(TOP_K):
            mine = expert == idx_ref[k:k + 1, chunk(c)]
            d = jnp.sum(jnp.where(mine, val, 0.0), axis=0, keepdims=True)
            out = jnp.where(slot == k, d, out)
        dest_ref[:, chunk(c)] = out.astype(I32)
        return 0

    for c in range(T // ch):
        place(c, 0)


def _route(idx_t, n_exp):
    T = idx_t.shape[1]
    return pl.pallas_call(
        functools.partial(_route_kernel, T=T, ch=256, n_exp=n_exp),
        out_shape=(jax.ShapeDtypeStruct((SUBLANES, T), I32),
                   jax.ShapeDtypeStruct((n_exp, LANES), I32)),
        grid=(1,),
        in_specs=[pl.BlockSpec((SUBLANES, T), lambda i: (0, 0))],
        out_specs=(pl.BlockSpec((SUBLANES, T), lambda i: (0, 0)),
                   pl.BlockSpec((n_exp, LANES), lambda i: (0, 0))),
        scratch_shapes=[pltpu.VMEM((n_exp, T), F32)],
        compiler_params=_params(("arbitrary",), VMEM_LIMIT),
        name="route_ranks",
    )(idx_t)


def _dispatch_kernel(dest_ref, cnt_ref, start_ref, h_ref, x_ref, z_ref, sem, zsem,
                     *, tb, n_exp, n_blocks, n_tokens):
    s = pl.program_id(0)

    @pl.when(s == 0)
    def _():
        z_ref[...] = jnp.zeros_like(z_ref)
        _dispatch_zero_fill(cnt_ref, start_ref, x_ref, z_ref, zsem, n_exp, n_blocks)

    base = s * tb
    for t in range(tb):
        for k in range(TOP_K):
            d = dest_ref[base + (k * n_tokens + t)]
            pltpu.make_async_copy(h_ref.at[pl.ds(t, 1), :], x_ref.at[pl.ds(d, 1), :],
                                  sem).start(priority=k % 2)

    for _ in range(TOP_K):
        pltpu.make_async_copy(h_ref, x_ref.at[pl.ds(0, tb), :], sem).wait()


def _dispatch_zero_fill(cnt_ref, start_ref, x_ref, z_ref, zsem, n_exp, n_blocks):
    def block_fill(blk, wait):
        r0 = pl.multiple_of(blk * EXPERT_ROWS, EXPERT_ROWS)
        cp = pltpu.make_async_copy(z_ref, x_ref.at[pl.ds(r0, EXPERT_ROWS), :], zsem)
        if wait:
            cp.wait()
        else:
            cp.start()

    def zero_fill(e, wait):
        cnt = cnt_ref[e]

        @pl.when((cnt & (EXPERT_ROWS - 1)) != 0)
        def _():
            block_fill((start_ref[e] + cnt) // EXPERT_ROWS, wait)

        return 0

    used = (start_ref[n_exp - 1] + cnt_ref[n_exp - 1] + EXPERT_ROWS - 1) // EXPERT_ROWS

    def tail_fill(blk, wait):
        block_fill(blk, wait)
        return 0

    lax.fori_loop(0, n_exp, lambda e, _: zero_fill(e, False), 0)
    lax.fori_loop(used, n_blocks, lambda b, _: tail_fill(b, False), 0)
    lax.fori_loop(0, n_exp, lambda e, _: zero_fill(e, True), 0)
    lax.fori_loop(used, n_blocks, lambda b, _: tail_fill(b, True), 0)


def _dispatch(dest_flat, cnt, starts, h2p, n_rows):
    T, W = h2p.shape
    n_exp = cnt.shape[0]
    tb = 256
    return pl.pallas_call(
        functools.partial(_dispatch_kernel, tb=tb, n_exp=n_exp, n_blocks=n_rows // EXPERT_ROWS,
                          n_tokens=T),
        out_shape=jax.ShapeDtypeStruct((n_rows, W), U32),
        grid_spec=pltpu.PrefetchScalarGridSpec(
            num_scalar_prefetch=3,
            grid=(T // tb,),
            in_specs=[pl.BlockSpec((tb, W), lambda s, *_: (s, 0))],
            out_specs=pl.BlockSpec(memory_space=pl.ANY),
            scratch_shapes=[pltpu.VMEM((EXPERT_ROWS, W), U32),
                            pltpu.SemaphoreType.DMA, pltpu.SemaphoreType.DMA]),
        compiler_params=_params(("arbitrary",), VMEM_LIMIT),
        name="dispatch_rows",
    )(dest_flat, cnt, starts, h2p)


def _ffn1_kernel(ri_ref, ro_ref, e_ref, j_ref, first_ref, valid_ref, slot_ref, ne_ref, nj_ref, more_ref, pieces_ref,
                 x_ref, w_ref, b_ref, o_ref, stage, wbf, sem, *, F, tn):
    q = pl.program_id(0)

    def wcopy(e, j, slot, part):
        c0 = pl.multiple_of(part * F + j * tn, tn)
        return pltpu.make_async_copy(w_ref.at[e, :, pl.ds(c0, tn)], stage.at[slot, part],
                                     sem.at[slot])

    @pl.when(q == 0)
    def _():
        for part in range(2):
            wcopy(e_ref[0], j_ref[0], 0, part).start(priority=WEIGHT_DMA_PRIORITY)

    @pl.when(first_ref[q] == 1)
    def _():
        slot = slot_ref[q]
        for part in range(2):
            wcopy(e_ref[q], j_ref[q], slot, part).wait()

        @pl.when(more_ref[q] == 1)
        def _():
            for part in range(2):
                wcopy(ne_ref[q], nj_ref[q], 1 - slot, part).start(priority=WEIGHT_DMA_PRIORITY)

    def swiglu_block(w_gate, w_lin, rows):
        xb = _unpack_rows(x_ref[:rows, :])
        g = jnp.dot(xb, w_gate, preferred_element_type=F32) + b_ref[0, 0, 0]
        lin = jnp.dot(xb, w_lin, preferred_element_type=F32) + b_ref[0, 1, 0]
        g = jnp.minimum(g, SWIGLU_LIMIT)
        lin = jnp.clip(lin, -SWIGLU_LIMIT, SWIGLU_LIMIT)
        act = g / (1.0 + jnp.exp(-SWIGLU_ALPHA * g)) * (lin + 1.0)
        o_ref[:rows, :] = act.astype(o_ref.dtype)
        if rows < EXPERT_ROWS:
            o_ref[rows:, :] = jnp.zeros((EXPERT_ROWS - rows, o_ref.shape[1]), o_ref.dtype)

    @pl.when(first_ref[q] == 1)
    def _():
        slot = slot_ref[q]
        w16 = [stage[slot, part].astype(BF16) for part in range(2)]
        for part in range(2):
            wbf[part] = w16[part]
        swiglu_block(w16[0], w16[1], EXPERT_ROWS)

    later = jnp.logical_and(valid_ref[q] == 1, first_ref[q] == 0)

    for pieces in range(MIN_TAIL_PIECES, EXPERT_ROWS // TAIL_ROWS + 1):
        @pl.when(jnp.logical_and(later, pieces_ref[q] == pieces))
        def _(pieces=pieces):
            swiglu_block(wbf[0], wbf[1], pieces * TAIL_ROWS)

    @pl.when(valid_ref[q] == 0)
    def _():
        o_ref[...] = jnp.zeros_like(o_ref)


def _ffn2_kernel(ri_ref, ro_ref, e_ref, j_ref, first_ref, valid_ref, slot_ref, ne_ref, nj_ref, more_ref, pieces_ref,
                 a_ref, w_ref, b_ref, o_ref, stage, wbf, sem):
    q = pl.program_id(0)

    def wcopy(e, slot):
        return pltpu.make_async_copy(w_ref.at[e], stage.at[slot], sem.at[slot])

    @pl.when(q == 0)
    def _():
        wcopy(e_ref[0], 0).start(priority=WEIGHT_DMA_PRIORITY)

    @pl.when(first_ref[q] == 1)
    def _():
        slot = slot_ref[q]
        wcopy(e_ref[q], slot).wait()

        @pl.when(more_ref[q] == 1)
        def _():
            wcopy(ne_ref[q], 1 - slot).start(priority=WEIGHT_DMA_PRIORITY)

    def out_block(w, rows):
        y = jnp.dot(a_ref[:rows, :], w, preferred_element_type=F32) + b_ref[0]
        o_ref[:rows, :] = _pack_rows(y)
        if rows < EXPERT_ROWS:
            o_ref[rows:, :] = jnp.zeros((EXPERT_ROWS - rows, o_ref.shape[1]), o_ref.dtype)

    @pl.when(first_ref[q] == 1)
    def _():
        w16 = stage[slot_ref[q]].astype(BF16)
        wbf[...] = w16
        out_block(w16, EXPERT_ROWS)

    later = jnp.logical_and(valid_ref[q] == 1, first_ref[q] == 0)

    for pieces in range(MIN_TAIL_PIECES, EXPERT_ROWS // TAIL_ROWS + 1):
        @pl.when(jnp.logical_and(later, pieces_ref[q] == pieces))
        def _(pieces=pieces):
            out_block(wbf[...], pieces * TAIL_ROWS)

    @pl.when(valid_ref[q] == 0)
    def _():
        o_ref[...] = jnp.zeros_like(o_ref)


def _work_items(cnt, n_col_tiles, n_blocks):
    n_exp = cnt.shape[0]
    nblk = (cnt + EXPERT_ROWS - 1) // EXPERT_ROWS
    bstart = jnp.cumsum(nblk) - nblk
    gsize = jnp.repeat(nblk, n_col_tiles)
    gend = jnp.cumsum(gsize)
    n_groups = n_exp * n_col_tiles
    gid = jnp.arange(n_groups, dtype=I32)
    total = gend[-1]
    q = jnp.arange(n_blocks * n_col_tiles, dtype=I32)
    qc = jnp.minimum(q, total - 1)
    g = jnp.sum((gend[None, :] <= qc[:, None]).astype(I32), axis=1)
    nonempty = gsize > 0
    ordinal = jnp.cumsum(nonempty.astype(I32)) - 1
    nxt_incl = lax.cummin(jnp.where(nonempty, gid, n_groups), reverse=True)
    nxt = jnp.concatenate([nxt_incl[1:], jnp.full((1,), n_groups, I32)])
    more = nxt < n_groups
    nxt = jnp.minimum(nxt, n_groups - 1)
    per_group = jnp.stack([gend - gsize, gid // n_col_tiles, gid % n_col_tiles,
                           jnp.repeat(bstart, n_col_tiles), ordinal % 2,
                           nxt // n_col_tiles, nxt % n_col_tiles, more.astype(I32),
                           jnp.repeat(cnt, n_col_tiles)])
    pick = (g[None, :, None] == gid[None, None, :]).astype(I32)
    gstart, e, j, brow, slot, ne, nj, more, rows = jnp.sum(pick * per_group[:, None, :], axis=2)
    r = qc - gstart
    valid = q < total
    first = jnp.logical_and(valid, r == 0)
    pieces = jnp.clip((rows - r * EXPERT_ROWS + TAIL_ROWS - 1) // TAIL_ROWS, MIN_TAIL_PIECES,
                      EXPERT_ROWS // TAIL_ROWS)
    over = q - total
    row_in = brow + r
    row_out = jnp.where(valid, row_in, jnp.sum(nblk) + over // n_col_tiles)
    col_out = jnp.where(valid, j, over % n_col_tiles)
    as_i32 = lambda a: a.astype(I32)
    return tuple(map(as_i32, (row_in, row_out, e, col_out, first, valid, slot, ne, nj, more,
                              pieces)))


def _ffn1(items, x_pad, w_exp_in, b_exp_in):
    P, W = x_pad.shape
    n_exp, D, F2 = w_exp_in.shape
    F = F2 // 2
    tn = 1024
    nj = F // tn
    n_items = items[0].shape[0]
    bias = b_exp_in.reshape(n_exp, 2, nj, 1, tn)
    return pl.pallas_call(
        functools.partial(_ffn1_kernel, F=F, tn=tn),
        out_shape=jax.ShapeDtypeStruct((P, F), BF16),
        grid_spec=pltpu.PrefetchScalarGridSpec(
            num_scalar_prefetch=11,
            grid=(n_items,),
            in_specs=[pl.BlockSpec((EXPERT_ROWS, W), lambda q, ri, *_: (ri[q], 0)),
                      pl.BlockSpec(memory_space=pl.ANY),
                      pl.BlockSpec((1, 2, 1, 1, tn),
                                   lambda q, ri, ro, e, j, *_: (e[q], 0, j[q], 0, 0))],
            out_specs=pl.BlockSpec((EXPERT_ROWS, tn),
                                   lambda q, ri, ro, e, j, *_: (ro[q], j[q])),
            scratch_shapes=[pltpu.VMEM((2, 2, D, tn), F32),
                            pltpu.VMEM((2, D, tn), BF16),
                            pltpu.SemaphoreType.DMA((2,))]),
        compiler_params=_params(("arbitrary",), VMEM_LIMIT),
        name="expert_in_swiglu",
    )(*items, x_pad, w_exp_in, bias)


def _ffn2(items, act, w_exp_out, b_exp_out):
    P, F = act.shape
    n_exp, _, D = w_exp_out.shape
    n_items = items[0].shape[0]
    return pl.pallas_call(
        _ffn2_kernel,
        out_shape=jax.ShapeDtypeStruct((P, D // 2), U32),
        grid_spec=pltpu.PrefetchScalarGridSpec(
            num_scalar_prefetch=11,
            grid=(n_items,),
            in_specs=[pl.BlockSpec((EXPERT_ROWS, F), lambda q, ri, *_: (ri[q], 0)),
                      pl.BlockSpec(memory_space=pl.ANY),
                      pl.BlockSpec((1, 1, D), lambda q, ri, ro, e, *_: (e[q], 0, 0))],
            out_specs=pl.BlockSpec((EXPERT_ROWS, D // 2), lambda q, ri, ro, *_: (ro[q], 0)),
            scratch_shapes=[pltpu.VMEM((2, F, D), F32),
                            pltpu.VMEM((F, D), BF16),
                            pltpu.SemaphoreType.DMA((2,))]),
        compiler_params=_params(("arbitrary",), VMEM_LIMIT),
        name="expert_out",
    )(*items, act, w_exp_out, b_exp_out[:, None, :])


def _combine_kernel(dest_ref, y_ref, gates_ref, x1_ref, g2_ref, o_ref, buf, sem,
                    *, tm, n_tokens):
    s = pl.program_id(0)
    ns = pl.num_programs(0)

    def gather(step, slot, unrolled):
        base = step * tm

        def row(t, k):
            d = dest_ref[base + (k * n_tokens + t)]
            pltpu.make_async_copy(y_ref.at[pl.ds(d, 1), :], buf.at[slot, k, pl.ds(t, 1), :],
                                  sem.at[slot]).start(priority=k % 2)

        if unrolled:
            for t in range(tm):
                for k in range(TOP_K):
                    row(t, k)
        else:
            def issue(t, _):
                for k in range(TOP_K):
                    row(t, k)
                return 0

            lax.fori_loop(0, tm, issue, 0)

    @pl.when(s == 0)
    def _():
        gather(0, 0, False)

    for par in range(2):
        @pl.when(jnp.logical_and(s + 1 < ns, (s + 1) % 2 == par))
        def _(par=par):
            gather(s + 1, par, True)

    slot = s % 2
    for k in range(TOP_K):
        pltpu.make_async_copy(y_ref.at[pl.ds(0, tm), :], buf.at[slot, k], sem.at[slot]).wait()
    gates = gates_ref[...]
    y_hi = y_lo = None
    for k in range(TOP_K):
        hi, lo = _unpack_halves(buf[slot, k])
        g = gates[:, k:k + 1]
        y_hi = g * hi if y_hi is None else y_hi + g * hi
        y_lo = g * lo if y_lo is None else y_lo + g * lo
    y = jnp.concatenate([y_hi, y_lo], axis=1)
    o_ref[...] = x1_ref[...] + g2_ref[0] * y


def _combine(dest_flat, y_pad, gates_wide, x1, gate2, S):
    T, D = x1.shape
    tm = 256
    per_b = S // tm
    return pl.pallas_call(
        functools.partial(_combine_kernel, tm=tm, n_tokens=T),
        out_shape=jax.ShapeDtypeStruct((T, D), F32),
        grid_spec=pltpu.PrefetchScalarGridSpec(
            num_scalar_prefetch=1,
            grid=(T // tm,),
            in_specs=[pl.BlockSpec(memory_space=pl.ANY),
                      pl.BlockSpec((tm, LANES), lambda i, d: (i, 0)),
                      pl.BlockSpec((tm, D), lambda i, d: (i, 0)),
                      pl.BlockSpec((1, 1, D), lambda i, d: (i // per_b, 0, 0))],
            out_specs=pl.BlockSpec((tm, D), lambda i, d: (i, 0)),
            scratch_shapes=[pltpu.VMEM((2, TOP_K, tm, D // 2), U32),
                            pltpu.SemaphoreType.DMA((2,))]),
        compiler_params=_params(("arbitrary",), VMEM_LIMIT),
        name="combine_rows",
    )(dest_flat, y_pad, gates_wide, x1, gate2[:, None, :])


def kernel(x, c, norm1_w, norm2_w, w_ada, b_ada, w_in, q_norm_w, k_norm_w, w_pool, pool_scale,
           w_o, w_router, b_router, w_exp_in, b_exp_in, w_exp_out, b_exp_out):
    B, S, D = x.shape
    T = B * S
    depth = w_ada.shape[0]
    n_exp = w_router.shape[-1]
    pool_width = pool_scale.shape[-1]
    sb_width = w_o.shape[1] - pool_width
    n_heads = sb_width // HEAD_DIM
    n_blocks = (T * TOP_K + n_exp * (EXPERT_ROWS - 1)) // EXPERT_ROWS
    n_rows = n_blocks * EXPERT_ROWS

    x2 = x.reshape(T, D)
    for l in range(depth):
        mod = _adaln(c, w_ada[l], b_ada[l])
        shift1, scale1, gate1, shift2, scale2, gate2 = jnp.split(mod, 6, axis=-1)

        proj = _inproj(x2, norm1_w[l], shift1, scale1, w_in[l].astype(BF16), S)
        proj3 = proj.reshape(B, S, -1)
        o_sb = _attention(proj3, q_norm_w[l], k_norm_w[l], n_heads)
        o_pool = _pool(proj3, w_pool[l], pool_scale[l], pool_width)
        x1, h2p, gates_wide, idx_t = _outproj(
            o_sb.reshape(T, sb_width), o_pool.reshape(T, pool_width), w_o[l].astype(BF16),
            x2, gate1, shift2, scale2, norm2_w[l], w_router[l], b_router[l], S)

        dest_t, meta = _route(idx_t, n_exp)
        cnt = meta[:, 0]
        starts = meta[:, 1]
        dest_flat = dest_t[:TOP_K].reshape(TOP_K * T)
        x_pad = _dispatch(dest_flat, cnt, starts, h2p, n_rows)

        F = w_exp_out.shape[2]
        act = _ffn1(_work_items(cnt, F // 1024, n_blocks), x_pad, w_exp_in[l], b_exp_in[l])
        y_pad = _ffn2(_work_items(cnt, 1, n_blocks), act, w_exp_out[l], b_exp_out[l])
        x2 = _combine(dest_flat, y_pad, gates_wide, x1, gate2, S)
    return x2.reshape(B, S, D)
```

```python
import functools
import math

import jax
import jax.numpy as jnp
from jax import lax
from jax.experimental import pallas as pl
from jax.experimental.pallas import tpu as pltpu

F32 = jnp.float32
BF16 = jnp.bfloat16
I32 = jnp.int32
U32 = jnp.uint32

EPS = 1e-6
HEAD_DIM = 128
POOL_WINDOWS = (2, 4, 8, 16)
TOP_K = 4
SWIGLU_ALPHA = 1.702
SWIGLU_LIMIT = 7.0

LANES = 128
SUBLANES = 8
EXPERT_ROWS = 256
TAIL_ROWS = 64
MIN_TAIL_PIECES = 2
LOG_UNDERFLOW = 104.0
VMEM_LIMIT = 56 * 1024 * 1024
WEIGHT_DMA_PRIORITY = 1


def _params(sem=None, vmem=None):
    return pltpu.CompilerParams(dimension_semantics=sem, vmem_limit_bytes=vmem)


_HIGH_HALF = 0xFFFF0000


def _pack_rows(v):
    bits = lax.bitcast_convert_type(v.astype(BF16).astype(F32), U32)
    half = v.shape[1] // 2
    return (bits[:, :half] & jnp.uint32(_HIGH_HALF)) | (bits[:, half:] >> 16)


def _unpack_halves(p):
    hi = lax.bitcast_convert_type(p & jnp.uint32(_HIGH_HALF), F32)
    lo = lax.bitcast_convert_type(p << 16, F32)
    return hi, lo


def _unpack_rows(p):
    hi, lo = _unpack_halves(p)
    return jnp.concatenate([hi.astype(BF16), lo.astype(BF16)], axis=1)


def _adaln_kernel(c_ref, w_ref, b_ref, o_ref):
    c = c_ref[...]
    ca = c / (1.0 + jnp.exp(-c))
    o_ref[...] = jnp.dot(ca.astype(BF16), w_ref[...].astype(BF16),
                         preferred_element_type=F32) + b_ref[...]


def _adaln(c, w_ada, b_ada):
    B, D = c.shape
    N = w_ada.shape[1]
    rows = 8
    tn = 1024
    cp = jnp.zeros((rows, D), F32).at[:B].set(c)
    out = pl.pallas_call(
        _adaln_kernel,
        out_shape=jax.ShapeDtypeStruct((rows, N), F32),
        grid=(N // tn,),
        in_specs=[pl.BlockSpec((rows, D), lambda j: (0, 0)),
                  pl.BlockSpec((D, tn), lambda j: (0, j)),
                  pl.BlockSpec((1, tn), lambda j: (0, j))],
        out_specs=pl.BlockSpec((rows, tn), lambda j: (0, j)),
        compiler_params=_params(("arbitrary",), VMEM_LIMIT),
        name="adaln",
    )(cp, w_ada, b_ada.reshape(1, N))
    return out[:B]


def _inproj_kernel(x_ref, nw_ref, sh_ref, sc_ref, w_ref, wo_ref, o_ref, wo16_ref, h_ref,
                   *, tm, ch):
    wo16_ref[...] = wo_ref[...].astype(BF16)

    @pl.when(pl.program_id(1) == 0)
    def _():
        mul = nw_ref[...] * (1.0 + sc_ref[0])
        add = sh_ref[0]

        def body(c, _):
            r0 = pl.multiple_of(c * ch, ch)
            x = x_ref[pl.ds(r0, ch), :]
            inv = lax.rsqrt(jnp.mean(x * x, axis=-1, keepdims=True) + EPS)
            h_ref[pl.ds(r0, ch), :] = (x * inv * mul + add).astype(BF16)
            return 0

        lax.fori_loop(0, tm // ch, body, 0)

    o_ref[...] = jnp.dot(h_ref[...], w_ref[...],
                         preferred_element_type=F32).astype(o_ref.dtype)


def _inproj(x2, norm_w, shift, scale, w_bf, w_o, S):
    T, D = x2.shape
    N = w_bf.shape[1]
    tm, tn, ch = 1024, 2048, 128
    per_b = S // tm
    n_i, n_j = T // tm, N // tn
    slab = w_o.shape[0] // (n_i * n_j)
    return pl.pallas_call(
        functools.partial(_inproj_kernel, tm=tm, ch=ch),
        out_shape=(jax.ShapeDtypeStruct((T, N), BF16),
                   jax.ShapeDtypeStruct(w_o.shape, BF16)),
        grid=(n_i, n_j),
        in_specs=[pl.BlockSpec((tm, D), lambda i, j: (i, 0)),
                  pl.BlockSpec((1, D), lambda i, j: (0, 0)),
                  pl.BlockSpec((1, 1, D), lambda i, j: (i // per_b, 0, 0)),
                  pl.BlockSpec((1, 1, D), lambda i, j: (i // per_b, 0, 0)),
                  pl.BlockSpec((D, tn), lambda i, j: (0, j)),
                  pl.BlockSpec((slab, w_o.shape[1]), lambda i, j: (i * n_j + j, 0))],
        out_specs=(pl.BlockSpec((tm, tn), lambda i, j: (i, j)),
                   pl.BlockSpec((slab, w_o.shape[1]), lambda i, j: (i * n_j + j, 0))),
        scratch_shapes=[pltpu.VMEM((tm, D), BF16)],
        compiler_params=_params(("arbitrary", "arbitrary"), VMEM_LIMIT),
        name="inproj",
    )(x2, norm_w.reshape(1, D), shift[:, None, :], scale[:, None, :], w_bf, w_o)


def _attn_kernel(q_ref, k_ref, v_ref, qw_ref, kw_ref, o_ref, kn_ref, carry_ref, acc_ref,
                 *, S, tq, hg, scale):
    i = pl.program_id(2)
    d = HEAD_DIM

    def head_norm(x, w):
        parts = []
        for h in range(hg):
            xh = x[:, h * d:(h + 1) * d]
            inv = lax.rsqrt(jnp.mean(xh * xh, axis=-1, keepdims=True) + EPS)
            parts.append(xh * inv * w)
        return parts

    @pl.when(i == 0)
    def _():
        def body(c, _):
            r0 = pl.multiple_of(c * tq, tq)
            parts = head_norm(k_ref[0, pl.ds(r0, tq), :].astype(F32), kw_ref[...])
            for h in range(hg):
                kn_ref[pl.ds(r0, tq), h * d:(h + 1) * d] = parts[h].astype(BF16)
            return 0

        lax.fori_loop(0, S // tq, body, 0)

    qb = [(p * scale).astype(BF16) for p in head_norm(q_ref[0].astype(F32), qw_ref[...])]

    row = lax.broadcasted_iota(I32, (tq, tq), 0)
    col = lax.broadcasted_iota(I32, (tq, tq), 1)
    causal = col < row
    tri = (row > col).astype(BF16)

    def scores(h, rows, r0, nk, mask=None):
        kblk = kn_ref[pl.ds(r0, nk), h * d:(h + 1) * d]
        z = lax.dot_general(qb[h][rows], kblk, (((1,), (1,)), ((), ())),
                            preferred_element_type=F32)
        t = jnp.log(1.0 + jnp.exp(-jnp.abs(z)))
        lsn = jnp.minimum(-z, 0.0) - t
        lsp = lsn + z
        if mask is not None:
            lsn = jnp.where(mask, lsn, 0.0)
        later = jnp.dot(lsn.astype(BF16), tri[:nk, :nk], preferred_element_type=F32)
        return lsp + later, later[:, :1] + lsn[:, :1]

    def weighted(a, h, r0, nk):
        vblk = v_ref[0, pl.ds(r0, nk), h * d:(h + 1) * d]
        return jnp.dot(a.astype(BF16), vblk, preferred_element_type=F32)

    has_prev = i > 0
    rd = pl.multiple_of(i * tq, tq)
    rp = pl.multiple_of(jnp.maximum(i - 1, 0) * tq, tq)
    half = tq // 2
    top, bottom, every = slice(0, half), slice(half, tq), slice(0, tq)
    worst = None
    for h in range(hg):
        cols = slice(h * d, (h + 1) * d)
        log_t, sum_t = scores(h, top, rd, half, causal[top, top])
        log_b, sum_b = scores(h, bottom, rd, tq, causal[bottom, :])
        log_p, sum_p = scores(h, every, rp, tq)
        sum_d = jnp.concatenate([sum_t, sum_b], axis=0)
        a_t = jnp.where(causal[top, top], jnp.exp(log_t), 0.0)
        a_b = jnp.where(causal[bottom, :], jnp.exp(log_b), 0.0)
        a_p = jnp.where(has_prev, jnp.exp(log_p + sum_d), 0.0)
        from_prev = weighted(a_p, h, rp, tq)
        acc_ref[top, cols] = weighted(a_t, h, rd, half) + from_prev[top]
        acc_ref[bottom, cols] = weighted(a_b, h, rd, tq) + from_prev[bottom]
        carry = jnp.where(has_prev, sum_d + sum_p, sum_d)
        carry_ref[h] = carry
        m = jnp.max(carry)
        worst = m if worst is None else jnp.maximum(worst, m)

    def earlier(kb):
        r0 = pl.multiple_of(kb * tq, tq)
        worst = None
        for h in range(hg):
            cols = slice(h * d, (h + 1) * d)
            log_a, row_sum = scores(h, every, r0, tq)
            acc_ref[:, cols] += weighted(jnp.exp(log_a + carry_ref[h]), h, r0, tq)
            carry = carry_ref[h] + row_sum
            carry_ref[h] = carry
            m = jnp.max(carry)
            worst = m if worst is None else jnp.maximum(worst, m)
        return worst

    def cond(st):
        kb, m = st
        return jnp.logical_and(kb >= 0, m > -LOG_UNDERFLOW)

    def body(st):
        kb, _ = st
        return kb - 1, earlier(kb)

    lax.while_loop(cond, body, (i - 2, worst))
    o_ref[0] = acc_ref[...].astype(o_ref.dtype)


def _attention(proj3, q_norm_w, k_norm_w, n_heads):
    B, S, _ = proj3.shape
    d = HEAD_DIM
    tq = 256
    hg = 8
    G = n_heads // hg
    w = hg * d
    return pl.pallas_call(
        functools.partial(_attn_kernel, S=S, tq=tq, hg=hg, scale=1.0 / math.sqrt(d)),
        out_shape=jax.ShapeDtypeStruct((B, S, n_heads * d), BF16),
        grid=(B, G, S // tq),
        in_specs=[pl.BlockSpec((1, tq, w), lambda b, g, i: (b, i, g)),
                  pl.BlockSpec((1, S, w), lambda b, g, i: (b, 0, G + g)),
                  pl.BlockSpec((1, S, w), lambda b, g, i: (b, 0, 2 * G + g)),
                  pl.BlockSpec((1, d), lambda b, g, i: (0, 0)),
                  pl.BlockSpec((1, d), lambda b, g, i: (0, 0))],
        out_specs=pl.BlockSpec((1, tq, w), lambda b, g, i: (b, i, g)),
        scratch_shapes=[pltpu.VMEM((S, w), BF16),
                        pltpu.VMEM((hg, tq, 1), F32),
                        pltpu.VMEM((tq, w), F32)],
        compiler_params=_params(("arbitrary", "arbitrary", "arbitrary"), VMEM_LIMIT),
        name="stickbreak_attn",
    )(proj3, proj3, proj3, q_norm_w.reshape(1, d), k_norm_w.reshape(1, d))


def _pool_kernel(u_ref, w_ref, ps_ref, o_ref, *, S, ch, gd):
    halo = 16
    wgs = [w_ref[g].astype(BF16) for g in range(len(POOL_WINDOWS))]

    def body(c, _):
        r0 = pl.multiple_of(c * ch, ch)
        p0 = pl.multiple_of(jnp.maximum(r0 - halo, 0), halo)
        t = r0 + lax.broadcasted_iota(I32, (ch, 1), 0)
        for g, win in enumerate(POOL_WINDOWS):
            lo, hi = g * gd, (g + 1) * gd
            cur = u_ref[0, pl.ds(r0, ch), lo:hi].astype(F32)
            prev = u_ref[0, pl.ds(p0, halo), lo:hi].astype(F32)
            prev = jnp.where(c > 0, prev, 0.0)
            s = jnp.concatenate([prev, cur], axis=0)
            n = 1
            while n < win:
                s = s + pltpu.roll(s, n, 0)
                n *= 2
            s = s[halo:]
            cnt = jnp.minimum(t + 1, win).astype(F32)
            p = s / cnt - cur
            y = jnp.dot(p.astype(BF16), wgs[g], preferred_element_type=F32) * ps_ref[:, lo:hi]
            o_ref[0, pl.ds(r0, ch), lo:hi] = y.astype(o_ref.dtype)
        return 0

    lax.fori_loop(0, S // ch, body, 0)


def _pool(proj3, w_pool, pool_scale, pool_width):
    B, S, NP = proj3.shape
    G, gd, _ = w_pool.shape
    return pl.pallas_call(
        functools.partial(_pool_kernel, S=S, ch=256, gd=gd),
        out_shape=jax.ShapeDtypeStruct((B, S, pool_width), BF16),
        grid=(B,),
        in_specs=[pl.BlockSpec((1, S, pool_width), lambda b: (b, 0, NP // pool_width - 1)),
                  pl.BlockSpec((G, gd, gd), lambda b: (0, 0, 0)),
                  pl.BlockSpec((1, pool_width), lambda b: (0, 0))],
        out_specs=pl.BlockSpec((1, S, pool_width), lambda b: (b, 0, 0)),
        compiler_params=_params(("arbitrary",), VMEM_LIMIT),
        name="pool_mixer",
    )(proj3, w_pool, pool_scale.reshape(1, pool_width))


def _outproj_kernel(osb_ref, opool_ref, wo_ref, x_ref, g1_ref, sh_ref, sc_ref, nw_ref,
                    wr_ref, br_ref, x1_ref, h2p_ref, gates_ref, idx_ref, *, sbw, n_exp, sub):
    for r0 in range(0, x_ref.shape[0], sub):
        rows = slice(r0, r0 + sub)
        _outproj_rows(osb_ref.at[rows], opool_ref.at[rows], wo_ref, x_ref.at[rows], g1_ref,
                      sh_ref, sc_ref, nw_ref, wr_ref, br_ref, x1_ref.at[rows], h2p_ref.at[rows],
                      gates_ref.at[rows], idx_ref.at[:, rows], sbw=sbw, n_exp=n_exp)


def _outproj_rows(osb_ref, opool_ref, wo_ref, x_ref, g1_ref, sh_ref, sc_ref, nw_ref,
                  wr_ref, br_ref, x1_ref, h2p_ref, gates_ref, idx_ref, *, sbw, n_exp):
    tm, D = x_ref.shape
    mixed = (jnp.dot(osb_ref[...], wo_ref[:sbw, :], preferred_element_type=F32)
             + jnp.dot(opool_ref[...], wo_ref[sbw:, :], preferred_element_type=F32))
    x1 = x_ref[...] + g1_ref[0] * mixed
    x1_ref[...] = x1
    inv = lax.rsqrt(jnp.mean(x1 * x1, axis=-1, keepdims=True) + EPS)
    h2 = x1 * inv * (nw_ref[...] * (1.0 + sc_ref[0])) + sh_ref[0]
    hb = h2.astype(BF16)
    h2p_ref[...] = _pack_rows(h2)

    vals = lax.dot_general(wr_ref[...].astype(BF16), hb, (((1,), (1,)), ((), ())),
                           preferred_element_type=F32) + br_ref[...]
    expert = lax.broadcasted_iota(I32, (n_exp, tm), 0).astype(F32)
    tops, ids = [], []
    for _ in range(TOP_K):
        m = jnp.max(vals, axis=0, keepdims=True)
        first = jnp.min(jnp.where(vals == m, expert, float(n_exp)), axis=0, keepdims=True)
        tops.append(m)
        ids.append(first)
        vals = jnp.where(expert == first, -jnp.inf, vals)
    es = [jnp.exp(m - tops[0]) for m in tops]
    den = es[0]
    for e in es[1:]:
        den = den + e
    slot = lax.broadcasted_iota(I32, (LANES, tm), 0)
    gates = jnp.zeros((LANES, tm), F32)
    for k in range(TOP_K):
        gates = jnp.where(slot == k, es[k] / den, gates)
    gates_ref[...] = gates.T
    slot8 = lax.broadcasted_iota(I32, (SUBLANES, tm), 0)
    idx = jnp.zeros((SUBLANES, tm), F32)
    for k in range(TOP_K):
        idx = jnp.where(slot8 == k, ids[k], idx)
    idx_ref[...] = idx.astype(I32)


def _outproj(o_sb, o_pool, wo_bf, x2, gate1, shift2, scale2, norm2_w, w_router, b_router, S):
    T, D = x2.shape
    sbw = o_sb.shape[1]
    pw = o_pool.shape[1]
    n_exp = w_router.shape[1]
    tm, sub = 512, 256
    per_b = S // tm
    wr = w_router.T
    br = b_router.reshape(n_exp, 1)
    mod_spec = pl.BlockSpec((1, 1, D), lambda i: (i // per_b, 0, 0))
    return pl.pallas_call(
        functools.partial(_outproj_kernel, sbw=sbw, n_exp=n_exp, sub=sub),
        out_shape=(jax.ShapeDtypeStruct((T, D), F32),
                   jax.ShapeDtypeStruct((T, D // 2), U32),
                   jax.ShapeDtypeStruct((T, LANES), F32),
                   jax.ShapeDtypeStruct((SUBLANES, T), I32)),
        grid=(T // tm,),
        in_specs=[pl.BlockSpec((tm, sbw), lambda i: (i, 0)),
                  pl.BlockSpec((tm, pw), lambda i: (i, 0)),
                  pl.BlockSpec((sbw + pw, D), lambda i: (0, 0)),
                  pl.BlockSpec((tm, D), lambda i: (i, 0)),
                  mod_spec, mod_spec, mod_spec,
                  pl.BlockSpec((1, D), lambda i: (0, 0)),
                  pl.BlockSpec((n_exp, D), lambda i: (0, 0)),
                  pl.BlockSpec((n_exp, 1), lambda i: (0, 0))],
        out_specs=(pl.BlockSpec((tm, D), lambda i: (i, 0)),
                   pl.BlockSpec((tm, D // 2), lambda i: (i, 0)),
                   pl.BlockSpec((tm, LANES), lambda i: (i, 0)),
                   pl.BlockSpec((SUBLANES, tm), lambda i: (0, i))),
        compiler_params=_params(("arbitrary",), VMEM_LIMIT),
        name="outproj_router",
    )(o_sb, o_pool, wo_bf, x2, gate1[:, None, :], shift2[:, None, :], scale2[:, None, :],
      norm2_w.reshape(1, D), wr, br)


def _route_kernel(idx_ref, dest_ref, meta_ref, rank_ref, *, T, ch, n_exp):
    expert = lax.broadcasted_iota(I32, (n_exp, ch), 0)
    row = lax.broadcasted_iota(I32, (ch, ch), 0)
    col = lax.broadcasted_iota(I32, (ch, ch), 1)
    earlier = (row < col).astype(BF16)

    def chunk(c):
        return slice(c * ch, (c + 1) * ch)

    def count(c, cnt):
        member = expert == idx_ref[0:1, chunk(c)]
        for k in range(1, TOP_K):
            member = jnp.logical_or(member, expert == idx_ref[k:k + 1, chunk(c)])
        mf = jnp.where(member, 1.0, 0.0)
        rank_ref[:, chunk(c)] = jnp.dot(mf.astype(BF16), earlier,
                                        preferred_element_type=F32) + cnt
        return cnt + jnp.sum(mf, axis=1, keepdims=True)

    cnt = jnp.zeros((n_exp, 1), F32)
    for c in range(T // ch):
        cnt = count(c, cnt)
    padded = jnp.broadcast_to(jnp.ceil(cnt / EXPERT_ROWS) * EXPERT_ROWS, (n_exp, LANES))
    sub = lax.broadcasted_iota(I32, (n_exp, LANES), 0)
    ends = padded
    sh = 1
    while sh < n_exp:
        ends = ends + jnp.where(sub >= sh, pltpu.roll(ends, sh, 0), 0.0)
        sh *= 2
    starts_wide = ends - padded
    starts = starts_wide[:, :1]
    lane = lax.broadcasted_iota(I32, (n_exp, LANES), 1)
    meta = jnp.where(lane == 0, jnp.broadcast_to(cnt, (n_exp, LANES)),
                     jnp.where(lane == 1, starts_wide, 0.0))
    meta_ref[...] = meta.astype(I32)

    slot = lax.broadcasted_iota(I32, (SUBLANES, ch), 0)

    def place(c, _):
        val = rank_ref[:, chunk(c)] + starts
        out = jnp.zeros((SUBLANES, ch), F32)
        for k in range(TOP_K):
            mine = expert == idx_ref[k:k + 1, chunk(c)]
            d = jnp.sum(jnp.where(mine, val, 0.0), axis=0, keepdims=True)
            out = jnp.where(slot == k, d, out)
        dest_ref[:, chunk(c)] = out.astype(I32)
        return 0

    for c in range(T // ch):
        place(c, 0)


def _route(idx_t, n_exp):
    T = idx_t.shape[1]
    return pl.pallas_call(
        functools.partial(_route_kernel, T=T, ch=256, n_exp=n_exp),
        out_shape=(jax.ShapeDtypeStruct((SUBLANES, T), I32),
                   jax.ShapeDtypeStruct((n_exp, LANES), I32)),
        grid=(1,),
        in_specs=[pl.BlockSpec((SUBLANES, T), lambda i: (0, 0))],
        out_specs=(pl.BlockSpec((SUBLANES, T), lambda i: (0, 0)),
                   pl.BlockSpec((n_exp, LANES), lambda i: (0, 0))),
        scratch_shapes=[pltpu.VMEM((n_exp, T), F32)],
        compiler_params=_params(("arbitrary",), VMEM_LIMIT),
        name="route_ranks",
    )(idx_t)


def _dispatch_kernel(dest_ref, cnt_ref, start_ref, h_ref, x_ref, z_ref, sem, zsem,
                     *, tb, n_exp, n_blocks, n_tokens):
    s = pl.program_id(0)

    @pl.when(s == 0)
    def _():
        z_ref[...] = jnp.zeros_like(z_ref)
        _dispatch_zero_fill(cnt_ref, start_ref, x_ref, z_ref, zsem, n_exp, n_blocks)

    base = s * tb
    for t in range(tb):
        for k in range(TOP_K):
            d = dest_ref[base + (k * n_tokens + t)]
            pltpu.make_async_copy(h_ref.at[pl.ds(t, 1), :], x_ref.at[pl.ds(d, 1), :],
                                  sem).start(priority=k % 2)

    for _ in range(TOP_K):
        pltpu.make_async_copy(h_ref, x_ref.at[pl.ds(0, tb), :], sem).wait()


def _dispatch_zero_fill(cnt_ref, start_ref, x_ref, z_ref, zsem, n_exp, n_blocks):
    def block_fill(blk, wait):
        r0 = pl.multiple_of(blk * EXPERT_ROWS, EXPERT_ROWS)
        cp = pltpu.make_async_copy(z_ref, x_ref.at[pl.ds(r0, EXPERT_ROWS), :], zsem)
        if wait:
            cp.wait()
        else:
            cp.start()

    def zero_fill(e, wait):
        cnt = cnt_ref[e]

        @pl.when((cnt & (EXPERT_ROWS - 1)) != 0)
        def _():
            block_fill((start_ref[e] + cnt) // EXPERT_ROWS, wait)

        return 0

    used = (start_ref[n_exp - 1] + cnt_ref[n_exp - 1] + EXPERT_ROWS - 1) // EXPERT_ROWS

    def tail_fill(blk, wait):
        block_fill(blk, wait)
        return 0

    lax.fori_loop(0, n_exp, lambda e, _: zero_fill(e, False), 0)
    lax.fori_loop(used, n_blocks, lambda b, _: tail_fill(b, False), 0)
    lax.fori_loop(0, n_exp, lambda e, _: zero_fill(e, True), 0)
    lax.fori_loop(used, n_blocks, lambda b, _: tail_fill(b, True), 0)


def _dispatch(dest_flat, cnt, starts, h2p, n_rows):
    T, W = h2p.shape
    n_exp = cnt.shape[0]
    tb = 256
    return pl.pallas_call(
        functools.partial(_dispatch_kernel, tb=tb, n_exp=n_exp, n_blocks=n_rows // EXPERT_ROWS,
                          n_tokens=T),
        out_shape=jax.ShapeDtypeStruct((n_rows, W), U32),
        grid_spec=pltpu.PrefetchScalarGridSpec(
            num_scalar_prefetch=3,
            grid=(T // tb,),
            in_specs=[pl.BlockSpec((tb, W), lambda s, *_: (s, 0))],
            out_specs=pl.BlockSpec(memory_space=pl.ANY),
            scratch_shapes=[pltpu.VMEM((EXPERT_ROWS, W), U32),
                            pltpu.SemaphoreType.DMA, pltpu.SemaphoreType.DMA]),
        compiler_params=_params(("arbitrary",), VMEM_LIMIT),
        name="dispatch_rows",
    )(dest_flat, cnt, starts, h2p)


def _ffn1_kernel(ri_ref, ro_ref, e_ref, j_ref, first_ref, valid_ref, slot_ref, ne_ref, nj_ref, more_ref, pieces_ref,
                 x_ref, w_ref, b_ref, o_ref, stage, wbf, sem, *, F, tn):
    q = pl.program_id(0)

    def wcopy(e, j, slot, part):
        c0 = pl.multiple_of(part * F + j * tn, tn)
        return pltpu.make_async_copy(w_ref.at[e, :, pl.ds(c0, tn)], stage.at[slot, part],
                                     sem.at[slot])

    @pl.when(q == 0)
    def _():
        for part in range(2):
            wcopy(e_ref[0], j_ref[0], 0, part).start(priority=WEIGHT_DMA_PRIORITY)

    @pl.when(first_ref[q] == 1)
    def _():
        slot = slot_ref[q]
        for part in range(2):
            wcopy(e_ref[q], j_ref[q], slot, part).wait()

        @pl.when(more_ref[q] == 1)
        def _():
            for part in range(2):
                wcopy(ne_ref[q], nj_ref[q], 1 - slot, part).start(priority=WEIGHT_DMA_PRIORITY)

    def swiglu_block(w_gate, w_lin, rows):
        xb = _unpack_rows(x_ref[:rows, :])
        g = jnp.dot(xb, w_gate, preferred_element_type=F32) + b_ref[0, 0, 0]
        lin = jnp.dot(xb, w_lin, preferred_element_type=F32) + b_ref[0, 1, 0]
        g = jnp.minimum(g, SWIGLU_LIMIT)
        lin = jnp.clip(lin, -SWIGLU_LIMIT, SWIGLU_LIMIT)
        act = g / (1.0 + jnp.exp(-SWIGLU_ALPHA * g)) * (lin + 1.0)
        o_ref[:rows, :] = act.astype(o_ref.dtype)
        if rows < EXPERT_ROWS:
            o_ref[rows:, :] = jnp.zeros((EXPERT_ROWS - rows, o_ref.shape[1]), o_ref.dtype)

    @pl.when(first_ref[q] == 1)
    def _():
        slot = slot_ref[q]
        w16 = [stage[slot, part].astype(BF16) for part in range(2)]
        for part in range(2):
            wbf[part] = w16[part]
        swiglu_block(w16[0], w16[1], EXPERT_ROWS)

    later = jnp.logical_and(valid_ref[q] == 1, first_ref[q] == 0)

    for pieces in range(MIN_TAIL_PIECES, EXPERT_ROWS // TAIL_ROWS + 1):
        @pl.when(jnp.logical_and(later, pieces_ref[q] == pieces))
        def _(pieces=pieces):
            swiglu_block(wbf[0], wbf[1], pieces * TAIL_ROWS)

    @pl.when(valid_ref[q] == 0)
    def _():
        o_ref[...] = jnp.zeros_like(o_ref)


def _ffn2_kernel(ri_ref, ro_ref, e_ref, j_ref, first_ref, valid_ref, slot_ref, ne_ref, nj_ref, more_ref, pieces_ref,
                 a_ref, w_ref, b_ref, o_ref, stage, wbf, sem):
    q = pl.program_id(0)

    def wcopy(e, slot):
        return pltpu.make_async_copy(w_ref.at[e], stage.at[slot], sem.at[slot])

    @pl.when(q == 0)
    def _():
        wcopy(e_ref[0], 0).start(priority=WEIGHT_DMA_PRIORITY)

    @pl.when(first_ref[q] == 1)
    def _():
        slot = slot_ref[q]
        wcopy(e_ref[q], slot).wait()

        @pl.when(more_ref[q] == 1)
        def _():
            wcopy(ne_ref[q], 1 - slot).start(priority=WEIGHT_DMA_PRIORITY)

    def out_block(w, rows):
        y = jnp.dot(a_ref[:rows, :], w, preferred_element_type=F32) + b_ref[0]
        o_ref[:rows, :] = _pack_rows(y)
        if rows < EXPERT_ROWS:
            o_ref[rows:, :] = jnp.zeros((EXPERT_ROWS - rows, o_ref.shape[1]), o_ref.dtype)

    @pl.when(first_ref[q] == 1)
    def _():
        w16 = stage[slot_ref[q]].astype(BF16)
        wbf[...] = w16
        out_block(w16, EXPERT_ROWS)

    later = jnp.logical_and(valid_ref[q] == 1, first_ref[q] == 0)

    for pieces in range(MIN_TAIL_PIECES, EXPERT_ROWS // TAIL_ROWS + 1):
        @pl.when(jnp.logical_and(later, pieces_ref[q] == pieces))
        def _(pieces=pieces):
            out_block(wbf[...], pieces * TAIL_ROWS)

    @pl.when(valid_ref[q] == 0)
    def _():
        o_ref[...] = jnp.zeros_like(o_ref)


def _work_items(cnt, n_col_tiles, n_blocks):
    n_exp = cnt.shape[0]
    nblk = (cnt + EXPERT_ROWS - 1) // EXPERT_ROWS
    bstart = jnp.cumsum(nblk) - nblk
    gsize = jnp.repeat(nblk, n_col_tiles)
    gend = jnp.cumsum(gsize)
    n_groups = n_exp * n_col_tiles
    gid = jnp.arange(n_groups, dtype=I32)
    total = gend[-1]
    q = jnp.arange(n_blocks * n_col_tiles, dtype=I32)
    qc = jnp.minimum(q, total - 1)
    g = jnp.sum((gend[None, :] <= qc[:, None]).astype(I32), axis=1)
    nonempty = gsize > 0
    ordinal = jnp.cumsum(nonempty.astype(I32)) - 1
    nxt_incl = lax.cummin(jnp.where(nonempty, gid, n_groups), reverse=True)
    nxt = jnp.concatenate([nxt_incl[1:], jnp.full((1,), n_groups, I32)])
    more = nxt < n_groups
    nxt = jnp.minimum(nxt, n_groups - 1)
    per_group = jnp.stack([gend - gsize, gid // n_col_tiles, gid % n_col_tiles,
                           jnp.repeat(bstart, n_col_tiles), ordinal % 2,
                           nxt // n_col_tiles, nxt % n_col_tiles, more.astype(I32),
                           jnp.repeat(cnt, n_col_tiles)])
    pick = (g[None, :, None] == gid[None, None, :]).astype(I32)
    gstart, e, j, brow, slot, ne, nj, more, rows = jnp.sum(pick * per_group[:, None, :], axis=2)
    r = qc - gstart
    valid = q < total
    first = jnp.logical_and(valid, r == 0)
    pieces = jnp.clip((rows - r * EXPERT_ROWS + TAIL_ROWS - 1) // TAIL_ROWS, MIN_TAIL_PIECES,
                      EXPERT_ROWS // TAIL_ROWS)
    over = q - total
    row_in = brow + r
    row_out = jnp.where(valid, row_in, jnp.sum(nblk) + over // n_col_tiles)
    col_out = jnp.where(valid, j, over % n_col_tiles)
    as_i32 = lambda a: a.astype(I32)
    return tuple(map(as_i32, (row_in, row_out, e, col_out, first, valid, slot, ne, nj, more,
                              pieces)))


def _ffn1(items, x_pad, w_exp_in, b_exp_in):
    P, W = x_pad.shape
    n_exp, D, F2 = w_exp_in.shape
    F = F2 // 2
    tn = 1024
    nj = F // tn
    n_items = items[0].shape[0]
    bias = b_exp_in.reshape(n_exp, 2, nj, 1, tn)
    return pl.pallas_call(
        functools.partial(_ffn1_kernel, F=F, tn=tn),
        out_shape=jax.ShapeDtypeStruct((P, F), BF16),
        grid_spec=pltpu.PrefetchScalarGridSpec(
            num_scalar_prefetch=11,
            grid=(n_items,),
            in_specs=[pl.BlockSpec((EXPERT_ROWS, W), lambda q, ri, *_: (ri[q], 0)),
                      pl.BlockSpec(memory_space=pl.ANY),
                      pl.BlockSpec((1, 2, 1, 1, tn),
                                   lambda q, ri, ro, e, j, *_: (e[q], 0, j[q], 0, 0))],
            out_specs=pl.BlockSpec((EXPERT_ROWS, tn),
                                   lambda q, ri, ro, e, j, *_: (ro[q], j[q])),
            scratch_shapes=[pltpu.VMEM((2, 2, D, tn), F32),
                            pltpu.VMEM((2, D, tn), BF16),
                            pltpu.SemaphoreType.DMA((2,))]),
        compiler_params=_params(("arbitrary",), VMEM_LIMIT),
        name="expert_in_swiglu",
    )(*items, x_pad, w_exp_in, bias)


def _ffn2(items, act, w_exp_out, b_exp_out):
    P, F = act.shape
    n_exp, _, D = w_exp_out.shape
    n_items = items[0].shape[0]
    return pl.pallas_call(
        _ffn2_kernel,
        out_shape=jax.ShapeDtypeStruct((P, D // 2), U32),
        grid_spec=pltpu.PrefetchScalarGridSpec(
            num_scalar_prefetch=11,
            grid=(n_items,),
            in_specs=[pl.BlockSpec((EXPERT_ROWS, F), lambda q, ri, *_: (ri[q], 0)),
                      pl.BlockSpec(memory_space=pl.ANY),
                      pl.BlockSpec((1, 1, D), lambda q, ri, ro, e, *_: (e[q], 0, 0))],
            out_specs=pl.BlockSpec((EXPERT_ROWS, D // 2), lambda q, ri, ro, *_: (ro[q], 0)),
            scratch_shapes=[pltpu.VMEM((2, F, D), F32),
                            pltpu.VMEM((F, D), BF16),
                            pltpu.SemaphoreType.DMA((2,))]),
        compiler_params=_params(("arbitrary",), VMEM_LIMIT),
        name="expert_out",
    )(*items, act, w_exp_out, b_exp_out[:, None, :])


def _combine_kernel(dest_ref, y_ref, gates_ref, x1_ref, g2_ref, o_ref, buf, sem,
                    *, tm, n_tokens):
    s = pl.program_id(0)
    ns = pl.num_programs(0)

    def gather(step, slot, unrolled):
        base = step * tm

        def row(t, k):
            d = dest_ref[base + (k * n_tokens + t)]
            pltpu.make_async_copy(y_ref.at[pl.ds(d, 1), :], buf.at[slot, k, pl.ds(t, 1), :],
                                  sem.at[slot]).start(priority=k % 2)

        if unrolled:
            for t in range(tm):
                for k in range(TOP_K):
                    row(t, k)
        else:
            def issue(t, _):
                for k in range(TOP_K):
                    row(t, k)
                return 0

            lax.fori_loop(0, tm, issue, 0)

    @pl.when(s == 0)
    def _():
        gather(0, 0, False)

    for par in range(2):
        @pl.when(jnp.logical_and(s + 1 < ns, (s + 1) % 2 == par))
        def _(par=par):
            gather(s + 1, par, True)

    slot = s % 2
    for k in range(TOP_K):
        pltpu.make_async_copy(y_ref.at[pl.ds(0, tm), :], buf.at[slot, k], sem.at[slot]).wait()
    gates = gates_ref[...]
    y_hi = y_lo = None
    for k in range(TOP_K):
        hi, lo = _unpack_halves(buf[slot, k])
        g = gates[:, k:k + 1]
        y_hi = g * hi if y_hi is None else y_hi + g * hi
        y_lo = g * lo if y_lo is None else y_lo + g * lo
    y = jnp.concatenate([y_hi, y_lo], axis=1)
    o_ref[...] = x1_ref[...] + g2_ref[0] * y


def _combine(dest_flat, y_pad, gates_wide, x1, gate2, S):
    T, D = x1.shape
    tm = 256
    per_b = S // tm
    return pl.pallas_call(
        functools.partial(_combine_kernel, tm=tm, n_tokens=T),
        out_shape=jax.ShapeDtypeStruct((T, D), F32),
        grid_spec=pltpu.PrefetchScalarGridSpec(
            num_scalar_prefetch=1,
            grid=(T // tm,),
            in_specs=[pl.BlockSpec(memory_space=pl.ANY),
                      pl.BlockSpec((tm, LANES), lambda i, d: (i, 0)),
                      pl.BlockSpec((tm, D), lambda i, d: (i, 0)),
                      pl.BlockSpec((1, 1, D), lambda i, d: (i // per_b, 0, 0))],
            out_specs=pl.BlockSpec((tm, D), lambda i, d: (i, 0)),
            scratch_shapes=[pltpu.VMEM((2, TOP_K, tm, D // 2), U32),
                            pltpu.SemaphoreType.DMA((2,))]),
        compiler_params=_params(("arbitrary",), VMEM_LIMIT),
        name="combine_rows",
    )(dest_flat, y_pad, gates_wide, x1, gate2[:, None, :])


def kernel(x, c, norm1_w, norm2_w, w_ada, b_ada, w_in, q_norm_w, k_norm_w, w_pool, pool_scale,
           w_o, w_router, b_router, w_exp_in, b_exp_in, w_exp_out, b_exp_out):
    B, S, D = x.shape
    T = B * S
    depth = w_ada.shape[0]
    n_exp = w_router.shape[-1]
    pool_width = pool_scale.shape[-1]
    sb_width = w_o.shape[1] - pool_width
    n_heads = sb_width // HEAD_DIM
    n_blocks = (T * TOP_K + n_exp * (EXPERT_ROWS - 1)) // EXPERT_ROWS
    n_rows = n_blocks * EXPERT_ROWS

    x2 = x.reshape(T, D)
    for l in range(depth):
        mod = _adaln(c, w_ada[l], b_ada[l])
        shift1, scale1, gate1, shift2, scale2, gate2 = jnp.split(mod, 6, axis=-1)

        proj, wo_bf = _inproj(x2, norm1_w[l], shift1, scale1, w_in[l].astype(BF16), w_o[l], S)
        proj3 = proj.reshape(B, S, -1)
        o_sb = _attention(proj3, q_norm_w[l], k_norm_w[l], n_heads)
        o_pool = _pool(proj3, w_pool[l], pool_scale[l], pool_width)
        x1, h2p, gates_wide, idx_t = _outproj(
            o_sb.reshape(T, sb_width), o_pool.reshape(T, pool_width), wo_bf,
            x2, gate1, shift2, scale2, norm2_w[l], w_router[l], b_router[l], S)

        dest_t, meta = _route(idx_t, n_exp)
        cnt = meta[:, 0]
        starts = meta[:, 1]
        dest_flat = dest_t[:TOP_K].reshape(TOP_K * T)
        x_pad = _dispatch(dest_flat, cnt, starts, h2p, n_rows)

        F = w_exp_out.shape[2]
        act = _ffn1(_work_items(cnt, F // 1024, n_blocks), x_pad, w_exp_in[l], b_exp_in[l])
        y_pad = _ffn2(_work_items(cnt, 1, n_blocks), act, w_exp_out[l], b_exp_out[l])
        x2 = _combine(dest_flat, y_pad, gates_wide, x1, gate2, S)
    return x2.reshape(B, S, D)
```

```python
import functools
import math

import jax
import jax.numpy as jnp
from jax import lax
from jax.experimental import pallas as pl
from jax.experimental.pallas import tpu as pltpu

F32 = jnp.float32
BF16 = jnp.bfloat16
I32 = jnp.int32
U32 = jnp.uint32

EPS = 1e-6
HEAD_DIM = 128
POOL_WINDOWS = (2, 4, 8, 16)
TOP_K = 4
SWIGLU_ALPHA = 1.702
SWIGLU_LIMIT = 7.0

LANES = 128
SUBLANES = 8
EXPERT_ROWS = 256
TAIL_ROWS = 64
MIN_TAIL_PIECES = 2
LOG_UNDERFLOW = 104.0
VMEM_LIMIT = 56 * 1024 * 1024
WEIGHT_DMA_PRIORITY = 1


def _params(sem=None, vmem=None):
    return pltpu.CompilerParams(dimension_semantics=sem, vmem_limit_bytes=vmem)


_HIGH_HALF = 0xFFFF0000


def _pack_rows(v):
    bits = lax.bitcast_convert_type(v.astype(BF16).astype(F32), U32)
    half = v.shape[1] // 2
    return (bits[:, :half] & jnp.uint32(_HIGH_HALF)) | (bits[:, half:] >> 16)


def _unpack_halves(p):
    hi = lax.bitcast_convert_type(p & jnp.uint32(_HIGH_HALF), F32)
    lo = lax.bitcast_convert_type(p << 16, F32)
    return hi, lo


def _unpack_rows(p):
    hi, lo = _unpack_halves(p)
    return jnp.concatenate([hi.astype(BF16), lo.astype(BF16)], axis=1)


def _adaln_kernel(c_ref, w_ref, b_ref, o_ref):
    c = c_ref[...]
    ca = c / (1.0 + jnp.exp(-c))
    o_ref[...] = jnp.dot(ca.astype(BF16), w_ref[...].astype(BF16),
                         preferred_element_type=F32) + b_ref[...]


def _adaln(c, w_ada, b_ada):
    B, D = c.shape
    N = w_ada.shape[1]
    rows = 8
    tn = 1024
    cp = jnp.zeros((rows, D), F32).at[:B].set(c)
    out = pl.pallas_call(
        _adaln_kernel,
        out_shape=jax.ShapeDtypeStruct((rows, N), F32),
        grid=(N // tn,),
        in_specs=[pl.BlockSpec((rows, D), lambda j: (0, 0)),
                  pl.BlockSpec((D, tn), lambda j: (0, j)),
                  pl.BlockSpec((1, tn), lambda j: (0, j))],
        out_specs=pl.BlockSpec((rows, tn), lambda j: (0, j)),
        compiler_params=_params(("arbitrary",), VMEM_LIMIT),
        name="adaln",
    )(cp, w_ada, b_ada.reshape(1, N))
    return out[:B]


def _inproj_kernel(x_ref, nw_ref, sh_ref, sc_ref, w_ref, wo_ref, o_ref, wo16_ref, h_ref,
                   *, tm, ch):
    wo16_ref[...] = wo_ref[...].astype(BF16)

    @pl.when(pl.program_id(1) == 0)
    def _():
        mul = nw_ref[...] * (1.0 + sc_ref[0])
        add = sh_ref[0]

        def body(c, _):
            r0 = pl.multiple_of(c * ch, ch)
            x = x_ref[pl.ds(r0, ch), :]
            inv = lax.rsqrt(jnp.mean(x * x, axis=-1, keepdims=True) + EPS)
            h_ref[pl.ds(r0, ch), :] = (x * inv * mul + add).astype(BF16)
            return 0

        lax.fori_loop(0, tm // ch, body, 0)

    o_ref[...] = jnp.dot(h_ref[...], w_ref[...],
                         preferred_element_type=F32).astype(o_ref.dtype)


def _inproj(x2, norm_w, shift, scale, w_bf, w_o, S):
    T, D = x2.shape
    N = w_bf.shape[1]
    tm, tn, ch = 1024, 2048, 128
    per_b = S // tm
    n_i, n_j = T // tm, N // tn
    slab = w_o.shape[0] // (n_i * n_j)
    return pl.pallas_call(
        functools.partial(_inproj_kernel, tm=tm, ch=ch),
        out_shape=(jax.ShapeDtypeStruct((T, N), BF16),
                   jax.ShapeDtypeStruct(w_o.shape, BF16)),
        grid=(n_i, n_j),
        in_specs=[pl.BlockSpec((tm, D), lambda i, j: (i, 0)),
                  pl.BlockSpec((1, D), lambda i, j: (0, 0)),
                  pl.BlockSpec((1, 1, D), lambda i, j: (i // per_b, 0, 0)),
                  pl.BlockSpec((1, 1, D), lambda i, j: (i // per_b, 0, 0)),
                  pl.BlockSpec((D, tn), lambda i, j: (0, j)),
                  pl.BlockSpec((slab, w_o.shape[1]), lambda i, j: (i * n_j + j, 0))],
        out_specs=(pl.BlockSpec((tm, tn), lambda i, j: (i, j)),
                   pl.BlockSpec((slab, w_o.shape[1]), lambda i, j: (i * n_j + j, 0))),
        scratch_shapes=[pltpu.VMEM((tm, D), BF16)],
        compiler_params=_params(("arbitrary", "arbitrary"), VMEM_LIMIT),
        name="inproj",
    )(x2, norm_w.reshape(1, D), shift[:, None, :], scale[:, None, :], w_bf, w_o)


def _attn_kernel(q_ref, k_ref, v_ref, qw_ref, kw_ref, o_ref, kn_ref, carry_ref, acc_ref,
                 *, S, tq, hg, scale):
    i = pl.program_id(2)
    d = HEAD_DIM

    def head_norm(x, w):
        parts = []
        for h in range(hg):
            xh = x[:, h * d:(h + 1) * d]
            inv = lax.rsqrt(jnp.mean(xh * xh, axis=-1, keepdims=True) + EPS)
            parts.append(xh * inv * w)
        return parts

    @pl.when(i == 0)
    def _():
        def body(c, _):
            r0 = pl.multiple_of(c * tq, tq)
            parts = head_norm(k_ref[0, pl.ds(r0, tq), :].astype(F32), kw_ref[...])
            for h in range(hg):
                kn_ref[pl.ds(r0, tq), h * d:(h + 1) * d] = parts[h].astype(BF16)
            return 0

        lax.fori_loop(0, S // tq, body, 0)

    qb = [(p * scale).astype(BF16) for p in head_norm(q_ref[0].astype(F32), qw_ref[...])]

    row = lax.broadcasted_iota(I32, (tq, tq), 0)
    col = lax.broadcasted_iota(I32, (tq, tq), 1)
    causal = col < row
    tri = (row > col).astype(BF16)

    def scores(h, rows, r0, nk, mask=None):
        kblk = kn_ref[pl.ds(r0, nk), h * d:(h + 1) * d]
        z = lax.dot_general(qb[h][rows], kblk, (((1,), (1,)), ((), ())),
                            preferred_element_type=F32)
        t = jnp.log(1.0 + jnp.exp(-jnp.abs(z)))
        lsn = jnp.minimum(-z, 0.0) - t
        lsp = lsn + z
        if mask is not None:
            lsn = jnp.where(mask, lsn, 0.0)
        later = jnp.dot(lsn.astype(BF16), tri[:nk, :nk], preferred_element_type=F32)
        return lsp + later, later[:, :1] + lsn[:, :1]

    def weighted(a, h, r0, nk):
        vblk = v_ref[0, pl.ds(r0, nk), h * d:(h + 1) * d]
        return jnp.dot(a.astype(BF16), vblk, preferred_element_type=F32)

    has_prev = i > 0
    rd = pl.multiple_of(i * tq, tq)
    rp = pl.multiple_of(jnp.maximum(i - 1, 0) * tq, tq)
    half = tq // 2
    top, bottom, every = slice(0, half), slice(half, tq), slice(0, tq)
    worst = None
    for h in range(hg):
        cols = slice(h * d, (h + 1) * d)
        log_t, sum_t = scores(h, top, rd, half, causal[top, top])
        log_b, sum_b = scores(h, bottom, rd, tq, causal[bottom, :])
        log_p, sum_p = scores(h, every, rp, tq)
        sum_d = jnp.concatenate([sum_t, sum_b], axis=0)
        a_t = jnp.where(causal[top, top], jnp.exp(log_t), 0.0)
        a_b = jnp.where(causal[bottom, :], jnp.exp(log_b), 0.0)
        a_p = jnp.where(has_prev, jnp.exp(log_p + sum_d), 0.0)
        from_prev = weighted(a_p, h, rp, tq)
        acc_ref[top, cols] = weighted(a_t, h, rd, half) + from_prev[top]
        acc_ref[bottom, cols] = weighted(a_b, h, rd, tq) + from_prev[bottom]
        carry = jnp.where(has_prev, sum_d + sum_p, sum_d)
        carry_ref[h] = carry
        m = jnp.max(carry)
        worst = m if worst is None else jnp.maximum(worst, m)

    def earlier(kb):
        r0 = pl.multiple_of(kb * tq, tq)
        worst = None
        for h in range(hg):
            cols = slice(h * d, (h + 1) * d)
            log_a, row_sum = scores(h, every, r0, tq)
            acc_ref[:, cols] += weighted(jnp.exp(log_a + carry_ref[h]), h, r0, tq)
            carry = carry_ref[h] + row_sum
            carry_ref[h] = carry
            m = jnp.max(carry)
            worst = m if worst is None else jnp.maximum(worst, m)
        return worst

    def cond(st):
        kb, m = st
        return jnp.logical_and(kb >= 0, m > -LOG_UNDERFLOW)

    def body(st):
        kb, _ = st
        return kb - 1, earlier(kb)

    lax.while_loop(cond, body, (i - 2, worst))
    o_ref[0] = acc_ref[...].astype(o_ref.dtype)


def _attention(proj3, q_norm_w, k_norm_w, n_heads):
    B, S, _ = proj3.shape
    d = HEAD_DIM
    tq = 256
    hg = 8
    G = n_heads // hg
    w = hg * d
    return pl.pallas_call(
        functools.partial(_attn_kernel, S=S, tq=tq, hg=hg, scale=1.0 / math.sqrt(d)),
        out_shape=jax.ShapeDtypeStruct((B, S, n_heads * d), BF16),
        grid=(B, G, S // tq),
        in_specs=[pl.BlockSpec((1, tq, w), lambda b, g, i: (b, i, g)),
                  pl.BlockSpec((1, S, w), lambda b, g, i: (b, 0, G + g)),
                  pl.BlockSpec((1, S, w), lambda b, g, i: (b, 0, 2 * G + g)),
                  pl.BlockSpec((1, d), lambda b, g, i: (0, 0)),
                  pl.BlockSpec((1, d), lambda b, g, i: (0, 0))],
        out_specs=pl.BlockSpec((1, tq, w), lambda b, g, i: (b, i, g)),
        scratch_shapes=[pltpu.VMEM((S, w), BF16),
                        pltpu.VMEM((hg, tq, 1), F32),
                        pltpu.VMEM((tq, w), F32)],
        compiler_params=_params(("arbitrary", "arbitrary", "arbitrary"), VMEM_LIMIT),
        name="stickbreak_attn",
    )(proj3, proj3, proj3, q_norm_w.reshape(1, d), k_norm_w.reshape(1, d))


def _pool_kernel(u_ref, w_ref, ps_ref, o_ref, *, S, ch, gd):
    halo = 16
    wgs = [w_ref[g].astype(BF16) for g in range(len(POOL_WINDOWS))]

    def body(c, _):
        r0 = pl.multiple_of(c * ch, ch)
        p0 = pl.multiple_of(jnp.maximum(r0 - halo, 0), halo)
        t = r0 + lax.broadcasted_iota(I32, (ch, 1), 0)
        for g, win in enumerate(POOL_WINDOWS):
            lo, hi = g * gd, (g + 1) * gd
            cur = u_ref[0, pl.ds(r0, ch), lo:hi].astype(F32)
            prev = u_ref[0, pl.ds(p0, halo), lo:hi].astype(F32)
            prev = jnp.where(c > 0, prev, 0.0)
            s = jnp.concatenate([prev, cur], axis=0)
            n = 1
            while n < win:
                s = s + pltpu.roll(s, n, 0)
                n *= 2
            s = s[halo:]
            cnt = jnp.minimum(t + 1, win).astype(F32)
            p = s / cnt - cur
            y = jnp.dot(p.astype(BF16), wgs[g], preferred_element_type=F32) * ps_ref[:, lo:hi]
            o_ref[0, pl.ds(r0, ch), lo:hi] = y.astype(o_ref.dtype)
        return 0

    lax.fori_loop(0, S // ch, body, 0)


def _pool(proj3, w_pool, pool_scale, pool_width):
    B, S, NP = proj3.shape
    G, gd, _ = w_pool.shape
    return pl.pallas_call(
        functools.partial(_pool_kernel, S=S, ch=256, gd=gd),
        out_shape=jax.ShapeDtypeStruct((B, S, pool_width), BF16),
        grid=(B,),
        in_specs=[pl.BlockSpec((1, S, pool_width), lambda b: (b, 0, NP // pool_width - 1)),
                  pl.BlockSpec((G, gd, gd), lambda b: (0, 0, 0)),
                  pl.BlockSpec((1, pool_width), lambda b: (0, 0))],
        out_specs=pl.BlockSpec((1, S, pool_width), lambda b: (b, 0, 0)),
        compiler_params=_params(("arbitrary",), VMEM_LIMIT),
        name="pool_mixer",
    )(proj3, w_pool, pool_scale.reshape(1, pool_width))


def _outproj_kernel(osb_ref, opool_ref, wo_ref, x_ref, g1_ref, sh_ref, sc_ref, nw_ref,
                    wr_ref, br_ref, x1_ref, h2p_ref, gates_ref, idx_ref, *, sbw, n_exp, sub):
    for r0 in range(0, x_ref.shape[0], sub):
        rows = slice(r0, r0 + sub)
        _outproj_rows(osb_ref.at[rows], opool_ref.at[rows], wo_ref, x_ref.at[rows], g1_ref,
                      sh_ref, sc_ref, nw_ref, wr_ref, br_ref, x1_ref.at[rows], h2p_ref.at[rows],
                      gates_ref.at[rows], idx_ref.at[:, rows], sbw=sbw, n_exp=n_exp)


def _outproj_rows(osb_ref, opool_ref, wo_ref, x_ref, g1_ref, sh_ref, sc_ref, nw_ref,
                  wr_ref, br_ref, x1_ref, h2p_ref, gates_ref, idx_ref, *, sbw, n_exp):
    tm, D = x_ref.shape
    mixed = (jnp.dot(osb_ref[...], wo_ref[:sbw, :], preferred_element_type=F32)
             + jnp.dot(opool_ref[...], wo_ref[sbw:, :], preferred_element_type=F32))
    x1 = x_ref[...] + g1_ref[0] * mixed
    x1_ref[...] = x1
    inv = lax.rsqrt(jnp.mean(x1 * x1, axis=-1, keepdims=True) + EPS)
    h2 = x1 * inv * (nw_ref[...] * (1.0 + sc_ref[0])) + sh_ref[0]
    hb = h2.astype(BF16)
    h2p_ref[...] = _pack_rows(h2)

    vals = lax.dot_general(wr_ref[...].astype(BF16), hb, (((1,), (1,)), ((), ())),
                           preferred_element_type=F32) + br_ref[...]
    expert = lax.broadcasted_iota(I32, (n_exp, tm), 0).astype(F32)
    tops, ids = [], []
    for _ in range(TOP_K):
        m = jnp.max(vals, axis=0, keepdims=True)
        first = jnp.min(jnp.where(vals == m, expert, float(n_exp)), axis=0, keepdims=True)
        tops.append(m)
        ids.append(first)
        vals = jnp.where(expert == first, -jnp.inf, vals)
    es = [jnp.exp(m - tops[0]) for m in tops]
    den = es[0]
    for e in es[1:]:
        den = den + e
    slot = lax.broadcasted_iota(I32, (LANES, tm), 0)
    gates = jnp.zeros((LANES, tm), F32)
    for k in range(TOP_K):
        gates = jnp.where(slot == k, es[k] / den, gates)
    gates_ref[...] = gates.T
    slot8 = lax.broadcasted_iota(I32, (SUBLANES, tm), 0)
    idx = jnp.zeros((SUBLANES, tm), F32)
    for k in range(TOP_K):
        idx = jnp.where(slot8 == k, ids[k], idx)
    idx_ref[...] = idx.astype(I32)


def _outproj(o_sb, o_pool, wo_bf, x2, gate1, shift2, scale2, norm2_w, w_router, b_router, S):
    T, D = x2.shape
    sbw = o_sb.shape[1]
    pw = o_pool.shape[1]
    n_exp = w_router.shape[1]
    tm, sub = 512, 256
    per_b = S // tm
    wr = w_router.T
    br = b_router.reshape(n_exp, 1)
    mod_spec = pl.BlockSpec((1, 1, D), lambda i: (i // per_b, 0, 0))
    return pl.pallas_call(
        functools.partial(_outproj_kernel, sbw=sbw, n_exp=n_exp, sub=sub),
        out_shape=(jax.ShapeDtypeStruct((T, D), F32),
                   jax.ShapeDtypeStruct((T, D // 2), U32),
                   jax.ShapeDtypeStruct((T, LANES), F32),
                   jax.ShapeDtypeStruct((SUBLANES, T), I32)),
        grid=(T // tm,),
        in_specs=[pl.BlockSpec((tm, sbw), lambda i: (i, 0)),
                  pl.BlockSpec((tm, pw), lambda i: (i, 0)),
                  pl.BlockSpec((sbw + pw, D), lambda i: (0, 0)),
                  pl.BlockSpec((tm, D), lambda i: (i, 0)),
                  mod_spec, mod_spec, mod_spec,
                  pl.BlockSpec((1, D), lambda i: (0, 0)),
                  pl.BlockSpec((n_exp, D), lambda i: (0, 0)),
                  pl.BlockSpec((n_exp, 1), lambda i: (0, 0))],
        out_specs=(pl.BlockSpec((tm, D), lambda i: (i, 0)),
                   pl.BlockSpec((tm, D // 2), lambda i: (i, 0)),
                   pl.BlockSpec((tm, LANES), lambda i: (i, 0)),
                   pl.BlockSpec((SUBLANES, tm), lambda i: (0, i))),
        compiler_params=_params(("arbitrary",), VMEM_LIMIT),
        name="outproj_router",
    )(o_sb, o_pool, wo_bf, x2, gate1[:, None, :], shift2[:, None, :], scale2[:, None, :],
      norm2_w.reshape(1, D), wr, br)


def _route_kernel(idx_ref, dest_ref, meta_ref, rank_ref, *, T, ch, n_exp):
    expert = lax.broadcasted_iota(I32, (n_exp, ch), 0)
    row = lax.broadcasted_iota(I32, (ch, ch), 0)
    col = lax.broadcasted_iota(I32, (ch, ch), 1)
    earlier = (row < col).astype(BF16)

    def chunk(c):
        return slice(c * ch, (c + 1) * ch)

    def count(c, cnt):
        member = expert == idx_ref[0:1, chunk(c)]
        for k in range(1, TOP_K):
            member = jnp.logical_or(member, expert == idx_ref[k:k + 1, chunk(c)])
        mf = jnp.where(member, 1.0, 0.0)
        rank_ref[:, chunk(c)] = jnp.dot(mf.astype(BF16), earlier,
                                        preferred_element_type=F32) + cnt
        return cnt + jnp.sum(mf, axis=1, keepdims=True)

    cnt = jnp.zeros((n_exp, 1), F32)
    for c in range(T // ch):
        cnt = count(c, cnt)
    padded = jnp.broadcast_to(jnp.ceil(cnt / EXPERT_ROWS) * EXPERT_ROWS, (n_exp, LANES))
    sub = lax.broadcasted_iota(I32, (n_exp, LANES), 0)
    ends = padded
    sh = 1
    while sh < n_exp:
        ends = ends + jnp.where(sub >= sh, pltpu.roll(ends, sh, 0), 0.0)
        sh *= 2
    starts_wide = ends - padded
    starts = starts_wide[:, :1]
    lane = lax.broadcasted_iota(I32, (n_exp, LANES), 1)
    meta = jnp.where(lane == 0, jnp.broadcast_to(cnt, (n_exp, LANES)),
                     jnp.where(lane == 1, starts_wide, 0.0))
    meta_ref[...] = meta.astype(I32)

    slot = lax.broadcasted_iota(I32, (SUBLANES, ch), 0)

    def place(c, _):
        val = rank_ref[:, chunk(c)] + starts
        out = jnp.zeros((SUBLANES, ch), F32)
        for k in range(TOP_K):
            mine = expert == idx_ref[k:k + 1, chunk(c)]
            d = jnp.sum(jnp.where(mine, val, 0.0), axis=0, keepdims=True)
            out = jnp.where(slot == k, d, out)
        dest_ref[:, chunk(c)] = out.astype(I32)
        return 0

    for c in range(T // ch):
        place(c, 0)


def _route(idx_t, n_exp):
    T = idx_t.shape[1]
    return pl.pallas_call(
        functools.partial(_route_kernel, T=T, ch=256, n_exp=n_exp),
        out_shape=(jax.ShapeDtypeStruct((SUBLANES, T), I32),
                   jax.ShapeDtypeStruct((n_exp, LANES), I32)),
        grid=(1,),
        in_specs=[pl.BlockSpec((SUBLANES, T), lambda i: (0, 0))],
        out_specs=(pl.BlockSpec((SUBLANES, T), lambda i: (0, 0)),
                   pl.BlockSpec((n_exp, LANES), lambda i: (0, 0))),
        scratch_shapes=[pltpu.VMEM((n_exp, T), F32)],
        compiler_params=_params(("arbitrary",), VMEM_LIMIT),
        name="route_ranks",
    )(idx_t)


def _dispatch_kernel(dest_ref, cnt_ref, start_ref, h_ref, x_ref, z_ref, sem, zsem,
                     *, tb, n_exp, n_blocks, n_tokens):
    s = pl.program_id(0)

    @pl.when(s == 0)
    def _():
        z_ref[...] = jnp.zeros_like(z_ref)
        _dispatch_zero_fill(cnt_ref, start_ref, x_ref, z_ref, zsem, n_exp, n_blocks)

    base = s * tb
    for t in range(tb):
        for k in range(TOP_K):
            d = dest_ref[base + (k * n_tokens + t)]
            pltpu.make_async_copy(h_ref.at[pl.ds(t, 1), :], x_ref.at[pl.ds(d, 1), :],
                                  sem).start(priority=k % 2)

    for _ in range(TOP_K):
        pltpu.make_async_copy(h_ref, x_ref.at[pl.ds(0, tb), :], sem).wait()


def _dispatch_zero_fill(cnt_ref, start_ref, x_ref, z_ref, zsem, n_exp, n_blocks):
    def block_fill(blk, wait):
        r0 = pl.multiple_of(blk * EXPERT_ROWS, EXPERT_ROWS)
        cp = pltpu.make_async_copy(z_ref, x_ref.at[pl.ds(r0, EXPERT_ROWS), :], zsem)
        if wait:
            cp.wait()
        else:
            cp.start()

    def zero_fill(e, wait):
        cnt = cnt_ref[e]

        @pl.when((cnt & (EXPERT_ROWS - 1)) != 0)
        def _():
            block_fill((start_ref[e] + cnt) // EXPERT_ROWS, wait)

        return 0

    used = (start_ref[n_exp - 1] + cnt_ref[n_exp - 1] + EXPERT_ROWS - 1) // EXPERT_ROWS

    def tail_fill(blk, wait):
        block_fill(blk, wait)
        return 0

    lax.fori_loop(0, n_exp, lambda e, _: zero_fill(e, False), 0)
    lax.fori_loop(used, n_blocks, lambda b, _: tail_fill(b, False), 0)
    lax.fori_loop(0, n_exp, lambda e, _: zero_fill(e, True), 0)
    lax.fori_loop(used, n_blocks, lambda b, _: tail_fill(b, True), 0)


def _dispatch(dest_flat, cnt, starts, h2p, n_rows):
    T, W = h2p.shape
    n_exp = cnt.shape[0]
    tb = 512
    return pl.pallas_call(
        functools.partial(_dispatch_kernel, tb=tb, n_exp=n_exp, n_blocks=n_rows // EXPERT_ROWS,
                          n_tokens=T),
        out_shape=jax.ShapeDtypeStruct((n_rows, W), U32),
        grid_spec=pltpu.PrefetchScalarGridSpec(
            num_scalar_prefetch=3,
            grid=(T // tb,),
            in_specs=[pl.BlockSpec((tb, W), lambda s, *_: (s, 0))],
            out_specs=pl.BlockSpec(memory_space=pl.ANY),
            scratch_shapes=[pltpu.VMEM((EXPERT_ROWS, W), U32),
                            pltpu.SemaphoreType.DMA, pltpu.SemaphoreType.DMA]),
        compiler_params=_params(("arbitrary",), VMEM_LIMIT),
        name="dispatch_rows",
    )(dest_flat, cnt, starts, h2p)


def _ffn1_kernel(ri_ref, ro_ref, e_ref, j_ref, first_ref, valid_ref, slot_ref, ne_ref, nj_ref, more_ref, pieces_ref,
                 x_ref, w_ref, b_ref, o_ref, stage, wbf, sem, *, F, tn):
    q = pl.program_id(0)

    def wcopy(e, j, slot, part):
        c0 = pl.multiple_of(part * F + j * tn, tn)
        return pltpu.make_async_copy(w_ref.at[e, :, pl.ds(c0, tn)], stage.at[slot, part],
                                     sem.at[slot])

    @pl.when(q == 0)
    def _():
        for part in range(2):
            wcopy(e_ref[0], j_ref[0], 0, part).start(priority=WEIGHT_DMA_PRIORITY)

    @pl.when(first_ref[q] == 1)
    def _():
        slot = slot_ref[q]
        for part in range(2):
            wcopy(e_ref[q], j_ref[q], slot, part).wait()

        @pl.when(more_ref[q] == 1)
        def _():
            for part in range(2):
                wcopy(ne_ref[q], nj_ref[q], 1 - slot, part).start(priority=WEIGHT_DMA_PRIORITY)

    def swiglu_block(w_gate, w_lin, rows):
        xb = _unpack_rows(x_ref[:rows, :])
        g = jnp.dot(xb, w_gate, preferred_element_type=F32) + b_ref[0, 0, 0]
        lin = jnp.dot(xb, w_lin, preferred_element_type=F32) + b_ref[0, 1, 0]
        g = jnp.minimum(g, SWIGLU_LIMIT)
        lin = jnp.clip(lin, -SWIGLU_LIMIT, SWIGLU_LIMIT)
        act = g / (1.0 + jnp.exp(-SWIGLU_ALPHA * g)) * (lin + 1.0)
        o_ref[:rows, :] = act.astype(o_ref.dtype)
        if rows < EXPERT_ROWS:
            o_ref[rows:, :] = jnp.zeros((EXPERT_ROWS - rows, o_ref.shape[1]), o_ref.dtype)

    @pl.when(first_ref[q] == 1)
    def _():
        slot = slot_ref[q]
        w16 = [stage[slot, part].astype(BF16) for part in range(2)]
        for part in range(2):
            wbf[part] = w16[part]
        swiglu_block(w16[0], w16[1], EXPERT_ROWS)

    later = jnp.logical_and(valid_ref[q] == 1, first_ref[q] == 0)

    for pieces in range(MIN_TAIL_PIECES, EXPERT_ROWS // TAIL_ROWS + 1):
        @pl.when(jnp.logical_and(later, pieces_ref[q] == pieces))
        def _(pieces=pieces):
            swiglu_block(wbf[0], wbf[1], pieces * TAIL_ROWS)

    @pl.when(valid_ref[q] == 0)
    def _():
        o_ref[...] = jnp.zeros_like(o_ref)


def _ffn2_kernel(ri_ref, ro_ref, e_ref, j_ref, first_ref, valid_ref, slot_ref, ne_ref, nj_ref, more_ref, pieces_ref,
                 a_ref, w_ref, b_ref, o_ref, stage, wbf, sem):
    q = pl.program_id(0)

    def wcopy(e, slot):
        return pltpu.make_async_copy(w_ref.at[e], stage.at[slot], sem.at[slot])

    @pl.when(q == 0)
    def _():
        wcopy(e_ref[0], 0).start(priority=WEIGHT_DMA_PRIORITY)

    @pl.when(first_ref[q] == 1)
    def _():
        slot = slot_ref[q]
        wcopy(e_ref[q], slot).wait()

        @pl.when(more_ref[q] == 1)
        def _():
            wcopy(ne_ref[q], 1 - slot).start(priority=WEIGHT_DMA_PRIORITY)

    def out_block(w, rows):
        y = jnp.dot(a_ref[:rows, :], w, preferred_element_type=F32) + b_ref[0]
        o_ref[:rows, :] = _pack_rows(y)
        if rows < EXPERT_ROWS:
            o_ref[rows:, :] = jnp.zeros((EXPERT_ROWS - rows, o_ref.shape[1]), o_ref.dtype)

    @pl.when(first_ref[q] == 1)
    def _():
        w16 = stage[slot_ref[q]].astype(BF16)
        wbf[...] = w16
        out_block(w16, EXPERT_ROWS)

    later = jnp.logical_and(valid_ref[q] == 1, first_ref[q] == 0)

    for pieces in range(MIN_TAIL_PIECES, EXPERT_ROWS // TAIL_ROWS + 1):
        @pl.when(jnp.logical_and(later, pieces_ref[q] == pieces))
        def _(pieces=pieces):
            out_block(wbf[...], pieces * TAIL_ROWS)

    @pl.when(valid_ref[q] == 0)
    def _():
        o_ref[...] = jnp.zeros_like(o_ref)


def _work_items(cnt, n_col_tiles, n_blocks):
    n_exp = cnt.shape[0]
    nblk = (cnt + EXPERT_ROWS - 1) // EXPERT_ROWS
    bstart = jnp.cumsum(nblk) - nblk
    gsize = jnp.repeat(nblk, n_col_tiles)
    gend = jnp.cumsum(gsize)
    n_groups = n_exp * n_col_tiles
    gid = jnp.arange(n_groups, dtype=I32)
    total = gend[-1]
    q = jnp.arange(n_blocks * n_col_tiles, dtype=I32)
    qc = jnp.minimum(q, total - 1)
    g = jnp.sum((gend[None, :] <= qc[:, None]).astype(I32), axis=1)
    nonempty = gsize > 0
    ordinal = jnp.cumsum(nonempty.astype(I32)) - 1
    nxt_incl = lax.cummin(jnp.where(nonempty, gid, n_groups), reverse=True)
    nxt = jnp.concatenate([nxt_incl[1:], jnp.full((1,), n_groups, I32)])
    more = nxt < n_groups
    nxt = jnp.minimum(nxt, n_groups - 1)
    per_group = jnp.stack([gend - gsize, gid // n_col_tiles, gid % n_col_tiles,
                           jnp.repeat(bstart, n_col_tiles), ordinal % 2,
                           nxt // n_col_tiles, nxt % n_col_tiles, more.astype(I32),
                           jnp.repeat(cnt, n_col_tiles)])
    pick = (g[None, :, None] == gid[None, None, :]).astype(I32)
    gstart, e, j, brow, slot, ne, nj, more, rows = jnp.sum(pick * per_group[:, None, :], axis=2)
    r = qc - gstart
    valid = q < total
    first = jnp.logical_and(valid, r == 0)
    pieces = jnp.clip((rows - r * EXPERT_ROWS + TAIL_ROWS - 1) // TAIL_ROWS, MIN_TAIL_PIECES,
                      EXPERT_ROWS // TAIL_ROWS)
    over = q - total
    row_in = brow + r
    row_out = jnp.where(valid, row_in, jnp.sum(nblk) + over // n_col_tiles)
    col_out = jnp.where(valid, j, over % n_col_tiles)
    as_i32 = lambda a: a.astype(I32)
    return tuple(map(as_i32, (row_in, row_out, e, col_out, first, valid, slot, ne, nj, more,
                              pieces)))


def _ffn1(items, x_pad, w_exp_in, b_exp_in):
    P, W = x_pad.shape
    n_exp, D, F2 = w_exp_in.shape
    F = F2 // 2
    tn = 1024
    nj = F // tn
    n_items = items[0].shape[0]
    bias = b_exp_in.reshape(n_exp, 2, nj, 1, tn)
    return pl.pallas_call(
        functools.partial(_ffn1_kernel, F=F, tn=tn),
        out_shape=jax.ShapeDtypeStruct((P, F), BF16),
        grid_spec=pltpu.PrefetchScalarGridSpec(
            num_scalar_prefetch=11,
            grid=(n_items,),
            in_specs=[pl.BlockSpec((EXPERT_ROWS, W), lambda q, ri, *_: (ri[q], 0)),
                      pl.BlockSpec(memory_space=pl.ANY),
                      pl.BlockSpec((1, 2, 1, 1, tn),
                                   lambda q, ri, ro, e, j, *_: (e[q], 0, j[q], 0, 0))],
            out_specs=pl.BlockSpec((EXPERT_ROWS, tn),
                                   lambda q, ri, ro, e, j, *_: (ro[q], j[q])),
            scratch_shapes=[pltpu.VMEM((2, 2, D, tn), F32),
                            pltpu.VMEM((2, D, tn), BF16),
                            pltpu.SemaphoreType.DMA((2,))]),
        compiler_params=_params(("arbitrary",), VMEM_LIMIT),
        name="expert_in_swiglu",
    )(*items, x_pad, w_exp_in, bias)


def _ffn2(items, act, w_exp_out, b_exp_out):
    P, F = act.shape
    n_exp, _, D = w_exp_out.shape
    n_items = items[0].shape[0]
    return pl.pallas_call(
        _ffn2_kernel,
        out_shape=jax.ShapeDtypeStruct((P, D // 2), U32),
        grid_spec=pltpu.PrefetchScalarGridSpec(
            num_scalar_prefetch=11,
            grid=(n_items,),
            in_specs=[pl.BlockSpec((EXPERT_ROWS, F), lambda q, ri, *_: (ri[q], 0)),
                      pl.BlockSpec(memory_space=pl.ANY),
                      pl.BlockSpec((1, 1, D), lambda q, ri, ro, e, *_: (e[q], 0, 0))],
            out_specs=pl.BlockSpec((EXPERT_ROWS, D // 2), lambda q, ri, ro, *_: (ro[q], 0)),
            scratch_shapes=[pltpu.VMEM((2, F, D), F32),
                            pltpu.VMEM((F, D), BF16),
                            pltpu.SemaphoreType.DMA((2,))]),
        compiler_params=_params(("arbitrary",), VMEM_LIMIT),
        name="expert_out",
    )(*items, act, w_exp_out, b_exp_out[:, None, :])


def _combine_kernel(dest_ref, y_ref, gates_ref, x1_ref, g2_ref, o_ref, buf, sem,
                    *, tm, n_tokens):
    s = pl.program_id(0)
    ns = pl.num_programs(0)

    def gather(step, slot, unrolled):
        base = step * tm

        def row(t, k):
            d = dest_ref[base + (k * n_tokens + t)]
            pltpu.make_async_copy(y_ref.at[pl.ds(d, 1), :], buf.at[slot, k, pl.ds(t, 1), :],
                                  sem.at[slot]).start(priority=k % 2)

        if unrolled:
            for t in range(tm):
                for k in range(TOP_K):
                    row(t, k)
        else:
            def issue(t, _):
                for k in range(TOP_K):
                    row(t, k)
                return 0

            lax.fori_loop(0, tm, issue, 0)

    @pl.when(s == 0)
    def _():
        gather(0, 0, False)

    for par in range(2):
        @pl.when(jnp.logical_and(s + 1 < ns, (s + 1) % 2 == par))
        def _(par=par):
            gather(s + 1, par, True)

    slot = s % 2
    for k in range(TOP_K):
        pltpu.make_async_copy(y_ref.at[pl.ds(0, tm), :], buf.at[slot, k], sem.at[slot]).wait()
    gates = gates_ref[...]
    y_hi = y_lo = None
    for k in range(TOP_K):
        hi, lo = _unpack_halves(buf[slot, k])
        g = gates[:, k:k + 1]
        y_hi = g * hi if y_hi is None else y_hi + g * hi
        y_lo = g * lo if y_lo is None else y_lo + g * lo
    y = jnp.concatenate([y_hi, y_lo], axis=1)
    o_ref[...] = x1_ref[...] + g2_ref[0] * y


def _combine(dest_flat, y_pad, gates_wide, x1, gate2, S):
    T, D = x1.shape
    tm = 256
    per_b = S // tm
    return pl.pallas_call(
        functools.partial(_combine_kernel, tm=tm, n_tokens=T),
        out_shape=jax.ShapeDtypeStruct((T, D), F32),
        grid_spec=pltpu.PrefetchScalarGridSpec(
            num_scalar_prefetch=1,
            grid=(T // tm,),
            in_specs=[pl.BlockSpec(memory_space=pl.ANY),
                      pl.BlockSpec((tm, LANES), lambda i, d: (i, 0)),
                      pl.BlockSpec((tm, D), lambda i, d: (i, 0)),
                      pl.BlockSpec((1, 1, D), lambda i, d: (i // per_b, 0, 0))],
            out_specs=pl.BlockSpec((tm, D), lambda i, d: (i, 0)),
            scratch_shapes=[pltpu.VMEM((2, TOP_K, tm, D // 2), U32),
                            pltpu.SemaphoreType.DMA((2,))]),
        compiler_params=_params(("arbitrary",), VMEM_LIMIT),
        name="combine_rows",
    )(dest_flat, y_pad, gates_wide, x1, gate2[:, None, :])


def kernel(x, c, norm1_w, norm2_w, w_ada, b_ada, w_in, q_norm_w, k_norm_w, w_pool, pool_scale,
           w_o, w_router, b_router, w_exp_in, b_exp_in, w_exp_out, b_exp_out):
    B, S, D = x.shape
    T = B * S
    depth = w_ada.shape[0]
    n_exp = w_router.shape[-1]
    pool_width = pool_scale.shape[-1]
    sb_width = w_o.shape[1] - pool_width
    n_heads = sb_width // HEAD_DIM
    n_blocks = (T * TOP_K + n_exp * (EXPERT_ROWS - 1)) // EXPERT_ROWS
    n_rows = n_blocks * EXPERT_ROWS

    x2 = x.reshape(T, D)
    for l in range(depth):
        mod = _adaln(c, w_ada[l], b_ada[l])
        shift1, scale1, gate1, shift2, scale2, gate2 = jnp.split(mod, 6, axis=-1)

        proj, wo_bf = _inproj(x2, norm1_w[l], shift1, scale1, w_in[l].astype(BF16), w_o[l], S)
        proj3 = proj.reshape(B, S, -1)
        o_sb = _attention(proj3, q_norm_w[l], k_norm_w[l], n_heads)
        o_pool = _pool(proj3, w_pool[l], pool_scale[l], pool_width)
        x1, h2p, gates_wide, idx_t = _outproj(
            o_sb.reshape(T, sb_width), o_pool.reshape(T, pool_width), wo_bf,
            x2, gate1, shift2, scale2, norm2_w[l], w_router[l], b_router[l], S)

        dest_t, meta = _route(idx_t, n_exp)
        cnt = meta[:, 0]
        starts = meta[:, 1]
        dest_flat = dest_t[:TOP_K].reshape(TOP_K * T)
        x_pad = _dispatch(dest_flat, cnt, starts, h2p, n_rows)

        F = w_exp_out.shape[2]
        act = _ffn1(_work_items(cnt, F // 1024, n_blocks), x_pad, w_exp_in[l], b_exp_in[l])
        y_pad = _ffn2(_work_items(cnt, 1, n_blocks), act, w_exp_out[l], b_exp_out[l])
        x2 = _combine(dest_flat, y_pad, gates_wide, x1, gate2, S)
    return x2.reshape(B, S, D)
```

```python
import functools
import math

import jax
import jax.numpy as jnp
from jax import lax
from jax.experimental import pallas as pl
from jax.experimental.pallas import tpu as pltpu

F32 = jnp.float32
BF16 = jnp.bfloat16
I32 = jnp.int32
U32 = jnp.uint32

EPS = 1e-6
HEAD_DIM = 128
POOL_WINDOWS = (2, 4, 8, 16)
TOP_K = 4
SWIGLU_ALPHA = 1.702
SWIGLU_LIMIT = 7.0

LANES = 128
SUBLANES = 8
EXPERT_ROWS = 256
TAIL_ROWS = 64
MIN_TAIL_PIECES = 2
LOG_UNDERFLOW = 104.0
VMEM_LIMIT = 56 * 1024 * 1024
WEIGHT_DMA_PRIORITY = 1


def _params(sem=None, vmem=None):
    return pltpu.CompilerParams(dimension_semantics=sem, vmem_limit_bytes=vmem)


_HIGH_HALF = 0xFFFF0000


def _pack_rows(v):
    bits = lax.bitcast_convert_type(v.astype(BF16).astype(F32), U32)
    half = v.shape[1] // 2
    return (bits[:, :half] & jnp.uint32(_HIGH_HALF)) | (bits[:, half:] >> 16)


def _unpack_halves(p):
    hi = lax.bitcast_convert_type(p & jnp.uint32(_HIGH_HALF), F32)
    lo = lax.bitcast_convert_type(p << 16, F32)
    return hi, lo


def _unpack_rows(p):
    hi, lo = _unpack_halves(p)
    return jnp.concatenate([hi.astype(BF16), lo.astype(BF16)], axis=1)


def _adaln_kernel(c_ref, w_ref, b_ref, o_ref):
    c = c_ref[...]
    ca = c / (1.0 + jnp.exp(-c))
    o_ref[...] = jnp.dot(ca.astype(BF16), w_ref[...].astype(BF16),
                         preferred_element_type=F32) + b_ref[...]


def _adaln(c, w_ada, b_ada):
    B, D = c.shape
    N = w_ada.shape[1]
    rows = 8
    tn = 1024
    cp = jnp.zeros((rows, D), F32).at[:B].set(c)
    out = pl.pallas_call(
        _adaln_kernel,
        out_shape=jax.ShapeDtypeStruct((rows, N), F32),
        grid=(N // tn,),
        in_specs=[pl.BlockSpec((rows, D), lambda j: (0, 0)),
                  pl.BlockSpec((D, tn), lambda j: (0, j)),
                  pl.BlockSpec((1, tn), lambda j: (0, j))],
        out_specs=pl.BlockSpec((rows, tn), lambda j: (0, j)),
        compiler_params=_params(("arbitrary",), VMEM_LIMIT),
        name="adaln",
    )(cp, w_ada, b_ada.reshape(1, N))
    return out[:B]


def _inproj_kernel(x_ref, nw_ref, sh_ref, sc_ref, w_ref, wo_ref, o_ref, wo16_ref, h_ref,
                   *, tm, ch):
    wo16_ref[...] = wo_ref[...].astype(BF16)

    @pl.when(pl.program_id(1) == 0)
    def _():
        mul = nw_ref[...] * (1.0 + sc_ref[0])
        add = sh_ref[0]

        def body(c, _):
            r0 = pl.multiple_of(c * ch, ch)
            x = x_ref[pl.ds(r0, ch), :]
            inv = lax.rsqrt(jnp.mean(x * x, axis=-1, keepdims=True) + EPS)
            h_ref[pl.ds(r0, ch), :] = (x * inv * mul + add).astype(BF16)
            return 0

        lax.fori_loop(0, tm // ch, body, 0)

    o_ref[...] = jnp.dot(h_ref[...], w_ref[...],
                         preferred_element_type=F32).astype(o_ref.dtype)


def _inproj(x2, norm_w, shift, scale, w_bf, w_o, S):
    T, D = x2.shape
    N = w_bf.shape[1]
    tm, tn, ch = 1024, 2048, 128
    per_b = S // tm
    n_i, n_j = T // tm, N // tn
    slab = w_o.shape[0] // (n_i * n_j)
    return pl.pallas_call(
        functools.partial(_inproj_kernel, tm=tm, ch=ch),
        out_shape=(jax.ShapeDtypeStruct((T, N), BF16),
                   jax.ShapeDtypeStruct(w_o.shape, BF16)),
        grid=(n_i, n_j),
        in_specs=[pl.BlockSpec((tm, D), lambda i, j: (i, 0)),
                  pl.BlockSpec((1, D), lambda i, j: (0, 0)),
                  pl.BlockSpec((1, 1, D), lambda i, j: (i // per_b, 0, 0)),
                  pl.BlockSpec((1, 1, D), lambda i, j: (i // per_b, 0, 0)),
                  pl.BlockSpec((D, tn), lambda i, j: (0, j)),
                  pl.BlockSpec((slab, w_o.shape[1]), lambda i, j: (i * n_j + j, 0))],
        out_specs=(pl.BlockSpec((tm, tn), lambda i, j: (i, j)),
                   pl.BlockSpec((slab, w_o.shape[1]), lambda i, j: (i * n_j + j, 0))),
        scratch_shapes=[pltpu.VMEM((tm, D), BF16)],
        compiler_params=_params(("arbitrary", "arbitrary"), VMEM_LIMIT),
        name="inproj",
    )(x2, norm_w.reshape(1, D), shift[:, None, :], scale[:, None, :], w_bf, w_o)


def _attn_kernel(q_ref, k_ref, v_ref, qw_ref, kw_ref, o_ref, kn_ref, carry_ref, acc_ref,
                 *, S, tq, hg, scale):
    i = pl.program_id(2)
    d = HEAD_DIM

    def head_norm(x, w):
        parts = []
        for h in range(hg):
            xh = x[:, h * d:(h + 1) * d]
            inv = lax.rsqrt(jnp.mean(xh * xh, axis=-1, keepdims=True) + EPS)
            parts.append(xh * inv * w)
        return parts

    @pl.when(i == 0)
    def _():
        def body(c, _):
            r0 = pl.multiple_of(c * tq, tq)
            parts = head_norm(k_ref[0, pl.ds(r0, tq), :].astype(F32), kw_ref[...])
            for h in range(hg):
                kn_ref[pl.ds(r0, tq), h * d:(h + 1) * d] = parts[h].astype(BF16)
            return 0

        lax.fori_loop(0, S // tq, body, 0)

    qb = [(p * scale).astype(BF16) for p in head_norm(q_ref[0].astype(F32), qw_ref[...])]

    row = lax.broadcasted_iota(I32, (tq, tq), 0)
    col = lax.broadcasted_iota(I32, (tq, tq), 1)
    causal = col < row
    tri = (row > col).astype(BF16)

    def scores(h, rows, r0, nk, mask=None):
        kblk = kn_ref[pl.ds(r0, nk), h * d:(h + 1) * d]
        z = lax.dot_general(qb[h][rows], kblk, (((1,), (1,)), ((), ())),
                            preferred_element_type=F32)
        t = jnp.log(1.0 + jnp.exp(-jnp.abs(z)))
        lsn = jnp.minimum(-z, 0.0) - t
        lsp = lsn + z
        if mask is not None:
            lsn = jnp.where(mask, lsn, 0.0)
        later = jnp.dot(lsn.astype(BF16), tri[:nk, :nk], preferred_element_type=F32)
        return lsp + later, later[:, :1] + lsn[:, :1]

    def weighted(a, h, r0, nk):
        vblk = v_ref[0, pl.ds(r0, nk), h * d:(h + 1) * d]
        return jnp.dot(a.astype(BF16), vblk, preferred_element_type=F32)

    has_prev = i > 0
    rd = pl.multiple_of(i * tq, tq)
    rp = pl.multiple_of(jnp.maximum(i - 1, 0) * tq, tq)
    half = tq // 2
    top, bottom, every = slice(0, half), slice(half, tq), slice(0, tq)
    worst = None
    for h in range(hg):
        cols = slice(h * d, (h + 1) * d)
        log_t, sum_t = scores(h, top, rd, half, causal[top, top])
        log_b, sum_b = scores(h, bottom, rd, tq, causal[bottom, :])
        log_p, sum_p = scores(h, every, rp, tq)
        sum_d = jnp.concatenate([sum_t, sum_b], axis=0)
        a_t = jnp.where(causal[top, top], jnp.exp(log_t), 0.0)
        a_b = jnp.where(causal[bottom, :], jnp.exp(log_b), 0.0)
        a_p = jnp.where(has_prev, jnp.exp(log_p + sum_d), 0.0)
        from_prev = weighted(a_p, h, rp, tq)
        acc_ref[top, cols] = weighted(a_t, h, rd, half) + from_prev[top]
        acc_ref[bottom, cols] = weighted(a_b, h, rd, tq) + from_prev[bottom]
        carry = jnp.where(has_prev, sum_d + sum_p, sum_d)
        carry_ref[h] = carry
        m = jnp.max(carry)
        worst = m if worst is None else jnp.maximum(worst, m)

    def earlier(kb):
        r0 = pl.multiple_of(kb * tq, tq)
        worst = None
        for h in range(hg):
            cols = slice(h * d, (h + 1) * d)
            log_a, row_sum = scores(h, every, r0, tq)
            acc_ref[:, cols] += weighted(jnp.exp(log_a + carry_ref[h]), h, r0, tq)
            carry = carry_ref[h] + row_sum
            carry_ref[h] = carry
            m = jnp.max(carry)
            worst = m if worst is None else jnp.maximum(worst, m)
        return worst

    def cond(st):
        kb, m = st
        return jnp.logical_and(kb >= 0, m > -LOG_UNDERFLOW)

    def body(st):
        kb, _ = st
        return kb - 1, earlier(kb)

    lax.while_loop(cond, body, (i - 2, worst))
    o_ref[0] = acc_ref[...].astype(o_ref.dtype)


def _attention(proj3, q_norm_w, k_norm_w, n_heads):
    B, S, _ = proj3.shape
    d = HEAD_DIM
    tq = 256
    hg = 8
    G = n_heads // hg
    w = hg * d
    return pl.pallas_call(
        functools.partial(_attn_kernel, S=S, tq=tq, hg=hg, scale=1.0 / math.sqrt(d)),
        out_shape=jax.ShapeDtypeStruct((B, S, n_heads * d), BF16),
        grid=(B, G, S // tq),
        in_specs=[pl.BlockSpec((1, tq, w), lambda b, g, i: (b, i, g)),
                  pl.BlockSpec((1, S, w), lambda b, g, i: (b, 0, G + g)),
                  pl.BlockSpec((1, S, w), lambda b, g, i: (b, 0, 2 * G + g)),
                  pl.BlockSpec((1, d), lambda b, g, i: (0, 0)),
                  pl.BlockSpec((1, d), lambda b, g, i: (0, 0))],
        out_specs=pl.BlockSpec((1, tq, w), lambda b, g, i: (b, i, g)),
        scratch_shapes=[pltpu.VMEM((S, w), BF16),
                        pltpu.VMEM((hg, tq, 1), F32),
                        pltpu.VMEM((tq, w), F32)],
        compiler_params=_params(("arbitrary", "arbitrary", "arbitrary"), VMEM_LIMIT),
        name="stickbreak_attn",
    )(proj3, proj3, proj3, q_norm_w.reshape(1, d), k_norm_w.reshape(1, d))


def _pool_kernel(u_ref, w_ref, ps_ref, o_ref, *, S, ch, gd):
    halo = 16
    wgs = [w_ref[g].astype(BF16) for g in range(len(POOL_WINDOWS))]

    def body(c, _):
        r0 = pl.multiple_of(c * ch, ch)
        p0 = pl.multiple_of(jnp.maximum(r0 - halo, 0), halo)
        t = r0 + lax.broadcasted_iota(I32, (ch, 1), 0)
        for g, win in enumerate(POOL_WINDOWS):
            lo, hi = g * gd, (g + 1) * gd
            cur = u_ref[0, pl.ds(r0, ch), lo:hi].astype(F32)
            prev = u_ref[0, pl.ds(p0, halo), lo:hi].astype(F32)
            prev = jnp.where(c > 0, prev, 0.0)
            s = jnp.concatenate([prev, cur], axis=0)
            n = 1
            while n < win:
                s = s + pltpu.roll(s, n, 0)
                n *= 2
            s = s[halo:]
            cnt = jnp.minimum(t + 1, win).astype(F32)
            p = s / cnt - cur
            y = jnp.dot(p.astype(BF16), wgs[g], preferred_element_type=F32) * ps_ref[:, lo:hi]
            o_ref[0, pl.ds(r0, ch), lo:hi] = y.astype(o_ref.dtype)
        return 0

    lax.fori_loop(0, S // ch, body, 0)


def _pool(proj3, w_pool, pool_scale, pool_width):
    B, S, NP = proj3.shape
    G, gd, _ = w_pool.shape
    return pl.pallas_call(
        functools.partial(_pool_kernel, S=S, ch=256, gd=gd),
        out_shape=jax.ShapeDtypeStruct((B, S, pool_width), BF16),
        grid=(B,),
        in_specs=[pl.BlockSpec((1, S, pool_width), lambda b: (b, 0, NP // pool_width - 1)),
                  pl.BlockSpec((G, gd, gd), lambda b: (0, 0, 0)),
                  pl.BlockSpec((1, pool_width), lambda b: (0, 0))],
        out_specs=pl.BlockSpec((1, S, pool_width), lambda b: (b, 0, 0)),
        compiler_params=_params(("arbitrary",), VMEM_LIMIT),
        name="pool_mixer",
    )(proj3, w_pool, pool_scale.reshape(1, pool_width))


def _outproj_kernel(osb_ref, opool_ref, wo_ref, x_ref, g1_ref, sh_ref, sc_ref, nw_ref,
                    wr_ref, br_ref, x1_ref, h2p_ref, gates_ref, idx_ref, *, sbw, n_exp, sub):
    for r0 in range(0, x_ref.shape[0], sub):
        rows = slice(r0, r0 + sub)
        _outproj_rows(osb_ref.at[rows], opool_ref.at[rows], wo_ref, x_ref.at[rows], g1_ref,
                      sh_ref, sc_ref, nw_ref, wr_ref, br_ref, x1_ref.at[rows], h2p_ref.at[rows],
                      gates_ref.at[rows], idx_ref.at[:, rows], sbw=sbw, n_exp=n_exp)


def _outproj_rows(osb_ref, opool_ref, wo_ref, x_ref, g1_ref, sh_ref, sc_ref, nw_ref,
                  wr_ref, br_ref, x1_ref, h2p_ref, gates_ref, idx_ref, *, sbw, n_exp):
    tm, D = x_ref.shape
    mixed = (jnp.dot(osb_ref[...], wo_ref[:sbw, :], preferred_element_type=F32)
             + jnp.dot(opool_ref[...], wo_ref[sbw:, :], preferred_element_type=F32))
    x1 = x_ref[...] + g1_ref[0] * mixed
    x1_ref[...] = x1
    inv = lax.rsqrt(jnp.mean(x1 * x1, axis=-1, keepdims=True) + EPS)
    h2 = x1 * inv * (nw_ref[...] * (1.0 + sc_ref[0])) + sh_ref[0]
    hb = h2.astype(BF16)
    h2p_ref[...] = _pack_rows(h2)

    vals = lax.dot_general(wr_ref[...].astype(BF16), hb, (((1,), (1,)), ((), ())),
                           preferred_element_type=F32) + br_ref[...]
    expert = lax.broadcasted_iota(I32, (n_exp, tm), 0).astype(F32)
    tops, ids = [], []
    for _ in range(TOP_K):
        m = jnp.max(vals, axis=0, keepdims=True)
        first = jnp.min(jnp.where(vals == m, expert, float(n_exp)), axis=0, keepdims=True)
        tops.append(m)
        ids.append(first)
        vals = jnp.where(expert == first, -jnp.inf, vals)
    es = [jnp.exp(m - tops[0]) for m in tops]
    den = es[0]
    for e in es[1:]:
        den = den + e
    slot = lax.broadcasted_iota(I32, (LANES, tm), 0)
    gates = jnp.zeros((LANES, tm), F32)
    for k in range(TOP_K):
        gates = jnp.where(slot == k, es[k] / den, gates)
    gates_ref[...] = gates.T
    slot8 = lax.broadcasted_iota(I32, (SUBLANES, tm), 0)
    idx = jnp.zeros((SUBLANES, tm), F32)
    for k in range(TOP_K):
        idx = jnp.where(slot8 == k, ids[k], idx)
    idx_ref[...] = idx.astype(I32)


def _outproj(o_sb, o_pool, wo_bf, x2, gate1, shift2, scale2, norm2_w, w_router, b_router, S):
    T, D = x2.shape
    sbw = o_sb.shape[1]
    pw = o_pool.shape[1]
    n_exp = w_router.shape[1]
    tm, sub = 512, 256
    per_b = S // tm
    wr = w_router.T
    br = b_router.reshape(n_exp, 1)
    mod_spec = pl.BlockSpec((1, 1, D), lambda i: (i // per_b, 0, 0))
    return pl.pallas_call(
        functools.partial(_outproj_kernel, sbw=sbw, n_exp=n_exp, sub=sub),
        out_shape=(jax.ShapeDtypeStruct((T, D), F32),
                   jax.ShapeDtypeStruct((T, D // 2), U32),
                   jax.ShapeDtypeStruct((T, LANES), F32),
                   jax.ShapeDtypeStruct((SUBLANES, T), I32)),
        grid=(T // tm,),
        in_specs=[pl.BlockSpec((tm, sbw), lambda i: (i, 0)),
                  pl.BlockSpec((tm, pw), lambda i: (i, 0)),
                  pl.BlockSpec((sbw + pw, D), lambda i: (0, 0)),
                  pl.BlockSpec((tm, D), lambda i: (i, 0)),
                  mod_spec, mod_spec, mod_spec,
                  pl.BlockSpec((1, D), lambda i: (0, 0)),
                  pl.BlockSpec((n_exp, D), lambda i: (0, 0)),
                  pl.BlockSpec((n_exp, 1), lambda i: (0, 0))],
        out_specs=(pl.BlockSpec((tm, D), lambda i: (i, 0)),
                   pl.BlockSpec((tm, D // 2), lambda i: (i, 0)),
                   pl.BlockSpec((tm, LANES), lambda i: (i, 0)),
                   pl.BlockSpec((SUBLANES, tm), lambda i: (0, i))),
        compiler_params=_params(("arbitrary",), VMEM_LIMIT),
        name="outproj_router",
    )(o_sb, o_pool, wo_bf, x2, gate1[:, None, :], shift2[:, None, :], scale2[:, None, :],
      norm2_w.reshape(1, D), wr, br)


def _route_kernel(idx_ref, dest_ref, meta_ref, rank_ref, *, T, ch, n_exp):
    expert = lax.broadcasted_iota(I32, (n_exp, ch), 0)
    row = lax.broadcasted_iota(I32, (ch, ch), 0)
    col = lax.broadcasted_iota(I32, (ch, ch), 1)
    earlier = (row < col).astype(BF16)

    def chunk(c):
        return slice(c * ch, (c + 1) * ch)

    def count(c, cnt):
        member = expert == idx_ref[0:1, chunk(c)]
        for k in range(1, TOP_K):
            member = jnp.logical_or(member, expert == idx_ref[k:k + 1, chunk(c)])
        mf = jnp.where(member, 1.0, 0.0)
        rank_ref[:, chunk(c)] = jnp.dot(mf.astype(BF16), earlier,
                                        preferred_element_type=F32) + cnt
        return cnt + jnp.sum(mf, axis=1, keepdims=True)

    cnt = jnp.zeros((n_exp, 1), F32)
    for c in range(T // ch):
        cnt = count(c, cnt)
    padded = jnp.broadcast_to(jnp.ceil(cnt / EXPERT_ROWS) * EXPERT_ROWS, (n_exp, LANES))
    sub = lax.broadcasted_iota(I32, (n_exp, LANES), 0)
    ends = padded
    sh = 1
    while sh < n_exp:
        ends = ends + jnp.where(sub >= sh, pltpu.roll(ends, sh, 0), 0.0)
        sh *= 2
    starts_wide = ends - padded
    starts = starts_wide[:, :1]
    lane = lax.broadcasted_iota(I32, (n_exp, LANES), 1)
    meta = jnp.where(lane == 0, jnp.broadcast_to(cnt, (n_exp, LANES)),
                     jnp.where(lane == 1, starts_wide, 0.0))
    meta_ref[...] = meta.astype(I32)

    slot = lax.broadcasted_iota(I32, (SUBLANES, ch), 0)

    def place(c, _):
        val = rank_ref[:, chunk(c)] + starts
        out = jnp.zeros((SUBLANES, ch), F32)
        for k in range(TOP_K):
            mine = expert == idx_ref[k:k + 1, chunk(c)]
            d = jnp.sum(jnp.where(mine, val, 0.0), axis=0, keepdims=True)
            out = jnp.where(slot == k, d, out)
        dest_ref[:, chunk(c)] = out.astype(I32)
        return 0

    for c in range(T // ch):
        place(c, 0)


def _route(idx_t, n_exp):
    T = idx_t.shape[1]
    return pl.pallas_call(
        functools.partial(_route_kernel, T=T, ch=256, n_exp=n_exp),
        out_shape=(jax.ShapeDtypeStruct((SUBLANES, T), I32),
                   jax.ShapeDtypeStruct((n_exp, LANES), I32)),
        grid=(1,),
        in_specs=[pl.BlockSpec((SUBLANES, T), lambda i: (0, 0))],
        out_specs=(pl.BlockSpec((SUBLANES, T), lambda i: (0, 0)),
                   pl.BlockSpec((n_exp, LANES), lambda i: (0, 0))),
        scratch_shapes=[pltpu.VMEM((n_exp, T), F32)],
        compiler_params=_params(("arbitrary",), VMEM_LIMIT),
        name="route_ranks",
    )(idx_t)


def _dispatch_kernel(dest_ref, cnt_ref, start_ref, h_ref, x_ref, z_ref, sem, zsem,
                     *, tb, n_exp, n_blocks, n_tokens):
    s = pl.program_id(0)

    @pl.when(s == 0)
    def _():
        z_ref[...] = jnp.zeros_like(z_ref)
        _dispatch_zero_fill(cnt_ref, start_ref, x_ref, z_ref, zsem, n_exp, n_blocks)

    base = s * tb
    for t in range(tb):
        for k in range(TOP_K):
            d = dest_ref[base + (k * n_tokens + t)]
            pltpu.make_async_copy(h_ref.at[pl.ds(t, 1), :], x_ref.at[pl.ds(d, 1), :],
                                  sem).start(priority=k % 2)

    for _ in range(TOP_K):
        pltpu.make_async_copy(h_ref, x_ref.at[pl.ds(0, tb), :], sem).wait()


def _dispatch_zero_fill(cnt_ref, start_ref, x_ref, z_ref, zsem, n_exp, n_blocks):
    def block_fill(blk, wait):
        r0 = pl.multiple_of(blk * EXPERT_ROWS, EXPERT_ROWS)
        cp = pltpu.make_async_copy(z_ref, x_ref.at[pl.ds(r0, EXPERT_ROWS), :], zsem)
        if wait:
            cp.wait()
        else:
            cp.start()

    def zero_fill(e, wait):
        cnt = cnt_ref[e]

        @pl.when((cnt & (EXPERT_ROWS - 1)) != 0)
        def _():
            block_fill((start_ref[e] + cnt) // EXPERT_ROWS, wait)

        return 0

    used = (start_ref[n_exp - 1] + cnt_ref[n_exp - 1] + EXPERT_ROWS - 1) // EXPERT_ROWS

    def tail_fill(blk, wait):
        block_fill(blk, wait)
        return 0

    lax.fori_loop(0, n_exp, lambda e, _: zero_fill(e, False), 0)
    lax.fori_loop(used, n_blocks, lambda b, _: tail_fill(b, False), 0)
    lax.fori_loop(0, n_exp, lambda e, _: zero_fill(e, True), 0)
    lax.fori_loop(used, n_blocks, lambda b, _: tail_fill(b, True), 0)


def _dispatch(dest_flat, cnt, starts, h2p, n_rows):
    T, W = h2p.shape
    n_exp = cnt.shape[0]
    tb = 1024
    return pl.pallas_call(
        functools.partial(_dispatch_kernel, tb=tb, n_exp=n_exp, n_blocks=n_rows // EXPERT_ROWS,
                          n_tokens=T),
        out_shape=jax.ShapeDtypeStruct((n_rows, W), U32),
        grid_spec=pltpu.PrefetchScalarGridSpec(
            num_scalar_prefetch=3,
            grid=(T // tb,),
            in_specs=[pl.BlockSpec((tb, W), lambda s, *_: (s, 0))],
            out_specs=pl.BlockSpec(memory_space=pl.ANY),
            scratch_shapes=[pltpu.VMEM((EXPERT_ROWS, W), U32),
                            pltpu.SemaphoreType.DMA, pltpu.SemaphoreType.DMA]),
        compiler_params=_params(("arbitrary",), VMEM_LIMIT),
        name="dispatch_rows",
    )(dest_flat, cnt, starts, h2p)


def _ffn1_kernel(ri_ref, ro_ref, e_ref, j_ref, first_ref, valid_ref, slot_ref, ne_ref, nj_ref, more_ref, pieces_ref,
                 x_ref, w_ref, b_ref, o_ref, stage, wbf, sem, *, F, tn):
    q = pl.program_id(0)

    def wcopy(e, j, slot, part):
        c0 = pl.multiple_of(part * F + j * tn, tn)
        return pltpu.make_async_copy(w_ref.at[e, :, pl.ds(c0, tn)], stage.at[slot, part],
                                     sem.at[slot])

    @pl.when(q == 0)
    def _():
        for part in range(2):
            wcopy(e_ref[0], j_ref[0], 0, part).start(priority=WEIGHT_DMA_PRIORITY)

    @pl.when(first_ref[q] == 1)
    def _():
        slot = slot_ref[q]
        for part in range(2):
            wcopy(e_ref[q], j_ref[q], slot, part).wait()

        @pl.when(more_ref[q] == 1)
        def _():
            for part in range(2):
                wcopy(ne_ref[q], nj_ref[q], 1 - slot, part).start(priority=WEIGHT_DMA_PRIORITY)

    def swiglu_block(w_gate, w_lin, rows):
        xb = _unpack_rows(x_ref[:rows, :])
        g = jnp.dot(xb, w_gate, preferred_element_type=F32) + b_ref[0, 0, 0]
        lin = jnp.dot(xb, w_lin, preferred_element_type=F32) + b_ref[0, 1, 0]
        g = jnp.minimum(g, SWIGLU_LIMIT)
        lin = jnp.clip(lin, -SWIGLU_LIMIT, SWIGLU_LIMIT)
        act = g / (1.0 + jnp.exp(-SWIGLU_ALPHA * g)) * (lin + 1.0)
        o_ref[:rows, :] = act.astype(o_ref.dtype)
        if rows < EXPERT_ROWS:
            o_ref[rows:, :] = jnp.zeros((EXPERT_ROWS - rows, o_ref.shape[1]), o_ref.dtype)

    @pl.when(first_ref[q] == 1)
    def _():
        slot = slot_ref[q]
        w16 = [stage[slot, part].astype(BF16) for part in range(2)]
        for part in range(2):
            wbf[part] = w16[part]
        swiglu_block(w16[0], w16[1], EXPERT_ROWS)

    later = jnp.logical_and(valid_ref[q] == 1, first_ref[q] == 0)

    for pieces in range(MIN_TAIL_PIECES, EXPERT_ROWS // TAIL_ROWS + 1):
        @pl.when(jnp.logical_and(later, pieces_ref[q] == pieces))
        def _(pieces=pieces):
            swiglu_block(wbf[0], wbf[1], pieces * TAIL_ROWS)

    @pl.when(valid_ref[q] == 0)
    def _():
        o_ref[...] = jnp.zeros_like(o_ref)


def _ffn2_kernel(ri_ref, ro_ref, e_ref, j_ref, first_ref, valid_ref, slot_ref, ne_ref, nj_ref, more_ref, pieces_ref,
                 a_ref, w_ref, b_ref, o_ref, stage, wbf, sem):
    q = pl.program_id(0)

    def wcopy(e, slot):
        return pltpu.make_async_copy(w_ref.at[e], stage.at[slot], sem.at[slot])

    @pl.when(q == 0)
    def _():
        wcopy(e_ref[0], 0).start(priority=WEIGHT_DMA_PRIORITY)

    @pl.when(first_ref[q] == 1)
    def _():
        slot = slot_ref[q]
        wcopy(e_ref[q], slot).wait()

        @pl.when(more_ref[q] == 1)
        def _():
            wcopy(ne_ref[q], 1 - slot).start(priority=WEIGHT_DMA_PRIORITY)

    def out_block(w, rows):
        y = jnp.dot(a_ref[:rows, :], w, preferred_element_type=F32) + b_ref[0]
        o_ref[:rows, :] = _pack_rows(y)
        if rows < EXPERT_ROWS:
            o_ref[rows:, :] = jnp.zeros((EXPERT_ROWS - rows, o_ref.shape[1]), o_ref.dtype)

    @pl.when(first_ref[q] == 1)
    def _():
        w16 = stage[slot_ref[q]].astype(BF16)
        wbf[...] = w16
        out_block(w16, EXPERT_ROWS)

    later = jnp.logical_and(valid_ref[q] == 1, first_ref[q] == 0)

    for pieces in range(MIN_TAIL_PIECES, EXPERT_ROWS // TAIL_ROWS + 1):
        @pl.when(jnp.logical_and(later, pieces_ref[q] == pieces))
        def _(pieces=pieces):
            out_block(wbf[...], pieces * TAIL_ROWS)

    @pl.when(valid_ref[q] == 0)
    def _():
        o_ref[...] = jnp.zeros_like(o_ref)


def _work_items(cnt, n_col_tiles, n_blocks):
    n_exp = cnt.shape[0]
    nblk = (cnt + EXPERT_ROWS - 1) // EXPERT_ROWS
    bstart = jnp.cumsum(nblk) - nblk
    gsize = jnp.repeat(nblk, n_col_tiles)
    gend = jnp.cumsum(gsize)
    n_groups = n_exp * n_col_tiles
    gid = jnp.arange(n_groups, dtype=I32)
    total = gend[-1]
    q = jnp.arange(n_blocks * n_col_tiles, dtype=I32)
    qc = jnp.minimum(q, total - 1)
    g = jnp.sum((gend[None, :] <= qc[:, None]).astype(I32), axis=1)
    nonempty = gsize > 0
    ordinal = jnp.cumsum(nonempty.astype(I32)) - 1
    nxt_incl = lax.cummin(jnp.where(nonempty, gid, n_groups), reverse=True)
    nxt = jnp.concatenate([nxt_incl[1:], jnp.full((1,), n_groups, I32)])
    more = nxt < n_groups
    nxt = jnp.minimum(nxt, n_groups - 1)
    per_group = jnp.stack([gend - gsize, gid // n_col_tiles, gid % n_col_tiles,
                           jnp.repeat(bstart, n_col_tiles), ordinal % 2,
                           nxt // n_col_tiles, nxt % n_col_tiles, more.astype(I32),
                           jnp.repeat(cnt, n_col_tiles)])
    pick = (g[None, :, None] == gid[None, None, :]).astype(I32)
    gstart, e, j, brow, slot, ne, nj, more, rows = jnp.sum(pick * per_group[:, None, :], axis=2)
    r = qc - gstart
    valid = q < total
    first = jnp.logical_and(valid, r == 0)
    pieces = jnp.clip((rows - r * EXPERT_ROWS + TAIL_ROWS - 1) // TAIL_ROWS, MIN_TAIL_PIECES,
                      EXPERT_ROWS // TAIL_ROWS)
    over = q - total
    row_in = brow + r
    row_out = jnp.where(valid, row_in, jnp.sum(nblk) + over // n_col_tiles)
    col_out = jnp.where(valid, j, over % n_col_tiles)
    as_i32 = lambda a: a.astype(I32)
    return tuple(map(as_i32, (row_in, row_out, e, col_out, first, valid, slot, ne, nj, more,
                              pieces)))


def _ffn1(items, x_pad, w_exp_in, b_exp_in):
    P, W = x_pad.shape
    n_exp, D, F2 = w_exp_in.shape
    F = F2 // 2
    tn = 1024
    nj = F // tn
    n_items = items[0].shape[0]
    bias = b_exp_in.reshape(n_exp, 2, nj, 1, tn)
    return pl.pallas_call(
        functools.partial(_ffn1_kernel, F=F, tn=tn),
        out_shape=jax.ShapeDtypeStruct((P, F), BF16),
        grid_spec=pltpu.PrefetchScalarGridSpec(
            num_scalar_prefetch=11,
            grid=(n_items,),
            in_specs=[pl.BlockSpec((EXPERT_ROWS, W), lambda q, ri, *_: (ri[q], 0)),
                      pl.BlockSpec(memory_space=pl.ANY),
                      pl.BlockSpec((1, 2, 1, 1, tn),
                                   lambda q, ri, ro, e, j, *_: (e[q], 0, j[q], 0, 0))],
            out_specs=pl.BlockSpec((EXPERT_ROWS, tn),
                                   lambda q, ri, ro, e, j, *_: (ro[q], j[q])),
            scratch_shapes=[pltpu.VMEM((2, 2, D, tn), F32),
                            pltpu.VMEM((2, D, tn), BF16),
                            pltpu.SemaphoreType.DMA((2,))]),
        compiler_params=_params(("arbitrary",), VMEM_LIMIT),
        name="expert_in_swiglu",
    )(*items, x_pad, w_exp_in, bias)


def _ffn2(items, act, w_exp_out, b_exp_out):
    P, F = act.shape
    n_exp, _, D = w_exp_out.shape
    n_items = items[0].shape[0]
    return pl.pallas_call(
        _ffn2_kernel,
        out_shape=jax.ShapeDtypeStruct((P, D // 2), U32),
        grid_spec=pltpu.PrefetchScalarGridSpec(
            num_scalar_prefetch=11,
            grid=(n_items,),
            in_specs=[pl.BlockSpec((EXPERT_ROWS, F), lambda q, ri, *_: (ri[q], 0)),
                      pl.BlockSpec(memory_space=pl.ANY),
                      pl.BlockSpec((1, 1, D), lambda q, ri, ro, e, *_: (e[q], 0, 0))],
            out_specs=pl.BlockSpec((EXPERT_ROWS, D // 2), lambda q, ri, ro, *_: (ro[q], 0)),
            scratch_shapes=[pltpu.VMEM((2, F, D), F32),
                            pltpu.VMEM((F, D), BF16),
                            pltpu.SemaphoreType.DMA((2,))]),
        compiler_params=_params(("arbitrary",), VMEM_LIMIT),
        name="expert_out",
    )(*items, act, w_exp_out, b_exp_out[:, None, :])


def _combine_kernel(dest_ref, y_ref, gates_ref, x1_ref, g2_ref, o_ref, buf, sem,
                    *, tm, n_tokens):
    s = pl.program_id(0)
    ns = pl.num_programs(0)

    def gather(step, slot, unrolled):
        base = step * tm

        def row(t, k):
            d = dest_ref[base + (k * n_tokens + t)]
            pltpu.make_async_copy(y_ref.at[pl.ds(d, 1), :], buf.at[slot, k, pl.ds(t, 1), :],
                                  sem.at[slot]).start(priority=k % 2)

        if unrolled:
            for t in range(tm):
                for k in range(TOP_K):
                    row(t, k)
        else:
            def issue(t, _):
                for k in range(TOP_K):
                    row(t, k)
                return 0

            lax.fori_loop(0, tm, issue, 0)

    @pl.when(s == 0)
    def _():
        gather(0, 0, False)

    for par in range(2):
        @pl.when(jnp.logical_and(s + 1 < ns, (s + 1) % 2 == par))
        def _(par=par):
            gather(s + 1, par, True)

    slot = s % 2
    for k in range(TOP_K):
        pltpu.make_async_copy(y_ref.at[pl.ds(0, tm), :], buf.at[slot, k], sem.at[slot]).wait()
    gates = gates_ref[...]
    y_hi = y_lo = None
    for k in range(TOP_K):
        hi, lo = _unpack_halves(buf[slot, k])
        g = gates[:, k:k + 1]
        y_hi = g * hi if y_hi is None else y_hi + g * hi
        y_lo = g * lo if y_lo is None else y_lo + g * lo
    y = jnp.concatenate([y_hi, y_lo], axis=1)
    o_ref[...] = x1_ref[...] + g2_ref[0] * y


def _combine(dest_flat, y_pad, gates_wide, x1, gate2, S):
    T, D = x1.shape
    tm = 512
    per_b = S // tm
    return pl.pallas_call(
        functools.partial(_combine_kernel, tm=tm, n_tokens=T),
        out_shape=jax.ShapeDtypeStruct((T, D), F32),
        grid_spec=pltpu.PrefetchScalarGridSpec(
            num_scalar_prefetch=1,
            grid=(T // tm,),
            in_specs=[pl.BlockSpec(memory_space=pl.ANY),
                      pl.BlockSpec((tm, LANES), lambda i, d: (i, 0)),
                      pl.BlockSpec((tm, D), lambda i, d: (i, 0)),
                      pl.BlockSpec((1, 1, D), lambda i, d: (i // per_b, 0, 0))],
            out_specs=pl.BlockSpec((tm, D), lambda i, d: (i, 0)),
            scratch_shapes=[pltpu.VMEM((2, TOP_K, tm, D // 2), U32),
                            pltpu.SemaphoreType.DMA((2,))]),
        compiler_params=_params(("arbitrary",), VMEM_LIMIT),
        name="combine_rows",
    )(dest_flat, y_pad, gates_wide, x1, gate2[:, None, :])


def kernel(x, c, norm1_w, norm2_w, w_ada, b_ada, w_in, q_norm_w, k_norm_w, w_pool, pool_scale,
           w_o, w_router, b_router, w_exp_in, b_exp_in, w_exp_out, b_exp_out):
    B, S, D = x.shape
    T = B * S
    depth = w_ada.shape[0]
    n_exp = w_router.shape[-1]
    pool_width = pool_scale.shape[-1]
    sb_width = w_o.shape[1] - pool_width
    n_heads = sb_width // HEAD_DIM
    n_blocks = (T * TOP_K + n_exp * (EXPERT_ROWS - 1)) // EXPERT_ROWS
    n_rows = n_blocks * EXPERT_ROWS

    x2 = x.reshape(T, D)
    for l in range(depth):
        mod = _adaln(c, w_ada[l], b_ada[l])
        shift1, scale1, gate1, shift2, scale2, gate2 = jnp.split(mod, 6, axis=-1)

        proj, wo_bf = _inproj(x2, norm1_w[l], shift1, scale1, w_in[l].astype(BF16), w_o[l], S)
        proj3 = proj.reshape(B, S, -1)
        o_sb = _attention(proj3, q_norm_w[l], k_norm_w[l], n_heads)
        o_pool = _pool(proj3, w_pool[l], pool_scale[l], pool_width)
        x1, h2p, gates_wide, idx_t = _outproj(
            o_sb.reshape(T, sb_width), o_pool.reshape(T, pool_width), wo_bf,
            x2, gate1, shift2, scale2, norm2_w[l], w_router[l], b_router[l], S)

        dest_t, meta = _route(idx_t, n_exp)
        cnt = meta[:, 0]
        starts = meta[:, 1]
        dest_flat = dest_t[:TOP_K].reshape(TOP_K * T)
        x_pad = _dispatch(dest_flat, cnt, starts, h2p, n_rows)

        F = w_exp_out.shape[2]
        act = _ffn1(_work_items(cnt, F // 1024, n_blocks), x_pad, w_exp_in[l], b_exp_in[l])
        y_pad = _ffn2(_work_items(cnt, 1, n_blocks), act, w_exp_out[l], b_exp_out[l])
        x2 = _combine(dest_flat, y_pad, gates_wide, x1, gate2, S)
    return x2.reshape(B, S, D)
```

```python
import functools
import math

import jax
import jax.numpy as jnp
from jax import lax
from jax.experimental import pallas as pl
from jax.experimental.pallas import tpu as pltpu

F32 = jnp.float32
BF16 = jnp.bfloat16
I32 = jnp.int32
U32 = jnp.uint32

EPS = 1e-6
HEAD_DIM = 128
POOL_WINDOWS = (2, 4, 8, 16)
TOP_K = 4
SWIGLU_ALPHA = 1.702
SWIGLU_LIMIT = 7.0

LANES = 128
SUBLANES = 8
EXPERT_ROWS = 256
TAIL_ROWS = 64
MIN_TAIL_PIECES = 2
LOG_UNDERFLOW = 104.0
VMEM_LIMIT = 56 * 1024 * 1024
WEIGHT_DMA_PRIORITY = 1


def _params(sem=None, vmem=None):
    return pltpu.CompilerParams(dimension_semantics=sem, vmem_limit_bytes=vmem)


_HIGH_HALF = 0xFFFF0000


def _pack_rows(v):
    bits = lax.bitcast_convert_type(v.astype(BF16).astype(F32), U32)
    half = v.shape[1] // 2
    return (bits[:, :half] & jnp.uint32(_HIGH_HALF)) | (bits[:, half:] >> 16)


def _unpack_halves(p):
    hi = lax.bitcast_convert_type(p & jnp.uint32(_HIGH_HALF), F32)
    lo = lax.bitcast_convert_type(p << 16, F32)
    return hi, lo


def _unpack_rows(p):
    hi, lo = _unpack_halves(p)
    return jnp.concatenate([hi.astype(BF16), lo.astype(BF16)], axis=1)


def _adaln_kernel(c_ref, w_ref, b_ref, o_ref):
    c = c_ref[...]
    ca = c / (1.0 + jnp.exp(-c))
    o_ref[...] = jnp.dot(ca.astype(BF16), w_ref[...].astype(BF16),
                         preferred_element_type=F32) + b_ref[...]


def _adaln(c, w_ada, b_ada):
    B, D = c.shape
    N = w_ada.shape[1]
    rows = 8
    tn = 2048
    cp = jnp.zeros((rows, D), F32).at[:B].set(c)
    out = pl.pallas_call(
        _adaln_kernel,
        out_shape=jax.ShapeDtypeStruct((rows, N), F32),
        grid=(N // tn,),
        in_specs=[pl.BlockSpec((rows, D), lambda j: (0, 0)),
                  pl.BlockSpec((D, tn), lambda j: (0, j)),
                  pl.BlockSpec((1, tn), lambda j: (0, j))],
        out_specs=pl.BlockSpec((rows, tn), lambda j: (0, j)),
        compiler_params=_params(("arbitrary",), VMEM_LIMIT),
        name="adaln",
    )(cp, w_ada, b_ada.reshape(1, N))
    return out[:B]


def _inproj_kernel(x_ref, nw_ref, sh_ref, sc_ref, w_ref, wo_ref, o_ref, wo16_ref, h_ref,
                   *, tm, ch):
    wo16_ref[...] = wo_ref[...].astype(BF16)

    @pl.when(pl.program_id(1) == 0)
    def _():
        mul = nw_ref[...] * (1.0 + sc_ref[0])
        add = sh_ref[0]

        def body(c, _):
            r0 = pl.multiple_of(c * ch, ch)
            x = x_ref[pl.ds(r0, ch), :]
            inv = lax.rsqrt(jnp.mean(x * x, axis=-1, keepdims=True) + EPS)
            h_ref[pl.ds(r0, ch), :] = (x * inv * mul + add).astype(BF16)
            return 0

        lax.fori_loop(0, tm // ch, body, 0)

    o_ref[...] = jnp.dot(h_ref[...], w_ref[...],
                         preferred_element_type=F32).astype(o_ref.dtype)


def _inproj(x2, norm_w, shift, scale, w_bf, w_o, S):
    T, D = x2.shape
    N = w_bf.shape[1]
    tm, tn, ch = 1024, 2048, 128
    per_b = S // tm
    n_i, n_j = T // tm, N // tn
    slab = w_o.shape[0] // (n_i * n_j)
    return pl.pallas_call(
        functools.partial(_inproj_kernel, tm=tm, ch=ch),
        out_shape=(jax.ShapeDtypeStruct((T, N), BF16),
                   jax.ShapeDtypeStruct(w_o.shape, BF16)),
        grid=(n_i, n_j),
        in_specs=[pl.BlockSpec((tm, D), lambda i, j: (i, 0)),
                  pl.BlockSpec((1, D), lambda i, j: (0, 0)),
                  pl.BlockSpec((1, 1, D), lambda i, j: (i // per_b, 0, 0)),
                  pl.BlockSpec((1, 1, D), lambda i, j: (i // per_b, 0, 0)),
                  pl.BlockSpec((D, tn), lambda i, j: (0, j)),
                  pl.BlockSpec((slab, w_o.shape[1]), lambda i, j: (i * n_j + j, 0))],
        out_specs=(pl.BlockSpec((tm, tn), lambda i, j: (i, j)),
                   pl.BlockSpec((slab, w_o.shape[1]), lambda i, j: (i * n_j + j, 0))),
        scratch_shapes=[pltpu.VMEM((tm, D), BF16)],
        compiler_params=_params(("arbitrary", "arbitrary"), VMEM_LIMIT),
        name="inproj",
    )(x2, norm_w.reshape(1, D), shift[:, None, :], scale[:, None, :], w_bf, w_o)


def _attn_kernel(q_ref, k_ref, v_ref, qw_ref, kw_ref, o_ref, kn_ref, carry_ref, acc_ref,
                 *, S, tq, hg, scale):
    i = pl.program_id(2)
    d = HEAD_DIM

    def head_norm(x, w):
        parts = []
        for h in range(hg):
            xh = x[:, h * d:(h + 1) * d]
            inv = lax.rsqrt(jnp.mean(xh * xh, axis=-1, keepdims=True) + EPS)
            parts.append(xh * inv * w)
        return parts

    @pl.when(i == 0)
    def _():
        def body(c, _):
            r0 = pl.multiple_of(c * tq, tq)
            parts = head_norm(k_ref[0, pl.ds(r0, tq), :].astype(F32), kw_ref[...])
            for h in range(hg):
                kn_ref[pl.ds(r0, tq), h * d:(h + 1) * d] = parts[h].astype(BF16)
            return 0

        lax.fori_loop(0, S // tq, body, 0)

    qb = [(p * scale).astype(BF16) for p in head_norm(q_ref[0].astype(F32), qw_ref[...])]

    row = lax.broadcasted_iota(I32, (tq, tq), 0)
    col = lax.broadcasted_iota(I32, (tq, tq), 1)
    causal = col < row
    tri = (row > col).astype(BF16)

    def scores(h, rows, r0, nk, mask=None):
        kblk = kn_ref[pl.ds(r0, nk), h * d:(h + 1) * d]
        z = lax.dot_general(qb[h][rows], kblk, (((1,), (1,)), ((), ())),
                            preferred_element_type=F32)
        t = jnp.log(1.0 + jnp.exp(-jnp.abs(z)))
        lsn = jnp.minimum(-z, 0.0) - t
        lsp = lsn + z
        if mask is not None:
            lsn = jnp.where(mask, lsn, 0.0)
        later = jnp.dot(lsn.astype(BF16), tri[:nk, :nk], preferred_element_type=F32)
        return lsp + later, later[:, :1] + lsn[:, :1]

    def weighted(a, h, r0, nk):
        vblk = v_ref[0, pl.ds(r0, nk), h * d:(h + 1) * d]
        return jnp.dot(a.astype(BF16), vblk, preferred_element_type=F32)

    has_prev = i > 0
    rd = pl.multiple_of(i * tq, tq)
    rp = pl.multiple_of(jnp.maximum(i - 1, 0) * tq, tq)
    half = tq // 2
    top, bottom, every = slice(0, half), slice(half, tq), slice(0, tq)
    worst = None
    for h in range(hg):
        cols = slice(h * d, (h + 1) * d)
        log_t, sum_t = scores(h, top, rd, half, causal[top, top])
        log_b, sum_b = scores(h, bottom, rd, tq, causal[bottom, :])
        log_p, sum_p = scores(h, every, rp, tq)
        sum_d = jnp.concatenate([sum_t, sum_b], axis=0)
        a_t = jnp.where(causal[top, top], jnp.exp(log_t), 0.0)
        a_b = jnp.where(causal[bottom, :], jnp.exp(log_b), 0.0)
        a_p = jnp.where(has_prev, jnp.exp(log_p + sum_d), 0.0)
        from_prev = weighted(a_p, h, rp, tq)
        acc_ref[top, cols] = weighted(a_t, h, rd, half) + from_prev[top]
        acc_ref[bottom, cols] = weighted(a_b, h, rd, tq) + from_prev[bottom]
        carry = jnp.where(has_prev, sum_d + sum_p, sum_d)
        carry_ref[h] = carry
        m = jnp.max(carry)
        worst = m if worst is None else jnp.maximum(worst, m)

    def earlier(kb):
        r0 = pl.multiple_of(kb * tq, tq)
        worst = None
        for h in range(hg):
            cols = slice(h * d, (h + 1) * d)
            log_a, row_sum = scores(h, every, r0, tq)
            acc_ref[:, cols] += weighted(jnp.exp(log_a + carry_ref[h]), h, r0, tq)
            carry = carry_ref[h] + row_sum
            carry_ref[h] = carry
            m = jnp.max(carry)
            worst = m if worst is None else jnp.maximum(worst, m)
        return worst

    def cond(st):
        kb, m = st
        return jnp.logical_and(kb >= 0, m > -LOG_UNDERFLOW)

    def body(st):
        kb, _ = st
        return kb - 1, earlier(kb)

    lax.while_loop(cond, body, (i - 2, worst))
    o_ref[0] = acc_ref[...].astype(o_ref.dtype)


def _attention(proj3, q_norm_w, k_norm_w, n_heads):
    B, S, _ = proj3.shape
    d = HEAD_DIM
    tq = 256
    hg = 8
    G = n_heads // hg
    w = hg * d
    return pl.pallas_call(
        functools.partial(_attn_kernel, S=S, tq=tq, hg=hg, scale=1.0 / math.sqrt(d)),
        out_shape=jax.ShapeDtypeStruct((B, S, n_heads * d), BF16),
        grid=(B, G, S // tq),
        in_specs=[pl.BlockSpec((1, tq, w), lambda b, g, i: (b, i, g)),
                  pl.BlockSpec((1, S, w), lambda b, g, i: (b, 0, G + g)),
                  pl.BlockSpec((1, S, w), lambda b, g, i: (b, 0, 2 * G + g)),
                  pl.BlockSpec((1, d), lambda b, g, i: (0, 0)),
                  pl.BlockSpec((1, d), lambda b, g, i: (0, 0))],
        out_specs=pl.BlockSpec((1, tq, w), lambda b, g, i: (b, i, g)),
        scratch_shapes=[pltpu.VMEM((S, w), BF16),
                        pltpu.VMEM((hg, tq, 1), F32),
                        pltpu.VMEM((tq, w), F32)],
        compiler_params=_params(("arbitrary", "arbitrary", "arbitrary"), VMEM_LIMIT),
        name="stickbreak_attn",
    )(proj3, proj3, proj3, q_norm_w.reshape(1, d), k_norm_w.reshape(1, d))


def _pool_kernel(u_ref, w_ref, ps_ref, o_ref, *, S, ch, gd):
    halo = 16
    wgs = [w_ref[g].astype(BF16) for g in range(len(POOL_WINDOWS))]

    def body(c, _):
        r0 = pl.multiple_of(c * ch, ch)
        p0 = pl.multiple_of(jnp.maximum(r0 - halo, 0), halo)
        t = r0 + lax.broadcasted_iota(I32, (ch, 1), 0)
        for g, win in enumerate(POOL_WINDOWS):
            lo, hi = g * gd, (g + 1) * gd
            cur = u_ref[0, pl.ds(r0, ch), lo:hi].astype(F32)
            prev = u_ref[0, pl.ds(p0, halo), lo:hi].astype(F32)
            prev = jnp.where(c > 0, prev, 0.0)
            s = jnp.concatenate([prev, cur], axis=0)
            n = 1
            while n < win:
                s = s + pltpu.roll(s, n, 0)
                n *= 2
            s = s[halo:]
            cnt = jnp.minimum(t + 1, win).astype(F32)
            p = s / cnt - cur
            y = jnp.dot(p.astype(BF16), wgs[g], preferred_element_type=F32) * ps_ref[:, lo:hi]
            o_ref[0, pl.ds(r0, ch), lo:hi] = y.astype(o_ref.dtype)
        return 0

    lax.fori_loop(0, S // ch, body, 0)


def _pool(proj3, w_pool, pool_scale, pool_width):
    B, S, NP = proj3.shape
    G, gd, _ = w_pool.shape
    return pl.pallas_call(
        functools.partial(_pool_kernel, S=S, ch=512, gd=gd),
        out_shape=jax.ShapeDtypeStruct((B, S, pool_width), BF16),
        grid=(B,),
        in_specs=[pl.BlockSpec((1, S, pool_width), lambda b: (b, 0, NP // pool_width - 1)),
                  pl.BlockSpec((G, gd, gd), lambda b: (0, 0, 0)),
                  pl.BlockSpec((1, pool_width), lambda b: (0, 0))],
        out_specs=pl.BlockSpec((1, S, pool_width), lambda b: (b, 0, 0)),
        compiler_params=_params(("arbitrary",), VMEM_LIMIT),
        name="pool_mixer",
    )(proj3, w_pool, pool_scale.reshape(1, pool_width))


def _outproj_kernel(osb_ref, opool_ref, wo_ref, x_ref, g1_ref, sh_ref, sc_ref, nw_ref,
                    wr_ref, br_ref, x1_ref, h2p_ref, gates_ref, idx_ref, *, sbw, n_exp, sub):
    for r0 in range(0, x_ref.shape[0], sub):
        rows = slice(r0, r0 + sub)
        _outproj_rows(osb_ref.at[rows], opool_ref.at[rows], wo_ref, x_ref.at[rows], g1_ref,
                      sh_ref, sc_ref, nw_ref, wr_ref, br_ref, x1_ref.at[rows], h2p_ref.at[rows],
                      gates_ref.at[rows], idx_ref.at[:, rows], sbw=sbw, n_exp=n_exp)


def _outproj_rows(osb_ref, opool_ref, wo_ref, x_ref, g1_ref, sh_ref, sc_ref, nw_ref,
                  wr_ref, br_ref, x1_ref, h2p_ref, gates_ref, idx_ref, *, sbw, n_exp):
    tm, D = x_ref.shape
    mixed = (jnp.dot(osb_ref[...], wo_ref[:sbw, :], preferred_element_type=F32)
             + jnp.dot(opool_ref[...], wo_ref[sbw:, :], preferred_element_type=F32))
    x1 = x_ref[...] + g1_ref[0] * mixed
    x1_ref[...] = x1
    inv = lax.rsqrt(jnp.mean(x1 * x1, axis=-1, keepdims=True) + EPS)
    h2 = x1 * inv * (nw_ref[...] * (1.0 + sc_ref[0])) + sh_ref[0]
    hb = h2.astype(BF16)
    h2p_ref[...] = _pack_rows(h2)

    vals = lax.dot_general(wr_ref[...].astype(BF16), hb, (((1,), (1,)), ((), ())),
                           preferred_element_type=F32) + br_ref[...]
    expert = lax.broadcasted_iota(I32, (n_exp, tm), 0).astype(F32)
    tops, ids = [], []
    for _ in range(TOP_K):
        m = jnp.max(vals, axis=0, keepdims=True)
        first = jnp.min(jnp.where(vals == m, expert, float(n_exp)), axis=0, keepdims=True)
        tops.append(m)
        ids.append(first)
        vals = jnp.where(expert == first, -jnp.inf, vals)
    es = [jnp.exp(m - tops[0]) for m in tops]
    den = es[0]
    for e in es[1:]:
        den = den + e
    slot = lax.broadcasted_iota(I32, (LANES, tm), 0)
    gates = jnp.zeros((LANES, tm), F32)
    for k in range(TOP_K):
        gates = jnp.where(slot == k, es[k] / den, gates)
    gates_ref[...] = gates.T
    slot8 = lax.broadcasted_iota(I32, (SUBLANES, tm), 0)
    idx = jnp.zeros((SUBLANES, tm), F32)
    for k in range(TOP_K):
        idx = jnp.where(slot8 == k, ids[k], idx)
    idx_ref[...] = idx.astype(I32)


def _outproj(o_sb, o_pool, wo_bf, x2, gate1, shift2, scale2, norm2_w, w_router, b_router, S):
    T, D = x2.shape
    sbw = o_sb.shape[1]
    pw = o_pool.shape[1]
    n_exp = w_router.shape[1]
    tm, sub = 512, 256
    per_b = S // tm
    wr = w_router.T
    br = b_router.reshape(n_exp, 1)
    mod_spec = pl.BlockSpec((1, 1, D), lambda i: (i // per_b, 0, 0))
    return pl.pallas_call(
        functools.partial(_outproj_kernel, sbw=sbw, n_exp=n_exp, sub=sub),
        out_shape=(jax.ShapeDtypeStruct((T, D), F32),
                   jax.ShapeDtypeStruct((T, D // 2), U32),
                   jax.ShapeDtypeStruct((T, LANES), F32),
                   jax.ShapeDtypeStruct((SUBLANES, T), I32)),
        grid=(T // tm,),
        in_specs=[pl.BlockSpec((tm, sbw), lambda i: (i, 0)),
                  pl.BlockSpec((tm, pw), lambda i: (i, 0)),
                  pl.BlockSpec((sbw + pw, D), lambda i: (0, 0)),
                  pl.BlockSpec((tm, D), lambda i: (i, 0)),
                  mod_spec, mod_spec, mod_spec,
                  pl.BlockSpec((1, D), lambda i: (0, 0)),
                  pl.BlockSpec((n_exp, D), lambda i: (0, 0)),
                  pl.BlockSpec((n_exp, 1), lambda i: (0, 0))],
        out_specs=(pl.BlockSpec((tm, D), lambda i: (i, 0)),
                   pl.BlockSpec((tm, D // 2), lambda i: (i, 0)),
                   pl.BlockSpec((tm, LANES), lambda i: (i, 0)),
                   pl.BlockSpec((SUBLANES, tm), lambda i: (0, i))),
        compiler_params=_params(("arbitrary",), VMEM_LIMIT),
        name="outproj_router",
    )(o_sb, o_pool, wo_bf, x2, gate1[:, None, :], shift2[:, None, :], scale2[:, None, :],
      norm2_w.reshape(1, D), wr, br)


def _route_kernel(idx_ref, dest_ref, meta_ref, rank_ref, *, T, ch, n_exp):
    expert = lax.broadcasted_iota(I32, (n_exp, ch), 0)
    row = lax.broadcasted_iota(I32, (ch, ch), 0)
    col = lax.broadcasted_iota(I32, (ch, ch), 1)
    earlier = (row < col).astype(BF16)

    def chunk(c):
        return slice(c * ch, (c + 1) * ch)

    def count(c, cnt):
        member = expert == idx_ref[0:1, chunk(c)]
        for k in range(1, TOP_K):
            member = jnp.logical_or(member, expert == idx_ref[k:k + 1, chunk(c)])
        mf = jnp.where(member, 1.0, 0.0)
        rank_ref[:, chunk(c)] = jnp.dot(mf.astype(BF16), earlier,
                                        preferred_element_type=F32) + cnt
        return cnt + jnp.sum(mf, axis=1, keepdims=True)

    cnt = jnp.zeros((n_exp, 1), F32)
    for c in range(T // ch):
        cnt = count(c, cnt)
    padded = jnp.broadcast_to(jnp.ceil(cnt / EXPERT_ROWS) * EXPERT_ROWS, (n_exp, LANES))
    sub = lax.broadcasted_iota(I32, (n_exp, LANES), 0)
    ends = padded
    sh = 1
    while sh < n_exp:
        ends = ends + jnp.where(sub >= sh, pltpu.roll(ends, sh, 0), 0.0)
        sh *= 2
    starts_wide = ends - padded
    starts = starts_wide[:, :1]
    lane = lax.broadcasted_iota(I32, (n_exp, LANES), 1)
    meta = jnp.where(lane == 0, jnp.broadcast_to(cnt, (n_exp, LANES)),
                     jnp.where(lane == 1, starts_wide, 0.0))
    meta_ref[...] = meta.astype(I32)

    slot = lax.broadcasted_iota(I32, (SUBLANES, ch), 0)

    def place(c, _):
        val = rank_ref[:, chunk(c)] + starts
        out = jnp.zeros((SUBLANES, ch), F32)
        for k in range(TOP_K):
            mine = expert == idx_ref[k:k + 1, chunk(c)]
            d = jnp.sum(jnp.where(mine, val, 0.0), axis=0, keepdims=True)
            out = jnp.where(slot == k, d, out)
        dest_ref[:, chunk(c)] = out.astype(I32)
        return 0

    for c in range(T // ch):
        place(c, 0)


def _route(idx_t, n_exp):
    T = idx_t.shape[1]
    return pl.pallas_call(
        functools.partial(_route_kernel, T=T, ch=256, n_exp=n_exp),
        out_shape=(jax.ShapeDtypeStruct((SUBLANES, T), I32),
                   jax.ShapeDtypeStruct((n_exp, LANES), I32)),
        grid=(1,),
        in_specs=[pl.BlockSpec((SUBLANES, T), lambda i: (0, 0))],
        out_specs=(pl.BlockSpec((SUBLANES, T), lambda i: (0, 0)),
                   pl.BlockSpec((n_exp, LANES), lambda i: (0, 0))),
        scratch_shapes=[pltpu.VMEM((n_exp, T), F32)],
        compiler_params=_params(("arbitrary",), VMEM_LIMIT),
        name="route_ranks",
    )(idx_t)


def _dispatch_kernel(dest_ref, cnt_ref, start_ref, h_ref, x_ref, z_ref, sem, zsem,
                     *, tb, n_exp, n_blocks, n_tokens):
    s = pl.program_id(0)

    @pl.when(s == 0)
    def _():
        z_ref[...] = jnp.zeros_like(z_ref)
        _dispatch_zero_fill(cnt_ref, start_ref, x_ref, z_ref, zsem, n_exp, n_blocks)

    base = s * tb
    for t in range(tb):
        for k in range(TOP_K):
            d = dest_ref[base + (k * n_tokens + t)]
            pltpu.make_async_copy(h_ref.at[pl.ds(t, 1), :], x_ref.at[pl.ds(d, 1), :],
                                  sem).start(priority=k % 2)

    for _ in range(TOP_K):
        pltpu.make_async_copy(h_ref, x_ref.at[pl.ds(0, tb), :], sem).wait()


def _dispatch_zero_fill(cnt_ref, start_ref, x_ref, z_ref, zsem, n_exp, n_blocks):
    def block_fill(blk, wait):
        r0 = pl.multiple_of(blk * EXPERT_ROWS, EXPERT_ROWS)
        cp = pltpu.make_async_copy(z_ref, x_ref.at[pl.ds(r0, EXPERT_ROWS), :], zsem)
        if wait:
            cp.wait()
        else:
            cp.start()

    def zero_fill(e, wait):
        cnt = cnt_ref[e]

        @pl.when((cnt & (EXPERT_ROWS - 1)) != 0)
        def _():
            block_fill((start_ref[e] + cnt) // EXPERT_ROWS, wait)

        return 0

    used = (start_ref[n_exp - 1] + cnt_ref[n_exp - 1] + EXPERT_ROWS - 1) // EXPERT_ROWS

    def tail_fill(blk, wait):
        block_fill(blk, wait)
        return 0

    lax.fori_loop(0, n_exp, lambda e, _: zero_fill(e, False), 0)
    lax.fori_loop(used, n_blocks, lambda b, _: tail_fill(b, False), 0)
    lax.fori_loop(0, n_exp, lambda e, _: zero_fill(e, True), 0)
    lax.fori_loop(used, n_blocks, lambda b, _: tail_fill(b, True), 0)


def _dispatch(dest_flat, cnt, starts, h2p, n_rows):
    T, W = h2p.shape
    n_exp = cnt.shape[0]
    tb = 512
    return pl.pallas_call(
        functools.partial(_dispatch_kernel, tb=tb, n_exp=n_exp, n_blocks=n_rows // EXPERT_ROWS,
                          n_tokens=T),
        out_shape=jax.ShapeDtypeStruct((n_rows, W), U32),
        grid_spec=pltpu.PrefetchScalarGridSpec(
            num_scalar_prefetch=3,
            grid=(T // tb,),
            in_specs=[pl.BlockSpec((tb, W), lambda s, *_: (s, 0))],
            out_specs=pl.BlockSpec(memory_space=pl.ANY),
            scratch_shapes=[pltpu.VMEM((EXPERT_ROWS, W), U32),
                            pltpu.SemaphoreType.DMA, pltpu.SemaphoreType.DMA]),
        compiler_params=_params(("arbitrary",), VMEM_LIMIT),
        name="dispatch_rows",
    )(dest_flat, cnt, starts, h2p)


def _ffn1_kernel(ri_ref, ro_ref, e_ref, j_ref, first_ref, valid_ref, slot_ref, ne_ref, nj_ref, more_ref, pieces_ref,
                 x_ref, w_ref, b_ref, o_ref, stage, wbf, sem, *, F, tn):
    q = pl.program_id(0)

    def wcopy(e, j, slot, part):
        c0 = pl.multiple_of(part * F + j * tn, tn)
        return pltpu.make_async_copy(w_ref.at[e, :, pl.ds(c0, tn)], stage.at[slot, part],
                                     sem.at[slot])

    @pl.when(q == 0)
    def _():
        for part in range(2):
            wcopy(e_ref[0], j_ref[0], 0, part).start(priority=WEIGHT_DMA_PRIORITY)

    @pl.when(first_ref[q] == 1)
    def _():
        slot = slot_ref[q]
        for part in range(2):
            wcopy(e_ref[q], j_ref[q], slot, part).wait()

        @pl.when(more_ref[q] == 1)
        def _():
            for part in range(2):
                wcopy(ne_ref[q], nj_ref[q], 1 - slot, part).start(priority=WEIGHT_DMA_PRIORITY)

    def swiglu_block(w_gate, w_lin, rows):
        xb = _unpack_rows(x_ref[:rows, :])
        g = jnp.dot(xb, w_gate, preferred_element_type=F32) + b_ref[0, 0, 0]
        lin = jnp.dot(xb, w_lin, preferred_element_type=F32) + b_ref[0, 1, 0]
        g = jnp.minimum(g, SWIGLU_LIMIT)
        lin = jnp.clip(lin, -SWIGLU_LIMIT, SWIGLU_LIMIT)
        act = g / (1.0 + jnp.exp(-SWIGLU_ALPHA * g)) * (lin + 1.0)
        o_ref[:rows, :] = act.astype(o_ref.dtype)
        if rows < EXPERT_ROWS:
            o_ref[rows:, :] = jnp.zeros((EXPERT_ROWS - rows, o_ref.shape[1]), o_ref.dtype)

    @pl.when(first_ref[q] == 1)
    def _():
        slot = slot_ref[q]
        w16 = [stage[slot, part].astype(BF16) for part in range(2)]
        for part in range(2):
            wbf[part] = w16[part]
        swiglu_block(w16[0], w16[1], EXPERT_ROWS)

    later = jnp.logical_and(valid_ref[q] == 1, first_ref[q] == 0)

    for pieces in range(MIN_TAIL_PIECES, EXPERT_ROWS // TAIL_ROWS + 1):
        @pl.when(jnp.logical_and(later, pieces_ref[q] == pieces))
        def _(pieces=pieces):
            swiglu_block(wbf[0], wbf[1], pieces * TAIL_ROWS)

    @pl.when(valid_ref[q] == 0)
    def _():
        o_ref[...] = jnp.zeros_like(o_ref)


def _ffn2_kernel(ri_ref, ro_ref, e_ref, j_ref, first_ref, valid_ref, slot_ref, ne_ref, nj_ref, more_ref, pieces_ref,
                 a_ref, w_ref, b_ref, o_ref, stage, wbf, sem):
    q = pl.program_id(0)

    def wcopy(e, slot):
        return pltpu.make_async_copy(w_ref.at[e], stage.at[slot], sem.at[slot])

    @pl.when(q == 0)
    def _():
        wcopy(e_ref[0], 0).start(priority=WEIGHT_DMA_PRIORITY)

    @pl.when(first_ref[q] == 1)
    def _():
        slot = slot_ref[q]
        wcopy(e_ref[q], slot).wait()

        @pl.when(more_ref[q] == 1)
        def _():
            wcopy(ne_ref[q], 1 - slot).start(priority=WEIGHT_DMA_PRIORITY)

    def out_block(w, rows):
        y = jnp.dot(a_ref[:rows, :], w, preferred_element_type=F32) + b_ref[0]
        o_ref[:rows, :] = _pack_rows(y)
        if rows < EXPERT_ROWS:
            o_ref[rows:, :] = jnp.zeros((EXPERT_ROWS - rows, o_ref.shape[1]), o_ref.dtype)

    @pl.when(first_ref[q] == 1)
    def _():
        w16 = stage[slot_ref[q]].astype(BF16)
        wbf[...] = w16
        out_block(w16, EXPERT_ROWS)

    later = jnp.logical_and(valid_ref[q] == 1, first_ref[q] == 0)

    for pieces in range(MIN_TAIL_PIECES, EXPERT_ROWS // TAIL_ROWS + 1):
        @pl.when(jnp.logical_and(later, pieces_ref[q] == pieces))
        def _(pieces=pieces):
            out_block(wbf[...], pieces * TAIL_ROWS)

    @pl.when(valid_ref[q] == 0)
    def _():
        o_ref[...] = jnp.zeros_like(o_ref)


def _work_items(cnt, n_col_tiles, n_blocks):
    n_exp = cnt.shape[0]
    nblk = (cnt + EXPERT_ROWS - 1) // EXPERT_ROWS
    bstart = jnp.cumsum(nblk) - nblk
    gsize = jnp.repeat(nblk, n_col_tiles)
    gend = jnp.cumsum(gsize)
    n_groups = n_exp * n_col_tiles
    gid = jnp.arange(n_groups, dtype=I32)
    total = gend[-1]
    q = jnp.arange(n_blocks * n_col_tiles, dtype=I32)
    qc = jnp.minimum(q, total - 1)
    g = jnp.sum((gend[None, :] <= qc[:, None]).astype(I32), axis=1)
    nonempty = gsize > 0
    ordinal = jnp.cumsum(nonempty.astype(I32)) - 1
    nxt_incl = lax.cummin(jnp.where(nonempty, gid, n_groups), reverse=True)
    nxt = jnp.concatenate([nxt_incl[1:], jnp.full((1,), n_groups, I32)])
    more = nxt < n_groups
    nxt = jnp.minimum(nxt, n_groups - 1)
    per_group = jnp.stack([gend - gsize, gid // n_col_tiles, gid % n_col_tiles,
                           jnp.repeat(bstart, n_col_tiles), ordinal % 2,
                           nxt // n_col_tiles, nxt % n_col_tiles, more.astype(I32),
                           jnp.repeat(cnt, n_col_tiles)])
    pick = (g[None, :, None] == gid[None, None, :]).astype(I32)
    gstart, e, j, brow, slot, ne, nj, more, rows = jnp.sum(pick * per_group[:, None, :], axis=2)
    r = qc - gstart
    valid = q < total
    first = jnp.logical_and(valid, r == 0)
    pieces = jnp.clip((rows - r * EXPERT_ROWS + TAIL_ROWS - 1) // TAIL_ROWS, MIN_TAIL_PIECES,
                      EXPERT_ROWS // TAIL_ROWS)
    over = q - total
    row_in = brow + r
    row_out = jnp.where(valid, row_in, jnp.sum(nblk) + over // n_col_tiles)
    col_out = jnp.where(valid, j, over % n_col_tiles)
    as_i32 = lambda a: a.astype(I32)
    return tuple(map(as_i32, (row_in, row_out, e, col_out, first, valid, slot, ne, nj, more,
                              pieces)))


def _ffn1(items, x_pad, w_exp_in, b_exp_in):
    P, W = x_pad.shape
    n_exp, D, F2 = w_exp_in.shape
    F = F2 // 2
    tn = 1024
    nj = F // tn
    n_items = items[0].shape[0]
    bias = b_exp_in.reshape(n_exp, 2, nj, 1, tn)
    return pl.pallas_call(
        functools.partial(_ffn1_kernel, F=F, tn=tn),
        out_shape=jax.ShapeDtypeStruct((P, F), BF16),
        grid_spec=pltpu.PrefetchScalarGridSpec(
            num_scalar_prefetch=11,
            grid=(n_items,),
            in_specs=[pl.BlockSpec((EXPERT_ROWS, W), lambda q, ri, *_: (ri[q], 0)),
                      pl.BlockSpec(memory_space=pl.ANY),
                      pl.BlockSpec((1, 2, 1, 1, tn),
                                   lambda q, ri, ro, e, j, *_: (e[q], 0, j[q], 0, 0))],
            out_specs=pl.BlockSpec((EXPERT_ROWS, tn),
                                   lambda q, ri, ro, e, j, *_: (ro[q], j[q])),
            scratch_shapes=[pltpu.VMEM((2, 2, D, tn), F32),
                            pltpu.VMEM((2, D, tn), BF16),
                            pltpu.SemaphoreType.DMA((2,))]),
        compiler_params=_params(("arbitrary",), VMEM_LIMIT),
        name="expert_in_swiglu",
    )(*items, x_pad, w_exp_in, bias)


def _ffn2(items, act, w_exp_out, b_exp_out):
    P, F = act.shape
    n_exp, _, D = w_exp_out.shape
    n_items = items[0].shape[0]
    return pl.pallas_call(
        _ffn2_kernel,
        out_shape=jax.ShapeDtypeStruct((P, D // 2), U32),
        grid_spec=pltpu.PrefetchScalarGridSpec(
            num_scalar_prefetch=11,
            grid=(n_items,),
            in_specs=[pl.BlockSpec((EXPERT_ROWS, F), lambda q, ri, *_: (ri[q], 0)),
                      pl.BlockSpec(memory_space=pl.ANY),
                      pl.BlockSpec((1, 1, D), lambda q, ri, ro, e, *_: (e[q], 0, 0))],
            out_specs=pl.BlockSpec((EXPERT_ROWS, D // 2), lambda q, ri, ro, *_: (ro[q], 0)),
            scratch_shapes=[pltpu.VMEM((2, F, D), F32),
                            pltpu.VMEM((F, D), BF16),
                            pltpu.SemaphoreType.DMA((2,))]),
        compiler_params=_params(("arbitrary",), VMEM_LIMIT),
        name="expert_out",
    )(*items, act, w_exp_out, b_exp_out[:, None, :])


def _combine_kernel(dest_ref, y_ref, gates_ref, x1_ref, g2_ref, o_ref, buf, sem,
                    *, tm, n_tokens):
    s = pl.program_id(0)
    ns = pl.num_programs(0)

    def gather(step, slot, unrolled):
        base = step * tm

        def row(t, k):
            d = dest_ref[base + (k * n_tokens + t)]
            pltpu.make_async_copy(y_ref.at[pl.ds(d, 1), :], buf.at[slot, k, pl.ds(t, 1), :],
                                  sem.at[slot]).start(priority=k % 2)

        if unrolled:
            for t in range(tm):
                for k in range(TOP_K):
                    row(t, k)
        else:
            def issue(t, _):
                for k in range(TOP_K):
                    row(t, k)
                return 0

            lax.fori_loop(0, tm, issue, 0)

    @pl.when(s == 0)
    def _():
        gather(0, 0, False)

    for par in range(2):
        @pl.when(jnp.logical_and(s + 1 < ns, (s + 1) % 2 == par))
        def _(par=par):
            gather(s + 1, par, True)

    slot = s % 2
    for k in range(TOP_K):
        pltpu.make_async_copy(y_ref.at[pl.ds(0, tm), :], buf.at[slot, k], sem.at[slot]).wait()
    gates = gates_ref[...]
    y_hi = y_lo = None
    for k in range(TOP_K):
        hi, lo = _unpack_halves(buf[slot, k])
        g = gates[:, k:k + 1]
        y_hi = g * hi if y_hi is None else y_hi + g * hi
        y_lo = g * lo if y_lo is None else y_lo + g * lo
    y = jnp.concatenate([y_hi, y_lo], axis=1)
    o_ref[...] = x1_ref[...] + g2_ref[0] * y


def _combine(dest_flat, y_pad, gates_wide, x1, gate2, S):
    T, D = x1.shape
    tm = 256
    per_b = S // tm
    return pl.pallas_call(
        functools.partial(_combine_kernel, tm=tm, n_tokens=T),
        out_shape=jax.ShapeDtypeStruct((T, D), F32),
        grid_spec=pltpu.PrefetchScalarGridSpec(
            num_scalar_prefetch=1,
            grid=(T // tm,),
            in_specs=[pl.BlockSpec(memory_space=pl.ANY),
                      pl.BlockSpec((tm, LANES), lambda i, d: (i, 0)),
                      pl.BlockSpec((tm, D), lambda i, d: (i, 0)),
                      pl.BlockSpec((1, 1, D), lambda i, d: (i // per_b, 0, 0))],
            out_specs=pl.BlockSpec((tm, D), lambda i, d: (i, 0)),
            scratch_shapes=[pltpu.VMEM((2, TOP_K, tm, D // 2), U32),
                            pltpu.SemaphoreType.DMA((2,))]),
        compiler_params=_params(("arbitrary",), VMEM_LIMIT),
        name="combine_rows",
    )(dest_flat, y_pad, gates_wide, x1, gate2[:, None, :])


def kernel(x, c, norm1_w, norm2_w, w_ada, b_ada, w_in, q_norm_w, k_norm_w, w_pool, pool_scale,
           w_o, w_router, b_router, w_exp_in, b_exp_in, w_exp_out, b_exp_out):
    B, S, D = x.shape
    T = B * S
    depth = w_ada.shape[0]
    n_exp = w_router.shape[-1]
    pool_width = pool_scale.shape[-1]
    sb_width = w_o.shape[1] - pool_width
    n_heads = sb_width // HEAD_DIM
    n_blocks = (T * TOP_K + n_exp * (EXPERT_ROWS - 1)) // EXPERT_ROWS
    n_rows = n_blocks * EXPERT_ROWS

    x2 = x.reshape(T, D)
    for l in range(depth):
        mod = _adaln(c, w_ada[l], b_ada[l])
        shift1, scale1, gate1, shift2, scale2, gate2 = jnp.split(mod, 6, axis=-1)

        proj, wo_bf = _inproj(x2, norm1_w[l], shift1, scale1, w_in[l].astype(BF16), w_o[l], S)
        proj3 = proj.reshape(B, S, -1)
        o_sb = _attention(proj3, q_norm_w[l], k_norm_w[l], n_heads)
        o_pool = _pool(proj3, w_pool[l], pool_scale[l], pool_width)
        x1, h2p, gates_wide, idx_t = _outproj(
            o_sb.reshape(T, sb_width), o_pool.reshape(T, pool_width), wo_bf,
            x2, gate1, shift2, scale2, norm2_w[l], w_router[l], b_router[l], S)

        dest_t, meta = _route(idx_t, n_exp)
        cnt = meta[:, 0]
        starts = meta[:, 1]
        dest_flat = dest_t[:TOP_K].reshape(TOP_K * T)
        x_pad = _dispatch(dest_flat, cnt, starts, h2p, n_rows)

        F = w_exp_out.shape[2]
        act = _ffn1(_work_items(cnt, F // 1024, n_blocks), x_pad, w_exp_in[l], b_exp_in[l])
        y_pad = _ffn2(_work_items(cnt, 1, n_blocks), act, w_exp_out[l], b_exp_out[l])
        x2 = _combine(dest_flat, y_pad, gates_wide, x1, gate2, S)
    return x2.reshape(B, S, D)
```
